```python
import jax, jax.numpy as jnp
from jax import lax
import numpy as np

D_MODEL = 1024
BATCH = 4
SEQ = 8192
DEPTH = 2

HEAD_DIM = 64
D_MIX = D_MODEL
N_HEADS_A = 4
N_HEADS_B = 4
N_HEADS_C = D_MIX // HEAD_DIM - N_HEADS_A - N_HEADS_B
ROPE_DIM = HEAD_DIM // 4
ROPE_THETA = 500000.0
MOBA_BLOCK = 256
MOBA_TOPK = 3
DSA_TOPK = 256
IDX_HEADS = 8
IDX_DIM = 64
Q_BLOCK = 128
D_FF = 2816
N_MOD = 9
RMS_EPS = 1e-6
NEG = -1e30

kernel_name = "hymba_style_moba_dsa_stickbreaking_macaron"

SPLIT_SIZES = (N_HEADS_A * HEAD_DIM, N_HEADS_A * HEAD_DIM, N_HEADS_A * HEAD_DIM,
               N_HEADS_B * HEAD_DIM, N_HEADS_B * HEAD_DIM, N_HEADS_B * HEAD_DIM,
               IDX_HEADS * IDX_DIM, IDX_DIM, IDX_HEADS,
               N_HEADS_C * HEAD_DIM, N_HEADS_C * HEAD_DIM, N_HEADS_C * HEAD_DIM)
D_IN = sum(SPLIT_SIZES)


def _rms(x):
    xf = x.astype(jnp.float32)
    return (xf * lax.rsqrt(jnp.mean(xf * xf, axis=-1, keepdims=True) + RMS_EPS))


def rmsnorm(x, g):
    return (_rms(x) * g.astype(jnp.float32)).astype(x.dtype)


def modulate(h, shift, scale):
    return h * (1 + scale) + shift


def swiglu(h, w1, w3, w2):
    return (jax.nn.silu(h @ w1) * (h @ w3)) @ w2


def rope_tables(seq):
    pos = jnp.arange(seq, dtype=jnp.float32)
    inv = ROPE_THETA ** (-jnp.arange(0, ROPE_DIM, 2, dtype=jnp.float32) / ROPE_DIM)
    ang = pos[:, None] * inv[None, :]
    return jnp.cos(ang), jnp.sin(ang)


def apply_partial_rope(x, cos, sin):
    half = ROPE_DIM // 2
    xf = x.astype(jnp.float32)
    x1, x2 = xf[..., :half], xf[..., half:ROPE_DIM]
    c, s = cos[None, :, None, :], sin[None, :, None, :]
    out = jnp.concatenate([x1 * c - x2 * s, x2 * c + x1 * s, xf[..., ROPE_DIM:]], axis=-1)
    return out.astype(x.dtype)


def _chunk(a, n):
    return jnp.moveaxis(a.reshape(a.shape[0], n, Q_BLOCK, *a.shape[2:]), 1, 0)


def _unchunk(a):
    a = jnp.moveaxis(a, 0, 1)
    return a.reshape(a.shape[0], a.shape[1] * a.shape[2], *a.shape[3:])


def moba_attention(q, k, v):
    B, S, H, d = q.shape
    nb = -(-S // MOBA_BLOCK)
    pad = nb * MOBA_BLOCK - S
    kp = jnp.pad(k, ((0, 0), (0, pad), (0, 0), (0, 0)))
    vp = jnp.pad(v, ((0, 0), (0, pad), (0, 0), (0, 0)))
    kb = kp.reshape(B, nb, MOBA_BLOCK, H, d)
    vb = vp.reshape(B, nb, MOBA_BLOCK, H, d)
    kmean = jnp.mean(kb.astype(jnp.float32), axis=2)
    kbt = jnp.transpose(kb, (0, 3, 1, 2, 4))
    vbt = jnp.transpose(vb, (0, 3, 1, 2, 4))
    n_sel = min(MOBA_TOPK, max(nb - 1, 1))
    scale = HEAD_DIM ** -0.5
    nq = S // Q_BLOCK
    gather_blocks = jax.vmap(jax.vmap(lambda blk, ii: blk[ii]))

    def one_block(args):
        qi, ci = args
        t = ci * Q_BLOCK + jnp.arange(Q_BLOCK)
        cur = (ci * Q_BLOCK) // MOBA_BLOCK
        gate = jnp.einsum('bqhd,bnhd->bqhn', qi.astype(jnp.float32), kmean)
        gate = jnp.where(jnp.arange(nb) < cur, gate, NEG)
        _, idx = lax.top_k(gate, n_sel)
        valid = jnp.transpose(idx < cur, (0, 2, 1, 3))
        idx_t = jnp.transpose(idx, (0, 2, 1, 3))
        ksel = gather_blocks(kbt, idx_t)
        vsel = gather_blocks(vbt, idx_t)
        qh = jnp.transpose(qi, (0, 2, 1, 3))
        s_past = jnp.einsum('bhqd,bhqnkd->bhqnk', qh, ksel).astype(jnp.float32) * scale
        s_past = jnp.where(valid[..., None], s_past, NEG)
        kown = lax.dynamic_slice_in_dim(kp, cur * MOBA_BLOCK, MOBA_BLOCK, axis=1)
        vown = lax.dynamic_slice_in_dim(vp, cur * MOBA_BLOCK, MOBA_BLOCK, axis=1)
        s_own = jnp.einsum('bhqd,bkhd->bhqk', qh, kown).astype(jnp.float32) * scale
        kpos = cur * MOBA_BLOCK + jnp.arange(MOBA_BLOCK)
        s_own = jnp.where(kpos[None, :] <= t[:, None], s_own, NEG)
        n_past = n_sel * MOBA_BLOCK
        s = jnp.concatenate([s_past.reshape(B, H, Q_BLOCK, n_past), s_own], axis=-1)
        p = jax.nn.softmax(s, axis=-1).astype(v.dtype)
        p_past = p[..., :n_past].reshape(B, H, Q_BLOCK, n_sel, MOBA_BLOCK)
        p_own = p[..., n_past:]
        return (jnp.einsum('bhqnk,bhqnkd->bqhd', p_past, vsel)
                + jnp.einsum('bhqk,bkhd->bqhd', p_own, vown))

    out = lax.map(one_block, (_chunk(q, nq), jnp.arange(nq)))
    return _unchunk(out)


def dsa_attention(q, k, v, q_idx, k_idx, w_idx):
    B, S, H, d = q.shape
    topk = min(DSA_TOPK, S // 4)
    scale = HEAD_DIM ** -0.5
    nq = S // Q_BLOCK
    kf = k.reshape(B, S, H * d)
    vf = v.reshape(B, S, H * d)
    kix = k_idx.astype(jnp.float32)
    gather_rows = jax.vmap(lambda rows, ii: rows[ii])

    def one_block(args):
        qb, qib, wib, ci = args
        t = ci * Q_BLOCK + jnp.arange(Q_BLOCK)
        logits = jnp.einsum('bqjd,bsd->bqjs', qib.astype(jnp.float32), kix)
        score = jnp.einsum('bqj,bqjs->bqs', wib.astype(jnp.float32), jax.nn.relu(logits))
        score = jnp.where(jnp.arange(S)[None, :] <= t[:, None], score, NEG)
        _, sel = lax.top_k(score, topk)
        valid = sel <= t[None, :, None]
        ksel = gather_rows(kf, sel).reshape(B, Q_BLOCK, topk, H, d)
        vsel = gather_rows(vf, sel).reshape(B, Q_BLOCK, topk, H, d)
        s = jnp.einsum('bqhd,bqkhd->bhqk', qb, ksel).astype(jnp.float32) * scale
        s = jnp.where(valid[:, None, :, :], s, NEG)
        p = jax.nn.softmax(s, axis=-1).astype(v.dtype)
        return jnp.einsum('bhqk,bqkhd->bqhd', p, vsel)

    out = lax.map(one_block, (_chunk(q, nq), _chunk(q_idx, nq), _chunk(w_idx, nq), jnp.arange(nq)))
    return _unchunk(out)


def stick_breaking_attention(q, k, v):
    B, S, H, d = q.shape
    scale = HEAD_DIM ** -0.5
    nq = S // Q_BLOCK
    kpos = jnp.arange(S)

    def one_block(args):
        qb, ci = args
        t = ci * Q_BLOCK + jnp.arange(Q_BLOCK)
        z = jnp.einsum('bqhd,bshd->bhqs', qb, k).astype(jnp.float32) * scale
        strict = kpos[None, :] < t[:, None]
        log_beta = jax.nn.log_sigmoid(z)
        log_1m = jnp.where(strict, jax.nn.log_sigmoid(-z), 0.0)
        after = lax.cumsum(log_1m, axis=3, reverse=True) - log_1m
        a = jnp.where(strict, jnp.exp(log_beta + after), 0.0).astype(v.dtype)
        return jnp.einsum('bhqs,bshd->bqhd', a, v)

    out = lax.map(one_block, (_chunk(q, nq), jnp.arange(nq)))
    return _unchunk(out)


def setup_inputs(seed: int = 0) -> dict:
    key = jax.random.key(seed)
    ks = jax.random.split(key, 12)
    f32 = jnp.float32
    x = jax.random.normal(ks[0], (BATCH, SEQ, D_MODEL), f32)
    c = jax.random.normal(ks[1], (BATCH, D_MODEL), f32)
    w_ada = jax.random.normal(ks[2], (DEPTH, D_MODEL, N_MOD * D_MODEL), f32) * D_MODEL ** -0.5
    b_ada = 0.01 * jax.random.normal(ks[3], (DEPTH, N_MOD * D_MODEL), f32)
    norm_g = 1.0 + 0.02 * jax.random.normal(ks[4], (DEPTH, 3, D_MODEL), f32)
    w_in = jax.random.normal(ks[5], (DEPTH, D_MODEL, D_IN), f32) * D_MODEL ** -0.5
    qk_g = 1.0 + 0.02 * jax.random.normal(ks[6], (DEPTH, 4, HEAD_DIM), f32)
    out_g = 1.0 + 0.02 * jax.random.normal(ks[7], (DEPTH, D_MIX), f32)
    w_out = jax.random.normal(ks[8], (DEPTH, D_MIX, D_MODEL), f32) * D_MIX ** -0.5
    ffn_w1 = jax.random.normal(ks[9], (DEPTH, 2, D_MODEL, D_FF), f32) * D_MODEL ** -0.5
    ffn_w3 = jax.random.normal(ks[10], (DEPTH, 2, D_MODEL, D_FF), f32) * D_MODEL ** -0.5
    ffn_w2 = jax.random.normal(ks[11], (DEPTH, 2, D_FF, D_MODEL), f32) * D_FF ** -0.5
    return {"x": x, "c": c, "w_ada": w_ada, "b_ada": b_ada, "norm_g": norm_g, "w_in": w_in,
            "qk_g": qk_g, "out_g": out_g, "w_out": w_out, "ffn_w1": ffn_w1,
            "ffn_w3": ffn_w3, "ffn_w2": ffn_w2}


def reference(x, c, w_ada, b_ada, norm_g, w_in, qk_g, out_g, w_out, ffn_w1, ffn_w3, ffn_w2):
    B, S, _ = x.shape
    cos, sin = rope_tables(S)
    split_points = [int(p) for p in np.cumsum(SPLIT_SIZES)[:-1]]
    silu_c = jax.nn.silu(c)
    for layer in range(DEPTH):
        mod = silu_c @ w_ada[layer] + b_ada[layer]
        sh1, sc1, g1, sh2, sc2, g2, sh3, sc3, g3 = [m[:, None, :] for m in jnp.split(mod, N_MOD, axis=-1)]

        h = modulate(rmsnorm(x, norm_g[layer, 0]), sh1, sc1)
        x = x + 0.5 * g1 * swiglu(h, ffn_w1[layer, 0], ffn_w3[layer, 0], ffn_w2[layer, 0])

        h = modulate(rmsnorm(x, norm_g[layer, 1]), sh2, sc2)
        proj = h @ w_in[layer]
        (qa, ka, va, qb, kb, vb, qi, ki, wi, qc, kc, vc) = jnp.split(proj, split_points, axis=-1)
        heads = lambda t, n: t.reshape(B, S, n, HEAD_DIM)
        qa = apply_partial_rope(rmsnorm(heads(qa, N_HEADS_A), qk_g[layer, 0]), cos, sin)
        ka = apply_partial_rope(rmsnorm(heads(ka, N_HEADS_A), qk_g[layer, 1]), cos, sin)
        qb = apply_partial_rope(rmsnorm(heads(qb, N_HEADS_B), qk_g[layer, 2]), cos, sin)
        kb = apply_partial_rope(rmsnorm(heads(kb, N_HEADS_B), qk_g[layer, 3]), cos, sin)
        qi = apply_partial_rope(qi.reshape(B, S, IDX_HEADS, IDX_DIM), cos, sin)
        ki = apply_partial_rope(ki[:, :, None, :], cos, sin)[:, :, 0, :]

        oa = moba_attention(qa, ka, heads(va, N_HEADS_A))
        ob = dsa_attention(qb, kb, heads(vb, N_HEADS_B), qi, ki, wi)
        oc = stick_breaking_attention(heads(qc, N_HEADS_C), heads(kc, N_HEADS_C), heads(vc, N_HEADS_C))

        o = jnp.concatenate([oa, ob, oc], axis=2)
        o = (_rms(o).reshape(B, S, D_MIX) * out_g[layer].astype(jnp.float32)).astype(x.dtype)
        x = x + g2 * (o @ w_out[layer])

        h = modulate(rmsnorm(x, norm_g[layer, 2]), sh3, sc3)
        x = x + 0.5 * g3 * swiglu(h, ffn_w1[layer, 1], ffn_w3[layer, 1], ffn_w2[layer, 1])
    return x
```

```python
import functools

import jax
import jax.numpy as jnp
import numpy as np
from jax import lax
from jax.experimental import pallas as pl
from jax.experimental.pallas import tpu as pltpu

F32 = jnp.float32
BF16 = jnp.bfloat16

HEAD_DIM = 64
N_HEADS_A = 4
N_HEADS_B = 4
N_HEADS_C = 8
ROPE_DIM = HEAD_DIM // 4
ROPE_HALF = ROPE_DIM // 2
ROPE_THETA = 500000.0
MOBA_BLOCK = 256
MOBA_TOPK = 3
DSA_TOPK = 256
IDX_HEADS = 8
IDX_DIM = 64
N_MOD = 9
RMS_EPS = 1e-6
NEG = -1e30
ATT_SCALE = HEAD_DIM ** -0.5

LANES = 128
MXU_N = 256
PAIR = 2 * HEAD_DIM
TQ = 256
TK = 256
VMEM_LIMIT = 56 * 1024 * 1024

SB_DEAD_LOG = -110.0

F_QA, F_QB, F_QC, F_WI, F_WIDTH = 0, 256, 512, 1024, 1152
H_KA, H_VA, H_KB, H_VB, H_KC, H_VC, H_QI, H_KI, H_WIDTH = 0, 256, 512, 768, 1024, 1536, 2048, 2560, 2688
W_PERM_WIDTH = 15 * MXU_N


def _dot(a, b):
    return jnp.dot(a, b, preferred_element_type=F32)


def _dot_nt(a, b):
    return lax.dot_general(a, b, (((1,), (1,)), ((), ())), preferred_element_type=F32)


def _split3(x):
    a = x.astype(BF16)
    r = x - a.astype(F32)
    b = r.astype(BF16)
    c = (r - b.astype(F32)).astype(BF16)
    return a, b, c


def _cparams(sem):
    return pltpu.CompilerParams(dimension_semantics=sem, vmem_limit_bytes=VMEM_LIMIT)


def _mod_kernel(c_ref, w_ref, b_ref, o_ref):
    c = c_ref[...]
    sc = c * (1.0 / (1.0 + jnp.exp(-c)))
    a, b, c3 = _split3(sc)
    w = w_ref[0]
    wa, wb, wc = _split3(w)
    acc = _dot(a, wa) + (_dot(a, wb) + _dot(b, wa)) + (_dot(a, wc) + _dot(b, wb) + _dot(c3, wa))
    o_ref[0] = acc + b_ref[0]


def _mod_call(c_pad, w_ada, b_ada):
    depth, d, n = w_ada.shape
    tn = 1024
    rows = c_pad.shape[0]
    return pl.pallas_call(
        _mod_kernel,
        grid=(depth, n // tn),
        in_specs=[pl.BlockSpec((rows, d), lambda l, j: (0, 0)),
                  pl.BlockSpec((1, d, tn), lambda l, j: (l, 0, j)),
                  pl.BlockSpec((1, 1, tn), lambda l, j: (l, 0, j))],
        out_specs=pl.BlockSpec((1, rows, tn), lambda l, j: (l, 0, j)),
        out_shape=jax.ShapeDtypeStruct((depth, rows, n), F32),
        compiler_params=_cparams(("parallel", "parallel")),
        name="adaln_mod",
    )(c_pad, w_ada, b_ada.reshape(depth, 1, n))


def _norm_modulate(x, ng, sh, sc):
    ms = jnp.mean(x * x, axis=-1, keepdims=True)
    h = x * lax.rsqrt(ms + RMS_EPS) * ng
    return h * (1.0 + sc) + sh


def _ffn_kernel(x_ref, ng_ref, sh_ref, sc_ref, gt_ref, w1_ref, w3_ref, w2_ref, o_ref, h_ref, acc_ref):
    j = pl.program_id(1)

    @pl.when(j == 0)
    def _():
        h = _norm_modulate(x_ref[...], ng_ref[...], sh_ref[0], sc_ref[0])
        h_ref[...] = h.astype(BF16)
        acc_ref[...] = jnp.zeros_like(acc_ref)

    h = h_ref[...]
    a = _dot(h, w1_ref[...])
    b = _dot(h, w3_ref[...])
    u = (a * (1.0 / (1.0 + jnp.exp(-a))) * b).astype(BF16)
    acc_ref[...] += _dot(u, w2_ref[...])

    @pl.when(j == pl.num_programs(1) - 1)
    def _():
        o_ref[...] = x_ref[...] + 0.5 * gt_ref[0] * acc_ref[...]


def _ffn_call(x2, ng, sh, sc, gt, w1, w3, w2, seq):
    m, d = x2.shape
    dff = w1.shape[1]
    tm, tf = 1024, 256
    bidx = lambda i, j: ((i * tm) // seq, 0, 0)
    return pl.pallas_call(
        _ffn_kernel,
        grid=(m // tm, dff // tf),
        in_specs=[pl.BlockSpec((tm, d), lambda i, j: (i, 0)),
                  pl.BlockSpec((1, d), lambda i, j: (0, 0)),
                  pl.BlockSpec((1, 1, d), bidx),
                  pl.BlockSpec((1, 1, d), bidx),
                  pl.BlockSpec((1, 1, d), bidx),
                  pl.BlockSpec((d, tf), lambda i, j: (0, j)),
                  pl.BlockSpec((d, tf), lambda i, j: (0, j)),
                  pl.BlockSpec((tf, d), lambda i, j: (j, 0))],
        out_specs=pl.BlockSpec((tm, d), lambda i, j: (i, 0)),
        out_shape=jax.ShapeDtypeStruct((m, d), F32),
        scratch_shapes=[pltpu.VMEM((tm, d), BF16), pltpu.VMEM((tm, d), F32)],
        compiler_params=_cparams(("parallel", "arbitrary")),
        name="swiglu_ffn",
    )(x2, ng, sh, sc, gt, w1, w3, w2)


def _proj_kernel(x_ref, ng_ref, sh_ref, sc_ref, w_ref, qkg_ref, cos_ref, s1_ref, s2_ref, bd_ref,
                 f_ref, h_ref, km_ref, hs_ref):
    hs_ref[...] = _norm_modulate(x_ref[...], ng_ref[...], sh_ref[0], sc_ref[0]).astype(BF16)
    cos, s1, s2 = cos_ref[...], s1_ref[...], s2_ref[...]
    bd = bd_ref[...]
    tm = x_ref.shape[0]

    def rope(v):
        return v * cos + pltpu.roll(v, LANES - ROPE_HALF, 1) * s1 + pltpu.roll(v, ROPE_HALF, 1) * s2

    def headnorm(v, g):
        a, b, c = _split3(v * v)
        ss = _dot(a, bd) + _dot(b, bd) + _dot(c, bd)
        return v * lax.rsqrt(ss * (1.0 / HEAD_DIM) + RMS_EPS) * g

    def chunk(c):
        y = _dot(hs_ref[...], w_ref[:, c * MXU_N:(c + 1) * MXU_N])
        return y[:, :LANES], y[:, LANES:]

    def normrope_chunk(c, grow):
        g = qkg_ref[grow:grow + 1, :]
        return [rope(headnorm(v, g)) for v in chunk(c)]

    def put(ref, off, halves, dtype):
        for k, v in enumerate(halves):
            ref[:, off + k * LANES: off + (k + 1) * LANES] = v.astype(dtype)

    put(f_ref, F_QA, normrope_chunk(0, 0), F32)
    put(f_ref, F_QB, normrope_chunk(1, 2), F32)
    put(f_ref, F_QC, chunk(2), F32)
    put(f_ref, F_QC + MXU_N, chunk(3), F32)

    ka = normrope_chunk(4, 1)
    put(h_ref, H_KA, ka, BF16)
    rows = lax.broadcasted_iota(jnp.int32, (8, LANES), 0)
    for k, v in enumerate(ka):
        km = jnp.zeros((8, LANES), F32)
        for r in range(tm // MOBA_BLOCK):
            s = jnp.sum(v[r * MOBA_BLOCK:(r + 1) * MOBA_BLOCK, :], axis=0, keepdims=True) * (1.0 / MOBA_BLOCK)
            km = jnp.where(rows == r, s, km)
        km_ref[0, :, k * LANES:(k + 1) * LANES] = km

    put(h_ref, H_VA, chunk(5), BF16)
    put(h_ref, H_KB, normrope_chunk(6, 3), BF16)
    put(h_ref, H_VB, chunk(7), BF16)
    put(h_ref, H_KC, chunk(8), BF16)
    put(h_ref, H_KC + MXU_N, chunk(9), BF16)
    put(h_ref, H_VC, chunk(10), BF16)
    put(h_ref, H_VC + MXU_N, chunk(11), BF16)
    put(h_ref, H_QI, [rope(v) for v in chunk(12)], BF16)
    put(h_ref, H_QI + MXU_N, [rope(v) for v in chunk(13)], BF16)
    ki, wi = chunk(14)
    h_ref[:, H_KI:H_KI + LANES] = rope(ki).astype(BF16)
    f_ref[:, F_WI:F_WI + LANES] = wi


def _proj_call(x2, ng, sh, sc, w_perm, qkg, cos_t, s1_t, s2_t, bd, seq):
    m, d = x2.shape
    tm = 512
    nt = seq // tm
    bidx = lambda i: ((i * tm) // seq, 0, 0)
    tab = pl.BlockSpec((tm, LANES), lambda i: (i % nt, 0))
    return pl.pallas_call(
        _proj_kernel,
        grid=(m // tm,),
        in_specs=[pl.BlockSpec((tm, d), lambda i: (i, 0)),
                  pl.BlockSpec((1, d), lambda i: (0, 0)),
                  pl.BlockSpec((1, 1, d), bidx),
                  pl.BlockSpec((1, 1, d), bidx),
                  pl.BlockSpec((d, W_PERM_WIDTH), lambda i: (0, 0)),
                  pl.BlockSpec((8, LANES), lambda i: (0, 0)),
                  tab, tab, tab,
                  pl.BlockSpec((LANES, LANES), lambda i: (0, 0))],
        out_specs=[pl.BlockSpec((tm, F_WIDTH), lambda i: (i, 0)),
                   pl.BlockSpec((tm, H_WIDTH), lambda i: (i, 0)),
                   pl.BlockSpec((1, 8, 2 * LANES), lambda i: (i, 0, 0))],
        out_shape=[jax.ShapeDtypeStruct((m, F_WIDTH), F32),
                   jax.ShapeDtypeStruct((m, H_WIDTH), BF16),
                   jax.ShapeDtypeStruct((m // tm, 8, 2 * LANES), F32)],
        scratch_shapes=[pltpu.VMEM((tm, d), BF16)],
        compiler_params=_cparams(("parallel",)),
        name="in_proj",
    )(x2, ng, sh, sc, w_perm, qkg, cos_t, s1_t, s2_t, bd)


def _head_masks():
    lane = lax.broadcasted_iota(jnp.int32, (1, LANES), 1)
    return [(lane >= HEAD_DIM * h) & (lane < HEAD_DIM * (h + 1)) for h in range(2)]


def _head_rms(o, hm):
    ss = jnp.sum(jnp.where(hm, o * o, 0.0), axis=1, keepdims=True) * (1.0 / HEAD_DIM)
    return o * lax.rsqrt(ss + RMS_EPS)


def _moba_kernel(q_ref, km_ref, k_ref, v_ref, o_ref):
    i = pl.program_id(2)
    nb = km_ref.shape[1]
    q = q_ref[...]
    km = km_ref[0]
    kma, kmb, kmc = _split3(km)
    blk = lax.broadcasted_iota(jnp.int32, (1, nb), 1).astype(F32)
    i_f = i.astype(F32)
    row = lax.broadcasted_iota(jnp.int32, (TQ, TK), 0)
    col = lax.broadcasted_iota(jnp.int32, (TQ, TK), 1)
    causal = col <= row
    out = jnp.zeros((TQ, LANES), F32)
    for hm in _head_masks():
        qh = jnp.where(hm, q, 0.0)
        qa, qb, qc = _split3(qh)
        gate = (_dot_nt(qa, kma) + (_dot_nt(qa, kmb) + _dot_nt(qb, kma))
                + (_dot_nt(qa, kmc) + _dot_nt(qb, kmb) + _dot_nt(qc, kma)))
        gate = jnp.where(blk < i_f, gate, NEG)
        sel = jnp.zeros((TQ, nb), F32)
        for _ in range(MOBA_TOPK):
            mx = jnp.max(gate, axis=1, keepdims=True)
            idx = jnp.min(jnp.where(gate == mx, blk, float(nb)), axis=1, keepdims=True)
            pick = blk == idx
            sel = jnp.where(pick, 1.0, sel)
            gate = jnp.where(pick, -jnp.inf, gate)
        sel = jnp.where(blk < i_f, sel, 0.0)

        qs = (qh * ATT_SCALE).astype(BF16)
        own = pl.multiple_of(i * TK, TK)
        s = jnp.where(causal, _dot_nt(qs, k_ref[pl.ds(own, TK), :]), NEG)
        m0 = jnp.max(s, axis=1, keepdims=True)
        p = jnp.exp(s - m0)
        l0 = jnp.sum(p, axis=1, keepdims=True)
        acc0 = _dot(p.astype(BF16), v_ref[pl.ds(own, TK), :])

        def body(j, carry):
            m, l, acc = carry
            off = pl.multiple_of(j * TK, TK)
            sj = jnp.sum(jnp.where(blk == j.astype(F32), sel, 0.0), axis=1, keepdims=True) > 0.5
            s = jnp.where(sj, _dot_nt(qs, k_ref[pl.ds(off, TK), :]), NEG)
            m_new = jnp.maximum(m, jnp.max(s, axis=1, keepdims=True))
            alpha = jnp.exp(m - m_new)
            p = jnp.where(sj, jnp.exp(s - m_new), 0.0)
            l = alpha * l + jnp.sum(p, axis=1, keepdims=True)
            acc = alpha * acc + _dot(p.astype(BF16), v_ref[pl.ds(off, TK), :])
            return m_new, l, acc

        _, l, acc = lax.fori_loop(0, i, body, (m0, l0, acc0))
        out = jnp.where(hm, _head_rms(acc / l, hm), out)
    o_ref[...] = out


def _moba_call(f_arr, h_arr, kmean, batch, seq):
    nq = seq // TQ
    nb = seq // MOBA_BLOCK
    npair = N_HEADS_A // 2
    return pl.pallas_call(
        _moba_kernel,
        grid=(batch, npair, nq),
        in_specs=[pl.BlockSpec((TQ, LANES), lambda b, p, i: (b * nq + i, F_QA // LANES + p)),
                  pl.BlockSpec((1, nb, LANES), lambda b, p, i: (b, 0, p)),
                  pl.BlockSpec((seq, LANES), lambda b, p, i: (b, H_KA // LANES + p)),
                  pl.BlockSpec((seq, LANES), lambda b, p, i: (b, H_VA // LANES + p))],
        out_specs=pl.BlockSpec((TQ, LANES), lambda b, p, i: (b * nq + i, p)),
        out_shape=jax.ShapeDtypeStruct((batch * seq, N_HEADS_A * HEAD_DIM), F32),
        compiler_params=_cparams(("parallel", "parallel", "arbitrary")),
        name="moba_attn",
    )(f_arr, kmean, h_arr, h_arr)


def _sb_kernel(q_ref, k_ref, v_ref, u_ref, o_ref, r_ref, acc_ref):
    i = pl.program_id(2)
    q = q_ref[...]
    u = u_ref[...]
    row = lax.broadcasted_iota(jnp.int32, (TQ, TK), 0)
    col = lax.broadcasted_iota(jnp.int32, (TQ, TK), 1)
    strict = col < row
    out = jnp.zeros((TQ, LANES), F32)
    for hm in _head_masks():
        qs = (jnp.where(hm, q, 0.0) * ATT_SCALE).astype(BF16)
        r_ref[...] = jnp.zeros_like(r_ref)
        acc_ref[...] = jnp.zeros_like(acc_ref)

        def block(j, diag):
            off = pl.multiple_of(j * TK, TK)
            z = _dot_nt(qs, k_ref[pl.ds(off, TK), :])
            log_beta = jnp.minimum(z, 0.0) - jnp.log(1.0 + jnp.exp(-jnp.abs(z)))
            log_1m = log_beta - z
            if diag:
                log_1m = jnp.where(strict, log_1m, 0.0)
            hi = log_1m.astype(BF16)
            lo = (log_1m - hi.astype(F32)).astype(BF16)
            r = r_ref[...]
            after = _dot(hi, u) + _dot(lo, u) + r
            a = jnp.exp(log_beta + after)
            if diag:
                a = jnp.where(strict, a, 0.0)
            acc_ref[...] += _dot(a.astype(BF16), v_ref[pl.ds(off, TK), :])
            r_new = r + jnp.sum(log_1m, axis=1, keepdims=True)
            r_ref[...] = r_new
            return jnp.max(r_new)

        rmax = block(i, True)

        def cond(c):
            return (c[0] >= 0) & (c[1] > SB_DEAD_LOG)

        def body(c):
            return c[0] - 1, block(c[0], False)

        lax.while_loop(cond, body, (i - 1, rmax))
        out = jnp.where(hm, _head_rms(acc_ref[...], hm), out)
    o_ref[...] = out


def _sb_call(f_arr, h_arr, u_mat, batch, seq):
    nq = seq // TQ
    npair = N_HEADS_C // 2
    return pl.pallas_call(
        _sb_kernel,
        grid=(batch, npair, nq),
        in_specs=[pl.BlockSpec((TQ, LANES), lambda b, p, i: (b * nq + i, F_QC // LANES + p)),
                  pl.BlockSpec((seq, LANES), lambda b, p, i: (b, H_KC // LANES + p)),
                  pl.BlockSpec((seq, LANES), lambda b, p, i: (b, H_VC // LANES + p)),
                  pl.BlockSpec((TK, TK), lambda b, p, i: (0, 0))],
        out_specs=pl.BlockSpec((TQ, LANES), lambda b, p, i: (b * nq + i, p)),
        out_shape=jax.ShapeDtypeStruct((batch * seq, N_HEADS_C * HEAD_DIM), F32),
        scratch_shapes=[pltpu.VMEM((TQ, 1), F32), pltpu.VMEM((TQ, LANES), F32)],
        compiler_params=_cparams(("parallel", "parallel", "arbitrary")),
        name="stickbreak_attn",
    )(f_arr, h_arr, h_arr, u_mat)


def _sortable_key(x):
    bits = lax.bitcast_convert_type(x, jnp.int32)
    return jnp.where(bits < 0, bits ^ jnp.int32(0x7FFFFFFF), bits)


_NEG_BITS = int(np.array(NEG, np.float32).view(np.int32))
NEG_KEY = _NEG_BITS ^ 0x7FFFFFFF
INT_MIN = -2 ** 31


def _dsa_kernel(qi_ref, wi_ref, ki_ref, q_ref, k_ref, v_ref, o_ref,
                key_ref, t_ref, x_ref, m_ref, l_ref, acc_ref):
    i = pl.program_id(1)
    seq = k_ref.shape[0]
    topk = min(DSA_TOPK, seq // 4)
    lane = lax.broadcasted_iota(jnp.int32, (1, LANES), 1)
    low = lane < HEAD_DIM
    row = lax.broadcasted_iota(jnp.int32, (TQ, TK), 0)
    col = lax.broadcasted_iota(jnp.int32, (TQ, TK), 1)
    diag_causal = col <= row

    qi = qi_ref[...].astype(F32)
    wi = wi_ref[...]
    qh, wh = [], []
    for h in range(IDX_HEADS):
        qp = qi[:, (h // 2) * LANES:(h // 2 + 1) * LANES]
        if h % 2:
            qp = pltpu.roll(qp, HEAD_DIM, 1)
        qh.append(jnp.where(low, qp, 0.0).astype(BF16))
        wh.append(wi[:, h:h + 1])

    def score_block(j, diag):
        off = pl.multiple_of(j * TK, TK)
        kz = ki_ref[pl.ds(off, TK), :]
        sc = jnp.zeros((TQ, TK), F32)
        for h in range(IDX_HEADS):
            sc = sc + wh[h] * jnp.maximum(_dot_nt(qh[h], kz), 0.0)
        if diag:
            sc = jnp.where(diag_causal, sc, NEG)
        key_ref[j] = _sortable_key(sc)

    def score_body(j, c):
        score_block(j, False)
        return c

    lax.fori_loop(0, i, score_body, 0)
    score_block(i, True)

    n_unscanned = (seq - (i + 1) * TK).astype(F32)

    def count(pred):
        def body(j, acc):
            hit = jnp.where(pred(key_ref[j], j), 1.0, 0.0)
            return acc + hit[:, :LANES] + hit[:, LANES:]
        acc = lax.fori_loop(0, i + 1, body, jnp.zeros((TQ, LANES), F32))
        return jnp.sum(acc, axis=1, keepdims=True)

    def count_ge(cand):
        return count(lambda key, j: key >= cand) + jnp.where(cand <= NEG_KEY, n_unscanned, 0.0)

    kf = float(topk)
    t0 = jnp.where(count_ge(jnp.zeros((TQ, 1), jnp.int32)) >= kf, 0, INT_MIN).astype(jnp.int32)

    def bit_body(b, t):
        cand = t | jnp.left_shift(jnp.int32(1), 30 - b)
        return jnp.where(count_ge(cand) >= kf, cand, t)

    t = lax.fori_loop(0, 31, bit_body, t0)
    t_ref[...] = t
    c_ge = count_ge(t)
    need = kf - count_ge(t + 1)
    x_ref[...] = jnp.full((TQ, 1), 2 * seq, jnp.int32)

    @pl.when(jnp.max(c_ge) > kf)
    def _():
        def ties_before(xc):
            return count(lambda key, j: (key == t) & (j * TK + col < xc))

        def xbit_body(b, x):
            cand = x | jnp.left_shift(jnp.int32(1), 14 - b)
            return jnp.where(ties_before(cand) < need, cand, x)

        x_ref[...] = lax.fori_loop(0, 15, xbit_body, jnp.zeros((TQ, 1), jnp.int32))

    q = q_ref[...]
    hms = _head_masks()
    nh = N_HEADS_B
    qs = [(jnp.where(hms[h % 2], q[:, (h // 2) * LANES:(h // 2 + 1) * LANES], 0.0) * ATT_SCALE).astype(BF16)
          for h in range(nh)]
    m_ref[...] = jnp.full(m_ref.shape, NEG, F32)
    l_ref[...] = jnp.zeros_like(l_ref)
    acc_ref[...] = jnp.zeros_like(acc_ref)

    def attend(j, diag):
        off = pl.multiple_of(j * TK, TK)
        key = key_ref[j]
        tt = t_ref[...]
        msk = (key > tt) | ((key == tt) & (j * TK + col <= x_ref[...]))
        if diag:
            msk = msk & diag_causal
        kk = k_ref[pl.ds(off, TK), :]
        vv = v_ref[pl.ds(off, TK), :]
        for h in range(nh):
            sl = slice((h // 2) * LANES, (h // 2 + 1) * LANES)
            s = jnp.where(msk, _dot_nt(qs[h], kk[:, sl]), NEG)
            m_old = m_ref[h]
            m_new = jnp.maximum(m_old, jnp.max(s, axis=1, keepdims=True))
            alpha = jnp.exp(m_old - m_new)
            p = jnp.where(msk, jnp.exp(s - m_new), 0.0)
            l_ref[h] = alpha * l_ref[h] + jnp.sum(p, axis=1, keepdims=True)
            acc_ref[h] = alpha * acc_ref[h] + _dot(p.astype(BF16), vv[:, sl])
            m_ref[h] = m_new

    def attend_body(j, c):
        attend(j, False)
        return c

    lax.fori_loop(0, i, attend_body, 0)
    attend(i, True)

    for pr in range(nh // 2):
        out = jnp.zeros((TQ, LANES), F32)
        for hh in range(2):
            h = 2 * pr + hh
            out = jnp.where(hms[hh], _head_rms(acc_ref[h] / l_ref[h], hms[hh]), out)
        o_ref[:, pr * LANES:(pr + 1) * LANES] = out


def _dsa_call(f_arr, h_arr, batch, seq):
    nq = seq // TQ
    wb = N_HEADS_B * HEAD_DIM
    return pl.pallas_call(
        _dsa_kernel,
        grid=(batch, nq),
        in_specs=[pl.BlockSpec((TQ, IDX_HEADS * IDX_DIM), lambda b, i: (b * nq + i, H_QI // (IDX_HEADS * IDX_DIM))),
                  pl.BlockSpec((TQ, LANES), lambda b, i: (b * nq + i, F_WI // LANES)),
                  pl.BlockSpec((seq, LANES), lambda b, i: (b, H_KI // LANES)),
                  pl.BlockSpec((TQ, wb), lambda b, i: (b * nq + i, F_QB // wb)),
                  pl.BlockSpec((seq, wb), lambda b, i: (b, H_KB // wb)),
                  pl.BlockSpec((seq, wb), lambda b, i: (b, H_VB // wb))],
        out_specs=pl.BlockSpec((TQ, wb), lambda b, i: (b * nq + i, 0)),
        out_shape=jax.ShapeDtypeStruct((batch * seq, wb), F32),
        scratch_shapes=[pltpu.VMEM((seq // TK, TQ, TK), jnp.int32),
                        pltpu.VMEM((TQ, 1), jnp.int32),
                        pltpu.VMEM((TQ, 1), jnp.int32),
                        pltpu.VMEM((N_HEADS_B, TQ, 1), F32),
                        pltpu.VMEM((N_HEADS_B, TQ, 1), F32),
                        pltpu.VMEM((N_HEADS_B, TQ, LANES), F32)],
        compiler_params=_cparams(("parallel", "arbitrary")),
        name="dsa_attn",
    )(h_arr, f_arr, h_arr, f_arr, h_arr, h_arr)


def _outproj_kernel(x_ref, oa_ref, ob_ref, oc_ref, og_ref, gt_ref, w_ref, o_ref):
    wa = N_HEADS_A * HEAD_DIM
    wb = wa + N_HEADS_B * HEAD_DIM
    og = og_ref[...]
    y = (_dot((oa_ref[...] * og[:, :wa]).astype(BF16), w_ref[:wa, :])
         + _dot((ob_ref[...] * og[:, wa:wb]).astype(BF16), w_ref[wa:wb, :])
         + _dot((oc_ref[...] * og[:, wb:]).astype(BF16), w_ref[wb:, :]))
    o_ref[...] = x_ref[...] + gt_ref[0] * y


def _outproj_call(x2, oa, ob, oc, og, gt, w_out, seq):
    m, d = x2.shape
    tm = 512
    row = lambda i: (i, 0)
    return pl.pallas_call(
        _outproj_kernel,
        grid=(m // tm,),
        in_specs=[pl.BlockSpec((tm, d), row),
                  pl.BlockSpec((tm, oa.shape[1]), row),
                  pl.BlockSpec((tm, ob.shape[1]), row),
                  pl.BlockSpec((tm, oc.shape[1]), row),
                  pl.BlockSpec((1, d), lambda i: (0, 0)),
                  pl.BlockSpec((1, 1, d), lambda i: ((i * tm) // seq, 0, 0)),
                  pl.BlockSpec((d, d), lambda i: (0, 0))],
        out_specs=pl.BlockSpec((tm, d), row),
        out_shape=jax.ShapeDtypeStruct((m, d), F32),
        compiler_params=_cparams(("parallel",)),
        name="out_proj",
    )(x2, oa, ob, oc, og, gt, w_out)


def _rope_tables(seq):
    pos = jnp.arange(seq, dtype=F32)
    inv = ROPE_THETA ** (-jnp.arange(0, ROPE_DIM, 2, dtype=F32) / ROPE_DIM)
    ang = pos[:, None] * inv[None, :]
    cos, sin = jnp.cos(ang), jnp.sin(ang)
    zeros = jnp.zeros((seq, HEAD_DIM - ROPE_DIM), F32)
    zh = jnp.zeros((seq, ROPE_HALF), F32)
    cos_h = jnp.concatenate([cos, cos, jnp.ones_like(zeros)], axis=1)
    s1_h = jnp.concatenate([-sin, zh, zeros], axis=1)
    s2_h = jnp.concatenate([zh, sin, zeros], axis=1)
    two = lambda t: jnp.concatenate([t, t], axis=1)
    return two(cos_h), two(s1_h), two(s2_h)


def _permute_w_in(w):
    d = w.shape[0]
    a, b, c = N_HEADS_A * HEAD_DIM, N_HEADS_B * HEAD_DIM, N_HEADS_C * HEAD_DIM
    sizes = (a, a, a, b, b, b, IDX_HEADS * IDX_DIM, IDX_DIM, IDX_HEADS, c, c, c)
    offs = np.concatenate([[0], np.cumsum(sizes)])
    qa, ka, va, qb, kb, vb, qi, ki, wi, qc, kc, vc = [w[:, int(offs[k]):int(offs[k + 1])] for k in range(12)]
    z = lambda n: jnp.zeros((d, n), w.dtype)
    out = jnp.concatenate([qa, qb, qc, ka, va, kb, vb, kc, vc, qi,
                           ki, z(LANES - IDX_DIM), wi, z(LANES - IDX_HEADS)], axis=1)
    assert out.shape[1] == W_PERM_WIDTH
    return out.astype(BF16)


def kernel(x, c, w_ada, b_ada, norm_g, w_in, qk_g, out_g, w_out, ffn_w1, ffn_w3, ffn_w2):
    batch, seq, d = x.shape
    depth = w_ada.shape[0]
    assert seq % 512 == 0 and d % LANES == 0 and batch <= 8

    c_pad = jnp.zeros((8, d), F32).at[:batch].set(c)
    mod = _mod_call(c_pad, w_ada, b_ada)
    cos_t, s1_t, s2_t = _rope_tables(seq)
    hd = np.arange(LANES) // HEAD_DIM
    bd = jnp.asarray(hd[:, None] == hd[None, :], BF16)
    kk = np.arange(TK)
    u_mat = jnp.asarray(kk[:, None] > kk[None, :], BF16)

    x2 = x.reshape(batch * seq, d)
    for layer in range(depth):
        mods = [mod[layer, :batch, k * d:(k + 1) * d].reshape(batch, 1, d) for k in range(N_MOD)]
        sh1, sc1, g1, sh2, sc2, g2, sh3, sc3, g3 = mods
        ng = norm_g[layer]
        w1 = ffn_w1[layer].astype(BF16)
        w3 = ffn_w3[layer].astype(BF16)
        w2 = ffn_w2[layer].astype(BF16)

        x2 = _ffn_call(x2, ng[0:1], sh1, sc1, g1, w1[0], w3[0], w2[0], seq)

        qkg = jnp.concatenate([jnp.tile(qk_g[layer], (1, 2)), jnp.ones((4, LANES), F32)], axis=0)
        f_arr, h_arr, kmean = _proj_call(x2, ng[1:2], sh2, sc2, _permute_w_in(w_in[layer]), qkg,
                                         cos_t, s1_t, s2_t, bd, seq)
        per_tile = 512 // MOBA_BLOCK
        kmean = kmean[:, :per_tile, :].reshape(batch, seq // MOBA_BLOCK, N_HEADS_A * HEAD_DIM)
        oa = _moba_call(f_arr, h_arr, kmean, batch, seq)
        ob = _dsa_call(f_arr, h_arr, batch, seq)
        oc = _sb_call(f_arr, h_arr, u_mat, batch, seq)
        x2 = _outproj_call(x2, oa, ob, oc, out_g[layer].reshape(1, d), g2, w_out[layer].astype(BF16), seq)

        x2 = _ffn_call(x2, ng[2:3], sh3, sc3, g3, w1[1], w3[1], w2[1], seq)
    return x2.reshape(batch, seq, d)
```

```python
import jax
import jax.numpy as jnp
import numpy as np
from jax import lax
from jax.experimental import pallas as pl
from jax.experimental.pallas import tpu as pltpu

F32 = jnp.float32
BF16 = jnp.bfloat16

HEAD_DIM = 64
N_HEADS_A = 4
N_HEADS_B = 4
N_HEADS_C = 8
ROPE_DIM = HEAD_DIM // 4
ROPE_HALF = ROPE_DIM // 2
ROPE_THETA = 500000.0
MOBA_BLOCK = 256
MOBA_TOPK = 3
DSA_TOPK = 256
IDX_HEADS = 8
IDX_DIM = 64
N_MOD = 9
RMS_EPS = 1e-6
NEG = -1e30
ATT_SCALE = HEAD_DIM ** -0.5

LANES = 128
MXU_N = 256
TQ = 256
TK = 256
PROJ_TM = 512
VMEM_LIMIT = 56 * 1024 * 1024

SB_DEAD_LOG = -110.0
SB_TAIL = 16
SB_HEADS = 4
MOBA_GROUP = 4
DSA_GROUP = 2

F_QA, F_QB, F_QC, F_WI, F_WIDTH = 0, 256, 512, 1024, 1152
H_KA, H_KB, H_KC, H_QI, H_KI, H_WIDTH = 0, 256, 512, 1024, 1536, 1664
W_PERM_WIDTH = 15 * MXU_N
VT_A, VT_B, VT_C, VT_ROWS = 0, 512, 1024, 1536


def _dot(a, b):
    return jnp.dot(a, b, preferred_element_type=F32)


def _dot_nt(a, b):
    return lax.dot_general(a, b, (((1,), (1,)), ((), ())), preferred_element_type=F32)


def _split3(x):
    a = x.astype(BF16)
    r = x - a.astype(F32)
    b = r.astype(BF16)
    c = (r - b.astype(F32)).astype(BF16)
    return a, b, c


def _cparams(sem):
    return pltpu.CompilerParams(dimension_semantics=sem, vmem_limit_bytes=VMEM_LIMIT)


def _mod_kernel(c_ref, w_ref, b_ref, o_ref):
    c = c_ref[...]
    sc = c * (1.0 / (1.0 + jnp.exp(-c)))
    a, b, c3 = _split3(sc)
    w = w_ref[0]
    wa, wb, wc = _split3(w)
    acc = _dot(a, wa) + (_dot(a, wb) + _dot(b, wa)) + (_dot(a, wc) + _dot(b, wb) + _dot(c3, wa))
    o_ref[0] = acc + b_ref[0]


def _mod_call(c_pad, w_ada, b_ada):
    depth, d, n = w_ada.shape
    tn = 1024
    rows = c_pad.shape[0]
    return pl.pallas_call(
        _mod_kernel,
        grid=(depth, n // tn),
        in_specs=[pl.BlockSpec((rows, d), lambda l, j: (0, 0)),
                  pl.BlockSpec((1, d, tn), lambda l, j: (l, 0, j)),
                  pl.BlockSpec((1, 1, tn), lambda l, j: (l, 0, j))],
        out_specs=pl.BlockSpec((1, rows, tn), lambda l, j: (l, 0, j)),
        out_shape=jax.ShapeDtypeStruct((depth, rows, n), F32),
        compiler_params=_cparams(("parallel", "parallel")),
        name="adaln_mod",
    )(c_pad, w_ada, b_ada.reshape(depth, 1, n))


def _norm_modulate(x, ng, sh, sc):
    ms = jnp.mean(x * x, axis=-1, keepdims=True)
    h = x * lax.rsqrt(ms + RMS_EPS) * ng
    return h * (1.0 + sc) + sh


def _ffn_kernel(x_ref, ng_ref, sh_ref, sc_ref, gt_ref, w1_ref, w3_ref, w2_ref, o_ref, h_ref, acc_ref):
    j = pl.program_id(1)

    @pl.when(j == 0)
    def _():
        h = _norm_modulate(x_ref[...], ng_ref[...], sh_ref[0], sc_ref[0])
        h_ref[...] = h.astype(BF16)
        acc_ref[...] = jnp.zeros_like(acc_ref)

    h = h_ref[...]
    a = _dot(h, w1_ref[...])
    b = _dot(h, w3_ref[...])
    u = (a * (1.0 / (1.0 + jnp.exp(-a))) * b).astype(BF16)
    acc_ref[...] += _dot(u, w2_ref[...])

    @pl.when(j == pl.num_programs(1) - 1)
    def _():
        o_ref[...] = x_ref[...] + 0.5 * gt_ref[0] * acc_ref[...]


def _ffn_call(x2, ng, sh, sc, gt, w1, w3, w2, seq):
    m, d = x2.shape
    dff = w1.shape[1]
    tm, tf = 1024, 256
    bidx = lambda i, j: ((i * tm) // seq, 0, 0)
    return pl.pallas_call(
        _ffn_kernel,
        grid=(m // tm, dff // tf),
        in_specs=[pl.BlockSpec((tm, d), lambda i, j: (i, 0)),
                  pl.BlockSpec((1, d), lambda i, j: (0, 0)),
                  pl.BlockSpec((1, 1, d), bidx),
                  pl.BlockSpec((1, 1, d), bidx),
                  pl.BlockSpec((1, 1, d), bidx),
                  pl.BlockSpec((d, tf), lambda i, j: (0, j)),
                  pl.BlockSpec((d, tf), lambda i, j: (0, j)),
                  pl.BlockSpec((tf, d), lambda i, j: (j, 0))],
        out_specs=pl.BlockSpec((tm, d), lambda i, j: (i, 0)),
        out_shape=jax.ShapeDtypeStruct((m, d), F32),
        scratch_shapes=[pltpu.VMEM((tm, d), BF16), pltpu.VMEM((tm, d), F32)],
        compiler_params=_cparams(("parallel", "arbitrary")),
        name="swiglu_ffn",
    )(x2, ng, sh, sc, gt, w1, w3, w2)


def _proj_kernel(x_ref, ng_ref, sh_ref, sc_ref, w_ref, qkg_ref, cos_ref, s1_ref, s2_ref, bd_ref,
                 f_ref, h_ref, km_ref, vt_ref, hs_ref):
    hs_ref[...] = _norm_modulate(x_ref[...], ng_ref[...], sh_ref[0], sc_ref[0]).astype(BF16)
    cos, s1, s2 = cos_ref[...], s1_ref[...], s2_ref[...]
    bd = bd_ref[...]
    tm = x_ref.shape[0]

    def rope(v):
        return v * cos + pltpu.roll(v, LANES - ROPE_HALF, 1) * s1 + pltpu.roll(v, ROPE_HALF, 1) * s2

    def headnorm(v, g):
        a, b, c = _split3(v * v)
        ss = _dot(a, bd) + _dot(b, bd) + _dot(c, bd)
        return v * lax.rsqrt(ss * (1.0 / HEAD_DIM) + RMS_EPS) * g

    def chunk(c):
        y = _dot(hs_ref[...], w_ref[:, c * MXU_N:(c + 1) * MXU_N])
        return y[:, :LANES], y[:, LANES:]

    def normrope_chunk(c, grow):
        g = qkg_ref[grow:grow + 1, :]
        return [rope(headnorm(v, g)) for v in chunk(c)]

    def put(ref, off, halves, dtype):
        for k, v in enumerate(halves):
            ref[:, off + k * LANES: off + (k + 1) * LANES] = v.astype(dtype)

    def put_vt(base, halves, with_ones):
        ones = jnp.ones((HEAD_DIM, TK), BF16)
        for r in range(tm // TK):
            for k, v in enumerate(halves):
                t = v[r * TK:(r + 1) * TK, :].T.astype(BF16)
                if with_ones:
                    for hh in range(2):
                        row = base + (2 * k + hh) * LANES
                        vt_ref[0, r, row:row + HEAD_DIM, :] = t[hh * HEAD_DIM:(hh + 1) * HEAD_DIM, :]
                        vt_ref[0, r, row + HEAD_DIM:row + LANES, :] = ones
                else:
                    vt_ref[0, r, base + k * LANES:base + (k + 1) * LANES, :] = t

    put(f_ref, F_QA, normrope_chunk(0, 0), F32)
    put(f_ref, F_QB, normrope_chunk(1, 2), F32)
    put(f_ref, F_QC, chunk(2), F32)
    put(f_ref, F_QC + MXU_N, chunk(3), F32)

    ka = normrope_chunk(4, 1)
    put(h_ref, H_KA, ka, BF16)
    rows = lax.broadcasted_iota(jnp.int32, (8, LANES), 0)
    for k, v in enumerate(ka):
        km = jnp.zeros((8, LANES), F32)
        for r in range(tm // MOBA_BLOCK):
            s = jnp.sum(v[r * MOBA_BLOCK:(r + 1) * MOBA_BLOCK, :], axis=0, keepdims=True) * (1.0 / MOBA_BLOCK)
            km = jnp.where(rows == r, s, km)
        km_ref[0, :, k * LANES:(k + 1) * LANES] = km

    put_vt(VT_A, chunk(5), True)
    put(h_ref, H_KB, normrope_chunk(6, 3), BF16)
    put_vt(VT_B, chunk(7), True)
    put(h_ref, H_KC, chunk(8), BF16)
    put(h_ref, H_KC + MXU_N, chunk(9), BF16)
    put_vt(VT_C, chunk(10), False)
    put_vt(VT_C + MXU_N, chunk(11), False)
    put(h_ref, H_QI, [rope(v) for v in chunk(12)], BF16)
    put(h_ref, H_QI + MXU_N, [rope(v) for v in chunk(13)], BF16)
    ki, wi = chunk(14)
    h_ref[:, H_KI:H_KI + LANES] = rope(ki).astype(BF16)
    f_ref[:, F_WI:F_WI + LANES] = wi


def _proj_call(x2, ng, sh, sc, w_perm, qkg, cos_t, s1_t, s2_t, bd, batch, seq):
    m, d = x2.shape
    tm = PROJ_TM
    nt = seq // tm
    bidx = lambda i: ((i * tm) // seq, 0, 0)
    tab = pl.BlockSpec((tm, LANES), lambda i: (i % nt, 0))
    return pl.pallas_call(
        _proj_kernel,
        grid=(m // tm,),
        in_specs=[pl.BlockSpec((tm, d), lambda i: (i, 0)),
                  pl.BlockSpec((1, d), lambda i: (0, 0)),
                  pl.BlockSpec((1, 1, d), bidx),
                  pl.BlockSpec((1, 1, d), bidx),
                  pl.BlockSpec((d, W_PERM_WIDTH), lambda i: (0, 0)),
                  pl.BlockSpec((8, LANES), lambda i: (0, 0)),
                  tab, tab, tab,
                  pl.BlockSpec((LANES, LANES), lambda i: (0, 0))],
        out_specs=[pl.BlockSpec((tm, F_WIDTH), lambda i: (i, 0)),
                   pl.BlockSpec((tm, H_WIDTH), lambda i: (i, 0)),
                   pl.BlockSpec((1, 8, 2 * LANES), lambda i: (i, 0, 0)),
                   pl.BlockSpec((1, tm // TK, VT_ROWS, TK), lambda i: (i // nt, i % nt, 0, 0))],
        out_shape=[jax.ShapeDtypeStruct((m, F_WIDTH), F32),
                   jax.ShapeDtypeStruct((m, H_WIDTH), BF16),
                   jax.ShapeDtypeStruct((m // tm, 8, 2 * LANES), F32),
                   jax.ShapeDtypeStruct((batch, seq // TK, VT_ROWS, TK), BF16)],
        scratch_shapes=[pltpu.VMEM((tm, d), BF16)],
        compiler_params=_cparams(("parallel",)),
        name="in_proj",
    )(x2, ng, sh, sc, w_perm, qkg, cos_t, s1_t, s2_t, bd)


def _head_masks():
    lane = lax.broadcasted_iota(jnp.int32, (1, LANES), 1)
    return [(lane >= HEAD_DIM * h) & (lane < HEAD_DIM * (h + 1)) for h in range(2)]


def _head_rms_t(o):
    ss = jnp.sum(o * o, axis=0, keepdims=True) * (1.0 / HEAD_DIM)
    return o * lax.rsqrt(ss + RMS_EPS)


def _finish_softmax_head(acc):
    return _head_rms_t(acc[:HEAD_DIM, :] / acc[HEAD_DIM:HEAD_DIM + 1, :])


def _moba_kernel(q_ref, km_ref, k_ref, vt_ref, o_ref, bias_ref, qs_ref, acc_ref):
    i = pl.program_id(2)
    nb = km_ref.shape[1]
    q = q_ref[...]
    kma, kmb, kmc = _split3(km_ref[0])
    blk = lax.broadcasted_iota(jnp.int32, (nb, TQ), 0).astype(F32)
    i_f = i.astype(F32)
    krow = lax.broadcasted_iota(jnp.int32, (TK, TQ), 0)
    qcol = lax.broadcasted_iota(jnp.int32, (TK, TQ), 1)
    causal = krow <= qcol

    for h, hm in enumerate(_head_masks()):
        qh = jnp.where(hm, q, 0.0)
        qa, qb, qc = _split3(qh)
        gate = (_dot_nt(kma, qa) + (_dot_nt(kmb, qa) + _dot_nt(kma, qb))
                + (_dot_nt(kmc, qa) + _dot_nt(kmb, qb) + _dot_nt(kma, qc)))
        gate = jnp.where(blk < i_f, gate, NEG)
        sel = jnp.zeros((nb, TQ), F32)
        for _ in range(MOBA_TOPK):
            mx = jnp.max(gate, axis=0, keepdims=True)
            idx = jnp.min(jnp.where(gate == mx, blk, float(nb)), axis=0, keepdims=True)
            pick = blk == idx
            sel = jnp.where(pick, 1.0, sel)
            gate = jnp.where(pick, -jnp.inf, gate)
        bias_ref[h] = jnp.where((sel > 0.5) & (blk < i_f), 0.0, NEG)
        qs_ref[h] = (qh * ATT_SCALE).astype(BF16)

    own = pl.multiple_of(i * TK, TK)
    k_own = k_ref[pl.ds(own, TK), :]
    m_init = []
    for h in range(2):
        s = jnp.where(causal, _dot_nt(k_own, qs_ref[h]), NEG)
        m0 = jnp.max(s, axis=0, keepdims=True)
        p = jnp.exp(s - m0)
        acc_ref[h] = _dot(vt_ref[0, i, h * LANES:(h + 1) * LANES, :], p.astype(BF16))
        m_init.append(m0)

    heads = range(2)
    grp = range(MOBA_GROUP)
    n_groups = (i + MOBA_GROUP - 1) // MOBA_GROUP
    last_group = km_ref.shape[1] // MOBA_GROUP - 1

    def scores(jg):
        off = pl.multiple_of(jnp.minimum(jg, last_group) * (MOBA_GROUP * TK), MOBA_GROUP * TK)
        kb = k_ref[pl.ds(off, MOBA_GROUP * TK), :]
        return tuple(_dot_nt(kb, qs_ref[h]) for h in heads)

    def body(jg, carry):
        ms, ss = carry
        ss_next = scores(jg + 1)
        base = jg * MOBA_GROUP
        sg = [[ss[h][g * TK:(g + 1) * TK, :] for g in grp] for h in heads]
        bs = [[bias_ref[h, pl.ds(base + g, 1), :] for g in grp] for h in heads]
        new = []
        for h in heads:
            m_new = ms[h]
            for g in grp:
                m_new = jnp.maximum(m_new, jnp.max(sg[h][g], axis=0, keepdims=True) + bs[h][g])
            new.append(m_new)
        ps = [[jnp.exp(sg[h][g] - (new[h] - bs[h][g])).astype(BF16) for g in grp] for h in heads]
        for h in heads:
            pv = None
            for g in grp:
                d = _dot(vt_ref[0, base + g, h * LANES:(h + 1) * LANES, :], ps[h][g])
                pv = d if pv is None else pv + d
            acc_ref[h] = jnp.exp(ms[h] - new[h]) * acc_ref[h] + pv
        return tuple(new), ss_next

    lax.fori_loop(0, n_groups, body, (tuple(m_init), scores(0)))
    out_t = jnp.concatenate([_finish_softmax_head(acc_ref[h]) for h in range(2)], axis=0)
    o_ref[...] = out_t.T


def _moba_call(f_arr, h_arr, kmean, vt, batch, seq):
    nq = seq // TQ
    nb = seq // MOBA_BLOCK
    npair = N_HEADS_A // 2
    return pl.pallas_call(
        _moba_kernel,
        grid=(batch, npair, nq),
        in_specs=[pl.BlockSpec((TQ, LANES), lambda b, p, i: (b * nq + i, F_QA // LANES + p)),
                  pl.BlockSpec((1, nb, LANES), lambda b, p, i: (b, 0, p)),
                  pl.BlockSpec((seq, LANES), lambda b, p, i: (b, H_KA // LANES + p)),
                  pl.BlockSpec((1, seq // TK, 2 * LANES, TK), lambda b, p, i: (b, 0, VT_A // (2 * LANES) + p, 0))],
        out_specs=pl.BlockSpec((TQ, LANES), lambda b, p, i: (b * nq + i, p)),
        out_shape=jax.ShapeDtypeStruct((batch * seq, N_HEADS_A * HEAD_DIM), F32),
        scratch_shapes=[pltpu.VMEM((2, nb, TQ), F32),
                        pltpu.VMEM((2, TQ, LANES), BF16),
                        pltpu.VMEM((2, LANES, TQ), F32)],
        compiler_params=_cparams(("parallel", "parallel", "arbitrary")),
        name="moba_attn",
    )(f_arr, kmean, h_arr, vt)


def _sb_kernel(q_ref, k_ref, vt_ref, u_ref, o_ref, qs_ref, acc_ref):
    i = pl.program_id(2)
    krow = lax.broadcasted_iota(jnp.int32, (TK, TQ), 0)
    qcol = lax.broadcasted_iota(jnp.int32, (TK, TQ), 1)
    strict = krow < qcol
    hms = _head_masks()
    for h in range(SB_HEADS):
        pr = slice((h // 2) * LANES, (h // 2 + 1) * LANES)
        qs_ref[h] = (jnp.where(hms[h % 2], q_ref[:, pr], 0.0) * ATT_SCALE).astype(BF16)
        acc_ref[h] = jnp.zeros((LANES, TQ), F32)

    heads = range(SB_HEADS)
    pairs = [slice((h // 2) * LANES, (h // 2 + 1) * LANES) for h in heads]

    def logits(j):
        off = pl.multiple_of(jnp.maximum(j, 0) * TK, TK)
        return tuple(_dot_nt(k_ref[pl.ds(off, TK), pairs[h]], qs_ref[h]) for h in heads)

    def block(j, rs, zs, diag):
        zs_next = logits(j - 1)
        u = u_ref[...]
        log_betas, his, los = [], [], []
        for h in heads:
            z = zs[h]
            log_beta = jnp.minimum(z, 0.0) - jnp.log(1.0 + jnp.exp(-jnp.abs(z)))
            log_1m = log_beta - z
            if diag:
                log_1m = jnp.where(strict, log_1m, 0.0)
            hi = log_1m.astype(BF16)
            log_betas.append(log_beta)
            his.append(hi)
            los.append((log_1m - hi.astype(F32)).astype(BF16))
        sums = [_dot(u, his[h]) + _dot(u, los[h]) for h in heads]
        weights = []
        for h in heads:
            after = sums[h][:TK, :] + rs[h]
            a = jnp.exp(log_betas[h] + after)
            if diag:
                a = jnp.where(strict, a, 0.0)
            weights.append(a.astype(BF16))
        for h in heads:
            acc_ref[h] += _dot(vt_ref[0, j, pairs[h], :], weights[h])
        return tuple(rs[h] + sums[h][TK:TK + 1, :] for h in heads), zs_next

    zero = jnp.zeros((1, TQ), F32)
    rs, zs = block(i, (zero,) * SB_HEADS, logits(i), True)

    def alive(rs):
        m = rs[0]
        for r in rs[1:]:
            m = jnp.maximum(m, r)
        return jnp.max(m)

    def cond(c):
        return (c[0] >= 0) & (c[1] > SB_DEAD_LOG)

    def body(c):
        rs, zs = block(c[0], c[2], c[3], False)
        return c[0] - 1, alive(rs), rs, zs

    lax.while_loop(cond, body, (i - 1, alive(rs), rs, zs))
    out_t = jnp.concatenate(
        [_head_rms_t(acc_ref[h][(h % 2) * HEAD_DIM:(h % 2 + 1) * HEAD_DIM, :]) for h in range(SB_HEADS)], axis=0)
    o_ref[...] = out_t.T


def _sb_call(f_arr, h_arr, vt, u_mat, batch, seq):
    nq = seq // TQ
    w = SB_HEADS * HEAD_DIM
    ngrp = N_HEADS_C // SB_HEADS
    return pl.pallas_call(
        _sb_kernel,
        grid=(batch, ngrp, nq),
        in_specs=[pl.BlockSpec((TQ, w), lambda b, p, i: (b * nq + i, F_QC // w + p)),
                  pl.BlockSpec((seq, w), lambda b, p, i: (b, H_KC // w + p)),
                  pl.BlockSpec((1, seq // TK, w, TK), lambda b, p, i: (b, 0, VT_C // w + p, 0)),
                  pl.BlockSpec((TK + SB_TAIL, TK), lambda b, p, i: (0, 0))],
        out_specs=pl.BlockSpec((TQ, w), lambda b, p, i: (b * nq + i, p)),
        out_shape=jax.ShapeDtypeStruct((batch * seq, N_HEADS_C * HEAD_DIM), F32),
        scratch_shapes=[pltpu.VMEM((SB_HEADS, TQ, LANES), BF16), pltpu.VMEM((SB_HEADS, LANES, TQ), F32)],
        compiler_params=_cparams(("parallel", "parallel", "arbitrary")),
        name="stickbreak_attn",
    )(f_arr, h_arr, vt, u_mat)


def _sortable_key(x):
    bits = lax.bitcast_convert_type(x, jnp.int32)
    return jnp.where(bits < 0, bits ^ jnp.int32(0x7FFFFFFF), bits)


_NEG_BITS = int(np.array(NEG, np.float32).view(np.int32))
NEG_KEY = _NEG_BITS ^ 0x7FFFFFFF
INT_MIN = -2 ** 31


def _dsa_kernel(qi_ref, wi_ref, ki_ref, q_ref, k_ref, vt_ref, o_ref,
                key_ref, qx_ref, qs_ref, acc_ref):
    i = pl.program_id(1)
    seq = k_ref.shape[0]
    topk = min(DSA_TOPK, seq // 4)
    nh = N_HEADS_B
    lane = lax.broadcasted_iota(jnp.int32, (1, LANES), 1)
    low = lane < HEAD_DIM
    krow = lax.broadcasted_iota(jnp.int32, (TK, TQ), 0)
    qcol = lax.broadcasted_iota(jnp.int32, (TK, TQ), 1)
    diag_causal = krow <= qcol
    hms = _head_masks()

    qi = qi_ref[...].astype(F32)
    for h in range(IDX_HEADS):
        qp = qi[:, (h // 2) * LANES:(h // 2 + 1) * LANES]
        if h % 2:
            qp = pltpu.roll(qp, HEAD_DIM, 1)
        qx_ref[h] = jnp.where(low, qp, 0.0).astype(BF16)
    w_t = wi_ref[...].T
    q = q_ref[...]
    for h in range(nh):
        qs_ref[h] = (jnp.where(hms[h % 2], q[:, (h // 2) * LANES:(h // 2 + 1) * LANES], 0.0) * ATT_SCALE).astype(BF16)

    def score_block(j, diag):
        off = pl.multiple_of(j * TK, TK)
        kz = ki_ref[pl.ds(off, TK), :]
        sc = jnp.zeros((TK, TQ), F32)
        for h in range(IDX_HEADS):
            sc = sc + w_t[h:h + 1, :] * jnp.maximum(_dot_nt(kz, qx_ref[h]), 0.0)
        if diag:
            sc = jnp.where(diag_causal, sc, NEG)
        key_ref[j] = _sortable_key(sc)

    def score_body(j, c):
        score_block(j, False)
        return c

    lax.fori_loop(0, i, score_body, 0)
    score_block(i, True)
    for g in range(1, DSA_GROUP):
        @pl.when(i % DSA_GROUP + g < DSA_GROUP)
        def _():
            key_ref[i + g] = jnp.full((TK, TQ), NEG_KEY, jnp.int32)

    n_unscanned = (seq - (i + 1) * TK).astype(F32)
    fold = TK // 4

    def count(pred):
        def body(j, acc):
            hit = jnp.where(pred(key_ref[j], j), 1.0, 0.0)
            return acc + ((hit[:fold] + hit[fold:2 * fold]) + (hit[2 * fold:3 * fold] + hit[3 * fold:]))
        acc = lax.fori_loop(0, i + 1, body, jnp.zeros((fold, TQ), F32))
        return jnp.sum(acc, axis=0, keepdims=True)

    def count_ge(cand):
        return count(lambda key, j: key >= cand) + jnp.where(cand <= NEG_KEY, n_unscanned, 0.0)

    kf = float(topk)
    c0 = count_ge(jnp.zeros((1, TQ), jnp.int32))
    t0 = jnp.where(c0 >= kf, 0, INT_MIN).astype(jnp.int32)
    ct0 = jnp.where(c0 >= kf, c0, float(seq))

    def bit_cond(c):
        return (c[0] < 31) & (c[1] > 0.5)

    def bit_body(c):
        b, _, t, ct = c
        cand = t | jnp.left_shift(jnp.int32(1), 30 - b)
        cc = count_ge(cand)
        ok = cc >= kf
        ct = jnp.where(ok, cc, ct)
        return b + 1, jnp.max(ct) - kf, jnp.where(ok, cand, t), ct

    _, _, t, c_ge = lax.while_loop(bit_cond, bit_body, (jnp.int32(0), jnp.max(ct0) - kf, t0, ct0))

    def tie_limit():
        need = kf - count_ge(t + 1)

        def ties_before(xc):
            return count(lambda key, j: (key == t) & (j * TK + krow < xc))

        def xbit_body(b, x):
            cand = x | jnp.left_shift(jnp.int32(1), 14 - b)
            return jnp.where(ties_before(cand) < need, cand, x)

        return lax.fori_loop(0, 15, xbit_body, jnp.zeros((1, TQ), jnp.int32))

    x = lax.cond(jnp.max(c_ge) > kf, tie_limit, lambda: jnp.full((1, TQ), 2 * seq, jnp.int32))

    heads = range(nh)
    grp = range(DSA_GROUP)
    pairs = [slice((h // 2) * LANES, (h // 2 + 1) * LANES) for h in heads]
    n_full = i // DSA_GROUP

    def scores(jg):
        off = pl.multiple_of(jg * (DSA_GROUP * TK), DSA_GROUP * TK)
        return tuple(_dot_nt(k_ref[pl.ds(off, DSA_GROUP * TK), pairs[h]], qs_ref[h]) for h in heads)

    def attend(jg, carry, causal):
        ms, ss = carry
        ss_next = None if causal else scores(jg + 1)
        base = jg * DSA_GROUP
        biases = []
        for g in grp:
            key = key_ref[base + g]
            pos = (base + g) * TK + krow
            msk = (key > t) | ((key == t) & (pos <= x))
            if causal:
                msk = msk & (pos <= i * TQ + qcol)
            biases.append(jnp.where(msk, 0.0, NEG))
        sg = [[ss[h][g * TK:(g + 1) * TK, :] + biases[g] for g in grp] for h in heads]
        new = []
        for h in heads:
            m_new = ms[h]
            for g in grp:
                m_new = jnp.maximum(m_new, jnp.max(sg[h][g], axis=0, keepdims=True))
            new.append(m_new)
        ps = [[jnp.exp(sg[h][g] - new[h]).astype(BF16) for g in grp] for h in heads]
        for h in heads:
            pv = None
            for g in grp:
                d = _dot(vt_ref[0, base + g, h * LANES:(h + 1) * LANES, :], ps[h][g])
                pv = d if pv is None else pv + d
            acc_ref[h] = jnp.exp(ms[h] - new[h]) * acc_ref[h] + pv
        return tuple(new), ss_next

    for h in heads:
        acc_ref[h] = jnp.zeros((LANES, TQ), F32)
    carry = lax.fori_loop(0, n_full, lambda jg, c: attend(jg, c, False),
                          (tuple(jnp.full((1, TQ), NEG, F32) for _ in heads), scores(0)))
    attend(n_full, carry, True)

    out_t = jnp.concatenate([_finish_softmax_head(acc_ref[h]) for h in range(nh)], axis=0)
    o_ref[...] = out_t.T


def _dsa_call(f_arr, h_arr, vt, batch, seq):
    nq = seq // TQ
    wb = N_HEADS_B * HEAD_DIM
    return pl.pallas_call(
        _dsa_kernel,
        grid=(batch, nq),
        in_specs=[pl.BlockSpec((TQ, IDX_HEADS * IDX_DIM), lambda b, i: (b * nq + i, H_QI // (IDX_HEADS * IDX_DIM))),
                  pl.BlockSpec((TQ, LANES), lambda b, i: (b * nq + i, F_WI // LANES)),
                  pl.BlockSpec((seq, LANES), lambda b, i: (b, H_KI // LANES)),
                  pl.BlockSpec((TQ, wb), lambda b, i: (b * nq + i, F_QB // wb)),
                  pl.BlockSpec((seq, wb), lambda b, i: (b, H_KB // wb)),
                  pl.BlockSpec((1, seq // TK, N_HEADS_B * LANES, TK), lambda b, i: (b, 0, VT_B // (N_HEADS_B * LANES), 0))],
        out_specs=pl.BlockSpec((TQ, wb), lambda b, i: (b * nq + i, 0)),
        out_shape=jax.ShapeDtypeStruct((batch * seq, wb), F32),
        scratch_shapes=[pltpu.VMEM((seq // TK, TK, TQ), jnp.int32),
                        pltpu.VMEM((IDX_HEADS, TQ, LANES), BF16),
                        pltpu.VMEM((N_HEADS_B, TQ, LANES), BF16),
                        pltpu.VMEM((N_HEADS_B, LANES, TQ), F32)],
        compiler_params=_cparams(("parallel", "arbitrary")),
        name="dsa_attn",
    )(h_arr, f_arr, h_arr, f_arr, h_arr, vt)


def _outproj_kernel(x_ref, oa_ref, ob_ref, oc_ref, og_ref, gt_ref, w_ref, o_ref):
    wa = N_HEADS_A * HEAD_DIM
    wb = wa + N_HEADS_B * HEAD_DIM
    og = og_ref[...]
    y = (_dot((oa_ref[...] * og[:, :wa]).astype(BF16), w_ref[:wa, :])
         + _dot((ob_ref[...] * og[:, wa:wb]).astype(BF16), w_ref[wa:wb, :])
         + _dot((oc_ref[...] * og[:, wb:]).astype(BF16), w_ref[wb:, :]))
    o_ref[...] = x_ref[...] + gt_ref[0] * y


def _outproj_call(x2, oa, ob, oc, og, gt, w_out, seq):
    m, d = x2.shape
    tm = 512
    row = lambda i: (i, 0)
    return pl.pallas_call(
        _outproj_kernel,
        grid=(m // tm,),
        in_specs=[pl.BlockSpec((tm, d), row),
                  pl.BlockSpec((tm, oa.shape[1]), row),
                  pl.BlockSpec((tm, ob.shape[1]), row),
                  pl.BlockSpec((tm, oc.shape[1]), row),
                  pl.BlockSpec((1, d), lambda i: (0, 0)),
                  pl.BlockSpec((1, 1, d), lambda i: ((i * tm) // seq, 0, 0)),
                  pl.BlockSpec((d, d), lambda i: (0, 0))],
        out_specs=pl.BlockSpec((tm, d), row),
        out_shape=jax.ShapeDtypeStruct((m, d), F32),
        compiler_params=_cparams(("parallel",)),
        name="out_proj",
    )(x2, oa, ob, oc, og, gt, w_out)


def _rope_tables(seq):
    pos = jnp.arange(seq, dtype=F32)
    inv = ROPE_THETA ** (-jnp.arange(0, ROPE_DIM, 2, dtype=F32) / ROPE_DIM)
    ang = pos[:, None] * inv[None, :]
    cos, sin = jnp.cos(ang), jnp.sin(ang)
    zeros = jnp.zeros((seq, HEAD_DIM - ROPE_DIM), F32)
    zh = jnp.zeros((seq, ROPE_HALF), F32)
    cos_h = jnp.concatenate([cos, cos, jnp.ones_like(zeros)], axis=1)
    s1_h = jnp.concatenate([-sin, zh, zeros], axis=1)
    s2_h = jnp.concatenate([zh, sin, zeros], axis=1)
    two = lambda t: jnp.concatenate([t, t], axis=1)
    return two(cos_h), two(s1_h), two(s2_h)


def _permute_w_in(w):
    d = w.shape[0]
    a, b, c = N_HEADS_A * HEAD_DIM, N_HEADS_B * HEAD_DIM, N_HEADS_C * HEAD_DIM
    sizes = (a, a, a, b, b, b, IDX_HEADS * IDX_DIM, IDX_DIM, IDX_HEADS, c, c, c)
    offs = np.concatenate([[0], np.cumsum(sizes)])
    qa, ka, va, qb, kb, vb, qi, ki, wi, qc, kc, vc = [w[:, int(offs[k]):int(offs[k + 1])] for k in range(12)]
    z = lambda n: jnp.zeros((d, n), w.dtype)
    out = jnp.concatenate([qa, qb, qc, ka, va, kb, vb, kc, vc, qi,
                           ki, z(LANES - IDX_DIM), wi, z(LANES - IDX_HEADS)], axis=1)
    assert out.shape[1] == W_PERM_WIDTH
    return out.astype(BF16)


def _suffix_sum_matrix():
    s = np.arange(TK + SB_TAIL)[:, None]
    j = np.arange(TK)[None, :]
    return jnp.asarray((j > s) | (s >= TK), BF16)


def kernel(x, c, w_ada, b_ada, norm_g, w_in, qk_g, out_g, w_out, ffn_w1, ffn_w3, ffn_w2):
    batch, seq, d = x.shape
    depth = w_ada.shape[0]
    assert seq % PROJ_TM == 0 and d % LANES == 0 and batch <= 8
    assert (seq // TK) % MOBA_GROUP == 0 and (seq // TK) % DSA_GROUP == 0

    c_pad = jnp.zeros((8, d), F32).at[:batch].set(c)
    mod = _mod_call(c_pad, w_ada, b_ada)
    cos_t, s1_t, s2_t = _rope_tables(seq)
    hd = np.arange(LANES) // HEAD_DIM
    bd = jnp.asarray(hd[:, None] == hd[None, :], BF16)
    u_mat = _suffix_sum_matrix()

    x2 = x.reshape(batch * seq, d)
    for layer in range(depth):
        mods = [mod[layer, :batch, k * d:(k + 1) * d].reshape(batch, 1, d) for k in range(N_MOD)]
        sh1, sc1, g1, sh2, sc2, g2, sh3, sc3, g3 = mods
        ng = norm_g[layer]
        w1 = ffn_w1[layer].astype(BF16)
        w3 = ffn_w3[layer].astype(BF16)
        w2 = ffn_w2[layer].astype(BF16)

        x2 = _ffn_call(x2, ng[0:1], sh1, sc1, g1, w1[0], w3[0], w2[0], seq)

        qkg = jnp.concatenate([jnp.tile(qk_g[layer], (1, 2)), jnp.ones((4, LANES), F32)], axis=0)
        f_arr, h_arr, kmean, vt = _proj_call(x2, ng[1:2], sh2, sc2, _permute_w_in(w_in[layer]), qkg,
                                             cos_t, s1_t, s2_t, bd, batch, seq)
        per_tile = PROJ_TM // MOBA_BLOCK
        kmean = kmean[:, :per_tile, :].reshape(batch, seq // MOBA_BLOCK, N_HEADS_A * HEAD_DIM)
        oa = _moba_call(f_arr, h_arr, kmean, vt, batch, seq)
        ob = _dsa_call(f_arr, h_arr, vt, batch, seq)
        oc = _sb_call(f_arr, h_arr, vt, u_mat, batch, seq)
        x2 = _outproj_call(x2, oa, ob, oc, out_g[layer].reshape(1, d), g2, w_out[layer].astype(BF16), seq)

        x2 = _ffn_call(x2, ng[2:3], sh3, sc3, g3, w1[1], w3[1], w2[1], seq)
    return x2.reshape(batch, seq, d)
```

```python
import jax
import jax.numpy as jnp
import numpy as np
from jax import lax
from jax.experimental import pallas as pl
from jax.experimental.pallas import tpu as pltpu

F32 = jnp.float32
BF16 = jnp.bfloat16

HEAD_DIM = 64
N_HEADS_A = 4
N_HEADS_B = 4
N_HEADS_C = 8
ROPE_DIM = HEAD_DIM // 4
ROPE_HALF = ROPE_DIM // 2
ROPE_THETA = 500000.0
MOBA_BLOCK = 256
MOBA_TOPK = 3
DSA_TOPK = 256
IDX_HEADS = 8
IDX_DIM = 64
N_MOD = 9
RMS_EPS = 1e-6
NEG = -1e30
ATT_SCALE = HEAD_DIM ** -0.5

LANES = 128
MXU_N = 256
TQ = 256
TK = 256
PROJ_TM = 512
VMEM_LIMIT = 56 * 1024 * 1024

SB_DEAD_LOG = -110.0
SB_TAIL = 16
SB_HEADS = 4
MOBA_GROUP = 4
DSA_GROUP = 2
BISECT_FIXED_BITS = 24

F_QA, F_QB, F_QC, F_WI, F_WIDTH = 0, 256, 512, 1024, 1152
H_KA, H_KB, H_KC, H_QI, H_KI, H_WIDTH = 0, 256, 512, 1024, 1536, 1664
W_PERM_WIDTH = 15 * MXU_N
VT_A, VT_B, VT_C, VT_ROWS = 0, 512, 1024, 1536


def _dot(a, b):
    return jnp.dot(a, b, preferred_element_type=F32)


def _dot_nt(a, b):
    return lax.dot_general(a, b, (((1,), (1,)), ((), ())), preferred_element_type=F32)


def _split3(x):
    a = x.astype(BF16)
    r = x - a.astype(F32)
    b = r.astype(BF16)
    c = (r - b.astype(F32)).astype(BF16)
    return a, b, c


def _cparams(sem):
    return pltpu.CompilerParams(dimension_semantics=sem, vmem_limit_bytes=VMEM_LIMIT)


def _mod_kernel(c_ref, w_ref, b_ref, o_ref):
    c = c_ref[...]
    sc = c * (1.0 / (1.0 + jnp.exp(-c)))
    a, b, c3 = _split3(sc)
    w = w_ref[0]
    wa, wb, wc = _split3(w)
    acc = _dot(a, wa) + (_dot(a, wb) + _dot(b, wa)) + (_dot(a, wc) + _dot(b, wb) + _dot(c3, wa))
    o_ref[0] = acc + b_ref[0]


def _mod_call(c_pad, w_ada, b_ada):
    depth, d, n = w_ada.shape
    tn = 1024
    rows = c_pad.shape[0]
    return pl.pallas_call(
        _mod_kernel,
        grid=(depth, n // tn),
        in_specs=[pl.BlockSpec((rows, d), lambda l, j: (0, 0)),
                  pl.BlockSpec((1, d, tn), lambda l, j: (l, 0, j)),
                  pl.BlockSpec((1, 1, tn), lambda l, j: (l, 0, j))],
        out_specs=pl.BlockSpec((1, rows, tn), lambda l, j: (l, 0, j)),
        out_shape=jax.ShapeDtypeStruct((depth, rows, n), F32),
        compiler_params=_cparams(("parallel", "parallel")),
        name="adaln_mod",
    )(c_pad, w_ada, b_ada.reshape(depth, 1, n))


def _norm_modulate(x, ng, sh, sc):
    ms = jnp.mean(x * x, axis=-1, keepdims=True)
    h = x * lax.rsqrt(ms + RMS_EPS) * ng
    return h * (1.0 + sc) + sh


def _ffn_kernel(x_ref, ng_ref, sh_ref, sc_ref, gt_ref, w1_ref, w3_ref, w2_ref, o_ref, h_ref, acc_ref):
    j = pl.program_id(1)

    @pl.when(j == 0)
    def _():
        h = _norm_modulate(x_ref[...], ng_ref[...], sh_ref[0], sc_ref[0])
        h_ref[...] = h.astype(BF16)
        acc_ref[...] = jnp.zeros_like(acc_ref)

    h = h_ref[...]
    a = _dot(h, w1_ref[...])
    b = _dot(h, w3_ref[...])
    u = (a * (1.0 / (1.0 + jnp.exp(-a))) * b).astype(BF16)
    acc_ref[...] += _dot(u, w2_ref[...])

    @pl.when(j == pl.num_programs(1) - 1)
    def _():
        o_ref[...] = x_ref[...] + 0.5 * gt_ref[0] * acc_ref[...]


def _ffn_call(x2, ng, sh, sc, gt, w1, w3, w2, seq):
    m, d = x2.shape
    dff = w1.shape[1]
    tm = 1024
    tf = dff // 2 if (dff // 2) % LANES == 0 else MXU_N
    bidx = lambda i, j: ((i * tm) // seq, 0, 0)
    return pl.pallas_call(
        _ffn_kernel,
        grid=(m // tm, dff // tf),
        in_specs=[pl.BlockSpec((tm, d), lambda i, j: (i, 0)),
                  pl.BlockSpec((1, d), lambda i, j: (0, 0)),
                  pl.BlockSpec((1, 1, d), bidx),
                  pl.BlockSpec((1, 1, d), bidx),
                  pl.BlockSpec((1, 1, d), bidx),
                  pl.BlockSpec((d, tf), lambda i, j: (0, j)),
                  pl.BlockSpec((d, tf), lambda i, j: (0, j)),
                  pl.BlockSpec((tf, d), lambda i, j: (j, 0))],
        out_specs=pl.BlockSpec((tm, d), lambda i, j: (i, 0)),
        out_shape=jax.ShapeDtypeStruct((m, d), F32),
        scratch_shapes=[pltpu.VMEM((tm, d), BF16), pltpu.VMEM((tm, d), F32)],
        compiler_params=_cparams(("parallel", "arbitrary")),
        name="swiglu_ffn",
    )(x2, ng, sh, sc, gt, w1, w3, w2)


def _proj_kernel(x_ref, ng_ref, sh_ref, sc_ref, w_ref, qkg_ref, cos_ref, s1_ref, s2_ref, bd_ref,
                 f_ref, h_ref, km_ref, vt_ref, hs_ref):
    hs_ref[...] = _norm_modulate(x_ref[...], ng_ref[...], sh_ref[0], sc_ref[0]).astype(BF16)
    cos, s1, s2 = cos_ref[...], s1_ref[...], s2_ref[...]
    bd = bd_ref[...]
    tm = x_ref.shape[0]

    def rope(v):
        return v * cos + pltpu.roll(v, LANES - ROPE_HALF, 1) * s1 + pltpu.roll(v, ROPE_HALF, 1) * s2

    def headnorm(v, g):
        a, b, c = _split3(v * v)
        ss = _dot(a, bd) + _dot(b, bd) + _dot(c, bd)
        return v * lax.rsqrt(ss * (1.0 / HEAD_DIM) + RMS_EPS) * g

    def chunk(c):
        y = _dot(hs_ref[...], w_ref[:, c * MXU_N:(c + 1) * MXU_N])
        return y[:, :LANES], y[:, LANES:]

    def normrope_chunk(c, grow):
        g = qkg_ref[grow:grow + 1, :]
        return [rope(headnorm(v, g)) for v in chunk(c)]

    def put(ref, off, halves, dtype):
        for k, v in enumerate(halves):
            ref[:, off + k * LANES: off + (k + 1) * LANES] = v.astype(dtype)

    def put_vt(base, halves, with_ones):
        ones = jnp.ones((HEAD_DIM, TK), BF16)
        for r in range(tm // TK):
            for k, v in enumerate(halves):
                t = v[r * TK:(r + 1) * TK, :].T.astype(BF16)
                if with_ones:
                    for hh in range(2):
                        row = base + (2 * k + hh) * LANES
                        vt_ref[0, r, row:row + HEAD_DIM, :] = t[hh * HEAD_DIM:(hh + 1) * HEAD_DIM, :]
                        vt_ref[0, r, row + HEAD_DIM:row + LANES, :] = ones
                else:
                    vt_ref[0, r, base + k * LANES:base + (k + 1) * LANES, :] = t

    put(f_ref, F_QA, normrope_chunk(0, 0), F32)
    put(f_ref, F_QB, normrope_chunk(1, 2), F32)
    put(f_ref, F_QC, chunk(2), F32)
    put(f_ref, F_QC + MXU_N, chunk(3), F32)

    ka = normrope_chunk(4, 1)
    put(h_ref, H_KA, ka, BF16)
    rows = lax.broadcasted_iota(jnp.int32, (8, LANES), 0)
    for k, v in enumerate(ka):
        km = jnp.zeros((8, LANES), F32)
        for r in range(tm // MOBA_BLOCK):
            s = jnp.sum(v[r * MOBA_BLOCK:(r + 1) * MOBA_BLOCK, :], axis=0, keepdims=True) * (1.0 / MOBA_BLOCK)
            km = jnp.where(rows == r, s, km)
        km_ref[0, :, k * LANES:(k + 1) * LANES] = km

    put_vt(VT_A, chunk(5), True)
    put(h_ref, H_KB, normrope_chunk(6, 3), BF16)
    put_vt(VT_B, chunk(7), True)
    put(h_ref, H_KC, chunk(8), BF16)
    put(h_ref, H_KC + MXU_N, chunk(9), BF16)
    put_vt(VT_C, chunk(10), False)
    put_vt(VT_C + MXU_N, chunk(11), False)
    put(h_ref, H_QI, [rope(v) for v in chunk(12)], BF16)
    put(h_ref, H_QI + MXU_N, [rope(v) for v in chunk(13)], BF16)
    ki, wi = chunk(14)
    h_ref[:, H_KI:H_KI + LANES] = rope(ki).astype(BF16)
    f_ref[:, F_WI:F_WI + LANES] = wi


def _proj_call(x2, ng, sh, sc, w_perm, qkg, cos_t, s1_t, s2_t, bd, batch, seq):
    m, d = x2.shape
    tm = PROJ_TM
    nt = seq // tm
    bidx = lambda i: ((i * tm) // seq, 0, 0)
    tab = pl.BlockSpec((tm, LANES), lambda i: (i % nt, 0))
    return pl.pallas_call(
        _proj_kernel,
        grid=(m // tm,),
        in_specs=[pl.BlockSpec((tm, d), lambda i: (i, 0)),
                  pl.BlockSpec((1, d), lambda i: (0, 0)),
                  pl.BlockSpec((1, 1, d), bidx),
                  pl.BlockSpec((1, 1, d), bidx),
                  pl.BlockSpec((d, W_PERM_WIDTH), lambda i: (0, 0)),
                  pl.BlockSpec((8, LANES), lambda i: (0, 0)),
                  tab, tab, tab,
                  pl.BlockSpec((LANES, LANES), lambda i: (0, 0))],
        out_specs=[pl.BlockSpec((tm, F_WIDTH), lambda i: (i, 0)),
                   pl.BlockSpec((tm, H_WIDTH), lambda i: (i, 0)),
                   pl.BlockSpec((1, 8, 2 * LANES), lambda i: (i, 0, 0)),
                   pl.BlockSpec((1, tm // TK, VT_ROWS, TK), lambda i: (i // nt, i % nt, 0, 0))],
        out_shape=[jax.ShapeDtypeStruct((m, F_WIDTH), F32),
                   jax.ShapeDtypeStruct((m, H_WIDTH), BF16),
                   jax.ShapeDtypeStruct((m // tm, 8, 2 * LANES), F32),
                   jax.ShapeDtypeStruct((batch, seq // TK, VT_ROWS, TK), BF16)],
        scratch_shapes=[pltpu.VMEM((tm, d), BF16)],
        compiler_params=_cparams(("parallel",)),
        name="in_proj",
    )(x2, ng, sh, sc, w_perm, qkg, cos_t, s1_t, s2_t, bd)


def _head_masks():
    lane = lax.broadcasted_iota(jnp.int32, (1, LANES), 1)
    return [(lane >= HEAD_DIM * h) & (lane < HEAD_DIM * (h + 1)) for h in range(2)]


def _head_rms_t(o):
    ss = jnp.sum(o * o, axis=0, keepdims=True) * (1.0 / HEAD_DIM)
    return o * lax.rsqrt(ss + RMS_EPS)


def _finish_softmax_head(acc):
    return _head_rms_t(acc[:HEAD_DIM, :] / acc[HEAD_DIM:HEAD_DIM + 1, :])


def _moba_kernel(q_ref, km_ref, k_ref, vt_ref, o_ref, bias_ref, qs_ref, acc_ref):
    i = pl.program_id(2)
    nb = km_ref.shape[1]
    q = q_ref[...]
    kma, kmb, _ = _split3(km_ref[0])
    blk = lax.broadcasted_iota(jnp.int32, (nb, TQ), 0).astype(F32)
    i_f = i.astype(F32)
    krow = lax.broadcasted_iota(jnp.int32, (TK, TQ), 0)
    qcol = lax.broadcasted_iota(jnp.int32, (TK, TQ), 1)
    causal = krow <= qcol

    for h, hm in enumerate(_head_masks()):
        qh_t = jnp.where(hm, q, 0.0).T
        qa, qb, _ = _split3(qh_t)
        gate = _dot(kma, qa) + (_dot(kmb, qa) + _dot(kma, qb))
        gate = jnp.where(blk < i_f, gate, NEG)
        sel = jnp.zeros((nb, TQ), F32)
        for _ in range(MOBA_TOPK):
            mx = jnp.max(gate, axis=0, keepdims=True)
            idx = jnp.min(jnp.where(gate == mx, blk, float(nb)), axis=0, keepdims=True)
            pick = blk == idx
            sel = jnp.where(pick, 1.0, sel)
            gate = jnp.where(pick, -jnp.inf, gate)
        bias_ref[h] = jnp.where((sel > 0.5) & (blk < i_f), 0.0, NEG)
        qs_ref[h] = (qh_t * ATT_SCALE).astype(BF16)

    own = pl.multiple_of(i * TK, TK)
    k_own = k_ref[pl.ds(own, TK), :]
    m_init = []
    for h in range(2):
        s = jnp.where(causal, _dot(k_own, qs_ref[h]), NEG)
        m0 = jnp.max(s, axis=0, keepdims=True)
        p = jnp.exp(s - m0)
        acc_ref[h] = _dot(vt_ref[0, i, h * LANES:(h + 1) * LANES, :], p.astype(BF16))
        m_init.append(m0)

    heads = range(2)
    grp = range(MOBA_GROUP)
    n_groups = (i + MOBA_GROUP - 1) // MOBA_GROUP
    last_group = km_ref.shape[1] // MOBA_GROUP - 1

    def scores(jg):
        off = pl.multiple_of(jnp.minimum(jg, last_group) * (MOBA_GROUP * TK), MOBA_GROUP * TK)
        kb = k_ref[pl.ds(off, MOBA_GROUP * TK), :]
        return tuple(_dot(kb, qs_ref[h]) for h in heads)

    def body(jg, carry):
        ms, ss = carry
        ss_next = scores(jg + 1)
        base = jg * MOBA_GROUP
        sg = [[ss[h][g * TK:(g + 1) * TK, :] for g in grp] for h in heads]
        bs = [[bias_ref[h, pl.ds(base + g, 1), :] for g in grp] for h in heads]
        new = []
        for h in heads:
            m_new = ms[h]
            for g in grp:
                m_new = jnp.maximum(m_new, jnp.max(sg[h][g], axis=0, keepdims=True) + bs[h][g])
            new.append(m_new)
        ps = [[jnp.exp(sg[h][g] - (new[h] - bs[h][g])).astype(BF16) for g in grp] for h in heads]
        for h in heads:
            pv = None
            for g in grp:
                d = _dot(vt_ref[0, base + g, h * LANES:(h + 1) * LANES, :], ps[h][g])
                pv = d if pv is None else pv + d
            acc_ref[h] = jnp.exp(ms[h] - new[h]) * acc_ref[h] + pv
        return tuple(new), ss_next

    lax.fori_loop(0, n_groups, body, (tuple(m_init), scores(0)))
    out_t = jnp.concatenate([_finish_softmax_head(acc_ref[h]) for h in range(2)], axis=0)
    o_ref[...] = out_t.T


def _moba_call(f_arr, h_arr, kmean, vt, batch, seq):
    nq = seq // TQ
    nb = seq // MOBA_BLOCK
    npair = N_HEADS_A // 2
    return pl.pallas_call(
        _moba_kernel,
        grid=(batch, npair, nq),
        in_specs=[pl.BlockSpec((TQ, LANES), lambda b, p, i: (b * nq + i, F_QA // LANES + p)),
                  pl.BlockSpec((1, nb, LANES), lambda b, p, i: (b, 0, p)),
                  pl.BlockSpec((seq, LANES), lambda b, p, i: (b, H_KA // LANES + p)),
                  pl.BlockSpec((1, seq // TK, 2 * LANES, TK), lambda b, p, i: (b, 0, VT_A // (2 * LANES) + p, 0))],
        out_specs=pl.BlockSpec((TQ, LANES), lambda b, p, i: (b * nq + i, p)),
        out_shape=jax.ShapeDtypeStruct((batch * seq, N_HEADS_A * HEAD_DIM), F32),
        scratch_shapes=[pltpu.VMEM((2, nb, TQ), F32),
                        pltpu.VMEM((2, LANES, TQ), BF16),
                        pltpu.VMEM((2, LANES, TQ), F32)],
        compiler_params=_cparams(("parallel", "parallel", "arbitrary")),
        name="moba_attn",
    )(f_arr, kmean, h_arr, vt)


def _sb_kernel(q_ref, k_ref, vt_ref, u_ref, o_ref, qs_ref, acc_ref):
    i = pl.program_id(2)
    krow = lax.broadcasted_iota(jnp.int32, (TK, TQ), 0)
    qcol = lax.broadcasted_iota(jnp.int32, (TK, TQ), 1)
    strict = krow < qcol
    hms = _head_masks()
    for h in range(SB_HEADS):
        pr = slice((h // 2) * LANES, (h // 2 + 1) * LANES)
        qs_ref[h] = (jnp.where(hms[h % 2], q_ref[:, pr], 0.0) * ATT_SCALE).T.astype(BF16)
        acc_ref[h] = jnp.zeros((LANES, TQ), F32)

    heads = range(SB_HEADS)
    pairs = [slice((h // 2) * LANES, (h // 2 + 1) * LANES) for h in heads]

    def logits(j):
        off = pl.multiple_of(jnp.maximum(j, 0) * TK, TK)
        return tuple(_dot(k_ref[pl.ds(off, TK), pairs[h]], qs_ref[h]) for h in heads)

    def block(j, rs, zs, diag):
        zs_next = logits(j - 1)
        u = u_ref[...]
        log_betas, his, los = [], [], []
        for h in heads:
            z = zs[h]
            log_beta = jnp.minimum(z, 0.0) - jnp.log(1.0 + jnp.exp(-jnp.abs(z)))
            log_1m = log_beta - z
            if diag:
                log_1m = jnp.where(strict, log_1m, 0.0)
            hi = log_1m.astype(BF16)
            log_betas.append(log_beta)
            his.append(hi)
            los.append((log_1m - hi.astype(F32)).astype(BF16))
        sums = [_dot(u, his[h]) + _dot(u, los[h]) for h in heads]
        weights = []
        for h in heads:
            after = sums[h][:TK, :] + rs[h]
            a = jnp.exp(log_betas[h] + after)
            if diag:
                a = jnp.where(strict, a, 0.0)
            weights.append(a.astype(BF16))
        for h in heads:
            acc_ref[h] += _dot(vt_ref[0, j, pairs[h], :], weights[h])
        return tuple(rs[h] + sums[h][TK:TK + 1, :] for h in heads), zs_next

    zero = jnp.zeros((1, TQ), F32)
    rs, zs = block(i, (zero,) * SB_HEADS, logits(i), True)

    def alive(rs):
        m = rs[0]
        for r in rs[1:]:
            m = jnp.maximum(m, r)
        return jnp.max(m)

    def cond(c):
        return (c[0] >= 0) & (c[1] > SB_DEAD_LOG)

    def body(c):
        rs, zs = block(c[0], c[2], c[3], False)
        return c[0] - 1, alive(rs), rs, zs

    lax.while_loop(cond, body, (i - 1, alive(rs), rs, zs))
    out_t = jnp.concatenate(
        [_head_rms_t(acc_ref[h][(h % 2) * HEAD_DIM:(h % 2 + 1) * HEAD_DIM, :]) for h in range(SB_HEADS)], axis=0)
    o_ref[...] = out_t.T


def _sb_call(f_arr, h_arr, vt, u_mat, batch, seq):
    nq = seq // TQ
    w = SB_HEADS * HEAD_DIM
    ngrp = N_HEADS_C // SB_HEADS
    return pl.pallas_call(
        _sb_kernel,
        grid=(batch, ngrp, nq),
        in_specs=[pl.BlockSpec((TQ, w), lambda b, p, i: (b * nq + i, F_QC // w + p)),
                  pl.BlockSpec((seq, w), lambda b, p, i: (b, H_KC // w + p)),
                  pl.BlockSpec((1, seq // TK, w, TK), lambda b, p, i: (b, 0, VT_C // w + p, 0)),
                  pl.BlockSpec((TK + SB_TAIL, TK), lambda b, p, i: (0, 0))],
        out_specs=pl.BlockSpec((TQ, w), lambda b, p, i: (b * nq + i, p)),
        out_shape=jax.ShapeDtypeStruct((batch * seq, N_HEADS_C * HEAD_DIM), F32),
        scratch_shapes=[pltpu.VMEM((SB_HEADS, LANES, TQ), BF16), pltpu.VMEM((SB_HEADS, LANES, TQ), F32)],
        compiler_params=_cparams(("parallel", "parallel", "arbitrary")),
        name="stickbreak_attn",
    )(f_arr, h_arr, vt, u_mat)


def _sortable_key(x):
    bits = lax.bitcast_convert_type(x, jnp.int32)
    return jnp.where(bits < 0, bits ^ jnp.int32(0x7FFFFFFF), bits)


_NEG_BITS = int(np.array(NEG, np.float32).view(np.int32))
NEG_KEY = _NEG_BITS ^ 0x7FFFFFFF
INT_MIN = -2 ** 31


def _dsa_kernel(qi_ref, wi_ref, ki_ref, q_ref, k_ref, vt_ref, o_ref,
                key_ref, qx_ref, qs_ref, acc_ref):
    i = pl.program_id(1)
    seq = k_ref.shape[0]
    topk = min(DSA_TOPK, seq // 4)
    nh = N_HEADS_B
    lane = lax.broadcasted_iota(jnp.int32, (1, LANES), 1)
    low = lane < HEAD_DIM
    krow = lax.broadcasted_iota(jnp.int32, (TK, TQ), 0)
    qcol = lax.broadcasted_iota(jnp.int32, (TK, TQ), 1)
    diag_causal = krow <= qcol
    hms = _head_masks()

    qi = qi_ref[...].astype(F32)
    for h in range(IDX_HEADS):
        qp = qi[:, (h // 2) * LANES:(h // 2 + 1) * LANES]
        if h % 2:
            qp = pltpu.roll(qp, HEAD_DIM, 1)
        qx_ref[h] = jnp.where(low, qp, 0.0).T.astype(BF16)
    w_t = wi_ref[...].T
    q = q_ref[...]
    for h in range(nh):
        qh = jnp.where(hms[h % 2], q[:, (h // 2) * LANES:(h // 2 + 1) * LANES], 0.0)
        qs_ref[h] = (qh * ATT_SCALE).T.astype(BF16)

    def score_block(j, diag):
        off = pl.multiple_of(j * TK, TK)
        kz = ki_ref[pl.ds(off, TK), :]
        sc = jnp.zeros((TK, TQ), F32)
        for h in range(IDX_HEADS):
            sc = sc + w_t[h:h + 1, :] * jnp.maximum(_dot(kz, qx_ref[h]), 0.0)
        if diag:
            sc = jnp.where(diag_causal, sc, NEG)
        key_ref[j] = _sortable_key(sc)

    def score_body(j, c):
        score_block(j, False)
        return c

    lax.fori_loop(0, i, score_body, 0)
    score_block(i, True)
    for g in range(1, DSA_GROUP):
        @pl.when(i % DSA_GROUP + g < DSA_GROUP)
        def _():
            key_ref[i + g] = jnp.full((TK, TQ), NEG_KEY, jnp.int32)

    n_unscanned = (seq - (i + 1) * TK).astype(F32)
    fold = TK // 4

    def count(pred):
        def body(j, acc):
            hit = jnp.where(pred(key_ref[j], j), 1.0, 0.0)
            return acc + ((hit[:fold] + hit[fold:2 * fold]) + (hit[2 * fold:3 * fold] + hit[3 * fold:]))
        acc = lax.fori_loop(0, i + 1, body, jnp.zeros((fold, TQ), F32))
        return jnp.sum(acc, axis=0, keepdims=True)

    def count_ge(cand):
        return count(lambda key, j: key >= cand) + jnp.where(cand <= NEG_KEY, n_unscanned, 0.0)

    kf = float(topk)
    c0 = count_ge(jnp.zeros((1, TQ), jnp.int32))
    t0 = jnp.where(c0 >= kf, 0, INT_MIN).astype(jnp.int32)
    ct0 = jnp.where(c0 >= kf, c0, float(seq))

    def bit_step(b, t, ct):
        cand = t | jnp.left_shift(jnp.int32(1), 30 - b)
        cc = count_ge(cand)
        ok = cc >= kf
        return jnp.where(ok, cand, t), jnp.where(ok, cc, ct)

    t, ct = lax.fori_loop(0, BISECT_FIXED_BITS, lambda b, c: bit_step(b, *c), (t0, ct0))

    def bit_cond(c):
        return (c[0] < 31) & (c[1] > 0.5)

    def bit_body(c):
        t, ct = bit_step(c[0], c[2], c[3])
        return c[0] + 1, jnp.max(ct) - kf, t, ct

    _, _, t, c_ge = lax.while_loop(bit_cond, bit_body, (jnp.int32(BISECT_FIXED_BITS), jnp.max(ct) - kf, t, ct))

    def tie_limit():
        need = kf - count_ge(t + 1)

        def ties_before(xc):
            return count(lambda key, j: (key == t) & (j * TK + krow < xc))

        def xbit_body(b, x):
            cand = x | jnp.left_shift(jnp.int32(1), 14 - b)
            return jnp.where(ties_before(cand) < need, cand, x)

        return lax.fori_loop(0, 15, xbit_body, jnp.zeros((1, TQ), jnp.int32))

    x = lax.cond(jnp.max(c_ge) > kf, tie_limit, lambda: jnp.full((1, TQ), 2 * seq, jnp.int32))

    heads = range(nh)
    grp = range(DSA_GROUP)
    pairs = [slice((h // 2) * LANES, (h // 2 + 1) * LANES) for h in heads]
    n_full = i // DSA_GROUP

    def scores(jg):
        off = pl.multiple_of(jg * (DSA_GROUP * TK), DSA_GROUP * TK)
        return tuple(_dot(k_ref[pl.ds(off, DSA_GROUP * TK), pairs[h]], qs_ref[h]) for h in heads)

    def attend(jg, carry, causal):
        ms, ss = carry
        ss_next = None if causal else scores(jg + 1)
        base = jg * DSA_GROUP
        biases = []
        for g in grp:
            key = key_ref[base + g]
            pos = (base + g) * TK + krow
            msk = (key > t) | ((key == t) & (pos <= x))
            if causal:
                msk = msk & (pos <= i * TQ + qcol)
            biases.append(jnp.where(msk, 0.0, NEG))
        sg = [[ss[h][g * TK:(g + 1) * TK, :] for g in grp] for h in heads]
        new = []
        for h in heads:
            m_new = ms[h]
            for g in grp:
                m_new = jnp.maximum(m_new, jnp.max(sg[h][g] + biases[g], axis=0, keepdims=True))
            new.append(m_new)
        ps = [[jnp.exp((sg[h][g] - new[h]) + biases[g]).astype(BF16) for g in grp] for h in heads]
        for h in heads:
            pv = None
            for g in grp:
                d = _dot(vt_ref[0, base + g, h * LANES:(h + 1) * LANES, :], ps[h][g])
                pv = d if pv is None else pv + d
            acc_ref[h] = jnp.exp(ms[h] - new[h]) * acc_ref[h] + pv
        return tuple(new), ss_next

    for h in heads:
        acc_ref[h] = jnp.zeros((LANES, TQ), F32)
    carry = lax.fori_loop(0, n_full, lambda jg, c: attend(jg, c, False),
                          (tuple(jnp.full((1, TQ), NEG, F32) for _ in heads), scores(0)))
    attend(n_full, carry, True)

    out_t = jnp.concatenate([_finish_softmax_head(acc_ref[h]) for h in range(nh)], axis=0)
    o_ref[...] = out_t.T


def _dsa_call(f_arr, h_arr, vt, batch, seq):
    nq = seq // TQ
    wb = N_HEADS_B * HEAD_DIM
    return pl.pallas_call(
        _dsa_kernel,
        grid=(batch, nq),
        in_specs=[pl.BlockSpec((TQ, IDX_HEADS * IDX_DIM), lambda b, i: (b * nq + i, H_QI // (IDX_HEADS * IDX_DIM))),
                  pl.BlockSpec((TQ, LANES), lambda b, i: (b * nq + i, F_WI // LANES)),
                  pl.BlockSpec((seq, LANES), lambda b, i: (b, H_KI // LANES)),
                  pl.BlockSpec((TQ, wb), lambda b, i: (b * nq + i, F_QB // wb)),
                  pl.BlockSpec((seq, wb), lambda b, i: (b, H_KB // wb)),
                  pl.BlockSpec((1, seq // TK, N_HEADS_B * LANES, TK), lambda b, i: (b, 0, VT_B // (N_HEADS_B * LANES), 0))],
        out_specs=pl.BlockSpec((TQ, wb), lambda b, i: (b * nq + i, 0)),
        out_shape=jax.ShapeDtypeStruct((batch * seq, wb), F32),
        scratch_shapes=[pltpu.VMEM((seq // TK, TK, TQ), jnp.int32),
                        pltpu.VMEM((IDX_HEADS, LANES, TQ), BF16),
                        pltpu.VMEM((N_HEADS_B, LANES, TQ), BF16),
                        pltpu.VMEM((N_HEADS_B, LANES, TQ), F32)],
        compiler_params=_cparams(("parallel", "arbitrary")),
        name="dsa_attn",
    )(h_arr, f_arr, h_arr, f_arr, h_arr, vt)


def _outproj_kernel(x_ref, oa_ref, ob_ref, oc_ref, og_ref, gt_ref, w_ref, o_ref):
    wa = N_HEADS_A * HEAD_DIM
    wb = wa + N_HEADS_B * HEAD_DIM
    og = og_ref[...]
    y = (_dot((oa_ref[...] * og[:, :wa]).astype(BF16), w_ref[:wa, :])
         + _dot((ob_ref[...] * og[:, wa:wb]).astype(BF16), w_ref[wa:wb, :])
         + _dot((oc_ref[...] * og[:, wb:]).astype(BF16), w_ref[wb:, :]))
    o_ref[...] = x_ref[...] + gt_ref[0] * y


def _outproj_call(x2, oa, ob, oc, og, gt, w_out, seq):
    m, d = x2.shape
    tm = 512
    row = lambda i: (i, 0)
    return pl.pallas_call(
        _outproj_kernel,
        grid=(m // tm,),
        in_specs=[pl.BlockSpec((tm, d), row),
                  pl.BlockSpec((tm, oa.shape[1]), row),
                  pl.BlockSpec((tm, ob.shape[1]), row),
                  pl.BlockSpec((tm, oc.shape[1]), row),
                  pl.BlockSpec((1, d), lambda i: (0, 0)),
                  pl.BlockSpec((1, 1, d), lambda i: ((i * tm) // seq, 0, 0)),
                  pl.BlockSpec((d, d), lambda i: (0, 0))],
        out_specs=pl.BlockSpec((tm, d), row),
        out_shape=jax.ShapeDtypeStruct((m, d), F32),
        compiler_params=_cparams(("parallel",)),
        name="out_proj",
    )(x2, oa, ob, oc, og, gt, w_out)


def _rope_tables(seq):
    pos = jnp.arange(seq, dtype=F32)
    inv = ROPE_THETA ** (-jnp.arange(0, ROPE_DIM, 2, dtype=F32) / ROPE_DIM)
    ang = pos[:, None] * inv[None, :]
    cos, sin = jnp.cos(ang), jnp.sin(ang)
    zeros = jnp.zeros((seq, HEAD_DIM - ROPE_DIM), F32)
    zh = jnp.zeros((seq, ROPE_HALF), F32)
    cos_h = jnp.concatenate([cos, cos, jnp.ones_like(zeros)], axis=1)
    s1_h = jnp.concatenate([-sin, zh, zeros], axis=1)
    s2_h = jnp.concatenate([zh, sin, zeros], axis=1)
    two = lambda t: jnp.concatenate([t, t], axis=1)
    return two(cos_h), two(s1_h), two(s2_h)


def _permute_w_in(w):
    d = w.shape[0]
    a, b, c = N_HEADS_A * HEAD_DIM, N_HEADS_B * HEAD_DIM, N_HEADS_C * HEAD_DIM
    sizes = (a, a, a, b, b, b, IDX_HEADS * IDX_DIM, IDX_DIM, IDX_HEADS, c, c, c)
    offs = np.concatenate([[0], np.cumsum(sizes)])
    qa, ka, va, qb, kb, vb, qi, ki, wi, qc, kc, vc = [w[:, int(offs[k]):int(offs[k + 1])] for k in range(12)]
    z = lambda n: jnp.zeros((d, n), w.dtype)
    out = jnp.concatenate([qa, qb, qc, ka, va, kb, vb, kc, vc, qi,
                           ki, z(LANES - IDX_DIM), wi, z(LANES - IDX_HEADS)], axis=1)
    assert out.shape[1] == W_PERM_WIDTH
    return out.astype(BF16)


def _suffix_sum_matrix():
    s = np.arange(TK + SB_TAIL)[:, None]
    j = np.arange(TK)[None, :]
    return jnp.asarray((j > s) | (s >= TK), BF16)


def kernel(x, c, w_ada, b_ada, norm_g, w_in, qk_g, out_g, w_out, ffn_w1, ffn_w3, ffn_w2):
    batch, seq, d = x.shape
    depth = w_ada.shape[0]
    assert seq % PROJ_TM == 0 and d % LANES == 0 and batch <= 8
    assert (seq // TK) % MOBA_GROUP == 0 and (seq // TK) % DSA_GROUP == 0

    c_pad = jnp.zeros((8, d), F32).at[:batch].set(c)
    mod = _mod_call(c_pad, w_ada, b_ada)
    cos_t, s1_t, s2_t = _rope_tables(seq)
    hd = np.arange(LANES) // HEAD_DIM
    bd = jnp.asarray(hd[:, None] == hd[None, :], BF16)
    u_mat = _suffix_sum_matrix()

    x2 = x.reshape(batch * seq, d)
    for layer in range(depth):
        mods = [mod[layer, :batch, k * d:(k + 1) * d].reshape(batch, 1, d) for k in range(N_MOD)]
        sh1, sc1, g1, sh2, sc2, g2, sh3, sc3, g3 = mods
        ng = norm_g[layer]
        w1 = ffn_w1[layer].astype(BF16)
        w3 = ffn_w3[layer].astype(BF16)
        w2 = ffn_w2[layer].astype(BF16)

        x2 = _ffn_call(x2, ng[0:1], sh1, sc1, g1, w1[0], w3[0], w2[0], seq)

        qkg = jnp.concatenate([jnp.tile(qk_g[layer], (1, 2)), jnp.ones((4, LANES), F32)], axis=0)
        f_arr, h_arr, kmean, vt = _proj_call(x2, ng[1:2], sh2, sc2, _permute_w_in(w_in[layer]), qkg,
                                             cos_t, s1_t, s2_t, bd, batch, seq)
        per_tile = PROJ_TM // MOBA_BLOCK
        kmean = kmean[:, :per_tile, :].reshape(batch, seq // MOBA_BLOCK, N_HEADS_A * HEAD_DIM)
        oa = _moba_call(f_arr, h_arr, kmean, vt, batch, seq)
        ob = _dsa_call(f_arr, h_arr, vt, batch, seq)
        oc = _sb_call(f_arr, h_arr, vt, u_mat, batch, seq)
        x2 = _outproj_call(x2, oa, ob, oc, out_g[layer].reshape(1, d), g2, w_out[layer].astype(BF16), seq)

        x2 = _ffn_call(x2, ng[2:3], sh3, sc3, g3, w1[1], w3[1], w2[1], seq)
    return x2.reshape(batch, seq, d)
```

```python
import jax
import jax.numpy as jnp
import numpy as np
from jax import lax
from jax.experimental import pallas as pl
from jax.experimental.pallas import tpu as pltpu

F32 = jnp.float32
BF16 = jnp.bfloat16

HEAD_DIM = 64
N_HEADS_A = 4
N_HEADS_B = 4
N_HEADS_C = 8
ROPE_DIM = HEAD_DIM // 4
ROPE_HALF = ROPE_DIM // 2
ROPE_THETA = 500000.0
MOBA_BLOCK = 256
MOBA_TOPK = 3
DSA_TOPK = 256
IDX_HEADS = 8
IDX_DIM = 64
N_MOD = 9
RMS_EPS = 1e-6
NEG = -1e30
ATT_SCALE = HEAD_DIM ** -0.5

LANES = 128
MXU_N = 256
TQ = 256
TK = 256
PROJ_TM = 512
VMEM_LIMIT = 56 * 1024 * 1024

SB_DEAD_LOG = -88.0
SB_TAIL = 16
SB_HEADS = 4
MOBA_GROUP = 4
DSA_GROUP = 1
BISECT_SCAN = 4
BISECT_FIXED = 14

F_QA, F_QB, F_QC, F_WI, F_WIDTH = 0, 256, 512, 1024, 1152
H_KA, H_KB, H_KC, H_QI, H_KI, H_WIDTH = 0, 256, 512, 1024, 1536, 1664
W_PERM_WIDTH = 15 * MXU_N
VT_A, VT_B, VT_C, VT_ROWS = 0, 512, 1024, 1536


def _dot(a, b):
    return jnp.dot(a, b, preferred_element_type=F32)


def _dot_nt(a, b):
    return lax.dot_general(a, b, (((1,), (1,)), ((), ())), preferred_element_type=F32)


def _split3(x):
    a = x.astype(BF16)
    r = x - a.astype(F32)
    b = r.astype(BF16)
    c = (r - b.astype(F32)).astype(BF16)
    return a, b, c


def _cparams(sem):
    return pltpu.CompilerParams(dimension_semantics=sem, vmem_limit_bytes=VMEM_LIMIT)


def _mod_kernel(c_ref, w_ref, b_ref, o_ref):
    c = c_ref[...]
    sc = c * (1.0 / (1.0 + jnp.exp(-c)))
    a, b, c3 = _split3(sc)
    w = w_ref[0]
    wa, wb, wc = _split3(w)
    acc = _dot(a, wa) + (_dot(a, wb) + _dot(b, wa)) + (_dot(a, wc) + _dot(b, wb) + _dot(c3, wa))
    o_ref[0] = acc + b_ref[0]


def _mod_call(c_pad, w_ada, b_ada):
    depth, d, n = w_ada.shape
    tn = 1024
    rows = c_pad.shape[0]
    return pl.pallas_call(
        _mod_kernel,
        grid=(depth, n // tn),
        in_specs=[pl.BlockSpec((rows, d), lambda l, j: (0, 0)),
                  pl.BlockSpec((1, d, tn), lambda l, j: (l, 0, j)),
                  pl.BlockSpec((1, 1, tn), lambda l, j: (l, 0, j))],
        out_specs=pl.BlockSpec((1, rows, tn), lambda l, j: (l, 0, j)),
        out_shape=jax.ShapeDtypeStruct((depth, rows, n), F32),
        compiler_params=_cparams(("parallel", "parallel")),
        name="adaln_mod",
    )(c_pad, w_ada, b_ada.reshape(depth, 1, n))


def _norm_modulate(x, ng, sh, sc):
    ms = jnp.mean(x * x, axis=-1, keepdims=True)
    h = x * lax.rsqrt(ms + RMS_EPS) * ng
    return h * (1.0 + sc) + sh


def _ffn_kernel(x_ref, ng_ref, sh_ref, sc_ref, gt_ref, w1_ref, w3_ref, w2_ref, o_ref, h_ref, acc_ref):
    j = pl.program_id(1)

    @pl.when(j == 0)
    def _():
        h = _norm_modulate(x_ref[...], ng_ref[...], sh_ref[0], sc_ref[0])
        h_ref[...] = h.astype(BF16)
        acc_ref[...] = jnp.zeros_like(acc_ref)

    h = h_ref[...]
    a = _dot(h, w1_ref[...])
    b = _dot(h, w3_ref[...])
    u = (a * (1.0 / (1.0 + jnp.exp(-a))) * b).astype(BF16)
    acc_ref[...] += _dot(u, w2_ref[...])

    @pl.when(j == pl.num_programs(1) - 1)
    def _():
        o_ref[...] = x_ref[...] + 0.5 * gt_ref[0] * acc_ref[...]


def _ffn_call(x2, ng, sh, sc, gt, w1, w3, w2, seq):
    m, d = x2.shape
    dff = w1.shape[1]
    tm = 1024
    tf = dff // 2 if (dff // 2) % LANES == 0 else MXU_N
    bidx = lambda i, j: ((i * tm) // seq, 0, 0)
    return pl.pallas_call(
        _ffn_kernel,
        grid=(m // tm, dff // tf),
        in_specs=[pl.BlockSpec((tm, d), lambda i, j: (i, 0)),
                  pl.BlockSpec((1, d), lambda i, j: (0, 0)),
                  pl.BlockSpec((1, 1, d), bidx),
                  pl.BlockSpec((1, 1, d), bidx),
                  pl.BlockSpec((1, 1, d), bidx),
                  pl.BlockSpec((d, tf), lambda i, j: (0, j)),
                  pl.BlockSpec((d, tf), lambda i, j: (0, j)),
                  pl.BlockSpec((tf, d), lambda i, j: (j, 0))],
        out_specs=pl.BlockSpec((tm, d), lambda i, j: (i, 0)),
        out_shape=jax.ShapeDtypeStruct((m, d), F32),
        scratch_shapes=[pltpu.VMEM((tm, d), BF16), pltpu.VMEM((tm, d), F32)],
        compiler_params=_cparams(("parallel", "arbitrary")),
        name="swiglu_ffn",
    )(x2, ng, sh, sc, gt, w1, w3, w2)


def _proj_kernel(x_ref, ng_ref, sh_ref, sc_ref, w_ref, qkg_ref, cos_ref, s1_ref, s2_ref, bd_ref,
                 f_ref, h_ref, km_ref, vt_ref, hs_ref):
    hs_ref[...] = _norm_modulate(x_ref[...], ng_ref[...], sh_ref[0], sc_ref[0]).astype(BF16)
    cos, s1, s2 = cos_ref[...], s1_ref[...], s2_ref[...]
    bd = bd_ref[...]
    tm = x_ref.shape[0]

    def rope(v):
        return v * cos + pltpu.roll(v, LANES - ROPE_HALF, 1) * s1 + pltpu.roll(v, ROPE_HALF, 1) * s2

    def headnorm(v, g):
        a, b, c = _split3(v * v)
        ss = _dot(a, bd) + _dot(b, bd) + _dot(c, bd)
        return v * lax.rsqrt(ss * (1.0 / HEAD_DIM) + RMS_EPS) * g

    def chunk(c):
        y = _dot(hs_ref[...], w_ref[:, c * MXU_N:(c + 1) * MXU_N])
        return y[:, :LANES], y[:, LANES:]

    def normrope_chunk(c, grow):
        g = qkg_ref[grow:grow + 1, :]
        return [rope(headnorm(v, g)) for v in chunk(c)]

    def put(ref, off, halves, dtype):
        for k, v in enumerate(halves):
            ref[:, off + k * LANES: off + (k + 1) * LANES] = v.astype(dtype)

    def put_vt(base, halves, with_ones):
        ones = jnp.ones((HEAD_DIM, TK), BF16)
        for r in range(tm // TK):
            for k, v in enumerate(halves):
                t = v[r * TK:(r + 1) * TK, :].T.astype(BF16)
                if with_ones:
                    for hh in range(2):
                        row = base + (2 * k + hh) * LANES
                        vt_ref[0, r, row:row + HEAD_DIM, :] = t[hh * HEAD_DIM:(hh + 1) * HEAD_DIM, :]
                        vt_ref[0, r, row + HEAD_DIM:row + LANES, :] = ones
                else:
                    vt_ref[0, r, base + k * LANES:base + (k + 1) * LANES, :] = t

    put(f_ref, F_QA, normrope_chunk(0, 0), F32)
    put(f_ref, F_QB, normrope_chunk(1, 2), F32)
    put(f_ref, F_QC, chunk(2), F32)
    put(f_ref, F_QC + MXU_N, chunk(3), F32)

    ka = normrope_chunk(4, 1)
    put(h_ref, H_KA, ka, BF16)
    rows = lax.broadcasted_iota(jnp.int32, (8, LANES), 0)
    for k, v in enumerate(ka):
        km = jnp.zeros((8, LANES), F32)
        for r in range(tm // MOBA_BLOCK):
            s = jnp.sum(v[r * MOBA_BLOCK:(r + 1) * MOBA_BLOCK, :], axis=0, keepdims=True) * (1.0 / MOBA_BLOCK)
            km = jnp.where(rows == r, s, km)
        km_ref[0, :, k * LANES:(k + 1) * LANES] = km

    put_vt(VT_A, chunk(5), True)
    put(h_ref, H_KB, normrope_chunk(6, 3), BF16)
    put_vt(VT_B, chunk(7), True)
    put(h_ref, H_KC, chunk(8), BF16)
    put(h_ref, H_KC + MXU_N, chunk(9), BF16)
    put_vt(VT_C, chunk(10), False)
    put_vt(VT_C + MXU_N, chunk(11), False)
    put(h_ref, H_QI, [rope(v) for v in chunk(12)], BF16)
    put(h_ref, H_QI + MXU_N, [rope(v) for v in chunk(13)], BF16)
    ki, wi = chunk(14)
    h_ref[:, H_KI:H_KI + LANES] = rope(ki).astype(BF16)
    f_ref[:, F_WI:F_WI + LANES] = wi


def _proj_call(x2, ng, sh, sc, w_perm, qkg, cos_t, s1_t, s2_t, bd, batch, seq):
    m, d = x2.shape
    tm = PROJ_TM
    nt = seq // tm
    bidx = lambda i: ((i * tm) // seq, 0, 0)
    tab = pl.BlockSpec((tm, LANES), lambda i: (i % nt, 0))
    return pl.pallas_call(
        _proj_kernel,
        grid=(m // tm,),
        in_specs=[pl.BlockSpec((tm, d), lambda i: (i, 0)),
                  pl.BlockSpec((1, d), lambda i: (0, 0)),
                  pl.BlockSpec((1, 1, d), bidx),
                  pl.BlockSpec((1, 1, d), bidx),
                  pl.BlockSpec((d, W_PERM_WIDTH), lambda i: (0, 0)),
                  pl.BlockSpec((8, LANES), lambda i: (0, 0)),
                  tab, tab, tab,
                  pl.BlockSpec((LANES, LANES), lambda i: (0, 0))],
        out_specs=[pl.BlockSpec((tm, F_WIDTH), lambda i: (i, 0)),
                   pl.BlockSpec((tm, H_WIDTH), lambda i: (i, 0)),
                   pl.BlockSpec((1, 8, 2 * LANES), lambda i: (i, 0, 0)),
                   pl.BlockSpec((1, tm // TK, VT_ROWS, TK), lambda i: (i // nt, i % nt, 0, 0))],
        out_shape=[jax.ShapeDtypeStruct((m, F_WIDTH), F32),
                   jax.ShapeDtypeStruct((m, H_WIDTH), BF16),
                   jax.ShapeDtypeStruct((m // tm, 8, 2 * LANES), F32),
                   jax.ShapeDtypeStruct((batch, seq // TK, VT_ROWS, TK), BF16)],
        scratch_shapes=[pltpu.VMEM((tm, d), BF16)],
        compiler_params=_cparams(("parallel",)),
        name="in_proj",
    )(x2, ng, sh, sc, w_perm, qkg, cos_t, s1_t, s2_t, bd)


def _head_masks():
    lane = lax.broadcasted_iota(jnp.int32, (1, LANES), 1)
    return [(lane >= HEAD_DIM * h) & (lane < HEAD_DIM * (h + 1)) for h in range(2)]


def _head_rms_t(o):
    ss = jnp.sum(o * o, axis=0, keepdims=True) * (1.0 / HEAD_DIM)
    return o * lax.rsqrt(ss + RMS_EPS)


def _finish_softmax_head(acc):
    return _head_rms_t(acc[:HEAD_DIM, :] / acc[HEAD_DIM:HEAD_DIM + 1, :])


def _moba_kernel(q_ref, km_ref, k_ref, vt_ref, o_ref, bias_ref, qs_ref, acc_ref):
    i = pl.program_id(2)
    nb = km_ref.shape[1]
    q = q_ref[...]
    kma, kmb, _ = _split3(km_ref[0])
    blk = lax.broadcasted_iota(jnp.int32, (nb, TQ), 0).astype(F32)
    i_f = i.astype(F32)
    krow = lax.broadcasted_iota(jnp.int32, (TK, TQ), 0)
    qcol = lax.broadcasted_iota(jnp.int32, (TK, TQ), 1)
    causal = krow <= qcol

    for h, hm in enumerate(_head_masks()):
        qh_t = jnp.where(hm, q, 0.0).T
        qa, qb, _ = _split3(qh_t)
        gate = _dot(kma, qa) + (_dot(kmb, qa) + _dot(kma, qb))
        gate = jnp.where(blk < i_f, gate, NEG)
        sel = jnp.zeros((nb, TQ), F32)
        for _ in range(MOBA_TOPK):
            mx = jnp.max(gate, axis=0, keepdims=True)
            idx = jnp.min(jnp.where(gate == mx, blk, float(nb)), axis=0, keepdims=True)
            pick = blk == idx
            sel = jnp.where(pick, 1.0, sel)
            gate = jnp.where(pick, -jnp.inf, gate)
        bias_ref[h] = jnp.where((sel > 0.5) & (blk < i_f), 0.0, NEG)
        qs_ref[h] = (qh_t * ATT_SCALE).astype(BF16)

    own = pl.multiple_of(i * TK, TK)
    k_own = k_ref[pl.ds(own, TK), :]
    m_init = []
    for h in range(2):
        s = jnp.where(causal, _dot(k_own, qs_ref[h]), NEG)
        m0 = jnp.max(s, axis=0, keepdims=True)
        p = jnp.exp(s - m0)
        acc_ref[h] = _dot(vt_ref[0, i, h * LANES:(h + 1) * LANES, :], p.astype(BF16))
        m_init.append(m0)

    heads = range(2)
    grp = range(MOBA_GROUP)
    n_groups = (i + MOBA_GROUP - 1) // MOBA_GROUP
    last_group = km_ref.shape[1] // MOBA_GROUP - 1

    def scores(jg):
        off = pl.multiple_of(jnp.minimum(jg, last_group) * (MOBA_GROUP * TK), MOBA_GROUP * TK)
        kb = k_ref[pl.ds(off, MOBA_GROUP * TK), :]
        return tuple(_dot(kb, qs_ref[h]) for h in heads)

    def body(jg, carry):
        ms, ss = carry
        ss_next = scores(jg + 1)
        base = jg * MOBA_GROUP
        sg = [[ss[h][g * TK:(g + 1) * TK, :] for g in grp] for h in heads]
        bs = [[bias_ref[h, pl.ds(base + g, 1), :] for g in grp] for h in heads]
        new = []
        for h in heads:
            m_new = ms[h]
            for g in grp:
                m_new = jnp.maximum(m_new, jnp.max(sg[h][g], axis=0, keepdims=True) + bs[h][g])
            new.append(m_new)
        ps = [[jnp.exp(sg[h][g] - (new[h] - bs[h][g])).astype(BF16) for g in grp] for h in heads]
        for h in heads:
            pv = None
            for g in grp:
                d = _dot(vt_ref[0, base + g, h * LANES:(h + 1) * LANES, :], ps[h][g])
                pv = d if pv is None else pv + d
            acc_ref[h] = jnp.exp(ms[h] - new[h]) * acc_ref[h] + pv
        return tuple(new), ss_next

    lax.fori_loop(0, n_groups, body, (tuple(m_init), scores(0)))
    out_t = jnp.concatenate([_finish_softmax_head(acc_ref[h]) for h in range(2)], axis=0)
    o_ref[...] = out_t.T


def _moba_call(f_arr, h_arr, kmean, vt, batch, seq):
    nq = seq // TQ
    nb = seq // MOBA_BLOCK
    npair = N_HEADS_A // 2
    return pl.pallas_call(
        _moba_kernel,
        grid=(batch, npair, nq),
        in_specs=[pl.BlockSpec((TQ, LANES), lambda b, p, i: (b * nq + i, F_QA // LANES + p)),
                  pl.BlockSpec((1, nb, LANES), lambda b, p, i: (b, 0, p)),
                  pl.BlockSpec((seq, LANES), lambda b, p, i: (b, H_KA // LANES + p)),
                  pl.BlockSpec((1, seq // TK, 2 * LANES, TK), lambda b, p, i: (b, 0, VT_A // (2 * LANES) + p, 0))],
        out_specs=pl.BlockSpec((TQ, LANES), lambda b, p, i: (b * nq + i, p)),
        out_shape=jax.ShapeDtypeStruct((batch * seq, N_HEADS_A * HEAD_DIM), F32),
        scratch_shapes=[pltpu.VMEM((2, nb, TQ), F32),
                        pltpu.VMEM((2, LANES, TQ), BF16),
                        pltpu.VMEM((2, LANES, TQ), F32)],
        compiler_params=_cparams(("parallel", "parallel", "arbitrary")),
        name="moba_attn",
    )(f_arr, kmean, h_arr, vt)


def _sb_kernel(q_ref, k_ref, vt_ref, u_ref, o_ref, qs_ref, acc_ref):
    i = pl.program_id(2)
    krow = lax.broadcasted_iota(jnp.int32, (TK, TQ), 0)
    qcol = lax.broadcasted_iota(jnp.int32, (TK, TQ), 1)
    strict = krow < qcol
    hms = _head_masks()
    for h in range(SB_HEADS):
        pr = slice((h // 2) * LANES, (h // 2 + 1) * LANES)
        qs_ref[h] = (jnp.where(hms[h % 2], q_ref[:, pr], 0.0) * ATT_SCALE).T.astype(BF16)
        acc_ref[h] = jnp.zeros((LANES, TQ), F32)

    heads = range(SB_HEADS)
    pairs = [slice((h // 2) * LANES, (h // 2 + 1) * LANES) for h in heads]

    def logits(j):
        off = pl.multiple_of(jnp.maximum(j, 0) * TK, TK)
        return tuple(_dot(k_ref[pl.ds(off, TK), pairs[h]], qs_ref[h]) for h in heads)

    def block(j, rs, zs, diag):
        zs_next = logits(j - 1)
        u = u_ref[...]
        log_betas, his, los = [], [], []
        for h in heads:
            z = zs[h]
            log_beta = jnp.minimum(z, 0.0) - jnp.log(1.0 + jnp.exp(-jnp.abs(z)))
            log_1m = log_beta - z
            if diag:
                log_1m = jnp.where(strict, log_1m, 0.0)
            hi = log_1m.astype(BF16)
            log_betas.append(log_beta)
            his.append(hi)
            los.append((log_1m - hi.astype(F32)).astype(BF16))
        sums = [_dot(u, his[h]) + _dot(u, los[h]) for h in heads]
        weights = []
        for h in heads:
            after = sums[h][:TK, :] + rs[h]
            a = jnp.exp(log_betas[h] + after)
            if diag:
                a = jnp.where(strict, a, 0.0)
            weights.append(a.astype(BF16))
        for h in heads:
            acc_ref[h] += _dot(vt_ref[0, j, pairs[h], :], weights[h])
        return tuple(rs[h] + sums[h][TK:TK + 1, :] for h in heads), zs_next

    zero = jnp.zeros((1, TQ), F32)
    rs, zs = block(i, (zero,) * SB_HEADS, logits(i), True)

    def alive(rs):
        m = rs[0]
        for r in rs[1:]:
            m = jnp.maximum(m, r)
        return jnp.max(m)

    def cond(c):
        return (c[0] >= 0) & (c[1] > SB_DEAD_LOG)

    def body(c):
        rs, zs = block(c[0], c[2], c[3], False)
        return c[0] - 1, alive(rs), rs, zs

    lax.while_loop(cond, body, (i - 1, alive(rs), rs, zs))
    out_t = jnp.concatenate(
        [_head_rms_t(acc_ref[h][(h % 2) * HEAD_DIM:(h % 2 + 1) * HEAD_DIM, :]) for h in range(SB_HEADS)], axis=0)
    o_ref[...] = out_t.T


def _sb_call(f_arr, h_arr, vt, u_mat, batch, seq):
    nq = seq // TQ
    w = SB_HEADS * HEAD_DIM
    ngrp = N_HEADS_C // SB_HEADS
    return pl.pallas_call(
        _sb_kernel,
        grid=(batch, ngrp, nq),
        in_specs=[pl.BlockSpec((TQ, w), lambda b, p, i: (b * nq + i, F_QC // w + p)),
                  pl.BlockSpec((seq, w), lambda b, p, i: (b, H_KC // w + p)),
                  pl.BlockSpec((1, seq // TK, w, TK), lambda b, p, i: (b, 0, VT_C // w + p, 0)),
                  pl.BlockSpec((TK + SB_TAIL, TK), lambda b, p, i: (0, 0))],
        out_specs=pl.BlockSpec((TQ, w), lambda b, p, i: (b * nq + i, p)),
        out_shape=jax.ShapeDtypeStruct((batch * seq, N_HEADS_C * HEAD_DIM), F32),
        scratch_shapes=[pltpu.VMEM((SB_HEADS, LANES, TQ), BF16), pltpu.VMEM((SB_HEADS, LANES, TQ), F32)],
        compiler_params=_cparams(("parallel", "parallel", "arbitrary")),
        name="stickbreak_attn",
    )(f_arr, h_arr, vt, u_mat)


def _sortable_key(x):
    bits = lax.bitcast_convert_type(x, jnp.int32)
    return jnp.where(bits < 0, bits ^ jnp.int32(0x7FFFFFFF), bits)


_NEG_BITS = int(np.array(NEG, np.float32).view(np.int32))
NEG_KEY = _NEG_BITS ^ 0x7FFFFFFF
INT_MIN = -2 ** 31


def _dsa_kernel(qi_ref, wi_ref, ki_ref, q_ref, k_ref, vt_ref, o_ref,
                key_ref, qx_ref, qs_ref, acc_ref):
    i = pl.program_id(1)
    seq = k_ref.shape[0]
    topk = min(DSA_TOPK, seq // 4)
    nh = N_HEADS_B
    lane = lax.broadcasted_iota(jnp.int32, (1, LANES), 1)
    low = lane < HEAD_DIM
    krow = lax.broadcasted_iota(jnp.int32, (TK, TQ), 0)
    qcol = lax.broadcasted_iota(jnp.int32, (TK, TQ), 1)
    diag_causal = krow <= qcol
    hms = _head_masks()

    qi = qi_ref[...].astype(F32)
    for h in range(IDX_HEADS):
        qp = qi[:, (h // 2) * LANES:(h // 2 + 1) * LANES]
        if h % 2:
            qp = pltpu.roll(qp, HEAD_DIM, 1)
        qx_ref[h] = jnp.where(low, qp, 0.0).T.astype(BF16)
    w_t = wi_ref[...].T
    q = q_ref[...]
    for h in range(nh):
        qh = jnp.where(hms[h % 2], q[:, (h // 2) * LANES:(h // 2 + 1) * LANES], 0.0)
        qs_ref[h] = (qh * ATT_SCALE).T.astype(BF16)

    def score_block(j, diag):
        off = pl.multiple_of(j * TK, TK)
        kz = ki_ref[pl.ds(off, TK), :]
        sc = jnp.zeros((TK, TQ), F32)
        for h in range(IDX_HEADS):
            sc = sc + w_t[h:h + 1, :] * jnp.maximum(_dot(kz, qx_ref[h]), 0.0)
        if diag:
            sc = jnp.where(diag_causal, sc, NEG)
        key_ref[j] = _sortable_key(sc)
        return jnp.max(sc, axis=0, keepdims=True)

    row_max = lax.fori_loop(0, i, lambda j, mx: jnp.maximum(mx, score_block(j, False)),
                            jnp.full((1, TQ), NEG, F32))
    row_max = jnp.maximum(row_max, score_block(i, True))
    for g in range(1, DSA_GROUP):
        @pl.when(i % DSA_GROUP + g < DSA_GROUP)
        def _():
            key_ref[i + g] = jnp.full((TK, TQ), NEG_KEY, jnp.int32)

    n_unscanned = (seq - (i + 1) * TK).astype(F32)
    fold = TK // 4

    def count(pred):
        def body(j, acc):
            hit = pred(key_ref[j], j)
            for r in range(TK // fold):
                acc = jnp.where(hit[r * fold:(r + 1) * fold, :], acc + 1.0, acc)
            return acc
        acc = lax.fori_loop(0, i + 1, body, jnp.zeros((fold, TQ), F32))
        return jnp.sum(acc, axis=0, keepdims=True)

    def count_ge(cand):
        return count(lambda key, j: key >= cand) + jnp.where(cand <= NEG_KEY, n_unscanned, 0.0)

    kf = float(topk)

    def probe(cand, state):
        lo, c_lo, hi = state
        inside = (cand > lo) & (cand < hi)
        c = count_ge(cand)
        up = inside & (c >= kf)
        down = inside & (c < kf)
        return jnp.where(up, cand, lo), jnp.where(up, c, c_lo), jnp.where(down, cand, hi)

    k_max = _sortable_key(row_max)
    state = (jnp.full((1, TQ), INT_MIN, jnp.int32), jnp.full((1, TQ), float(seq), F32),
             jnp.maximum(k_max, NEG_KEY) + 1)

    exp_bits = k_max & jnp.int32(0x7F800000)
    for m in range(BISECT_SCAN):
        cand = exp_bits - jnp.int32(m << 23)
        state = probe(jnp.where((k_max > 0) & (cand > 0), cand, INT_MIN), state)

    def midpoint(state):
        lo, _, hi = state
        return lo + lax.shift_right_logical(hi - lo, 1)

    def pending(state):
        lo, c_lo, hi = state
        return jnp.max(jnp.where((c_lo > kf) & (lax.shift_right_logical(hi - lo, 1) > 0), 1.0, 0.0))

    state = lax.fori_loop(0, BISECT_FIXED, lambda b, s: probe(midpoint(s), s), state)
    _, state = lax.while_loop(lambda c: c[0] > 0.5,
                              lambda c: (lambda s: (pending(s), s))(probe(midpoint(c[1]), c[1])),
                              (pending(state), state))
    t, c_ge, _ = state

    def tie_limit():
        need = kf - count_ge(t + 1)

        def ties_before(xc):
            return count(lambda key, j: (key == t) & (j * TK + krow < xc))

        def xbit_body(b, x):
            cand = x | jnp.left_shift(jnp.int32(1), 14 - b)
            return jnp.where(ties_before(cand) < need, cand, x)

        return lax.fori_loop(0, 15, xbit_body, jnp.zeros((1, TQ), jnp.int32))

    x = lax.cond(jnp.max(c_ge) > kf, tie_limit, lambda: jnp.full((1, TQ), 2 * seq, jnp.int32))

    heads = range(nh)
    grp = range(DSA_GROUP)
    pairs = [slice((h // 2) * LANES, (h // 2 + 1) * LANES) for h in heads]
    n_full = i // DSA_GROUP

    def scores(jg):
        off = pl.multiple_of(jg * (DSA_GROUP * TK), DSA_GROUP * TK)
        return tuple(_dot(k_ref[pl.ds(off, DSA_GROUP * TK), pairs[h]], qs_ref[h]) for h in heads)

    def attend(jg, carry, causal):
        ms, ss = carry
        ss_next = None if causal else scores(jg + 1)
        base = jg * DSA_GROUP
        biases = []
        for g in grp:
            key = key_ref[base + g]
            pos = (base + g) * TK + krow
            msk = (key > t) | ((key == t) & (pos <= x))
            if causal:
                msk = msk & (pos <= i * TQ + qcol)
            biases.append(jnp.where(msk, 0.0, NEG))
        sg = [[ss[h][g * TK:(g + 1) * TK, :] for g in grp] for h in heads]
        new, ps = {}, {}

        def stage_max(h):
            m_new = ms[h]
            for g in grp:
                m_new = jnp.maximum(m_new, jnp.max(sg[h][g] + biases[g], axis=0, keepdims=True))
            new[h] = m_new

        def stage_exp(h):
            ps[h] = [jnp.exp((sg[h][g] - new[h]) + biases[g]).astype(BF16) for g in grp]

        def stage_pv(h):
            pv = None
            for g in grp:
                d = _dot(vt_ref[0, base + g, h * LANES:(h + 1) * LANES, :], ps[h][g])
                pv = d if pv is None else pv + d
            acc_ref[h] = jnp.exp(ms[h] - new[h]) * acc_ref[h] + pv

        for step in range(nh + 2):
            if step < nh:
                stage_max(step)
            if 0 <= step - 1 < nh:
                stage_exp(step - 1)
            if 0 <= step - 2 < nh:
                stage_pv(step - 2)
        return tuple(new[h] for h in heads), ss_next

    for h in heads:
        acc_ref[h] = jnp.zeros((LANES, TQ), F32)
    carry = lax.fori_loop(0, n_full, lambda jg, c: attend(jg, c, False),
                          (tuple(jnp.full((1, TQ), NEG, F32) for _ in heads), scores(0)))
    attend(n_full, carry, True)

    out_t = jnp.concatenate([_finish_softmax_head(acc_ref[h]) for h in range(nh)], axis=0)
    o_ref[...] = out_t.T


def _dsa_call(f_arr, h_arr, vt, batch, seq):
    nq = seq // TQ
    wb = N_HEADS_B * HEAD_DIM
    return pl.pallas_call(
        _dsa_kernel,
        grid=(batch, nq),
        in_specs=[pl.BlockSpec((TQ, IDX_HEADS * IDX_DIM), lambda b, i: (b * nq + i, H_QI // (IDX_HEADS * IDX_DIM))),
                  pl.BlockSpec((TQ, LANES), lambda b, i: (b * nq + i, F_WI // LANES)),
                  pl.BlockSpec((seq, LANES), lambda b, i: (b, H_KI // LANES)),
                  pl.BlockSpec((TQ, wb), lambda b, i: (b * nq + i, F_QB // wb)),
                  pl.BlockSpec((seq, wb), lambda b, i: (b, H_KB // wb)),
                  pl.BlockSpec((1, seq // TK, N_HEADS_B * LANES, TK), lambda b, i: (b, 0, VT_B // (N_HEADS_B * LANES), 0))],
        out_specs=pl.BlockSpec((TQ, wb), lambda b, i: (b * nq + i, 0)),
        out_shape=jax.ShapeDtypeStruct((batch * seq, wb), F32),
        scratch_shapes=[pltpu.VMEM((seq // TK, TK, TQ), jnp.int32),
                        pltpu.VMEM((IDX_HEADS, LANES, TQ), BF16),
                        pltpu.VMEM((N_HEADS_B, LANES, TQ), BF16),
                        pltpu.VMEM((N_HEADS_B, LANES, TQ), F32)],
        compiler_params=_cparams(("parallel", "arbitrary")),
        name="dsa_attn",
    )(h_arr, f_arr, h_arr, f_arr, h_arr, vt)


def _outproj_kernel(x_ref, oa_ref, ob_ref, oc_ref, og_ref, gt_ref, w_ref, o_ref):
    wa = N_HEADS_A * HEAD_DIM
    wb = wa + N_HEADS_B * HEAD_DIM
    og = og_ref[...]
    y = (_dot((oa_ref[...] * og[:, :wa]).astype(BF16), w_ref[:wa, :])
         + _dot((ob_ref[...] * og[:, wa:wb]).astype(BF16), w_ref[wa:wb, :])
         + _dot((oc_ref[...] * og[:, wb:]).astype(BF16), w_ref[wb:, :]))
    o_ref[...] = x_ref[...] + gt_ref[0] * y


def _outproj_call(x2, oa, ob, oc, og, gt, w_out, seq):
    m, d = x2.shape
    tm = 512
    row = lambda i: (i, 0)
    return pl.pallas_call(
        _outproj_kernel,
        grid=(m // tm,),
        in_specs=[pl.BlockSpec((tm, d), row),
                  pl.BlockSpec((tm, oa.shape[1]), row),
                  pl.BlockSpec((tm, ob.shape[1]), row),
                  pl.BlockSpec((tm, oc.shape[1]), row),
                  pl.BlockSpec((1, d), lambda i: (0, 0)),
                  pl.BlockSpec((1, 1, d), lambda i: ((i * tm) // seq, 0, 0)),
                  pl.BlockSpec((d, d), lambda i: (0, 0))],
        out_specs=pl.BlockSpec((tm, d), row),
        out_shape=jax.ShapeDtypeStruct((m, d), F32),
        compiler_params=_cparams(("parallel",)),
        name="out_proj",
    )(x2, oa, ob, oc, og, gt, w_out)


def _rope_tables(seq):
    pos = jnp.arange(seq, dtype=F32)
    inv = ROPE_THETA ** (-jnp.arange(0, ROPE_DIM, 2, dtype=F32) / ROPE_DIM)
    ang = pos[:, None] * inv[None, :]
    cos, sin = jnp.cos(ang), jnp.sin(ang)
    zeros = jnp.zeros((seq, HEAD_DIM - ROPE_DIM), F32)
    zh = jnp.zeros((seq, ROPE_HALF), F32)
    cos_h = jnp.concatenate([cos, cos, jnp.ones_like(zeros)], axis=1)
    s1_h = jnp.concatenate([-sin, zh, zeros], axis=1)
    s2_h = jnp.concatenate([zh, sin, zeros], axis=1)
    two = lambda t: jnp.concatenate([t, t], axis=1)
    return two(cos_h), two(s1_h), two(s2_h)


def _permute_w_in(w):
    d = w.shape[0]
    a, b, c = N_HEADS_A * HEAD_DIM, N_HEADS_B * HEAD_DIM, N_HEADS_C * HEAD_DIM
    sizes = (a, a, a, b, b, b, IDX_HEADS * IDX_DIM, IDX_DIM, IDX_HEADS, c, c, c)
    offs = np.concatenate([[0], np.cumsum(sizes)])
    qa, ka, va, qb, kb, vb, qi, ki, wi, qc, kc, vc = [w[:, int(offs[k]):int(offs[k + 1])] for k in range(12)]
    z = lambda n: jnp.zeros((d, n), w.dtype)
    out = jnp.concatenate([qa, qb, qc, ka, va, kb, vb, kc, vc, qi,
                           ki, z(LANES - IDX_DIM), wi, z(LANES - IDX_HEADS)], axis=1)
    assert out.shape[1] == W_PERM_WIDTH
    return out.astype(BF16)


def _suffix_sum_matrix():
    s = np.arange(TK + SB_TAIL)[:, None]
    j = np.arange(TK)[None, :]
    return jnp.asarray((j > s) | (s >= TK), BF16)


def kernel(x, c, w_ada, b_ada, norm_g, w_in, qk_g, out_g, w_out, ffn_w1, ffn_w3, ffn_w2):
    batch, seq, d = x.shape
    depth = w_ada.shape[0]
    assert seq % PROJ_TM == 0 and d % LANES == 0 and batch <= 8
    assert (seq // TK) % MOBA_GROUP == 0 and (seq // TK) % DSA_GROUP == 0

    c_pad = jnp.zeros((8, d), F32).at[:batch].set(c)
    mod = _mod_call(c_pad, w_ada, b_ada)
    cos_t, s1_t, s2_t = _rope_tables(seq)
    hd = np.arange(LANES) // HEAD_DIM
    bd = jnp.asarray(hd[:, None] == hd[None, :], BF16)
    u_mat = _suffix_sum_matrix()

    x2 = x.reshape(batch * seq, d)
    for layer in range(depth):
        mods = [mod[layer, :batch, k * d:(k + 1) * d].reshape(batch, 1, d) for k in range(N_MOD)]
        sh1, sc1, g1, sh2, sc2, g2, sh3, sc3, g3 = mods
        ng = norm_g[layer]
        w1 = ffn_w1[layer].astype(BF16)
        w3 = ffn_w3[layer].astype(BF16)
        w2 = ffn_w2[layer].astype(BF16)

        x2 = _ffn_call(x2, ng[0:1], sh1, sc1, g1, w1[0], w3[0], w2[0], seq)

        qkg = jnp.concatenate([jnp.tile(qk_g[layer], (1, 2)), jnp.ones((4, LANES), F32)], axis=0)
        f_arr, h_arr, kmean, vt = _proj_call(x2, ng[1:2], sh2, sc2, _permute_w_in(w_in[layer]), qkg,
                                             cos_t, s1_t, s2_t, bd, batch, seq)
        per_tile = PROJ_TM // MOBA_BLOCK
        kmean = kmean[:, :per_tile, :].reshape(batch, seq // MOBA_BLOCK, N_HEADS_A * HEAD_DIM)
        oa = _moba_call(f_arr, h_arr, kmean, vt, batch, seq)
        ob = _dsa_call(f_arr, h_arr, vt, batch, seq)
        oc = _sb_call(f_arr, h_arr, vt, u_mat, batch, seq)
        x2 = _outproj_call(x2, oa, ob, oc, out_g[layer].reshape(1, d), g2, w_out[layer].astype(BF16), seq)

        x2 = _ffn_call(x2, ng[2:3], sh3, sc3, g3, w1[1], w3[1], w2[1], seq)
    return x2.reshape(batch, seq, d)
```

```python
import jax
import jax.numpy as jnp
import numpy as np
from jax import lax
from jax.experimental import pallas as pl
from jax.experimental.pallas import tpu as pltpu

F32 = jnp.float32
BF16 = jnp.bfloat16

HEAD_DIM = 64
N_HEADS_A = 4
N_HEADS_B = 4
N_HEADS_C = 8
ROPE_DIM = HEAD_DIM // 4
ROPE_HALF = ROPE_DIM // 2
ROPE_THETA = 500000.0
MOBA_BLOCK = 256
MOBA_TOPK = 3
DSA_TOPK = 256
IDX_HEADS = 8
IDX_DIM = 64
N_MOD = 9
RMS_EPS = 1e-6
NEG = -1e30
ATT_SCALE = HEAD_DIM ** -0.5

LANES = 128
MXU_N = 256
TQ = 256
TK = 256
PROJ_TM = 512
VMEM_LIMIT = 56 * 1024 * 1024

SB_DEAD_LOG = -88.0
SB_TAIL = 16
SB_HEADS = 4
MOBA_GROUP = 4
DSA_GROUP = 1
BISECT_SCAN = 4
BISECT_COARSE = 7
BISECT_FIXED = 7

F_QA, F_QB, F_QC, F_WI, F_WIDTH = 0, 256, 512, 1024, 1152
H_KA, H_KB, H_KC, H_QI, H_KI, H_WIDTH = 0, 256, 512, 1024, 1536, 1664
W_PERM_WIDTH = 15 * MXU_N
VT_A, VT_B, VT_C, VT_ROWS = 0, 512, 1024, 1536


def _dot(a, b):
    return jnp.dot(a, b, preferred_element_type=F32)


def _dot_nt(a, b):
    return lax.dot_general(a, b, (((1,), (1,)), ((), ())), preferred_element_type=F32)


def _split3(x):
    a = x.astype(BF16)
    r = x - a.astype(F32)
    b = r.astype(BF16)
    c = (r - b.astype(F32)).astype(BF16)
    return a, b, c


def _cparams(sem):
    return pltpu.CompilerParams(dimension_semantics=sem, vmem_limit_bytes=VMEM_LIMIT)


def _mod_kernel(c_ref, w_ref, b_ref, o_ref):
    c = c_ref[...]
    sc = c * (1.0 / (1.0 + jnp.exp(-c)))
    a, b, c3 = _split3(sc)
    w = w_ref[0]
    wa, wb, wc = _split3(w)
    acc = _dot(a, wa) + (_dot(a, wb) + _dot(b, wa)) + (_dot(a, wc) + _dot(b, wb) + _dot(c3, wa))
    o_ref[0] = acc + b_ref[0]


def _mod_call(c_pad, w_ada, b_ada):
    depth, d, n = w_ada.shape
    tn = 1024
    rows = c_pad.shape[0]
    return pl.pallas_call(
        _mod_kernel,
        grid=(depth, n // tn),
        in_specs=[pl.BlockSpec((rows, d), lambda l, j: (0, 0)),
                  pl.BlockSpec((1, d, tn), lambda l, j: (l, 0, j)),
                  pl.BlockSpec((1, 1, tn), lambda l, j: (l, 0, j))],
        out_specs=pl.BlockSpec((1, rows, tn), lambda l, j: (l, 0, j)),
        out_shape=jax.ShapeDtypeStruct((depth, rows, n), F32),
        compiler_params=_cparams(("parallel", "parallel")),
        name="adaln_mod",
    )(c_pad, w_ada, b_ada.reshape(depth, 1, n))


def _norm_modulate(x, ng, sh, sc):
    ms = jnp.mean(x * x, axis=-1, keepdims=True)
    h = x * lax.rsqrt(ms + RMS_EPS) * ng
    return h * (1.0 + sc) + sh


def _ffn_kernel(x_ref, ng_ref, sh_ref, sc_ref, gt_ref, w1_ref, w3_ref, w2_ref, o_ref, h_ref, acc_ref):
    j = pl.program_id(1)

    @pl.when(j == 0)
    def _():
        h = _norm_modulate(x_ref[...], ng_ref[...], sh_ref[0], sc_ref[0])
        h_ref[...] = h.astype(BF16)
        acc_ref[...] = jnp.zeros_like(acc_ref)

    h = h_ref[...]
    a = _dot(h, w1_ref[...])
    b = _dot(h, w3_ref[...])
    u = (a * (1.0 / (1.0 + jnp.exp(-a))) * b).astype(BF16)
    acc_ref[...] += _dot(u, w2_ref[...])

    @pl.when(j == pl.num_programs(1) - 1)
    def _():
        o_ref[...] = x_ref[...] + 0.5 * gt_ref[0] * acc_ref[...]


def _ffn_call(x2, ng, sh, sc, gt, w1, w3, w2, seq):
    m, d = x2.shape
    dff = w1.shape[1]
    tm = 1024
    tf = dff // 2 if (dff // 2) % LANES == 0 else MXU_N
    bidx = lambda i, j: ((i * tm) // seq, 0, 0)
    return pl.pallas_call(
        _ffn_kernel,
        grid=(m // tm, dff // tf),
        in_specs=[pl.BlockSpec((tm, d), lambda i, j: (i, 0)),
                  pl.BlockSpec((1, d), lambda i, j: (0, 0)),
                  pl.BlockSpec((1, 1, d), bidx),
                  pl.BlockSpec((1, 1, d), bidx),
                  pl.BlockSpec((1, 1, d), bidx),
                  pl.BlockSpec((d, tf), lambda i, j: (0, j)),
                  pl.BlockSpec((d, tf), lambda i, j: (0, j)),
                  pl.BlockSpec((tf, d), lambda i, j: (j, 0))],
        out_specs=pl.BlockSpec((tm, d), lambda i, j: (i, 0)),
        out_shape=jax.ShapeDtypeStruct((m, d), F32),
        scratch_shapes=[pltpu.VMEM((tm, d), BF16), pltpu.VMEM((tm, d), F32)],
        compiler_params=_cparams(("parallel", "arbitrary")),
        name="swiglu_ffn",
    )(x2, ng, sh, sc, gt, w1, w3, w2)


def _proj_kernel(x_ref, ng_ref, sh_ref, sc_ref, w_ref, qkg_ref, cos_ref, s1_ref, s2_ref, bd_ref,
                 f_ref, h_ref, km_ref, vt_ref, hs_ref):
    hs_ref[...] = _norm_modulate(x_ref[...], ng_ref[...], sh_ref[0], sc_ref[0]).astype(BF16)
    cos, s1, s2 = cos_ref[...], s1_ref[...], s2_ref[...]
    bd = bd_ref[...]
    tm = x_ref.shape[0]

    def rope(v):
        return v * cos + pltpu.roll(v, LANES - ROPE_HALF, 1) * s1 + pltpu.roll(v, ROPE_HALF, 1) * s2

    def headnorm(v, g):
        a, b, c = _split3(v * v)
        ss = _dot(a, bd) + _dot(b, bd) + _dot(c, bd)
        return v * lax.rsqrt(ss * (1.0 / HEAD_DIM) + RMS_EPS) * g

    def chunk(c):
        y = _dot(hs_ref[...], w_ref[:, c * MXU_N:(c + 1) * MXU_N])
        return y[:, :LANES], y[:, LANES:]

    def normrope_chunk(c, grow):
        g = qkg_ref[grow:grow + 1, :]
        return [rope(headnorm(v, g)) for v in chunk(c)]

    def put(ref, off, halves, dtype):
        for k, v in enumerate(halves):
            ref[:, off + k * LANES: off + (k + 1) * LANES] = v.astype(dtype)

    def put_vt(base, halves, with_ones):
        ones = jnp.ones((HEAD_DIM, TK), BF16)
        for r in range(tm // TK):
            for k, v in enumerate(halves):
                t = v[r * TK:(r + 1) * TK, :].T.astype(BF16)
                if with_ones:
                    for hh in range(2):
                        row = base + (2 * k + hh) * LANES
                        vt_ref[0, r, row:row + HEAD_DIM, :] = t[hh * HEAD_DIM:(hh + 1) * HEAD_DIM, :]
                        vt_ref[0, r, row + HEAD_DIM:row + LANES, :] = ones
                else:
                    vt_ref[0, r, base + k * LANES:base + (k + 1) * LANES, :] = t

    put(f_ref, F_QA, normrope_chunk(0, 0), F32)
    put(f_ref, F_QB, normrope_chunk(1, 2), F32)
    put(f_ref, F_QC, chunk(2), F32)
    put(f_ref, F_QC + MXU_N, chunk(3), F32)

    ka = normrope_chunk(4, 1)
    put(h_ref, H_KA, ka, BF16)
    rows = lax.broadcasted_iota(jnp.int32, (8, LANES), 0)
    for k, v in enumerate(ka):
        km = jnp.zeros((8, LANES), F32)
        for r in range(tm // MOBA_BLOCK):
            s = jnp.sum(v[r * MOBA_BLOCK:(r + 1) * MOBA_BLOCK, :], axis=0, keepdims=True) * (1.0 / MOBA_BLOCK)
            km = jnp.where(rows == r, s, km)
        km_ref[0, :, k * LANES:(k + 1) * LANES] = km

    put_vt(VT_A, chunk(5), True)
    put(h_ref, H_KB, normrope_chunk(6, 3), BF16)
    put_vt(VT_B, chunk(7), True)
    put(h_ref, H_KC, chunk(8), BF16)
    put(h_ref, H_KC + MXU_N, chunk(9), BF16)
    put_vt(VT_C, chunk(10), False)
    put_vt(VT_C + MXU_N, chunk(11), False)
    put(h_ref, H_QI, [rope(v) for v in chunk(12)], BF16)
    put(h_ref, H_QI + MXU_N, [rope(v) for v in chunk(13)], BF16)
    ki, wi = chunk(14)
    h_ref[:, H_KI:H_KI + LANES] = rope(ki).astype(BF16)
    f_ref[:, F_WI:F_WI + LANES] = wi


def _proj_call(x2, ng, sh, sc, w_perm, qkg, cos_t, s1_t, s2_t, bd, batch, seq):
    m, d = x2.shape
    tm = PROJ_TM
    nt = seq // tm
    bidx = lambda i: ((i * tm) // seq, 0, 0)
    tab = pl.BlockSpec((tm, LANES), lambda i: (i % nt, 0))
    return pl.pallas_call(
        _proj_kernel,
        grid=(m // tm,),
        in_specs=[pl.BlockSpec((tm, d), lambda i: (i, 0)),
                  pl.BlockSpec((1, d), lambda i: (0, 0)),
                  pl.BlockSpec((1, 1, d), bidx),
                  pl.BlockSpec((1, 1, d), bidx),
                  pl.BlockSpec((d, W_PERM_WIDTH), lambda i: (0, 0)),
                  pl.BlockSpec((8, LANES), lambda i: (0, 0)),
                  tab, tab, tab,
                  pl.BlockSpec((LANES, LANES), lambda i: (0, 0))],
        out_specs=[pl.BlockSpec((tm, F_WIDTH), lambda i: (i, 0)),
                   pl.BlockSpec((tm, H_WIDTH), lambda i: (i, 0)),
                   pl.BlockSpec((1, 8, 2 * LANES), lambda i: (i, 0, 0)),
                   pl.BlockSpec((1, tm // TK, VT_ROWS, TK), lambda i: (i // nt, i % nt, 0, 0))],
        out_shape=[jax.ShapeDtypeStruct((m, F_WIDTH), F32),
                   jax.ShapeDtypeStruct((m, H_WIDTH), BF16),
                   jax.ShapeDtypeStruct((m // tm, 8, 2 * LANES), F32),
                   jax.ShapeDtypeStruct((batch, seq // TK, VT_ROWS, TK), BF16)],
        scratch_shapes=[pltpu.VMEM((tm, d), BF16)],
        compiler_params=_cparams(("parallel",)),
        name="in_proj",
    )(x2, ng, sh, sc, w_perm, qkg, cos_t, s1_t, s2_t, bd)


def _head_masks():
    lane = lax.broadcasted_iota(jnp.int32, (1, LANES), 1)
    return [(lane >= HEAD_DIM * h) & (lane < HEAD_DIM * (h + 1)) for h in range(2)]


def _head_rms_t(o):
    ss = jnp.sum(o * o, axis=0, keepdims=True) * (1.0 / HEAD_DIM)
    return o * lax.rsqrt(ss + RMS_EPS)


def _finish_softmax_head(acc):
    return _head_rms_t(acc[:HEAD_DIM, :] / acc[HEAD_DIM:HEAD_DIM + 1, :])


def _moba_kernel(q_ref, km_ref, k_ref, vt_ref, o_ref, bias_ref, qs_ref, acc_ref):
    i = pl.program_id(2)
    nb = km_ref.shape[1]
    q = q_ref[...]
    kma, kmb, _ = _split3(km_ref[0])
    blk = lax.broadcasted_iota(jnp.int32, (nb, TQ), 0).astype(F32)
    i_f = i.astype(F32)
    krow = lax.broadcasted_iota(jnp.int32, (TK, TQ), 0)
    qcol = lax.broadcasted_iota(jnp.int32, (TK, TQ), 1)
    causal = krow <= qcol

    for h, hm in enumerate(_head_masks()):
        qh_t = jnp.where(hm, q, 0.0).T
        qa, qb, _ = _split3(qh_t)
        gate = _dot(kma, qa) + (_dot(kmb, qa) + _dot(kma, qb))
        gate = jnp.where(blk < i_f, gate, NEG)
        sel = jnp.zeros((nb, TQ), F32)
        for _ in range(MOBA_TOPK):
            mx = jnp.max(gate, axis=0, keepdims=True)
            idx = jnp.min(jnp.where(gate == mx, blk, float(nb)), axis=0, keepdims=True)
            pick = blk == idx
            sel = jnp.where(pick, 1.0, sel)
            gate = jnp.where(pick, -jnp.inf, gate)
        bias_ref[h] = jnp.where((sel > 0.5) & (blk < i_f), 0.0, NEG)
        qs_ref[h] = (qh_t * ATT_SCALE).astype(BF16)

    own = pl.multiple_of(i * TK, TK)
    k_own = k_ref[pl.ds(own, TK), :]
    m_init = []
    for h in range(2):
        s = jnp.where(causal, _dot(k_own, qs_ref[h]), NEG)
        m0 = jnp.max(s, axis=0, keepdims=True)
        p = jnp.exp(s - m0)
        acc_ref[h] = _dot(vt_ref[0, i, h * LANES:(h + 1) * LANES, :], p.astype(BF16))
        m_init.append(m0)

    heads = range(2)
    grp = range(MOBA_GROUP)
    n_groups = (i + MOBA_GROUP - 1) // MOBA_GROUP
    last_group = km_ref.shape[1] // MOBA_GROUP - 1

    def scores(jg):
        off = pl.multiple_of(jnp.minimum(jg, last_group) * (MOBA_GROUP * TK), MOBA_GROUP * TK)
        kb = k_ref[pl.ds(off, MOBA_GROUP * TK), :]
        return tuple(_dot(kb, qs_ref[h]) for h in heads)

    def body(jg, carry):
        ms, ss = carry
        ss_next = scores(jg + 1)
        base = jg * MOBA_GROUP
        sg = [[ss[h][g * TK:(g + 1) * TK, :] for g in grp] for h in heads]
        bs = [[bias_ref[h, pl.ds(base + g, 1), :] for g in grp] for h in heads]
        new = []
        for h in heads:
            m_new = ms[h]
            for g in grp:
                m_new = jnp.maximum(m_new, jnp.max(sg[h][g], axis=0, keepdims=True) + bs[h][g])
            new.append(m_new)
        ps = [[jnp.exp(sg[h][g] - (new[h] - bs[h][g])).astype(BF16) for g in grp] for h in heads]
        for h in heads:
            pv = None
            for g in grp:
                d = _dot(vt_ref[0, base + g, h * LANES:(h + 1) * LANES, :], ps[h][g])
                pv = d if pv is None else pv + d
            acc_ref[h] = jnp.exp(ms[h] - new[h]) * acc_ref[h] + pv
        return tuple(new), ss_next

    lax.fori_loop(0, n_groups, body, (tuple(m_init), scores(0)))
    out_t = jnp.concatenate([_finish_softmax_head(acc_ref[h]) for h in range(2)], axis=0)
    o_ref[...] = out_t.T


def _moba_call(f_arr, h_arr, kmean, vt, batch, seq):
    nq = seq // TQ
    nb = seq // MOBA_BLOCK
    npair = N_HEADS_A // 2
    return pl.pallas_call(
        _moba_kernel,
        grid=(batch, npair, nq),
        in_specs=[pl.BlockSpec((TQ, LANES), lambda b, p, i: (b * nq + i, F_QA // LANES + p)),
                  pl.BlockSpec((1, nb, LANES), lambda b, p, i: (b, 0, p)),
                  pl.BlockSpec((seq, LANES), lambda b, p, i: (b, H_KA // LANES + p)),
                  pl.BlockSpec((1, seq // TK, 2 * LANES, TK), lambda b, p, i: (b, 0, VT_A // (2 * LANES) + p, 0))],
        out_specs=pl.BlockSpec((TQ, LANES), lambda b, p, i: (b * nq + i, p)),
        out_shape=jax.ShapeDtypeStruct((batch * seq, N_HEADS_A * HEAD_DIM), F32),
        scratch_shapes=[pltpu.VMEM((2, nb, TQ), F32),
                        pltpu.VMEM((2, LANES, TQ), BF16),
                        pltpu.VMEM((2, LANES, TQ), F32)],
        compiler_params=_cparams(("parallel", "parallel", "arbitrary")),
        name="moba_attn",
    )(f_arr, kmean, h_arr, vt)


def _sb_kernel(q_ref, k_ref, vt_ref, u_ref, o_ref, qs_ref, acc_ref):
    i = pl.program_id(2)
    krow = lax.broadcasted_iota(jnp.int32, (TK, TQ), 0)
    qcol = lax.broadcasted_iota(jnp.int32, (TK, TQ), 1)
    strict = krow < qcol
    hms = _head_masks()
    for h in range(SB_HEADS):
        pr = slice((h // 2) * LANES, (h // 2 + 1) * LANES)
        qs_ref[h] = (jnp.where(hms[h % 2], q_ref[:, pr], 0.0) * ATT_SCALE).T.astype(BF16)
        acc_ref[h] = jnp.zeros((LANES, TQ), F32)

    heads = range(SB_HEADS)
    pairs = [slice((h // 2) * LANES, (h // 2 + 1) * LANES) for h in heads]

    def logits(j):
        off = pl.multiple_of(jnp.maximum(j, 0) * TK, TK)
        return tuple(_dot(k_ref[pl.ds(off, TK), pairs[h]], qs_ref[h]) for h in heads)

    def block(j, rs, zs, diag):
        zs_next = logits(j - 1)
        u = u_ref[...]
        log_betas, his, los = [], [], []
        for h in heads:
            z = zs[h]
            log_beta = jnp.minimum(z, 0.0) - jnp.log(1.0 + jnp.exp(-jnp.abs(z)))
            log_1m = log_beta - z
            if diag:
                log_1m = jnp.where(strict, log_1m, 0.0)
            hi = log_1m.astype(BF16)
            log_betas.append(log_beta)
            his.append(hi)
            los.append((log_1m - hi.astype(F32)).astype(BF16))
        sums = [_dot(u, his[h]) + _dot(u, los[h]) for h in heads]
        weights = []
        for h in heads:
            after = sums[h][:TK, :] + rs[h]
            a = jnp.exp(log_betas[h] + after)
            if diag:
                a = jnp.where(strict, a, 0.0)
            weights.append(a.astype(BF16))
        for h in heads:
            acc_ref[h] += _dot(vt_ref[0, j, pairs[h], :], weights[h])
        return tuple(rs[h] + sums[h][TK:TK + 1, :] for h in heads), zs_next

    zero = jnp.zeros((1, TQ), F32)
    rs, zs = block(i, (zero,) * SB_HEADS, logits(i), True)

    def alive(rs):
        m = rs[0]
        for r in rs[1:]:
            m = jnp.maximum(m, r)
        return jnp.max(m)

    def cond(c):
        return (c[0] >= 0) & (c[1] > SB_DEAD_LOG)

    def body(c):
        rs, zs = block(c[0], c[2], c[3], False)
        return c[0] - 1, alive(rs), rs, zs

    lax.while_loop(cond, body, (i - 1, alive(rs), rs, zs))
    out_t = jnp.concatenate(
        [_head_rms_t(acc_ref[h][(h % 2) * HEAD_DIM:(h % 2 + 1) * HEAD_DIM, :]) for h in range(SB_HEADS)], axis=0)
    o_ref[...] = out_t.T


def _sb_call(f_arr, h_arr, vt, u_mat, batch, seq):
    nq = seq // TQ
    w = SB_HEADS * HEAD_DIM
    ngrp = N_HEADS_C // SB_HEADS
    return pl.pallas_call(
        _sb_kernel,
        grid=(batch, ngrp, nq),
        in_specs=[pl.BlockSpec((TQ, w), lambda b, p, i: (b * nq + i, F_QC // w + p)),
                  pl.BlockSpec((seq, w), lambda b, p, i: (b, H_KC // w + p)),
                  pl.BlockSpec((1, seq // TK, w, TK), lambda b, p, i: (b, 0, VT_C // w + p, 0)),
                  pl.BlockSpec((TK + SB_TAIL, TK), lambda b, p, i: (0, 0))],
        out_specs=pl.BlockSpec((TQ, w), lambda b, p, i: (b * nq + i, p)),
        out_shape=jax.ShapeDtypeStruct((batch * seq, N_HEADS_C * HEAD_DIM), F32),
        scratch_shapes=[pltpu.VMEM((SB_HEADS, LANES, TQ), BF16), pltpu.VMEM((SB_HEADS, LANES, TQ), F32)],
        compiler_params=_cparams(("parallel", "parallel", "arbitrary")),
        name="stickbreak_attn",
    )(f_arr, h_arr, vt, u_mat)


def _sortable_key(x):
    bits = lax.bitcast_convert_type(x, jnp.int32)
    return jnp.where(bits < 0, bits ^ jnp.int32(0x7FFFFFFF), bits)


_NEG_BITS = int(np.array(NEG, np.float32).view(np.int32))
NEG_KEY = _NEG_BITS ^ 0x7FFFFFFF
INT_MIN = -2 ** 31


def _dsa_kernel(qi_ref, wi_ref, ki_ref, q_ref, k_ref, vt_ref, o_ref,
                key_ref, coarse_ref, qx_ref, qs_ref, acc_ref):
    i = pl.program_id(1)
    seq = k_ref.shape[0]
    topk = min(DSA_TOPK, seq // 4)
    nh = N_HEADS_B
    lane = lax.broadcasted_iota(jnp.int32, (1, LANES), 1)
    low = lane < HEAD_DIM
    krow = lax.broadcasted_iota(jnp.int32, (TK, TQ), 0)
    qcol = lax.broadcasted_iota(jnp.int32, (TK, TQ), 1)
    diag_causal = krow <= qcol
    hms = _head_masks()

    qi = qi_ref[...].astype(F32)
    for h in range(IDX_HEADS):
        qp = qi[:, (h // 2) * LANES:(h // 2 + 1) * LANES]
        if h % 2:
            qp = pltpu.roll(qp, HEAD_DIM, 1)
        qx_ref[h] = jnp.where(low, qp, 0.0).T.astype(BF16)
    w_t = wi_ref[...].T
    q = q_ref[...]
    for h in range(nh):
        qh = jnp.where(hms[h % 2], q[:, (h // 2) * LANES:(h // 2 + 1) * LANES], 0.0)
        qs_ref[h] = (qh * ATT_SCALE).T.astype(BF16)

    def score_block(j, diag):
        off = pl.multiple_of(j * TK, TK)
        kz = ki_ref[pl.ds(off, TK), :]
        sc = jnp.zeros((TK, TQ), F32)
        for h in range(IDX_HEADS):
            sc = sc + w_t[h:h + 1, :] * jnp.maximum(_dot(kz, qx_ref[h]), 0.0)
        if diag:
            sc = jnp.where(diag_causal, sc, NEG)
        bits = lax.bitcast_convert_type(sc, jnp.int32)
        key_ref[j] = jnp.where(bits < 0, bits ^ jnp.int32(0x7FFFFFFF), bits)
        coarse_ref[j] = lax.bitcast_convert_type(bits & jnp.int32(-65536), F32).astype(BF16)
        return jnp.max(sc, axis=0, keepdims=True)

    row_max = lax.fori_loop(0, i, lambda j, mx: jnp.maximum(mx, score_block(j, False)),
                            jnp.full((1, TQ), NEG, F32))
    row_max = jnp.maximum(row_max, score_block(i, True))
    for g in range(1, DSA_GROUP):
        @pl.when(i % DSA_GROUP + g < DSA_GROUP)
        def _():
            key_ref[i + g] = jnp.full((TK, TQ), NEG_KEY, jnp.int32)

    n_unscanned = (seq - (i + 1) * TK).astype(F32)
    fold = TK // 4

    def count(pred):
        def body(j, acc):
            hit = pred(key_ref[j], j)
            for r in range(TK // fold):
                acc = jnp.where(hit[r * fold:(r + 1) * fold, :], acc + 1.0, acc)
            return acc
        acc = lax.fori_loop(0, i + 1, body, jnp.zeros((fold, TQ), F32))
        return jnp.sum(acc, axis=0, keepdims=True)

    def count_ge(cand):
        return count(lambda key, j: key >= cand) + jnp.where(cand <= NEG_KEY, n_unscanned, 0.0)

    kf = float(topk)

    def count_ge_coarse(cand):
        cand_b = lax.bitcast_convert_type(cand, F32).astype(BF16)
        one, zero = jnp.ones((), BF16), jnp.zeros((), BF16)

        def body(j, acc):
            hit = jnp.where(coarse_ref[j] >= cand_b, one, zero)
            return acc + ((hit[:fold] + hit[fold:2 * fold]) + (hit[2 * fold:3 * fold] + hit[3 * fold:]))
        acc = lax.fori_loop(0, i + 1, body, jnp.zeros((fold, TQ), BF16))
        return jnp.sum(acc.astype(F32), axis=0, keepdims=True)

    def probe(cand, state, counter):
        lo, c_lo, hi = state
        inside = (cand > lo) & (cand < hi)
        c = counter(cand)
        up = inside & (c >= kf)
        down = inside & (c < kf)
        return jnp.where(up, cand, lo), jnp.where(up, c, c_lo), jnp.where(down, cand, hi)

    def midpoint(state):
        lo, _, hi = state
        return lo + lax.shift_right_logical(hi - lo, 1)

    def pending(state):
        lo, c_lo, hi = state
        return jnp.max(jnp.where((c_lo > kf) & (lax.shift_right_logical(hi - lo, 1) > 0), 1.0, 0.0))

    k_max = _sortable_key(row_max)
    exp_bits = k_max & jnp.int32(0x7F800000)
    positive = k_max > 0
    state = (jnp.full((1, TQ), INT_MIN, jnp.int32), jnp.full((1, TQ), float(seq), F32),
             jnp.where(positive, exp_bits + jnp.int32(1 << 23), jnp.maximum(k_max, NEG_KEY) + 1))

    for m in range(BISECT_SCAN):
        cand = exp_bits - jnp.int32(m << 23)
        state = probe(jnp.where(positive & (cand > 0), cand, INT_MIN), state, count_ge_coarse)

    def coarse_step(b, s):
        cand = midpoint(s)
        return probe(jnp.where((cand > 0) & ((cand & 0xFFFF) == 0), cand, INT_MIN), s, count_ge_coarse)

    state = lax.fori_loop(0, BISECT_COARSE, coarse_step, state)
    state = lax.fori_loop(0, BISECT_FIXED, lambda b, s: probe(midpoint(s), s, count_ge), state)
    _, state = lax.while_loop(lambda c: c[0] > 0.5,
                              lambda c: (lambda s: (pending(s), s))(probe(midpoint(c[1]), c[1], count_ge)),
                              (pending(state), state))
    t, c_ge, _ = state

    def tie_limit():
        need = kf - count_ge(t + 1)

        def ties_before(xc):
            return count(lambda key, j: (key == t) & (j * TK + krow < xc))

        def xbit_body(b, x):
            cand = x | jnp.left_shift(jnp.int32(1), 14 - b)
            return jnp.where(ties_before(cand) < need, cand, x)

        return lax.fori_loop(0, 15, xbit_body, jnp.zeros((1, TQ), jnp.int32))

    x = lax.cond(jnp.max(c_ge) > kf, tie_limit, lambda: jnp.full((1, TQ), 2 * seq, jnp.int32))

    heads = range(nh)
    grp = range(DSA_GROUP)
    pairs = [slice((h // 2) * LANES, (h // 2 + 1) * LANES) for h in heads]
    n_full = i // DSA_GROUP

    def scores(jg):
        off = pl.multiple_of(jg * (DSA_GROUP * TK), DSA_GROUP * TK)
        return tuple(_dot(k_ref[pl.ds(off, DSA_GROUP * TK), pairs[h]], qs_ref[h]) for h in heads)

    def attend(jg, carry, causal):
        ms, ss = carry
        ss_next = None if causal else scores(jg + 1)
        base = jg * DSA_GROUP
        biases = []
        for g in grp:
            key = key_ref[base + g]
            pos = (base + g) * TK + krow
            msk = (key > t) | ((key == t) & (pos <= x))
            if causal:
                msk = msk & (pos <= i * TQ + qcol)
            biases.append(jnp.where(msk, 0.0, NEG))
        sg = [[ss[h][g * TK:(g + 1) * TK, :] for g in grp] for h in heads]
        new, ps = {}, {}

        def stage_max(h):
            m_new = ms[h]
            for g in grp:
                m_new = jnp.maximum(m_new, jnp.max(sg[h][g] + biases[g], axis=0, keepdims=True))
            new[h] = m_new

        def stage_exp(h):
            ps[h] = [jnp.exp((sg[h][g] - new[h]) + biases[g]).astype(BF16) for g in grp]

        def stage_pv(h):
            pv = None
            for g in grp:
                d = _dot(vt_ref[0, base + g, h * LANES:(h + 1) * LANES, :], ps[h][g])
                pv = d if pv is None else pv + d
            acc_ref[h] = jnp.exp(ms[h] - new[h]) * acc_ref[h] + pv

        for step in range(nh + 2):
            if step < nh:
                stage_max(step)
            if 0 <= step - 1 < nh:
                stage_exp(step - 1)
            if 0 <= step - 2 < nh:
                stage_pv(step - 2)
        return tuple(new[h] for h in heads), ss_next

    for h in heads:
        acc_ref[h] = jnp.zeros((LANES, TQ), F32)
    carry = lax.fori_loop(0, n_full, lambda jg, c: attend(jg, c, False),
                          (tuple(jnp.full((1, TQ), NEG, F32) for _ in heads), scores(0)))
    attend(n_full, carry, True)

    out_t = jnp.concatenate([_finish_softmax_head(acc_ref[h]) for h in range(nh)], axis=0)
    o_ref[...] = out_t.T


def _dsa_call(f_arr, h_arr, vt, batch, seq):
    nq = seq // TQ
    wb = N_HEADS_B * HEAD_DIM
    return pl.pallas_call(
        _dsa_kernel,
        grid=(batch, nq),
        in_specs=[pl.BlockSpec((TQ, IDX_HEADS * IDX_DIM), lambda b, i: (b * nq + i, H_QI // (IDX_HEADS * IDX_DIM))),
                  pl.BlockSpec((TQ, LANES), lambda b, i: (b * nq + i, F_WI // LANES)),
                  pl.BlockSpec((seq, LANES), lambda b, i: (b, H_KI // LANES)),
                  pl.BlockSpec((TQ, wb), lambda b, i: (b * nq + i, F_QB // wb)),
                  pl.BlockSpec((seq, wb), lambda b, i: (b, H_KB // wb)),
                  pl.BlockSpec((1, seq // TK, N_HEADS_B * LANES, TK), lambda b, i: (b, 0, VT_B // (N_HEADS_B * LANES), 0))],
        out_specs=pl.BlockSpec((TQ, wb), lambda b, i: (b * nq + i, 0)),
        out_shape=jax.ShapeDtypeStruct((batch * seq, wb), F32),
        scratch_shapes=[pltpu.VMEM((seq // TK, TK, TQ), jnp.int32),
                        pltpu.VMEM((seq // TK, TK, TQ), BF16),
                        pltpu.VMEM((IDX_HEADS, LANES, TQ), BF16),
                        pltpu.VMEM((N_HEADS_B, LANES, TQ), BF16),
                        pltpu.VMEM((N_HEADS_B, LANES, TQ), F32)],
        compiler_params=_cparams(("parallel", "arbitrary")),
        name="dsa_attn",
    )(h_arr, f_arr, h_arr, f_arr, h_arr, vt)


def _outproj_kernel(x_ref, oa_ref, ob_ref, oc_ref, og_ref, gt_ref, w_ref, o_ref):
    wa = N_HEADS_A * HEAD_DIM
    wb = wa + N_HEADS_B * HEAD_DIM
    og = og_ref[...]
    y = (_dot((oa_ref[...] * og[:, :wa]).astype(BF16), w_ref[:wa, :])
         + _dot((ob_ref[...] * og[:, wa:wb]).astype(BF16), w_ref[wa:wb, :])
         + _dot((oc_ref[...] * og[:, wb:]).astype(BF16), w_ref[wb:, :]))
    o_ref[...] = x_ref[...] + gt_ref[0] * y


def _outproj_call(x2, oa, ob, oc, og, gt, w_out, seq):
    m, d = x2.shape
    tm = 512
    row = lambda i: (i, 0)
    return pl.pallas_call(
        _outproj_kernel,
        grid=(m // tm,),
        in_specs=[pl.BlockSpec((tm, d), row),
                  pl.BlockSpec((tm, oa.shape[1]), row),
                  pl.BlockSpec((tm, ob.shape[1]), row),
                  pl.BlockSpec((tm, oc.shape[1]), row),
                  pl.BlockSpec((1, d), lambda i: (0, 0)),
                  pl.BlockSpec((1, 1, d), lambda i: ((i * tm) // seq, 0, 0)),
                  pl.BlockSpec((d, d), lambda i: (0, 0))],
        out_specs=pl.BlockSpec((tm, d), row),
        out_shape=jax.ShapeDtypeStruct((m, d), F32),
        compiler_params=_cparams(("parallel",)),
        name="out_proj",
    )(x2, oa, ob, oc, og, gt, w_out)


def _rope_tables(seq):
    pos = jnp.arange(seq, dtype=F32)
    inv = ROPE_THETA ** (-jnp.arange(0, ROPE_DIM, 2, dtype=F32) / ROPE_DIM)
    ang = pos[:, None] * inv[None, :]
    cos, sin = jnp.cos(ang), jnp.sin(ang)
    zeros = jnp.zeros((seq, HEAD_DIM - ROPE_DIM), F32)
    zh = jnp.zeros((seq, ROPE_HALF), F32)
    cos_h = jnp.concatenate([cos, cos, jnp.ones_like(zeros)], axis=1)
    s1_h = jnp.concatenate([-sin, zh, zeros], axis=1)
    s2_h = jnp.concatenate([zh, sin, zeros], axis=1)
    two = lambda t: jnp.concatenate([t, t], axis=1)
    return two(cos_h), two(s1_h), two(s2_h)


def _permute_w_in(w):
    d = w.shape[0]
    a, b, c = N_HEADS_A * HEAD_DIM, N_HEADS_B * HEAD_DIM, N_HEADS_C * HEAD_DIM
    sizes = (a, a, a, b, b, b, IDX_HEADS * IDX_DIM, IDX_DIM, IDX_HEADS, c, c, c)
    offs = np.concatenate([[0], np.cumsum(sizes)])
    qa, ka, va, qb, kb, vb, qi, ki, wi, qc, kc, vc = [w[:, int(offs[k]):int(offs[k + 1])] for k in range(12)]
    z = lambda n: jnp.zeros((d, n), w.dtype)
    out = jnp.concatenate([qa, qb, qc, ka, va, kb, vb, kc, vc, qi,
                           ki, z(LANES - IDX_DIM), wi, z(LANES - IDX_HEADS)], axis=1)
    assert out.shape[1] == W_PERM_WIDTH
    return out.astype(BF16)


def _suffix_sum_matrix():
    s = np.arange(TK + SB_TAIL)[:, None]
    j = np.arange(TK)[None, :]
    return jnp.asarray((j > s) | (s >= TK), BF16)


def kernel(x, c, w_ada, b_ada, norm_g, w_in, qk_g, out_g, w_out, ffn_w1, ffn_w3, ffn_w2):
    batch, seq, d = x.shape
    depth = w_ada.shape[0]
    assert seq % PROJ_TM == 0 and d % LANES == 0 and batch <= 8
    assert (seq // TK) % MOBA_GROUP == 0 and (seq // TK) % DSA_GROUP == 0

    c_pad = jnp.zeros((8, d), F32).at[:batch].set(c)
    mod = _mod_call(c_pad, w_ada, b_ada)
    cos_t, s1_t, s2_t = _rope_tables(seq)
    hd = np.arange(LANES) // HEAD_DIM
    bd = jnp.asarray(hd[:, None] == hd[None, :], BF16)
    u_mat = _suffix_sum_matrix()

    x2 = x.reshape(batch * seq, d)
    for layer in range(depth):
        mods = [mod[layer, :batch, k * d:(k + 1) * d].reshape(batch, 1, d) for k in range(N_MOD)]
        sh1, sc1, g1, sh2, sc2, g2, sh3, sc3, g3 = mods
        ng = norm_g[layer]
        w1 = ffn_w1[layer].astype(BF16)
        w3 = ffn_w3[layer].astype(BF16)
        w2 = ffn_w2[layer].astype(BF16)

        x2 = _ffn_call(x2, ng[0:1], sh1, sc1, g1, w1[0], w3[0], w2[0], seq)

        qkg = jnp.concatenate([jnp.tile(qk_g[layer], (1, 2)), jnp.ones((4, LANES), F32)], axis=0)
        f_arr, h_arr, kmean, vt = _proj_call(x2, ng[1:2], sh2, sc2, _permute_w_in(w_in[layer]), qkg,
                                             cos_t, s1_t, s2_t, bd, batch, seq)
        per_tile = PROJ_TM // MOBA_BLOCK
        kmean = kmean[:, :per_tile, :].reshape(batch, seq // MOBA_BLOCK, N_HEADS_A * HEAD_DIM)
        oa = _moba_call(f_arr, h_arr, kmean, vt, batch, seq)
        ob = _dsa_call(f_arr, h_arr, vt, batch, seq)
        oc = _sb_call(f_arr, h_arr, vt, u_mat, batch, seq)
        x2 = _outproj_call(x2, oa, ob, oc, out_g[layer].reshape(1, d), g2, w_out[layer].astype(BF16), seq)

        x2 = _ffn_call(x2, ng[2:3], sh3, sc3, g3, w1[1], w3[1], w2[1], seq)
    return x2.reshape(batch, seq, d)
```

```python
import jax
import jax.numpy as jnp
import numpy as np
from jax import lax
from jax.experimental import pallas as pl
from jax.experimental.pallas import tpu as pltpu

F32 = jnp.float32
BF16 = jnp.bfloat16

HEAD_DIM = 64
N_HEADS_A = 4
N_HEADS_B = 4
N_HEADS_C = 8
ROPE_DIM = HEAD_DIM // 4
ROPE_HALF = ROPE_DIM // 2
ROPE_THETA = 500000.0
MOBA_BLOCK = 256
MOBA_TOPK = 3
DSA_TOPK = 256
IDX_HEADS = 8
IDX_DIM = 64
N_MOD = 9
RMS_EPS = 1e-6
NEG = -1e30
ATT_SCALE = HEAD_DIM ** -0.5

LANES = 128
MXU_N = 256
TQ = 256
TK = 256
PROJ_TM = 512
VMEM_LIMIT = 56 * 1024 * 1024

SB_DEAD_LOG = -88.0
SB_TAIL = 16
SB_HEADS = 4
MOBA_GROUP = 4
DSA_GROUP = 1
BISECT_COARSE = 15
BISECT_FIXED = 6

F_QA, F_QB, F_QC, F_WI, F_WIDTH = 0, 256, 512, 1024, 1152
H_KA, H_KB, H_KC, H_QI, H_KI, H_WIDTH = 0, 256, 512, 1024, 1536, 1664
W_PERM_WIDTH = 15 * MXU_N
VT_A, VT_B, VT_C, VT_ROWS = 0, 512, 1024, 1536


def _dot(a, b):
    return jnp.dot(a, b, preferred_element_type=F32)


def _dot_nt(a, b):
    return lax.dot_general(a, b, (((1,), (1,)), ((), ())), preferred_element_type=F32)


def _split3(x):
    a = x.astype(BF16)
    r = x - a.astype(F32)
    b = r.astype(BF16)
    c = (r - b.astype(F32)).astype(BF16)
    return a, b, c


def _cparams(sem):
    return pltpu.CompilerParams(dimension_semantics=sem, vmem_limit_bytes=VMEM_LIMIT)


def _mod_kernel(c_ref, w_ref, b_ref, o_ref):
    c = c_ref[...]
    sc = c * (1.0 / (1.0 + jnp.exp(-c)))
    a, b, c3 = _split3(sc)
    w = w_ref[0]
    wa, wb, wc = _split3(w)
    acc = _dot(a, wa) + (_dot(a, wb) + _dot(b, wa)) + (_dot(a, wc) + _dot(b, wb) + _dot(c3, wa))
    o_ref[0] = acc + b_ref[0]


def _mod_call(c_pad, w_ada, b_ada):
    depth, d, n = w_ada.shape
    tn = 1024
    rows = c_pad.shape[0]
    return pl.pallas_call(
        _mod_kernel,
        grid=(depth, n // tn),
        in_specs=[pl.BlockSpec((rows, d), lambda l, j: (0, 0)),
                  pl.BlockSpec((1, d, tn), lambda l, j: (l, 0, j)),
                  pl.BlockSpec((1, 1, tn), lambda l, j: (l, 0, j))],
        out_specs=pl.BlockSpec((1, rows, tn), lambda l, j: (l, 0, j)),
        out_shape=jax.ShapeDtypeStruct((depth, rows, n), F32),
        compiler_params=_cparams(("parallel", "parallel")),
        name="adaln_mod",
    )(c_pad, w_ada, b_ada.reshape(depth, 1, n))


def _norm_modulate(x, ng, sh, sc):
    ms = jnp.mean(x * x, axis=-1, keepdims=True)
    h = x * lax.rsqrt(ms + RMS_EPS) * ng
    return h * (1.0 + sc) + sh


def _ffn_kernel(x_ref, ng_ref, sh_ref, sc_ref, gt_ref, w1_ref, w3_ref, w2_ref, o_ref, h_ref, acc_ref):
    j = pl.program_id(1)

    @pl.when(j == 0)
    def _():
        h = _norm_modulate(x_ref[...], ng_ref[...], sh_ref[0], sc_ref[0])
        h_ref[...] = h.astype(BF16)
        acc_ref[...] = jnp.zeros_like(acc_ref)

    h = h_ref[...]
    a = _dot(h, w1_ref[...])
    b = _dot(h, w3_ref[...])
    u = (a * (1.0 / (1.0 + jnp.exp(-a))) * b).astype(BF16)
    acc_ref[...] += _dot(u, w2_ref[...])

    @pl.when(j == pl.num_programs(1) - 1)
    def _():
        o_ref[...] = x_ref[...] + 0.5 * gt_ref[0] * acc_ref[...]


def _ffn_call(x2, ng, sh, sc, gt, w1, w3, w2, seq):
    m, d = x2.shape
    dff = w1.shape[1]
    tm = 1024
    tf = dff // 2 if (dff // 2) % LANES == 0 else MXU_N
    bidx = lambda i, j: ((i * tm) // seq, 0, 0)
    return pl.pallas_call(
        _ffn_kernel,
        grid=(m // tm, dff // tf),
        in_specs=[pl.BlockSpec((tm, d), lambda i, j: (i, 0)),
                  pl.BlockSpec((1, d), lambda i, j: (0, 0)),
                  pl.BlockSpec((1, 1, d), bidx),
                  pl.BlockSpec((1, 1, d), bidx),
                  pl.BlockSpec((1, 1, d), bidx),
                  pl.BlockSpec((d, tf), lambda i, j: (0, j)),
                  pl.BlockSpec((d, tf), lambda i, j: (0, j)),
                  pl.BlockSpec((tf, d), lambda i, j: (j, 0))],
        out_specs=pl.BlockSpec((tm, d), lambda i, j: (i, 0)),
        out_shape=jax.ShapeDtypeStruct((m, d), F32),
        scratch_shapes=[pltpu.VMEM((tm, d), BF16), pltpu.VMEM((tm, d), F32)],
        compiler_params=_cparams(("parallel", "arbitrary")),
        name="swiglu_ffn",
    )(x2, ng, sh, sc, gt, w1, w3, w2)


def _proj_kernel(x_ref, ng_ref, sh_ref, sc_ref, w_ref, qkg_ref, cos_ref, s1_ref, s2_ref, bd_ref,
                 f_ref, h_ref, km_ref, vt_ref, hs_ref):
    hs_ref[...] = _norm_modulate(x_ref[...], ng_ref[...], sh_ref[0], sc_ref[0]).astype(BF16)
    cos, s1, s2 = cos_ref[...], s1_ref[...], s2_ref[...]
    bd = bd_ref[...]
    tm = x_ref.shape[0]

    def rope(v):
        return v * cos + pltpu.roll(v, LANES - ROPE_HALF, 1) * s1 + pltpu.roll(v, ROPE_HALF, 1) * s2

    def headnorm(v, g):
        a, b, c = _split3(v * v)
        ss = _dot(a, bd) + _dot(b, bd) + _dot(c, bd)
        return v * lax.rsqrt(ss * (1.0 / HEAD_DIM) + RMS_EPS) * g

    def chunk(c):
        y = _dot(hs_ref[...], w_ref[:, c * MXU_N:(c + 1) * MXU_N])
        return y[:, :LANES], y[:, LANES:]

    def normrope_chunk(c, grow):
        g = qkg_ref[grow:grow + 1, :]
        return [rope(headnorm(v, g)) for v in chunk(c)]

    def put(ref, off, halves, dtype):
        for k, v in enumerate(halves):
            ref[:, off + k * LANES: off + (k + 1) * LANES] = v.astype(dtype)

    def put_vt(base, halves, with_ones):
        ones = jnp.ones((HEAD_DIM, TK), BF16)
        for r in range(tm // TK):
            for k, v in enumerate(halves):
                t = v[r * TK:(r + 1) * TK, :].T.astype(BF16)
                if with_ones:
                    for hh in range(2):
                        row = base + (2 * k + hh) * LANES
                        vt_ref[0, r, row:row + HEAD_DIM, :] = t[hh * HEAD_DIM:(hh + 1) * HEAD_DIM, :]
                        vt_ref[0, r, row + HEAD_DIM:row + LANES, :] = ones
                else:
                    vt_ref[0, r, base + k * LANES:base + (k + 1) * LANES, :] = t

    put(f_ref, F_QA, normrope_chunk(0, 0), F32)
    put(f_ref, F_QB, normrope_chunk(1, 2), F32)
    put(f_ref, F_QC, chunk(2), F32)
    put(f_ref, F_QC + MXU_N, chunk(3), F32)

    ka = normrope_chunk(4, 1)
    put(h_ref, H_KA, ka, BF16)
    rows = lax.broadcasted_iota(jnp.int32, (8, LANES), 0)
    for k, v in enumerate(ka):
        km = jnp.zeros((8, LANES), F32)
        for r in range(tm // MOBA_BLOCK):
            s = jnp.sum(v[r * MOBA_BLOCK:(r + 1) * MOBA_BLOCK, :], axis=0, keepdims=True) * (1.0 / MOBA_BLOCK)
            km = jnp.where(rows == r, s, km)
        km_ref[0, :, k * LANES:(k + 1) * LANES] = km

    put_vt(VT_A, chunk(5), True)
    put(h_ref, H_KB, normrope_chunk(6, 3), BF16)
    put_vt(VT_B, chunk(7), True)
    put(h_ref, H_KC, chunk(8), BF16)
    put(h_ref, H_KC + MXU_N, chunk(9), BF16)
    put_vt(VT_C, chunk(10), False)
    put_vt(VT_C + MXU_N, chunk(11), False)
    put(h_ref, H_QI, [rope(v) for v in chunk(12)], BF16)
    put(h_ref, H_QI + MXU_N, [rope(v) for v in chunk(13)], BF16)
    ki, wi = chunk(14)
    h_ref[:, H_KI:H_KI + LANES] = rope(ki).astype(BF16)
    f_ref[:, F_WI:F_WI + LANES] = wi


def _proj_call(x2, ng, sh, sc, w_perm, qkg, cos_t, s1_t, s2_t, bd, batch, seq):
    m, d = x2.shape
    tm = PROJ_TM
    nt = seq // tm
    bidx = lambda i: ((i * tm) // seq, 0, 0)
    tab = pl.BlockSpec((tm, LANES), lambda i: (i % nt, 0))
    return pl.pallas_call(
        _proj_kernel,
        grid=(m // tm,),
        in_specs=[pl.BlockSpec((tm, d), lambda i: (i, 0)),
                  pl.BlockSpec((1, d), lambda i: (0, 0)),
                  pl.BlockSpec((1, 1, d), bidx),
                  pl.BlockSpec((1, 1, d), bidx),
                  pl.BlockSpec((d, W_PERM_WIDTH), lambda i: (0, 0)),
                  pl.BlockSpec((8, LANES), lambda i: (0, 0)),
                  tab, tab, tab,
                  pl.BlockSpec((LANES, LANES), lambda i: (0, 0))],
        out_specs=[pl.BlockSpec((tm, F_WIDTH), lambda i: (i, 0)),
                   pl.BlockSpec((tm, H_WIDTH), lambda i: (i, 0)),
                   pl.BlockSpec((1, 8, 2 * LANES), lambda i: (i, 0, 0)),
                   pl.BlockSpec((1, tm // TK, VT_ROWS, TK), lambda i: (i // nt, i % nt, 0, 0))],
        out_shape=[jax.ShapeDtypeStruct((m, F_WIDTH), F32),
                   jax.ShapeDtypeStruct((m, H_WIDTH), BF16),
                   jax.ShapeDtypeStruct((m // tm, 8, 2 * LANES), F32),
                   jax.ShapeDtypeStruct((batch, seq // TK, VT_ROWS, TK), BF16)],
        scratch_shapes=[pltpu.VMEM((tm, d), BF16)],
        compiler_params=_cparams(("parallel",)),
        name="in_proj",
    )(x2, ng, sh, sc, w_perm, qkg, cos_t, s1_t, s2_t, bd)


def _head_masks():
    lane = lax.broadcasted_iota(jnp.int32, (1, LANES), 1)
    return [(lane >= HEAD_DIM * h) & (lane < HEAD_DIM * (h + 1)) for h in range(2)]


def _head_rms_t(o):
    ss = jnp.sum(o * o, axis=0, keepdims=True) * (1.0 / HEAD_DIM)
    return o * lax.rsqrt(ss + RMS_EPS)


def _finish_softmax_head(acc):
    return _head_rms_t(acc[:HEAD_DIM, :] / acc[HEAD_DIM:HEAD_DIM + 1, :])


def _moba_kernel(q_ref, km_ref, k_ref, vt_ref, o_ref, bias_ref, qs_ref, acc_ref):
    i = pl.program_id(2)
    nb = km_ref.shape[1]
    q = q_ref[...]
    kma, kmb, _ = _split3(km_ref[0])
    blk = lax.broadcasted_iota(jnp.int32, (nb, TQ), 0).astype(F32)
    i_f = i.astype(F32)
    krow = lax.broadcasted_iota(jnp.int32, (TK, TQ), 0)
    qcol = lax.broadcasted_iota(jnp.int32, (TK, TQ), 1)
    causal = krow <= qcol

    for h, hm in enumerate(_head_masks()):
        qh_t = jnp.where(hm, q, 0.0).T
        qa, qb, _ = _split3(qh_t)
        gate = _dot(kma, qa) + (_dot(kmb, qa) + _dot(kma, qb))
        gate = jnp.where(blk < i_f, gate, NEG)
        sel = jnp.zeros((nb, TQ), F32)
        for _ in range(MOBA_TOPK):
            mx = jnp.max(gate, axis=0, keepdims=True)
            idx = jnp.min(jnp.where(gate == mx, blk, float(nb)), axis=0, keepdims=True)
            pick = blk == idx
            sel = jnp.where(pick, 1.0, sel)
            gate = jnp.where(pick, -jnp.inf, gate)
        bias_ref[h] = jnp.where((sel > 0.5) & (blk < i_f), 0.0, NEG)
        qs_ref[h] = (qh_t * ATT_SCALE).astype(BF16)

    own = pl.multiple_of(i * TK, TK)
    k_own = k_ref[pl.ds(own, TK), :]
    m_init = []
    for h in range(2):
        s = jnp.where(causal, _dot(k_own, qs_ref[h]), NEG)
        m0 = jnp.max(s, axis=0, keepdims=True)
        p = jnp.exp(s - m0)
        acc_ref[h] = _dot(vt_ref[0, i, h * LANES:(h + 1) * LANES, :], p.astype(BF16))
        m_init.append(m0)

    heads = range(2)
    grp = range(MOBA_GROUP)
    n_groups = (i + MOBA_GROUP - 1) // MOBA_GROUP
    last_group = km_ref.shape[1] // MOBA_GROUP - 1

    def scores(jg):
        off = pl.multiple_of(jnp.minimum(jg, last_group) * (MOBA_GROUP * TK), MOBA_GROUP * TK)
        kb = k_ref[pl.ds(off, MOBA_GROUP * TK), :]
        return tuple(_dot(kb, qs_ref[h]) for h in heads)

    def body(jg, carry):
        ms, ss = carry
        ss_next = scores(jg + 1)
        base = jg * MOBA_GROUP
        sg = [[ss[h][g * TK:(g + 1) * TK, :] for g in grp] for h in heads]
        bs = [[bias_ref[h, pl.ds(base + g, 1), :] for g in grp] for h in heads]
        new = []
        for h in heads:
            m_new = ms[h]
            for g in grp:
                m_new = jnp.maximum(m_new, jnp.max(sg[h][g], axis=0, keepdims=True) + bs[h][g])
            new.append(m_new)
        ps = [[jnp.exp(sg[h][g] - (new[h] - bs[h][g])).astype(BF16) for g in grp] for h in heads]
        for h in heads:
            pv = None
            for g in grp:
                d = _dot(vt_ref[0, base + g, h * LANES:(h + 1) * LANES, :], ps[h][g])
                pv = d if pv is None else pv + d
            acc_ref[h] = jnp.exp(ms[h] - new[h]) * acc_ref[h] + pv
        return tuple(new), ss_next

    lax.fori_loop(0, n_groups, body, (tuple(m_init), scores(0)))
    out_t = jnp.concatenate([_finish_softmax_head(acc_ref[h]) for h in range(2)], axis=0)
    o_ref[...] = out_t.T


def _moba_call(f_arr, h_arr, kmean, vt, batch, seq):
    nq = seq // TQ
    nb = seq // MOBA_BLOCK
    npair = N_HEADS_A // 2
    return pl.pallas_call(
        _moba_kernel,
        grid=(batch, npair, nq),
        in_specs=[pl.BlockSpec((TQ, LANES), lambda b, p, i: (b * nq + i, F_QA // LANES + p)),
                  pl.BlockSpec((1, nb, LANES), lambda b, p, i: (b, 0, p)),
                  pl.BlockSpec((seq, LANES), lambda b, p, i: (b, H_KA // LANES + p)),
                  pl.BlockSpec((1, seq // TK, 2 * LANES, TK), lambda b, p, i: (b, 0, VT_A // (2 * LANES) + p, 0))],
        out_specs=pl.BlockSpec((TQ, LANES), lambda b, p, i: (b * nq + i, p)),
        out_shape=jax.ShapeDtypeStruct((batch * seq, N_HEADS_A * HEAD_DIM), F32),
        scratch_shapes=[pltpu.VMEM((2, nb, TQ), F32),
                        pltpu.VMEM((2, LANES, TQ), BF16),
                        pltpu.VMEM((2, LANES, TQ), F32)],
        compiler_params=_cparams(("parallel", "parallel", "arbitrary")),
        name="moba_attn",
    )(f_arr, kmean, h_arr, vt)


def _sb_kernel(q_ref, k_ref, vt_ref, u_ref, o_ref, qs_ref, acc_ref):
    i = pl.program_id(2)
    krow = lax.broadcasted_iota(jnp.int32, (TK, TQ), 0)
    qcol = lax.broadcasted_iota(jnp.int32, (TK, TQ), 1)
    strict = krow < qcol
    hms = _head_masks()
    for h in range(SB_HEADS):
        pr = slice((h // 2) * LANES, (h // 2 + 1) * LANES)
        qs_ref[h] = (jnp.where(hms[h % 2], q_ref[:, pr], 0.0) * ATT_SCALE).T.astype(BF16)
        acc_ref[h] = jnp.zeros((LANES, TQ), F32)

    heads = range(SB_HEADS)
    pairs = [slice((h // 2) * LANES, (h // 2 + 1) * LANES) for h in heads]

    def logits(j):
        off = pl.multiple_of(jnp.maximum(j, 0) * TK, TK)
        return tuple(_dot(k_ref[pl.ds(off, TK), pairs[h]], qs_ref[h]) for h in heads)

    def block(j, rs, zs, diag):
        zs_next = logits(j - 1)
        u = u_ref[...]
        log_betas, his, los = [], [], []
        for h in heads:
            z = zs[h]
            log_beta = jnp.minimum(z, 0.0) - jnp.log(1.0 + jnp.exp(-jnp.abs(z)))
            log_1m = log_beta - z
            if diag:
                log_1m = jnp.where(strict, log_1m, 0.0)
            hi = log_1m.astype(BF16)
            log_betas.append(log_beta)
            his.append(hi)
            los.append((log_1m - hi.astype(F32)).astype(BF16))
        sums = [_dot(u, his[h]) + _dot(u, los[h]) for h in heads]
        weights = []
        for h in heads:
            after = sums[h][:TK, :] + rs[h]
            a = jnp.exp(log_betas[h] + after)
            if diag:
                a = jnp.where(strict, a, 0.0)
            weights.append(a.astype(BF16))
        for h in heads:
            acc_ref[h] += _dot(vt_ref[0, j, pairs[h], :], weights[h])
        return tuple(rs[h] + sums[h][TK:TK + 1, :] for h in heads), zs_next

    zero = jnp.zeros((1, TQ), F32)
    rs, zs = block(i, (zero,) * SB_HEADS, logits(i), True)

    def alive(rs):
        m = rs[0]
        for r in rs[1:]:
            m = jnp.maximum(m, r)
        return jnp.max(m)

    def cond(c):
        return (c[0] >= 0) & (c[1] > SB_DEAD_LOG)

    def body(c):
        rs, zs = block(c[0], c[2], c[3], False)
        return c[0] - 1, alive(rs), rs, zs

    lax.while_loop(cond, body, (i - 1, alive(rs), rs, zs))
    out_t = jnp.concatenate(
        [_head_rms_t(acc_ref[h][(h % 2) * HEAD_DIM:(h % 2 + 1) * HEAD_DIM, :]) for h in range(SB_HEADS)], axis=0)
    o_ref[...] = out_t.T


def _sb_call(f_arr, h_arr, vt, u_mat, batch, seq):
    nq = seq // TQ
    w = SB_HEADS * HEAD_DIM
    ngrp = N_HEADS_C // SB_HEADS
    return pl.pallas_call(
        _sb_kernel,
        grid=(batch, ngrp, nq),
        in_specs=[pl.BlockSpec((TQ, w), lambda b, p, i: (b * nq + i, F_QC // w + p)),
                  pl.BlockSpec((seq, w), lambda b, p, i: (b, H_KC // w + p)),
                  pl.BlockSpec((1, seq // TK, w, TK), lambda b, p, i: (b, 0, VT_C // w + p, 0)),
                  pl.BlockSpec((TK + SB_TAIL, TK), lambda b, p, i: (0, 0))],
        out_specs=pl.BlockSpec((TQ, w), lambda b, p, i: (b * nq + i, p)),
        out_shape=jax.ShapeDtypeStruct((batch * seq, N_HEADS_C * HEAD_DIM), F32),
        scratch_shapes=[pltpu.VMEM((SB_HEADS, LANES, TQ), BF16), pltpu.VMEM((SB_HEADS, LANES, TQ), F32)],
        compiler_params=_cparams(("parallel", "parallel", "arbitrary")),
        name="stickbreak_attn",
    )(f_arr, h_arr, vt, u_mat)


def _sortable_key(x):
    bits = lax.bitcast_convert_type(x, jnp.int32)
    return jnp.where(bits < 0, bits ^ jnp.int32(0x7FFFFFFF), bits)


_NEG_BITS = int(np.array(NEG, np.float32).view(np.int32))
NEG_KEY = _NEG_BITS ^ 0x7FFFFFFF
INT_MIN = -2 ** 31
MIN_NORMAL_BITS = 0x00800000


def _dsa_kernel(qi_ref, wi_ref, ki_ref, q_ref, k_ref, vt_ref, o_ref,
                key_ref, coarse_ref, qx_ref, qs_ref, acc_ref):
    i = pl.program_id(1)
    seq = k_ref.shape[0]
    topk = min(DSA_TOPK, seq // 4)
    nh = N_HEADS_B
    lane = lax.broadcasted_iota(jnp.int32, (1, LANES), 1)
    low = lane < HEAD_DIM
    krow = lax.broadcasted_iota(jnp.int32, (TK, TQ), 0)
    qcol = lax.broadcasted_iota(jnp.int32, (TK, TQ), 1)
    diag_causal = krow <= qcol
    hms = _head_masks()

    qi = qi_ref[...].astype(F32)
    for h in range(IDX_HEADS):
        qp = qi[:, (h // 2) * LANES:(h // 2 + 1) * LANES]
        if h % 2:
            qp = pltpu.roll(qp, HEAD_DIM, 1)
        qx_ref[h] = jnp.where(low, qp, 0.0).T.astype(BF16)
    w_t = wi_ref[...].T
    q = q_ref[...]
    for h in range(nh):
        qh = jnp.where(hms[h % 2], q[:, (h // 2) * LANES:(h // 2 + 1) * LANES], 0.0)
        qs_ref[h] = (qh * ATT_SCALE).T.astype(BF16)

    def score_block(j, diag):
        off = pl.multiple_of(j * TK, TK)
        kz = ki_ref[pl.ds(off, TK), :]
        sc = jnp.zeros((TK, TQ), F32)
        for h in range(IDX_HEADS):
            sc = sc + w_t[h:h + 1, :] * jnp.maximum(_dot(kz, qx_ref[h]), 0.0)
        if diag:
            sc = jnp.where(diag_causal, sc, NEG)
        bits = lax.bitcast_convert_type(sc, jnp.int32)
        key_ref[j] = jnp.where(bits < 0, bits ^ jnp.int32(0x7FFFFFFF), bits)
        coarse_ref[j] = lax.bitcast_convert_type(bits & jnp.int32(-65536), F32).astype(BF16)

    def score_body(j, c):
        score_block(j, False)
        return c

    lax.fori_loop(0, i, score_body, 0)
    score_block(i, True)
    for g in range(1, DSA_GROUP):
        @pl.when(i % DSA_GROUP + g < DSA_GROUP)
        def _():
            key_ref[i + g] = jnp.full((TK, TQ), NEG_KEY, jnp.int32)

    n_unscanned = (seq - (i + 1) * TK).astype(F32)
    fold = TK // 4

    def count(pred):
        def body(j, acc):
            hit = pred(key_ref[j], j)
            for r in range(TK // fold):
                acc = jnp.where(hit[r * fold:(r + 1) * fold, :], acc + 1.0, acc)
            return acc
        acc = lax.fori_loop(0, i + 1, body, jnp.zeros((fold, TQ), F32))
        return jnp.sum(acc, axis=0, keepdims=True)

    def count_ge(cand):
        return count(lambda key, j: key >= cand) + jnp.where(cand <= NEG_KEY, n_unscanned, 0.0)

    kf = float(topk)

    def count_ge_coarse(cand):
        cand_bits = jnp.where(cand < 0, (cand ^ jnp.int32(0x7FFFFFFF)) & jnp.int32(-65536), cand)
        cand_bits = jnp.where((cand > 0) & (cand < MIN_NORMAL_BITS), MIN_NORMAL_BITS, cand_bits)
        cand_b = lax.bitcast_convert_type(cand_bits, F32).astype(BF16)
        one, zero = jnp.ones((), BF16), jnp.zeros((), BF16)

        def body(j, acc):
            hit = jnp.where(coarse_ref[j] >= cand_b, one, zero)
            return acc + ((hit[:fold] + hit[fold:2 * fold]) + (hit[2 * fold:3 * fold] + hit[3 * fold:]))
        acc = lax.fori_loop(0, i + 1, body, jnp.zeros((fold, TQ), BF16))
        return (jnp.sum(acc.astype(F32), axis=0, keepdims=True)
                + jnp.where(cand <= NEG_KEY, n_unscanned, 0.0))

    c0 = count_ge_coarse(jnp.zeros((1, TQ), jnp.int32))
    t0 = jnp.where(c0 >= kf, 0, INT_MIN).astype(jnp.int32)
    ct0 = jnp.where(c0 >= kf, c0, float(seq))

    def bit_step(b, t, ct, counter):
        cand = t | jnp.left_shift(jnp.int32(1), 30 - b)
        cc = counter(cand)
        ok = cc >= kf
        return jnp.where(ok, cand, t), jnp.where(ok, cc, ct)

    t, ct = lax.fori_loop(0, BISECT_COARSE, lambda b, c: bit_step(b, *c, count_ge_coarse), (t0, ct0))
    t, ct = lax.fori_loop(BISECT_COARSE, BISECT_COARSE + BISECT_FIXED,
                          lambda b, c: bit_step(b, *c, count_ge), (t, ct))

    def bit_body(c):
        t, ct = bit_step(c[0], c[2], c[3], count_ge)
        return c[0] + 1, jnp.max(ct) - kf, t, ct

    _, _, t, c_ge = lax.while_loop(lambda c: (c[0] < 31) & (c[1] > 0.5), bit_body,
                                   (jnp.int32(BISECT_COARSE + BISECT_FIXED), jnp.max(ct) - kf, t, ct))

    def tie_limit():
        need = kf - count_ge(t + 1)

        def ties_before(xc):
            return count(lambda key, j: (key == t) & (j * TK + krow < xc))

        def xbit_body(b, x):
            cand = x | jnp.left_shift(jnp.int32(1), 14 - b)
            return jnp.where(ties_before(cand) < need, cand, x)

        return lax.fori_loop(0, 15, xbit_body, jnp.zeros((1, TQ), jnp.int32))

    x = lax.cond(jnp.max(c_ge) > kf, tie_limit, lambda: jnp.full((1, TQ), 2 * seq, jnp.int32))

    heads = range(nh)
    grp = range(DSA_GROUP)
    pairs = [slice((h // 2) * LANES, (h // 2 + 1) * LANES) for h in heads]
    n_full = i // DSA_GROUP

    def scores(jg):
        off = pl.multiple_of(jg * (DSA_GROUP * TK), DSA_GROUP * TK)
        return tuple(_dot(k_ref[pl.ds(off, DSA_GROUP * TK), pairs[h]], qs_ref[h]) for h in heads)

    def attend(jg, carry, causal):
        ms, ss = carry
        ss_next = None if causal else scores(jg + 1)
        base = jg * DSA_GROUP
        biases = []
        for g in grp:
            key = key_ref[base + g]
            pos = (base + g) * TK + krow
            msk = (key > t) | ((key == t) & (pos <= x))
            if causal:
                msk = msk & (pos <= i * TQ + qcol)
            biases.append(jnp.where(msk, 0.0, NEG))
        sg = [[ss[h][g * TK:(g + 1) * TK, :] for g in grp] for h in heads]
        new, ps = {}, {}

        def stage_max(h):
            m_new = ms[h]
            for g in grp:
                m_new = jnp.maximum(m_new, jnp.max(sg[h][g] + biases[g], axis=0, keepdims=True))
            new[h] = m_new

        def stage_exp(h):
            ps[h] = [jnp.exp((sg[h][g] - new[h]) + biases[g]).astype(BF16) for g in grp]

        def stage_pv(h):
            pv = None
            for g in grp:
                d = _dot(vt_ref[0, base + g, h * LANES:(h + 1) * LANES, :], ps[h][g])
                pv = d if pv is None else pv + d
            acc_ref[h] = jnp.exp(ms[h] - new[h]) * acc_ref[h] + pv

        for step in range(nh + 2):
            if step < nh:
                stage_max(step)
            if 0 <= step - 1 < nh:
                stage_exp(step - 1)
            if 0 <= step - 2 < nh:
                stage_pv(step - 2)
        return tuple(new[h] for h in heads), ss_next

    for h in heads:
        acc_ref[h] = jnp.zeros((LANES, TQ), F32)
    carry = lax.fori_loop(0, n_full, lambda jg, c: attend(jg, c, False),
                          (tuple(jnp.full((1, TQ), NEG, F32) for _ in heads), scores(0)))
    attend(n_full, carry, True)

    out_t = jnp.concatenate([_finish_softmax_head(acc_ref[h]) for h in range(nh)], axis=0)
    o_ref[...] = out_t.T


def _dsa_call(f_arr, h_arr, vt, batch, seq):
    nq = seq // TQ
    wb = N_HEADS_B * HEAD_DIM
    return pl.pallas_call(
        _dsa_kernel,
        grid=(batch, nq),
        in_specs=[pl.BlockSpec((TQ, IDX_HEADS * IDX_DIM), lambda b, i: (b * nq + i, H_QI // (IDX_HEADS * IDX_DIM))),
                  pl.BlockSpec((TQ, LANES), lambda b, i: (b * nq + i, F_WI // LANES)),
                  pl.BlockSpec((seq, LANES), lambda b, i: (b, H_KI // LANES)),
                  pl.BlockSpec((TQ, wb), lambda b, i: (b * nq + i, F_QB // wb)),
                  pl.BlockSpec((seq, wb), lambda b, i: (b, H_KB // wb)),
                  pl.BlockSpec((1, seq // TK, N_HEADS_B * LANES, TK), lambda b, i: (b, 0, VT_B // (N_HEADS_B * LANES), 0))],
        out_specs=pl.BlockSpec((TQ, wb), lambda b, i: (b * nq + i, 0)),
        out_shape=jax.ShapeDtypeStruct((batch * seq, wb), F32),
        scratch_shapes=[pltpu.VMEM((seq // TK, TK, TQ), jnp.int32),
                        pltpu.VMEM((seq // TK, TK, TQ), BF16),
                        pltpu.VMEM((IDX_HEADS, LANES, TQ), BF16),
                        pltpu.VMEM((N_HEADS_B, LANES, TQ), BF16),
                        pltpu.VMEM((N_HEADS_B, LANES, TQ), F32)],
        compiler_params=_cparams(("parallel", "arbitrary")),
        name="dsa_attn",
    )(h_arr, f_arr, h_arr, f_arr, h_arr, vt)


def _outproj_kernel(x_ref, oa_ref, ob_ref, oc_ref, og_ref, gt_ref, w_ref, o_ref):
    wa = N_HEADS_A * HEAD_DIM
    wb = wa + N_HEADS_B * HEAD_DIM
    og = og_ref[...]
    y = (_dot((oa_ref[...] * og[:, :wa]).astype(BF16), w_ref[:wa, :])
         + _dot((ob_ref[...] * og[:, wa:wb]).astype(BF16), w_ref[wa:wb, :])
         + _dot((oc_ref[...] * og[:, wb:]).astype(BF16), w_ref[wb:, :]))
    o_ref[...] = x_ref[...] + gt_ref[0] * y


def _outproj_call(x2, oa, ob, oc, og, gt, w_out, seq):
    m, d = x2.shape
    tm = 512
    row = lambda i: (i, 0)
    return pl.pallas_call(
        _outproj_kernel,
        grid=(m // tm,),
        in_specs=[pl.BlockSpec((tm, d), row),
                  pl.BlockSpec((tm, oa.shape[1]), row),
                  pl.BlockSpec((tm, ob.shape[1]), row),
                  pl.BlockSpec((tm, oc.shape[1]), row),
                  pl.BlockSpec((1, d), lambda i: (0, 0)),
                  pl.BlockSpec((1, 1, d), lambda i: ((i * tm) // seq, 0, 0)),
                  pl.BlockSpec((d, d), lambda i: (0, 0))],
        out_specs=pl.BlockSpec((tm, d), row),
        out_shape=jax.ShapeDtypeStruct((m, d), F32),
        compiler_params=_cparams(("parallel",)),
        name="out_proj",
    )(x2, oa, ob, oc, og, gt, w_out)


def _rope_tables(seq):
    pos = jnp.arange(seq, dtype=F32)
    inv = ROPE_THETA ** (-jnp.arange(0, ROPE_DIM, 2, dtype=F32) / ROPE_DIM)
    ang = pos[:, None] * inv[None, :]
    cos, sin = jnp.cos(ang), jnp.sin(ang)
    zeros = jnp.zeros((seq, HEAD_DIM - ROPE_DIM), F32)
    zh = jnp.zeros((seq, ROPE_HALF), F32)
    cos_h = jnp.concatenate([cos, cos, jnp.ones_like(zeros)], axis=1)
    s1_h = jnp.concatenate([-sin, zh, zeros], axis=1)
    s2_h = jnp.concatenate([zh, sin, zeros], axis=1)
    two = lambda t: jnp.concatenate([t, t], axis=1)
    return two(cos_h), two(s1_h), two(s2_h)


def _permute_w_in(w):
    d = w.shape[0]
    a, b, c = N_HEADS_A * HEAD_DIM, N_HEADS_B * HEAD_DIM, N_HEADS_C * HEAD_DIM
    sizes = (a, a, a, b, b, b, IDX_HEADS * IDX_DIM, IDX_DIM, IDX_HEADS, c, c, c)
    offs = np.concatenate([[0], np.cumsum(sizes)])
    qa, ka, va, qb, kb, vb, qi, ki, wi, qc, kc, vc = [w[:, int(offs[k]):int(offs[k + 1])] for k in range(12)]
    z = lambda n: jnp.zeros((d, n), w.dtype)
    out = jnp.concatenate([qa, qb, qc, ka, va, kb, vb, kc, vc, qi,
                           ki, z(LANES - IDX_DIM), wi, z(LANES - IDX_HEADS)], axis=1)
    assert out.shape[1] == W_PERM_WIDTH
    return out.astype(BF16)


def _suffix_sum_matrix():
    s = np.arange(TK + SB_TAIL)[:, None]
    j = np.arange(TK)[None, :]
    return jnp.asarray((j > s) | (s >= TK), BF16)


def kernel(x, c, w_ada, b_ada, norm_g, w_in, qk_g, out_g, w_out, ffn_w1, ffn_w3, ffn_w2):
    batch, seq, d = x.shape
    depth = w_ada.shape[0]
    assert seq % PROJ_TM == 0 and d % LANES == 0 and batch <= 8
    assert (seq // TK) % MOBA_GROUP == 0 and (seq // TK) % DSA_GROUP == 0

    c_pad = jnp.zeros((8, d), F32).at[:batch].set(c)
    mod = _mod_call(c_pad, w_ada, b_ada)
    cos_t, s1_t, s2_t = _rope_tables(seq)
    hd = np.arange(LANES) // HEAD_DIM
    bd = jnp.asarray(hd[:, None] == hd[None, :], BF16)
    u_mat = _suffix_sum_matrix()

    x2 = x.reshape(batch * seq, d)
    for layer in range(depth):
        mods = [mod[layer, :batch, k * d:(k + 1) * d].reshape(batch, 1, d) for k in range(N_MOD)]
        sh1, sc1, g1, sh2, sc2, g2, sh3, sc3, g3 = mods
        ng = norm_g[layer]
        w1 = ffn_w1[layer].astype(BF16)
        w3 = ffn_w3[layer].astype(BF16)
        w2 = ffn_w2[layer].astype(BF16)

        x2 = _ffn_call(x2, ng[0:1], sh1, sc1, g1, w1[0], w3[0], w2[0], seq)

        qkg = jnp.concatenate([jnp.tile(qk_g[layer], (1, 2)), jnp.ones((4, LANES), F32)], axis=0)
        f_arr, h_arr, kmean, vt = _proj_call(x2, ng[1:2], sh2, sc2, _permute_w_in(w_in[layer]), qkg,
                                             cos_t, s1_t, s2_t, bd, batch, seq)
        per_tile = PROJ_TM // MOBA_BLOCK
        kmean = kmean[:, :per_tile, :].reshape(batch, seq // MOBA_BLOCK, N_HEADS_A * HEAD_DIM)
        oa = _moba_call(f_arr, h_arr, kmean, vt, batch, seq)
        ob = _dsa_call(f_arr, h_arr, vt, batch, seq)
        oc = _sb_call(f_arr, h_arr, vt, u_mat, batch, seq)
        x2 = _outproj_call(x2, oa, ob, oc, out_g[layer].reshape(1, d), g2, w_out[layer].astype(BF16), seq)

        x2 = _ffn_call(x2, ng[2:3], sh3, sc3, g3, w1[1], w3[1], w2[1], seq)
    return x2.reshape(batch, seq, d)
```

```python
import jax
import jax.numpy as jnp
import numpy as np
from jax import lax
from jax.experimental import pallas as pl
from jax.experimental.pallas import tpu as pltpu

F32 = jnp.float32
BF16 = jnp.bfloat16

HEAD_DIM = 64
N_HEADS_A = 4
N_HEADS_B = 4
N_HEADS_C = 8
ROPE_DIM = HEAD_DIM // 4
ROPE_HALF = ROPE_DIM // 2
ROPE_THETA = 500000.0
MOBA_BLOCK = 256
MOBA_TOPK = 3
DSA_TOPK = 256
IDX_HEADS = 8
IDX_DIM = 64
N_MOD = 9
RMS_EPS = 1e-6
NEG = -1e30
ATT_SCALE = HEAD_DIM ** -0.5

LANES = 128
MXU_N = 256
TQ = 256
TK = 256
PROJ_TM = 512
VMEM_LIMIT = 56 * 1024 * 1024

SB_DEAD_LOG = -88.0
SB_TAIL = 16
SB_HEADS = 8
MOBA_GROUP = 4
DSA_GROUP = 1
BISECT_COARSE = 15
BISECT_FIXED = 6

F_QA, F_QB, F_QC, F_WI, F_WIDTH = 0, 256, 512, 1024, 1152
H_KA, H_KB, H_KC, H_QI, H_KI, H_WIDTH = 0, 256, 512, 1024, 1536, 1664
W_PERM_WIDTH = 15 * MXU_N
VT_A, VT_B, VT_C, VT_ROWS = 0, 512, 1024, 1536


def _dot(a, b):
    return jnp.dot(a, b, preferred_element_type=F32)


def _dot_nt(a, b):
    return lax.dot_general(a, b, (((1,), (1,)), ((), ())), preferred_element_type=F32)


def _split3(x):
    a = x.astype(BF16)
    r = x - a.astype(F32)
    b = r.astype(BF16)
    c = (r - b.astype(F32)).astype(BF16)
    return a, b, c


def _cparams(sem):
    return pltpu.CompilerParams(dimension_semantics=sem, vmem_limit_bytes=VMEM_LIMIT)


def _mod_kernel(c_ref, w_ref, b_ref, o_ref):
    c = c_ref[...]
    sc = c * (1.0 / (1.0 + jnp.exp(-c)))
    a, b, c3 = _split3(sc)
    w = w_ref[0]
    wa, wb, wc = _split3(w)
    acc = _dot(a, wa) + (_dot(a, wb) + _dot(b, wa)) + (_dot(a, wc) + _dot(b, wb) + _dot(c3, wa))
    o_ref[0] = acc + b_ref[0]


def _mod_call(c_pad, w_ada, b_ada):
    depth, d, n = w_ada.shape
    tn = 1024
    rows = c_pad.shape[0]
    return pl.pallas_call(
        _mod_kernel,
        grid=(depth, n // tn),
        in_specs=[pl.BlockSpec((rows, d), lambda l, j: (0, 0)),
                  pl.BlockSpec((1, d, tn), lambda l, j: (l, 0, j)),
                  pl.BlockSpec((1, 1, tn), lambda l, j: (l, 0, j))],
        out_specs=pl.BlockSpec((1, rows, tn), lambda l, j: (l, 0, j)),
        out_shape=jax.ShapeDtypeStruct((depth, rows, n), F32),
        compiler_params=_cparams(("parallel", "parallel")),
        name="adaln_mod",
    )(c_pad, w_ada, b_ada.reshape(depth, 1, n))


def _norm_modulate(x, ng, sh, sc):
    ms = jnp.mean(x * x, axis=-1, keepdims=True)
    h = x * lax.rsqrt(ms + RMS_EPS) * ng
    return h * (1.0 + sc) + sh


def _ffn_kernel(x_ref, ng_ref, sh_ref, sc_ref, gt_ref, w1_ref, w3_ref, w2_ref, o_ref, h_ref, acc_ref):
    j = pl.program_id(1)

    @pl.when(j == 0)
    def _():
        h = _norm_modulate(x_ref[...], ng_ref[...], sh_ref[0], sc_ref[0])
        h_ref[...] = h.astype(BF16)
        acc_ref[...] = jnp.zeros_like(acc_ref)

    h = h_ref[...]
    a = _dot(h, w1_ref[...])
    b = _dot(h, w3_ref[...])
    u = (a * (1.0 / (1.0 + jnp.exp(-a))) * b).astype(BF16)
    acc_ref[...] += _dot(u, w2_ref[...])

    @pl.when(j == pl.num_programs(1) - 1)
    def _():
        o_ref[...] = x_ref[...] + 0.5 * gt_ref[0] * acc_ref[...]


def _ffn_call(x2, ng, sh, sc, gt, w1, w3, w2, seq):
    m, d = x2.shape
    dff = w1.shape[1]
    tm = 1024
    tf = dff // 2 if (dff // 2) % LANES == 0 else MXU_N
    bidx = lambda i, j: ((i * tm) // seq, 0, 0)
    return pl.pallas_call(
        _ffn_kernel,
        grid=(m // tm, dff // tf),
        in_specs=[pl.BlockSpec((tm, d), lambda i, j: (i, 0)),
                  pl.BlockSpec((1, d), lambda i, j: (0, 0)),
                  pl.BlockSpec((1, 1, d), bidx),
                  pl.BlockSpec((1, 1, d), bidx),
                  pl.BlockSpec((1, 1, d), bidx),
                  pl.BlockSpec((d, tf), lambda i, j: (0, j)),
                  pl.BlockSpec((d, tf), lambda i, j: (0, j)),
                  pl.BlockSpec((tf, d), lambda i, j: (j, 0))],
        out_specs=pl.BlockSpec((tm, d), lambda i, j: (i, 0)),
        out_shape=jax.ShapeDtypeStruct((m, d), F32),
        scratch_shapes=[pltpu.VMEM((tm, d), BF16), pltpu.VMEM((tm, d), F32)],
        compiler_params=_cparams(("parallel", "arbitrary")),
        name="swiglu_ffn",
    )(x2, ng, sh, sc, gt, w1, w3, w2)


def _proj_kernel(x_ref, ng_ref, sh_ref, sc_ref, w_ref, qkg_ref, cos_ref, s1_ref, s2_ref, bd_ref,
                 f_ref, h_ref, km_ref, vt_ref, hs_ref):
    hs_ref[...] = _norm_modulate(x_ref[...], ng_ref[...], sh_ref[0], sc_ref[0]).astype(BF16)
    cos, s1, s2 = cos_ref[...], s1_ref[...], s2_ref[...]
    bd = bd_ref[...]
    tm = x_ref.shape[0]

    def rope(v):
        return v * cos + pltpu.roll(v, LANES - ROPE_HALF, 1) * s1 + pltpu.roll(v, ROPE_HALF, 1) * s2

    def headnorm(v, g):
        a, b, _ = _split3(v * v)
        ss = _dot(a, bd) + _dot(b, bd)
        return v * lax.rsqrt(ss * (1.0 / HEAD_DIM) + RMS_EPS) * g

    def chunk(c):
        y = _dot(hs_ref[...], w_ref[:, c * MXU_N:(c + 1) * MXU_N])
        return y[:, :LANES], y[:, LANES:]

    def normrope_chunk(c, grow):
        g = qkg_ref[grow:grow + 1, :]
        return [rope(headnorm(v, g)) for v in chunk(c)]

    def put(ref, off, halves, dtype):
        for k, v in enumerate(halves):
            ref[:, off + k * LANES: off + (k + 1) * LANES] = v.astype(dtype)

    def put_vt(base, halves, with_ones):
        ones = jnp.ones((HEAD_DIM, TK), BF16)
        for r in range(tm // TK):
            for k, v in enumerate(halves):
                t = v[r * TK:(r + 1) * TK, :].T.astype(BF16)
                if with_ones:
                    for hh in range(2):
                        row = base + (2 * k + hh) * LANES
                        vt_ref[0, r, row:row + HEAD_DIM, :] = t[hh * HEAD_DIM:(hh + 1) * HEAD_DIM, :]
                        vt_ref[0, r, row + HEAD_DIM:row + LANES, :] = ones
                else:
                    vt_ref[0, r, base + k * LANES:base + (k + 1) * LANES, :] = t

    put(f_ref, F_QA, normrope_chunk(0, 0), F32)
    put(f_ref, F_QB, normrope_chunk(1, 2), F32)
    put(f_ref, F_QC, chunk(2), F32)
    put(f_ref, F_QC + MXU_N, chunk(3), F32)

    ka = normrope_chunk(4, 1)
    put(h_ref, H_KA, ka, BF16)
    rows = lax.broadcasted_iota(jnp.int32, (8, LANES), 0)
    for k, v in enumerate(ka):
        km = jnp.zeros((8, LANES), F32)
        for r in range(tm // MOBA_BLOCK):
            s = jnp.sum(v[r * MOBA_BLOCK:(r + 1) * MOBA_BLOCK, :], axis=0, keepdims=True) * (1.0 / MOBA_BLOCK)
            km = jnp.where(rows == r, s, km)
        km_ref[0, :, k * LANES:(k + 1) * LANES] = km

    put_vt(VT_A, chunk(5), True)
    put(h_ref, H_KB, normrope_chunk(6, 3), BF16)
    put_vt(VT_B, chunk(7), True)
    put(h_ref, H_KC, chunk(8), BF16)
    put(h_ref, H_KC + MXU_N, chunk(9), BF16)
    put_vt(VT_C, chunk(10), False)
    put_vt(VT_C + MXU_N, chunk(11), False)
    put(h_ref, H_QI, [rope(v) for v in chunk(12)], BF16)
    put(h_ref, H_QI + MXU_N, [rope(v) for v in chunk(13)], BF16)
    ki, wi = chunk(14)
    h_ref[:, H_KI:H_KI + LANES] = rope(ki).astype(BF16)
    f_ref[:, F_WI:F_WI + LANES] = wi


def _proj_call(x2, ng, sh, sc, w_perm, qkg, cos_t, s1_t, s2_t, bd, batch, seq):
    m, d = x2.shape
    tm = PROJ_TM
    nt = seq // tm
    bidx = lambda i: ((i * tm) // seq, 0, 0)
    tab = pl.BlockSpec((tm, LANES), lambda i: (i % nt, 0))
    return pl.pallas_call(
        _proj_kernel,
        grid=(m // tm,),
        in_specs=[pl.BlockSpec((tm, d), lambda i: (i, 0)),
                  pl.BlockSpec((1, d), lambda i: (0, 0)),
                  pl.BlockSpec((1, 1, d), bidx),
                  pl.BlockSpec((1, 1, d), bidx),
                  pl.BlockSpec((d, W_PERM_WIDTH), lambda i: (0, 0)),
                  pl.BlockSpec((8, LANES), lambda i: (0, 0)),
                  tab, tab, tab,
                  pl.BlockSpec((LANES, LANES), lambda i: (0, 0))],
        out_specs=[pl.BlockSpec((tm, F_WIDTH), lambda i: (i, 0)),
                   pl.BlockSpec((tm, H_WIDTH), lambda i: (i, 0)),
                   pl.BlockSpec((1, 8, 2 * LANES), lambda i: (i, 0, 0)),
                   pl.BlockSpec((1, tm // TK, VT_ROWS, TK), lambda i: (i // nt, i % nt, 0, 0))],
        out_shape=[jax.ShapeDtypeStruct((m, F_WIDTH), F32),
                   jax.ShapeDtypeStruct((m, H_WIDTH), BF16),
                   jax.ShapeDtypeStruct((m // tm, 8, 2 * LANES), F32),
                   jax.ShapeDtypeStruct((batch, seq // TK, VT_ROWS, TK), BF16)],
        scratch_shapes=[pltpu.VMEM((tm, d), BF16)],
        compiler_params=_cparams(("parallel",)),
        name="in_proj",
    )(x2, ng, sh, sc, w_perm, qkg, cos_t, s1_t, s2_t, bd)


def _head_masks():
    lane = lax.broadcasted_iota(jnp.int32, (1, LANES), 1)
    return [(lane >= HEAD_DIM * h) & (lane < HEAD_DIM * (h + 1)) for h in range(2)]


def _head_rms_t(o):
    ss = jnp.sum(o * o, axis=0, keepdims=True) * (1.0 / HEAD_DIM)
    return o * lax.rsqrt(ss + RMS_EPS)


def _finish_softmax_head(acc):
    return _head_rms_t(acc[:HEAD_DIM, :] / acc[HEAD_DIM:HEAD_DIM + 1, :])


def _moba_kernel(q_ref, km_ref, k_ref, vt_ref, o_ref, bias_ref, qs_ref, acc_ref):
    i = pl.program_id(2)
    nb = km_ref.shape[1]
    q = q_ref[...]
    kma, kmb, _ = _split3(km_ref[0])
    blk = lax.broadcasted_iota(jnp.int32, (nb, TQ), 0).astype(F32)
    i_f = i.astype(F32)
    krow = lax.broadcasted_iota(jnp.int32, (TK, TQ), 0)
    qcol = lax.broadcasted_iota(jnp.int32, (TK, TQ), 1)
    causal = krow <= qcol

    for h, hm in enumerate(_head_masks()):
        qh_t = jnp.where(hm, q, 0.0).T
        qa, qb, _ = _split3(qh_t)
        gate = _dot(kma, qa) + (_dot(kmb, qa) + _dot(kma, qb))
        gate = jnp.where(blk < i_f, gate, NEG)
        sel = jnp.zeros((nb, TQ), F32)
        for _ in range(MOBA_TOPK):
            mx = jnp.max(gate, axis=0, keepdims=True)
            idx = jnp.min(jnp.where(gate == mx, blk, float(nb)), axis=0, keepdims=True)
            pick = blk == idx
            sel = jnp.where(pick, 1.0, sel)
            gate = jnp.where(pick, -jnp.inf, gate)
        bias_ref[h] = jnp.where((sel > 0.5) & (blk < i_f), 0.0, NEG)
        qs_ref[h] = (qh_t * ATT_SCALE).astype(BF16)

    own = pl.multiple_of(i * TK, TK)
    k_own = k_ref[pl.ds(own, TK), :]
    m_init = []
    for h in range(2):
        s = jnp.where(causal, _dot(k_own, qs_ref[h]), NEG)
        m0 = jnp.max(s, axis=0, keepdims=True)
        p = jnp.exp(s - m0)
        acc_ref[h] = _dot(vt_ref[0, i, h * LANES:(h + 1) * LANES, :], p.astype(BF16))
        m_init.append(m0)

    heads = range(2)
    grp = range(MOBA_GROUP)
    n_groups = (i + MOBA_GROUP - 1) // MOBA_GROUP
    last_group = km_ref.shape[1] // MOBA_GROUP - 1

    def scores(jg):
        off = pl.multiple_of(jnp.minimum(jg, last_group) * (MOBA_GROUP * TK), MOBA_GROUP * TK)
        kb = k_ref[pl.ds(off, MOBA_GROUP * TK), :]
        return tuple(_dot(kb, qs_ref[h]) for h in heads)

    def body(jg, carry):
        ms, ss = carry
        ss_next = scores(jg + 1)
        base = jg * MOBA_GROUP
        sg = [[ss[h][g * TK:(g + 1) * TK, :] for g in grp] for h in heads]
        bs = [[bias_ref[h, pl.ds(base + g, 1), :] for g in grp] for h in heads]
        new = []
        for h in heads:
            m_new = ms[h]
            for g in grp:
                m_new = jnp.maximum(m_new, jnp.max(sg[h][g], axis=0, keepdims=True) + bs[h][g])
            new.append(m_new)
        ps = [[jnp.exp(sg[h][g] - (new[h] - bs[h][g])).astype(BF16) for g in grp] for h in heads]
        for h in heads:
            pv = None
            for g in grp:
                d = _dot(vt_ref[0, base + g, h * LANES:(h + 1) * LANES, :], ps[h][g])
                pv = d if pv is None else pv + d
            acc_ref[h] = jnp.exp(ms[h] - new[h]) * acc_ref[h] + pv
        return tuple(new), ss_next

    lax.fori_loop(0, n_groups, body, (tuple(m_init), scores(0)))
    out_t = jnp.concatenate([_finish_softmax_head(acc_ref[h]) for h in range(2)], axis=0)
    o_ref[...] = out_t.T


def _moba_call(f_arr, h_arr, kmean, vt, batch, seq):
    nq = seq // TQ
    nb = seq // MOBA_BLOCK
    npair = N_HEADS_A // 2
    return pl.pallas_call(
        _moba_kernel,
        grid=(batch, npair, nq),
        in_specs=[pl.BlockSpec((TQ, LANES), lambda b, p, i: (b * nq + i, F_QA // LANES + p)),
                  pl.BlockSpec((1, nb, LANES), lambda b, p, i: (b, 0, p)),
                  pl.BlockSpec((seq, LANES), lambda b, p, i: (b, H_KA // LANES + p)),
                  pl.BlockSpec((1, seq // TK, 2 * LANES, TK), lambda b, p, i: (b, 0, VT_A // (2 * LANES) + p, 0))],
        out_specs=pl.BlockSpec((TQ, LANES), lambda b, p, i: (b * nq + i, p)),
        out_shape=jax.ShapeDtypeStruct((batch * seq, N_HEADS_A * HEAD_DIM), F32),
        scratch_shapes=[pltpu.VMEM((2, nb, TQ), F32),
                        pltpu.VMEM((2, LANES, TQ), BF16),
                        pltpu.VMEM((2, LANES, TQ), F32)],
        compiler_params=_cparams(("parallel", "parallel", "arbitrary")),
        name="moba_attn",
    )(f_arr, kmean, h_arr, vt)


def _sb_kernel(q_ref, k_ref, vt_ref, u_ref, o_ref, qs_ref, acc_ref):
    i = pl.program_id(2)
    krow = lax.broadcasted_iota(jnp.int32, (TK, TQ), 0)
    qcol = lax.broadcasted_iota(jnp.int32, (TK, TQ), 1)
    strict = krow < qcol
    hms = _head_masks()
    for h in range(SB_HEADS):
        pr = slice((h // 2) * LANES, (h // 2 + 1) * LANES)
        qs_ref[h] = (jnp.where(hms[h % 2], q_ref[:, pr], 0.0) * ATT_SCALE).T.astype(BF16)
        acc_ref[h] = jnp.zeros((LANES, TQ), F32)

    heads = range(SB_HEADS)
    pairs = [slice((h // 2) * LANES, (h // 2 + 1) * LANES) for h in heads]

    def logits(j):
        off = pl.multiple_of(jnp.maximum(j, 0) * TK, TK)
        return tuple(_dot(k_ref[pl.ds(off, TK), pairs[h]], qs_ref[h]) for h in heads)

    def block(j, rs, zs, diag):
        zs_next = logits(j - 1)
        u = u_ref[...]
        log_betas, his, los = [], [], []
        for h in heads:
            z = zs[h]
            log_beta = jnp.minimum(z, 0.0) - jnp.log(1.0 + jnp.exp(-jnp.abs(z)))
            log_1m = log_beta - z
            if diag:
                log_1m = jnp.where(strict, log_1m, 0.0)
            hi = log_1m.astype(BF16)
            log_betas.append(log_beta)
            his.append(hi)
            los.append((log_1m - hi.astype(F32)).astype(BF16))
        sums = [_dot(u, his[h]) + _dot(u, los[h]) for h in heads]
        weights = []
        for h in heads:
            after = sums[h][:TK, :] + rs[h]
            a = jnp.exp(log_betas[h] + after)
            if diag:
                a = jnp.where(strict, a, 0.0)
            weights.append(a.astype(BF16))
        for h in heads:
            acc_ref[h] += _dot(vt_ref[0, j, pairs[h], :], weights[h])
        return tuple(rs[h] + sums[h][TK:TK + 1, :] for h in heads), zs_next

    zero = jnp.zeros((1, TQ), F32)
    rs, zs = block(i, (zero,) * SB_HEADS, logits(i), True)

    def alive(rs):
        m = rs[0]
        for r in rs[1:]:
            m = jnp.maximum(m, r)
        return jnp.max(m)

    def cond(c):
        return (c[0] >= 0) & (c[1] > SB_DEAD_LOG)

    def body(c):
        rs, zs = block(c[0], c[2], c[3], False)
        return c[0] - 1, alive(rs), rs, zs

    lax.while_loop(cond, body, (i - 1, alive(rs), rs, zs))
    out_t = jnp.concatenate(
        [_head_rms_t(acc_ref[h][(h % 2) * HEAD_DIM:(h % 2 + 1) * HEAD_DIM, :]) for h in range(SB_HEADS)], axis=0)
    o_ref[...] = out_t.T


def _sb_call(f_arr, h_arr, vt, u_mat, batch, seq):
    nq = seq // TQ
    w = SB_HEADS * HEAD_DIM
    ngrp = N_HEADS_C // SB_HEADS
    return pl.pallas_call(
        _sb_kernel,
        grid=(batch, ngrp, nq),
        in_specs=[pl.BlockSpec((TQ, w), lambda b, p, i: (b * nq + i, F_QC // w + p)),
                  pl.BlockSpec((seq, w), lambda b, p, i: (b, H_KC // w + p)),
                  pl.BlockSpec((1, seq // TK, w, TK), lambda b, p, i: (b, 0, VT_C // w + p, 0)),
                  pl.BlockSpec((TK + SB_TAIL, TK), lambda b, p, i: (0, 0))],
        out_specs=pl.BlockSpec((TQ, w), lambda b, p, i: (b * nq + i, p)),
        out_shape=jax.ShapeDtypeStruct((batch * seq, N_HEADS_C * HEAD_DIM), F32),
        scratch_shapes=[pltpu.VMEM((SB_HEADS, LANES, TQ), BF16), pltpu.VMEM((SB_HEADS, LANES, TQ), F32)],
        compiler_params=_cparams(("parallel", "parallel", "arbitrary")),
        name="stickbreak_attn",
    )(f_arr, h_arr, vt, u_mat)


def _sortable_key(x):
    bits = lax.bitcast_convert_type(x, jnp.int32)
    return jnp.where(bits < 0, bits ^ jnp.int32(0x7FFFFFFF), bits)


_NEG_BITS = int(np.array(NEG, np.float32).view(np.int32))
NEG_KEY = _NEG_BITS ^ 0x7FFFFFFF
INT_MIN = -2 ** 31
MIN_NORMAL_BITS = 0x00800000


def _dsa_kernel(qi_ref, wi_ref, ki_ref, q_ref, k_ref, vt_ref, o_ref,
                key_ref, coarse_ref, qx_ref, qs_ref, acc_ref):
    i = pl.program_id(1)
    seq = k_ref.shape[0]
    topk = min(DSA_TOPK, seq // 4)
    nh = N_HEADS_B
    lane = lax.broadcasted_iota(jnp.int32, (1, LANES), 1)
    low = lane < HEAD_DIM
    krow = lax.broadcasted_iota(jnp.int32, (TK, TQ), 0)
    qcol = lax.broadcasted_iota(jnp.int32, (TK, TQ), 1)
    diag_causal = krow <= qcol
    hms = _head_masks()

    qi = qi_ref[...].astype(F32)
    for h in range(IDX_HEADS):
        qp = qi[:, (h // 2) * LANES:(h // 2 + 1) * LANES]
        if h % 2:
            qp = pltpu.roll(qp, HEAD_DIM, 1)
        qx_ref[h] = jnp.where(low, qp, 0.0).T.astype(BF16)
    w_t = wi_ref[...].T
    q = q_ref[...]
    for h in range(nh):
        qh = jnp.where(hms[h % 2], q[:, (h // 2) * LANES:(h // 2 + 1) * LANES], 0.0)
        qs_ref[h] = (qh * ATT_SCALE).T.astype(BF16)

    def score_block(j, diag):
        off = pl.multiple_of(j * TK, TK)
        kz = ki_ref[pl.ds(off, TK), :]
        sc = jnp.zeros((TK, TQ), F32)
        for h in range(IDX_HEADS):
            sc = sc + w_t[h:h + 1, :] * jnp.maximum(_dot(kz, qx_ref[h]), 0.0)
        if diag:
            sc = jnp.where(diag_causal, sc, NEG)
        bits = lax.bitcast_convert_type(sc, jnp.int32)
        key_ref[j] = jnp.where(bits < 0, bits ^ jnp.int32(0x7FFFFFFF), bits)
        coarse_ref[j] = lax.bitcast_convert_type(bits & jnp.int32(-65536), F32).astype(BF16)

    def score_pair(jj, c):
        score_block(2 * jj, False)
        score_block(2 * jj + 1, False)
        return c

    lax.fori_loop(0, i // 2, score_pair, 0)

    @pl.when(i % 2 == 1)
    def _():
        score_block(i - 1, False)

    score_block(i, True)
    for g in range(1, DSA_GROUP):
        @pl.when(i % DSA_GROUP + g < DSA_GROUP)
        def _():
            key_ref[i + g] = jnp.full((TK, TQ), NEG_KEY, jnp.int32)

    n_unscanned = (seq - (i + 1) * TK).astype(F32)
    fold = TK // 4

    def count(pred):
        def body(j, acc):
            hit = pred(key_ref[j], j)
            for r in range(TK // fold):
                acc = jnp.where(hit[r * fold:(r + 1) * fold, :], acc + 1.0, acc)
            return acc
        acc = lax.fori_loop(0, i + 1, body, jnp.zeros((fold, TQ), F32))
        return jnp.sum(acc, axis=0, keepdims=True)

    def count_ge(cand):
        return count(lambda key, j: key >= cand) + jnp.where(cand <= NEG_KEY, n_unscanned, 0.0)

    kf = float(topk)

    def count_ge_coarse(cand):
        cand_bits = jnp.where(cand < 0, (cand ^ jnp.int32(0x7FFFFFFF)) & jnp.int32(-65536), cand)
        cand_bits = jnp.where((cand > 0) & (cand < MIN_NORMAL_BITS), MIN_NORMAL_BITS, cand_bits)
        cand_b = lax.bitcast_convert_type(cand_bits, F32).astype(BF16)
        one, zero = jnp.ones((), BF16), jnp.zeros((), BF16)

        def body(j, acc):
            hit = jnp.where(coarse_ref[j] >= cand_b, one, zero)
            return acc + ((hit[:fold] + hit[fold:2 * fold]) + (hit[2 * fold:3 * fold] + hit[3 * fold:]))
        acc = lax.fori_loop(0, i + 1, body, jnp.zeros((fold, TQ), BF16))
        return (jnp.sum(acc.astype(F32), axis=0, keepdims=True)
                + jnp.where(cand <= NEG_KEY, n_unscanned, 0.0))

    c0 = count_ge_coarse(jnp.zeros((1, TQ), jnp.int32))
    t0 = jnp.where(c0 >= kf, 0, INT_MIN).astype(jnp.int32)
    ct0 = jnp.where(c0 >= kf, c0, float(seq))

    def bit_step(b, t, ct, counter):
        cand = t | jnp.left_shift(jnp.int32(1), 30 - b)
        cc = counter(cand)
        ok = cc >= kf
        return jnp.where(ok, cand, t), jnp.where(ok, cc, ct)

    t, ct = lax.fori_loop(0, BISECT_COARSE, lambda b, c: bit_step(b, *c, count_ge_coarse), (t0, ct0))
    t, ct = lax.fori_loop(BISECT_COARSE, BISECT_COARSE + BISECT_FIXED,
                          lambda b, c: bit_step(b, *c, count_ge), (t, ct))

    def bit_body(c):
        t, ct = bit_step(c[0], c[2], c[3], count_ge)
        t, ct = bit_step(c[0] + 1, t, ct, count_ge)
        return c[0] + 2, jnp.max(ct) - kf, t, ct

    assert (31 - BISECT_COARSE - BISECT_FIXED) % 2 == 0
    _, _, t, c_ge = lax.while_loop(lambda c: (c[0] < 31) & (c[1] > 0.5), bit_body,
                                   (jnp.int32(BISECT_COARSE + BISECT_FIXED), jnp.max(ct) - kf, t, ct))

    def tie_limit():
        need = kf - count_ge(t + 1)

        def ties_before(xc):
            return count(lambda key, j: (key == t) & (j * TK + krow < xc))

        def xbit_body(b, x):
            cand = x | jnp.left_shift(jnp.int32(1), 14 - b)
            return jnp.where(ties_before(cand) < need, cand, x)

        return lax.fori_loop(0, 15, xbit_body, jnp.zeros((1, TQ), jnp.int32))

    x = lax.cond(jnp.max(c_ge) > kf, tie_limit, lambda: jnp.full((1, TQ), 2 * seq, jnp.int32))

    heads = range(nh)
    grp = range(DSA_GROUP)
    pairs = [slice((h // 2) * LANES, (h // 2 + 1) * LANES) for h in heads]
    n_full = i // DSA_GROUP

    def scores(jg):
        off = pl.multiple_of(jg * (DSA_GROUP * TK), DSA_GROUP * TK)
        return tuple(_dot(k_ref[pl.ds(off, DSA_GROUP * TK), pairs[h]], qs_ref[h]) for h in heads)

    def attend(jg, carry, causal):
        ms, ss = carry
        ss_next = None if causal else scores(jg + 1)
        base = jg * DSA_GROUP
        biases = []
        for g in grp:
            key = key_ref[base + g]
            pos = (base + g) * TK + krow
            msk = (key > t) | ((key == t) & (pos <= x))
            if causal:
                msk = msk & (pos <= i * TQ + qcol)
            biases.append(jnp.where(msk, 0.0, NEG))
        sg = [[ss[h][g * TK:(g + 1) * TK, :] for g in grp] for h in heads]
        new = []
        for h in heads:
            m_new = ms[h]
            for g in grp:
                m_new = jnp.maximum(m_new, jnp.max(sg[h][g] + biases[g], axis=0, keepdims=True))
            new.append(m_new)
        ps = [[jnp.exp((sg[h][g] - new[h]) + biases[g]).astype(BF16) for g in grp] for h in heads]
        for h in heads:
            pv = None
            for g in grp:
                d = _dot(vt_ref[0, base + g, h * LANES:(h + 1) * LANES, :], ps[h][g])
                pv = d if pv is None else pv + d
            acc_ref[h] = jnp.exp(ms[h] - new[h]) * acc_ref[h] + pv
        return tuple(new), ss_next

    for h in heads:
        acc_ref[h] = jnp.zeros((LANES, TQ), F32)
    carry = lax.fori_loop(0, n_full, lambda jg, c: attend(jg, c, False),
                          (tuple(jnp.full((1, TQ), NEG, F32) for _ in heads), scores(0)))
    attend(n_full, carry, True)

    out_t = jnp.concatenate([_finish_softmax_head(acc_ref[h]) for h in range(nh)], axis=0)
    o_ref[...] = out_t.T


def _dsa_call(f_arr, h_arr, vt, batch, seq):
    nq = seq // TQ
    wb = N_HEADS_B * HEAD_DIM
    return pl.pallas_call(
        _dsa_kernel,
        grid=(batch, nq),
        in_specs=[pl.BlockSpec((TQ, IDX_HEADS * IDX_DIM), lambda b, i: (b * nq + i, H_QI // (IDX_HEADS * IDX_DIM))),
                  pl.BlockSpec((TQ, LANES), lambda b, i: (b * nq + i, F_WI // LANES)),
                  pl.BlockSpec((seq, LANES), lambda b, i: (b, H_KI // LANES)),
                  pl.BlockSpec((TQ, wb), lambda b, i: (b * nq + i, F_QB // wb)),
                  pl.BlockSpec((seq, wb), lambda b, i: (b, H_KB // wb)),
                  pl.BlockSpec((1, seq // TK, N_HEADS_B * LANES, TK), lambda b, i: (b, 0, VT_B // (N_HEADS_B * LANES), 0))],
        out_specs=pl.BlockSpec((TQ, wb), lambda b, i: (b * nq + i, 0)),
        out_shape=jax.ShapeDtypeStruct((batch * seq, wb), F32),
        scratch_shapes=[pltpu.VMEM((seq // TK, TK, TQ), jnp.int32),
                        pltpu.VMEM((seq // TK, TK, TQ), BF16),
                        pltpu.VMEM((IDX_HEADS, LANES, TQ), BF16),
                        pltpu.VMEM((N_HEADS_B, LANES, TQ), BF16),
                        pltpu.VMEM((N_HEADS_B, LANES, TQ), F32)],
        compiler_params=_cparams(("parallel", "arbitrary")),
        name="dsa_attn",
    )(h_arr, f_arr, h_arr, f_arr, h_arr, vt)


def _outproj_kernel(x_ref, oa_ref, ob_ref, oc_ref, og_ref, gt_ref, w_ref, o_ref):
    wa = N_HEADS_A * HEAD_DIM
    wb = wa + N_HEADS_B * HEAD_DIM
    og = og_ref[...]
    y = (_dot((oa_ref[...] * og[:, :wa]).astype(BF16), w_ref[:wa, :])
         + _dot((ob_ref[...] * og[:, wa:wb]).astype(BF16), w_ref[wa:wb, :])
         + _dot((oc_ref[...] * og[:, wb:]).astype(BF16), w_ref[wb:, :]))
    o_ref[...] = x_ref[...] + gt_ref[0] * y


def _outproj_call(x2, oa, ob, oc, og, gt, w_out, seq):
    m, d = x2.shape
    tm = 512
    row = lambda i: (i, 0)
    return pl.pallas_call(
        _outproj_kernel,
        grid=(m // tm,),
        in_specs=[pl.BlockSpec((tm, d), row),
                  pl.BlockSpec((tm, oa.shape[1]), row),
                  pl.BlockSpec((tm, ob.shape[1]), row),
                  pl.BlockSpec((tm, oc.shape[1]), row),
                  pl.BlockSpec((1, d), lambda i: (0, 0)),
                  pl.BlockSpec((1, 1, d), lambda i: ((i * tm) // seq, 0, 0)),
                  pl.BlockSpec((d, d), lambda i: (0, 0))],
        out_specs=pl.BlockSpec((tm, d), row),
        out_shape=jax.ShapeDtypeStruct((m, d), F32),
        compiler_params=_cparams(("parallel",)),
        name="out_proj",
    )(x2, oa, ob, oc, og, gt, w_out)


def _rope_tables(seq):
    pos = jnp.arange(seq, dtype=F32)
    inv = ROPE_THETA ** (-jnp.arange(0, ROPE_DIM, 2, dtype=F32) / ROPE_DIM)
    ang = pos[:, None] * inv[None, :]
    cos, sin = jnp.cos(ang), jnp.sin(ang)
    zeros = jnp.zeros((seq, HEAD_DIM - ROPE_DIM), F32)
    zh = jnp.zeros((seq, ROPE_HALF), F32)
    cos_h = jnp.concatenate([cos, cos, jnp.ones_like(zeros)], axis=1)
    s1_h = jnp.concatenate([-sin, zh, zeros], axis=1)
    s2_h = jnp.concatenate([zh, sin, zeros], axis=1)
    two = lambda t: jnp.concatenate([t, t], axis=1)
    return two(cos_h), two(s1_h), two(s2_h)


def _permute_w_in(w):
    d = w.shape[0]
    a, b, c = N_HEADS_A * HEAD_DIM, N_HEADS_B * HEAD_DIM, N_HEADS_C * HEAD_DIM
    sizes = (a, a, a, b, b, b, IDX_HEADS * IDX_DIM, IDX_DIM, IDX_HEADS, c, c, c)
    offs = np.concatenate([[0], np.cumsum(sizes)])
    qa, ka, va, qb, kb, vb, qi, ki, wi, qc, kc, vc = [w[:, int(offs[k]):int(offs[k + 1])] for k in range(12)]
    z = lambda n: jnp.zeros((d, n), w.dtype)
    out = jnp.concatenate([qa, qb, qc, ka, va, kb, vb, kc, vc, qi,
                           ki, z(LANES - IDX_DIM), wi, z(LANES - IDX_HEADS)], axis=1)
    assert out.shape[1] == W_PERM_WIDTH
    return out.astype(BF16)


def _suffix_sum_matrix():
    s = np.arange(TK + SB_TAIL)[:, None]
    j = np.arange(TK)[None, :]
    return jnp.asarray((j > s) | (s >= TK), BF16)


def kernel(x, c, w_ada, b_ada, norm_g, w_in, qk_g, out_g, w_out, ffn_w1, ffn_w3, ffn_w2):
    batch, seq, d = x.shape
    depth = w_ada.shape[0]
    assert seq % PROJ_TM == 0 and d % LANES == 0 and batch <= 8
    assert (seq // TK) % MOBA_GROUP == 0 and (seq // TK) % DSA_GROUP == 0

    c_pad = jnp.zeros((8, d), F32).at[:batch].set(c)
    mod = _mod_call(c_pad, w_ada, b_ada)
    cos_t, s1_t, s2_t = _rope_tables(seq)
    hd = np.arange(LANES) // HEAD_DIM
    bd = jnp.asarray(hd[:, None] == hd[None, :], BF16)
    u_mat = _suffix_sum_matrix()

    x2 = x.reshape(batch * seq, d)
    for layer in range(depth):
        mods = [mod[layer, :batch, k * d:(k + 1) * d].reshape(batch, 1, d) for k in range(N_MOD)]
        sh1, sc1, g1, sh2, sc2, g2, sh3, sc3, g3 = mods
        ng = norm_g[layer]
        w1 = ffn_w1[layer].astype(BF16)
        w3 = ffn_w3[layer].astype(BF16)
        w2 = ffn_w2[layer].astype(BF16)

        x2 = _ffn_call(x2, ng[0:1], sh1, sc1, g1, w1[0], w3[0], w2[0], seq)

        qkg = jnp.concatenate([jnp.tile(qk_g[layer], (1, 2)), jnp.ones((4, LANES), F32)], axis=0)
        f_arr, h_arr, kmean, vt = _proj_call(x2, ng[1:2], sh2, sc2, _permute_w_in(w_in[layer]), qkg,
                                             cos_t, s1_t, s2_t, bd, batch, seq)
        per_tile = PROJ_TM // MOBA_BLOCK
        kmean = kmean[:, :per_tile, :].reshape(batch, seq // MOBA_BLOCK, N_HEADS_A * HEAD_DIM)
        oa = _moba_call(f_arr, h_arr, kmean, vt, batch, seq)
        ob = _dsa_call(f_arr, h_arr, vt, batch, seq)
        oc = _sb_call(f_arr, h_arr, vt, u_mat, batch, seq)
        x2 = _outproj_call(x2, oa, ob, oc, out_g[layer].reshape(1, d), g2, w_out[layer].astype(BF16), seq)

        x2 = _ffn_call(x2, ng[2:3], sh3, sc3, g3, w1[1], w3[1], w2[1], seq)
    return x2.reshape(batch, seq, d)
```

```python
import jax
import jax.numpy as jnp
import numpy as np
from jax import lax
from jax.experimental import pallas as pl
from jax.experimental.pallas import tpu as pltpu

F32 = jnp.float32
BF16 = jnp.bfloat16

HEAD_DIM = 64
N_HEADS_A = 4
N_HEADS_B = 4
N_HEADS_C = 8
ROPE_DIM = HEAD_DIM // 4
ROPE_HALF = ROPE_DIM // 2
ROPE_THETA = 500000.0
MOBA_BLOCK = 256
MOBA_TOPK = 3
DSA_TOPK = 256
IDX_HEADS = 8
IDX_DIM = 64
N_MOD = 9
RMS_EPS = 1e-6
NEG = -1e30
ATT_SCALE = HEAD_DIM ** -0.5

LANES = 128
MXU_N = 256
TQ = 256
TK = 256
PROJ_TM = 512
VMEM_LIMIT = 56 * 1024 * 1024

SB_DEAD_LOG = -88.0
SB_TAIL = 16
SB_HEADS = 8
MOBA_GROUP = 4
DSA_GROUP = 1
BISECT_COARSE = 15
BISECT_FIXED = 6

F_QA, F_QB, F_QC, F_WI, F_WIDTH = 0, 256, 512, 1024, 1152
H_KA, H_KB, H_KC, H_QI, H_KI, H_WIDTH = 0, 256, 512, 1024, 1536, 1664
W_PERM_WIDTH = 15 * MXU_N
VT_A, VT_B, VT_C, VT_ROWS = 0, 512, 1024, 1536


def _dot(a, b):
    return jnp.dot(a, b, preferred_element_type=F32)


def _dot_nt(a, b):
    return lax.dot_general(a, b, (((1,), (1,)), ((), ())), preferred_element_type=F32)


def _split3(x):
    a = x.astype(BF16)
    r = x - a.astype(F32)
    b = r.astype(BF16)
    c = (r - b.astype(F32)).astype(BF16)
    return a, b, c


def _cparams(sem):
    return pltpu.CompilerParams(dimension_semantics=sem, vmem_limit_bytes=VMEM_LIMIT)


def _mod_kernel(c_ref, w_ref, b_ref, o_ref):
    c = c_ref[...]
    sc = c * (1.0 / (1.0 + jnp.exp(-c)))
    a, b, c3 = _split3(sc)
    w = w_ref[0]
    wa, wb, wc = _split3(w)
    acc = _dot(a, wa) + (_dot(a, wb) + _dot(b, wa)) + (_dot(a, wc) + _dot(b, wb) + _dot(c3, wa))
    o_ref[0] = acc + b_ref[0]


def _mod_call(c_pad, w_ada, b_ada):
    depth, d, n = w_ada.shape
    tn = 1024
    rows = c_pad.shape[0]
    return pl.pallas_call(
        _mod_kernel,
        grid=(depth, n // tn),
        in_specs=[pl.BlockSpec((rows, d), lambda l, j: (0, 0)),
                  pl.BlockSpec((1, d, tn), lambda l, j: (l, 0, j)),
                  pl.BlockSpec((1, 1, tn), lambda l, j: (l, 0, j))],
        out_specs=pl.BlockSpec((1, rows, tn), lambda l, j: (l, 0, j)),
        out_shape=jax.ShapeDtypeStruct((depth, rows, n), F32),
        compiler_params=_cparams(("parallel", "parallel")),
        name="adaln_mod",
    )(c_pad, w_ada, b_ada.reshape(depth, 1, n))


def _norm_modulate(x, ng, sh, sc):
    ms = jnp.mean(x * x, axis=-1, keepdims=True)
    h = x * lax.rsqrt(ms + RMS_EPS) * ng
    return h * (1.0 + sc) + sh


def _ffn_kernel(x_ref, ng_ref, sh_ref, sc_ref, gt_ref, w1_ref, w3_ref, w2_ref, o_ref, h_ref, acc_ref):
    j = pl.program_id(1)

    @pl.when(j == 0)
    def _():
        h = _norm_modulate(x_ref[...], ng_ref[...], sh_ref[0], sc_ref[0])
        h_ref[...] = h.astype(BF16)
        acc_ref[...] = jnp.zeros_like(acc_ref)

    h = h_ref[...]
    a = _dot(h, w1_ref[...])
    b = _dot(h, w3_ref[...])
    u = (a * (1.0 / (1.0 + jnp.exp(-a))) * b).astype(BF16)
    acc_ref[...] += _dot(u, w2_ref[...])

    @pl.when(j == pl.num_programs(1) - 1)
    def _():
        o_ref[...] = x_ref[...] + 0.5 * gt_ref[0] * acc_ref[...]


def _ffn_call(x2, ng, sh, sc, gt, w1, w3, w2, seq):
    m, d = x2.shape
    dff = w1.shape[1]
    tm = 1024
    tf = dff // 2 if (dff // 2) % LANES == 0 else MXU_N
    bidx = lambda i, j: ((i * tm) // seq, 0, 0)
    return pl.pallas_call(
        _ffn_kernel,
        grid=(m // tm, dff // tf),
        in_specs=[pl.BlockSpec((tm, d), lambda i, j: (i, 0)),
                  pl.BlockSpec((1, d), lambda i, j: (0, 0)),
                  pl.BlockSpec((1, 1, d), bidx),
                  pl.BlockSpec((1, 1, d), bidx),
                  pl.BlockSpec((1, 1, d), bidx),
                  pl.BlockSpec((d, tf), lambda i, j: (0, j)),
                  pl.BlockSpec((d, tf), lambda i, j: (0, j)),
                  pl.BlockSpec((tf, d), lambda i, j: (j, 0))],
        out_specs=pl.BlockSpec((tm, d), lambda i, j: (i, 0)),
        out_shape=jax.ShapeDtypeStruct((m, d), F32),
        scratch_shapes=[pltpu.VMEM((tm, d), BF16), pltpu.VMEM((tm, d), F32)],
        compiler_params=_cparams(("parallel", "arbitrary")),
        name="swiglu_ffn",
    )(x2, ng, sh, sc, gt, w1, w3, w2)


def _proj_kernel(x_ref, ng_ref, sh_ref, sc_ref, w_ref, qkg_ref, cos_ref, s1_ref, s2_ref, bd_ref,
                 f_ref, h_ref, km_ref, vt_ref, hs_ref):
    hs_ref[...] = _norm_modulate(x_ref[...], ng_ref[...], sh_ref[0], sc_ref[0]).astype(BF16)
    cos, s1, s2 = cos_ref[...], s1_ref[...], s2_ref[...]
    bd = bd_ref[...]
    tm = x_ref.shape[0]

    def rope(v):
        return v * cos + pltpu.roll(v, LANES - ROPE_HALF, 1) * s1 + pltpu.roll(v, ROPE_HALF, 1) * s2

    def headnorm(v, g):
        a, b, _ = _split3(v * v)
        ss = _dot(a, bd) + _dot(b, bd)
        return v * lax.rsqrt(ss * (1.0 / HEAD_DIM) + RMS_EPS) * g

    def chunk(c):
        y = _dot(hs_ref[...], w_ref[:, c * MXU_N:(c + 1) * MXU_N])
        return y[:, :LANES], y[:, LANES:]

    def normrope_chunk(c, grow):
        g = qkg_ref[grow:grow + 1, :]
        return [rope(headnorm(v, g)) for v in chunk(c)]

    def put(ref, off, halves, dtype):
        for k, v in enumerate(halves):
            ref[:, off + k * LANES: off + (k + 1) * LANES] = v.astype(dtype)

    def put_vt(base, halves, with_ones):
        ones = jnp.ones((HEAD_DIM, TK), BF16)
        for r in range(tm // TK):
            for k, v in enumerate(halves):
                t = v[r * TK:(r + 1) * TK, :].T.astype(BF16)
                if with_ones:
                    for hh in range(2):
                        row = base + (2 * k + hh) * LANES
                        vt_ref[0, r, row:row + HEAD_DIM, :] = t[hh * HEAD_DIM:(hh + 1) * HEAD_DIM, :]
                        vt_ref[0, r, row + HEAD_DIM:row + LANES, :] = ones
                else:
                    vt_ref[0, r, base + k * LANES:base + (k + 1) * LANES, :] = t

    put(f_ref, F_QA, normrope_chunk(0, 0), F32)
    put(f_ref, F_QB, normrope_chunk(1, 2), F32)
    put(f_ref, F_QC, chunk(2), F32)
    put(f_ref, F_QC + MXU_N, chunk(3), F32)

    ka = normrope_chunk(4, 1)
    put(h_ref, H_KA, ka, BF16)
    rows = lax.broadcasted_iota(jnp.int32, (8, LANES), 0)
    for k, v in enumerate(ka):
        km = jnp.zeros((8, LANES), F32)
        for r in range(tm // MOBA_BLOCK):
            s = jnp.sum(v[r * MOBA_BLOCK:(r + 1) * MOBA_BLOCK, :], axis=0, keepdims=True) * (1.0 / MOBA_BLOCK)
            km = jnp.where(rows == r, s, km)
        km_ref[0, :, k * LANES:(k + 1) * LANES] = km

    put_vt(VT_A, chunk(5), True)
    put(h_ref, H_KB, normrope_chunk(6, 3), BF16)
    put_vt(VT_B, chunk(7), True)
    put(h_ref, H_KC, chunk(8), BF16)
    put(h_ref, H_KC + MXU_N, chunk(9), BF16)
    put_vt(VT_C, chunk(10), False)
    put_vt(VT_C + MXU_N, chunk(11), False)
    put(h_ref, H_QI, [rope(v) for v in chunk(12)], BF16)
    put(h_ref, H_QI + MXU_N, [rope(v) for v in chunk(13)], BF16)
    ki, wi = chunk(14)
    h_ref[:, H_KI:H_KI + LANES] = rope(ki).astype(BF16)
    f_ref[:, F_WI:F_WI + LANES] = wi


def _proj_call(x2, ng, sh, sc, w_perm, qkg, cos_t, s1_t, s2_t, bd, batch, seq):
    m, d = x2.shape
    tm = PROJ_TM
    nt = seq // tm
    bidx = lambda i: ((i * tm) // seq, 0, 0)
    tab = pl.BlockSpec((tm, LANES), lambda i: (i % nt, 0))
    return pl.pallas_call(
        _proj_kernel,
        grid=(m // tm,),
        in_specs=[pl.BlockSpec((tm, d), lambda i: (i, 0)),
                  pl.BlockSpec((1, d), lambda i: (0, 0)),
                  pl.BlockSpec((1, 1, d), bidx),
                  pl.BlockSpec((1, 1, d), bidx),
                  pl.BlockSpec((d, W_PERM_WIDTH), lambda i: (0, 0)),
                  pl.BlockSpec((8, LANES), lambda i: (0, 0)),
                  tab, tab, tab,
                  pl.BlockSpec((LANES, LANES), lambda i: (0, 0))],
        out_specs=[pl.BlockSpec((tm, F_WIDTH), lambda i: (i, 0)),
                   pl.BlockSpec((tm, H_WIDTH), lambda i: (i, 0)),
                   pl.BlockSpec((1, 8, 2 * LANES), lambda i: (i, 0, 0)),
                   pl.BlockSpec((1, tm // TK, VT_ROWS, TK), lambda i: (i // nt, i % nt, 0, 0))],
        out_shape=[jax.ShapeDtypeStruct((m, F_WIDTH), F32),
                   jax.ShapeDtypeStruct((m, H_WIDTH), BF16),
                   jax.ShapeDtypeStruct((m // tm, 8, 2 * LANES), F32),
                   jax.ShapeDtypeStruct((batch, seq // TK, VT_ROWS, TK), BF16)],
        scratch_shapes=[pltpu.VMEM((tm, d), BF16)],
        compiler_params=_cparams(("parallel",)),
        name="in_proj",
    )(x2, ng, sh, sc, w_perm, qkg, cos_t, s1_t, s2_t, bd)


def _head_masks():
    lane = lax.broadcasted_iota(jnp.int32, (1, LANES), 1)
    return [(lane >= HEAD_DIM * h) & (lane < HEAD_DIM * (h + 1)) for h in range(2)]


def _head_rms_t(o):
    ss = jnp.sum(o * o, axis=0, keepdims=True) * (1.0 / HEAD_DIM)
    return o * lax.rsqrt(ss + RMS_EPS)


def _finish_softmax_head(acc):
    return _head_rms_t(acc[:HEAD_DIM, :] / acc[HEAD_DIM:HEAD_DIM + 1, :])


def _moba_kernel(q_ref, km_ref, k_ref, vt_ref, o_ref, bias_ref, qs_ref, acc_ref):
    i = pl.program_id(2)
    nb = km_ref.shape[1]
    q = q_ref[...]
    kma, kmb, _ = _split3(km_ref[0])
    blk = lax.broadcasted_iota(jnp.int32, (nb, TQ), 0).astype(F32)
    i_f = i.astype(F32)
    krow = lax.broadcasted_iota(jnp.int32, (TK, TQ), 0)
    qcol = lax.broadcasted_iota(jnp.int32, (TK, TQ), 1)
    causal = krow <= qcol

    for h, hm in enumerate(_head_masks()):
        qh_t = jnp.where(hm, q, 0.0).T
        qa, qb, _ = _split3(qh_t)
        gate = _dot(kma, qa) + (_dot(kmb, qa) + _dot(kma, qb))
        gate = jnp.where(blk < i_f, gate, NEG)
        sel = jnp.zeros((nb, TQ), F32)
        for _ in range(MOBA_TOPK):
            mx = jnp.max(gate, axis=0, keepdims=True)
            idx = jnp.min(jnp.where(gate == mx, blk, float(nb)), axis=0, keepdims=True)
            pick = blk == idx
            sel = jnp.where(pick, 1.0, sel)
            gate = jnp.where(pick, -jnp.inf, gate)
        bias_ref[h] = jnp.where((sel > 0.5) & (blk < i_f), 0.0, NEG)
        qs_ref[h] = (qh_t * ATT_SCALE).astype(BF16)

    own = pl.multiple_of(i * TK, TK)
    k_own = k_ref[pl.ds(own, TK), :]
    m_init = []
    for h in range(2):
        s = jnp.where(causal, _dot(k_own, qs_ref[h]), NEG)
        m0 = jnp.max(s, axis=0, keepdims=True)
        p = jnp.exp(s - m0)
        acc_ref[h] = _dot(vt_ref[0, i, h * LANES:(h + 1) * LANES, :], p.astype(BF16))
        m_init.append(m0)

    heads = range(2)
    grp = range(MOBA_GROUP)
    n_groups = (i + MOBA_GROUP - 1) // MOBA_GROUP
    last_group = km_ref.shape[1] // MOBA_GROUP - 1

    def scores(jg):
        off = pl.multiple_of(jnp.minimum(jg, last_group) * (MOBA_GROUP * TK), MOBA_GROUP * TK)
        kb = k_ref[pl.ds(off, MOBA_GROUP * TK), :]
        return tuple(_dot(kb, qs_ref[h]) for h in heads)

    def body(jg, carry):
        ms, ss = carry
        ss_next = scores(jg + 1)
        base = jg * MOBA_GROUP
        sg = [[ss[h][g * TK:(g + 1) * TK, :] for g in grp] for h in heads]
        bs = [[bias_ref[h, pl.ds(base + g, 1), :] for g in grp] for h in heads]
        new = []
        for h in heads:
            m_new = ms[h]
            for g in grp:
                m_new = jnp.maximum(m_new, jnp.max(sg[h][g], axis=0, keepdims=True) + bs[h][g])
            new.append(m_new)
        ps = [[jnp.exp(sg[h][g] - (new[h] - bs[h][g])).astype(BF16) for g in grp] for h in heads]
        for h in heads:
            pv = None
            for g in grp:
                d = _dot(vt_ref[0, base + g, h * LANES:(h + 1) * LANES, :], ps[h][g])
                pv = d if pv is None else pv + d
            acc_ref[h] = jnp.exp(ms[h] - new[h]) * acc_ref[h] + pv
        return tuple(new), ss_next

    lax.fori_loop(0, n_groups, body, (tuple(m_init), scores(0)))
    out_t = jnp.concatenate([_finish_softmax_head(acc_ref[h]) for h in range(2)], axis=0)
    o_ref[...] = out_t.T


def _moba_call(f_arr, h_arr, kmean, vt, batch, seq):
    nq = seq // TQ
    nb = seq // MOBA_BLOCK
    npair = N_HEADS_A // 2
    return pl.pallas_call(
        _moba_kernel,
        grid=(batch, npair, nq),
        in_specs=[pl.BlockSpec((TQ, LANES), lambda b, p, i: (b * nq + i, F_QA // LANES + p)),
                  pl.BlockSpec((1, nb, LANES), lambda b, p, i: (b, 0, p)),
                  pl.BlockSpec((seq, LANES), lambda b, p, i: (b, H_KA // LANES + p)),
                  pl.BlockSpec((1, seq // TK, 2 * LANES, TK), lambda b, p, i: (b, 0, VT_A // (2 * LANES) + p, 0))],
        out_specs=pl.BlockSpec((TQ, LANES), lambda b, p, i: (b * nq + i, p)),
        out_shape=jax.ShapeDtypeStruct((batch * seq, N_HEADS_A * HEAD_DIM), F32),
        scratch_shapes=[pltpu.VMEM((2, nb, TQ), F32),
                        pltpu.VMEM((2, LANES, TQ), BF16),
                        pltpu.VMEM((2, LANES, TQ), F32)],
        compiler_params=_cparams(("parallel", "parallel", "arbitrary")),
        name="moba_attn",
    )(f_arr, kmean, h_arr, vt)


def _sb_kernel(q_ref, k_ref, vt_ref, u_ref, o_ref, qs_ref, acc_ref):
    i = pl.program_id(2)
    krow = lax.broadcasted_iota(jnp.int32, (TK, TQ), 0)
    qcol = lax.broadcasted_iota(jnp.int32, (TK, TQ), 1)
    strict = krow < qcol
    hms = _head_masks()
    for h in range(SB_HEADS):
        pr = slice((h // 2) * LANES, (h // 2 + 1) * LANES)
        qs_ref[h] = (jnp.where(hms[h % 2], q_ref[:, pr], 0.0) * ATT_SCALE).T.astype(BF16)
        acc_ref[h] = jnp.zeros((LANES, TQ), F32)

    heads = range(SB_HEADS)
    pairs = [slice((h // 2) * LANES, (h // 2 + 1) * LANES) for h in heads]

    def logits(j):
        off = pl.multiple_of(jnp.maximum(j, 0) * TK, TK)
        return tuple(_dot(k_ref[pl.ds(off, TK), pairs[h]], qs_ref[h]) for h in heads)

    def block(j, rs, zs, diag):
        zs_next = logits(j - 1)
        u = u_ref[...]
        log_betas, his, los = [], [], []
        for h in heads:
            z = zs[h]
            log_beta = jnp.minimum(z, 0.0) - jnp.log(1.0 + jnp.exp(-jnp.abs(z)))
            log_1m = log_beta - z
            if diag:
                log_1m = jnp.where(strict, log_1m, 0.0)
            hi = log_1m.astype(BF16)
            log_betas.append(log_beta)
            his.append(hi)
            los.append((log_1m - hi.astype(F32)).astype(BF16))
        sums = [_dot(u, his[h]) + _dot(u, los[h]) for h in heads]
        weights = []
        for h in heads:
            after = sums[h][:TK, :] + rs[h]
            a = jnp.exp(log_betas[h] + after)
            if diag:
                a = jnp.where(strict, a, 0.0)
            weights.append(a.astype(BF16))
        for h in heads:
            acc_ref[h] += _dot(vt_ref[0, j, pairs[h], :], weights[h])
        return tuple(rs[h] + sums[h][TK:TK + 1, :] for h in heads), zs_next

    zero = jnp.zeros((1, TQ), F32)
    rs, zs = block(i, (zero,) * SB_HEADS, logits(i), True)

    def alive(rs):
        m = rs[0]
        for r in rs[1:]:
            m = jnp.maximum(m, r)
        return jnp.max(m)

    def cond(c):
        return (c[0] >= 0) & (c[1] > SB_DEAD_LOG)

    def body(c):
        rs, zs = block(c[0], c[2], c[3], False)
        return c[0] - 1, alive(rs), rs, zs

    lax.while_loop(cond, body, (i - 1, alive(rs), rs, zs))
    out_t = jnp.concatenate(
        [_head_rms_t(acc_ref[h][(h % 2) * HEAD_DIM:(h % 2 + 1) * HEAD_DIM, :]) for h in range(SB_HEADS)], axis=0)
    o_ref[...] = out_t.T


def _sb_call(f_arr, h_arr, vt, u_mat, batch, seq):
    nq = seq // TQ
    w = SB_HEADS * HEAD_DIM
    ngrp = N_HEADS_C // SB_HEADS
    return pl.pallas_call(
        _sb_kernel,
        grid=(batch, ngrp, nq),
        in_specs=[pl.BlockSpec((TQ, w), lambda b, p, i: (b * nq + i, F_QC // w + p)),
                  pl.BlockSpec((seq, w), lambda b, p, i: (b, H_KC // w + p)),
                  pl.BlockSpec((1, seq // TK, w, TK), lambda b, p, i: (b, 0, VT_C // w + p, 0)),
                  pl.BlockSpec((TK + SB_TAIL, TK), lambda b, p, i: (0, 0))],
        out_specs=pl.BlockSpec((TQ, w), lambda b, p, i: (b * nq + i, p)),
        out_shape=jax.ShapeDtypeStruct((batch * seq, N_HEADS_C * HEAD_DIM), F32),
        scratch_shapes=[pltpu.VMEM((SB_HEADS, LANES, TQ), BF16), pltpu.VMEM((SB_HEADS, LANES, TQ), F32)],
        compiler_params=_cparams(("parallel", "parallel", "arbitrary")),
        name="stickbreak_attn",
    )(f_arr, h_arr, vt, u_mat)


def _sortable_key(x):
    bits = lax.bitcast_convert_type(x, jnp.int32)
    return jnp.where(bits < 0, bits ^ jnp.int32(0x7FFFFFFF), bits)


_NEG_BITS = int(np.array(NEG, np.float32).view(np.int32))
NEG_KEY = _NEG_BITS ^ 0x7FFFFFFF
INT_MIN = -2 ** 31
MIN_NORMAL_BITS = 0x00800000


def _dsa_kernel(qi_ref, wi_ref, ki_ref, q_ref, k_ref, vt_ref, o_ref,
                key_ref, coarse_ref, qx_ref, qs_ref, acc_ref, cnt_ref, m_ref):
    i = pl.program_id(1)
    seq = k_ref.shape[0]
    topk = min(DSA_TOPK, seq // 4)
    nh = N_HEADS_B
    lane = lax.broadcasted_iota(jnp.int32, (1, LANES), 1)
    low = lane < HEAD_DIM
    krow = lax.broadcasted_iota(jnp.int32, (TK, TQ), 0)
    qcol = lax.broadcasted_iota(jnp.int32, (TK, TQ), 1)
    diag_causal = krow <= qcol
    hms = _head_masks()

    qi = qi_ref[...].astype(F32)
    for h in range(IDX_HEADS):
        qp = qi[:, (h // 2) * LANES:(h // 2 + 1) * LANES]
        if h % 2:
            qp = pltpu.roll(qp, HEAD_DIM, 1)
        qx_ref[h] = jnp.where(low, qp, 0.0).T.astype(BF16)
    w_t = wi_ref[...].T
    q = q_ref[...]
    for h in range(nh):
        qh = jnp.where(hms[h % 2], q[:, (h // 2) * LANES:(h // 2 + 1) * LANES], 0.0)
        qs_ref[h] = (qh * ATT_SCALE).T.astype(BF16)

    def score_block(j, diag):
        off = pl.multiple_of(j * TK, TK)
        kz = ki_ref[pl.ds(off, TK), :]
        sc = jnp.zeros((TK, TQ), F32)
        for h in range(IDX_HEADS):
            sc = sc + w_t[h:h + 1, :] * jnp.maximum(_dot(kz, qx_ref[h]), 0.0)
        if diag:
            sc = jnp.where(diag_causal, sc, NEG)
        bits = lax.bitcast_convert_type(sc, jnp.int32)
        key_ref[j] = jnp.where(bits < 0, bits ^ jnp.int32(0x7FFFFFFF), bits)
        coarse_ref[j] = lax.bitcast_convert_type(bits & jnp.int32(-65536), F32).astype(BF16)

    def score_pair(jj, c):
        score_block(2 * jj, False)
        score_block(2 * jj + 1, False)
        return c

    lax.fori_loop(0, i // 2, score_pair, 0)

    @pl.when(i % 2 == 1)
    def _():
        score_block(i - 1, False)

    score_block(i, True)
    for g in range(1, DSA_GROUP):
        @pl.when(i % DSA_GROUP + g < DSA_GROUP)
        def _():
            key_ref[i + g] = jnp.full((TK, TQ), NEG_KEY, jnp.int32)

    n_unscanned = (seq - (i + 1) * TK).astype(F32)
    fold = TK // 4

    def count(pred):
        def body(j, acc):
            hit = pred(key_ref[j], j)
            for r in range(TK // fold):
                acc = jnp.where(hit[r * fold:(r + 1) * fold, :], acc + 1.0, acc)
            return acc
        acc = lax.fori_loop(0, i + 1, body, jnp.zeros((fold, TQ), F32))
        return jnp.sum(acc, axis=0, keepdims=True)

    def count_ge(cand):
        return count(lambda key, j: key >= cand) + jnp.where(cand <= NEG_KEY, n_unscanned, 0.0)

    kf = float(topk)

    def count_ge_coarse(cand):
        cand_bits = jnp.where(cand < 0, (cand ^ jnp.int32(0x7FFFFFFF)) & jnp.int32(-65536), cand)
        cand_bits = jnp.where((cand > 0) & (cand < MIN_NORMAL_BITS), MIN_NORMAL_BITS, cand_bits)
        cand_b = lax.bitcast_convert_type(cand_bits, F32).astype(BF16)
        one, zero = jnp.ones((), BF16), jnp.zeros((), BF16)

        def body(j, acc):
            hit = jnp.where(coarse_ref[j] >= cand_b, one, zero)
            return acc + ((hit[:fold] + hit[fold:2 * fold]) + (hit[2 * fold:3 * fold] + hit[3 * fold:]))
        acc = lax.fori_loop(0, i + 1, body, jnp.zeros((fold, TQ), BF16))
        return (jnp.sum(acc.astype(F32), axis=0, keepdims=True)
                + jnp.where(cand <= NEG_KEY, n_unscanned, 0.0))

    c0 = count_ge_coarse(jnp.zeros((1, TQ), jnp.int32))
    t0 = jnp.where(c0 >= kf, 0, INT_MIN).astype(jnp.int32)
    ct0 = jnp.where(c0 >= kf, c0, float(seq))

    def bit_step(b, t, ct, counter):
        cand = t | jnp.left_shift(jnp.int32(1), 30 - b)
        cc = counter(cand)
        ok = cc >= kf
        return jnp.where(ok, cand, t), jnp.where(ok, cc, ct)

    t, ct = lax.fori_loop(0, BISECT_COARSE, lambda b, c: bit_step(b, *c, count_ge_coarse), (t0, ct0))
    t, ct = lax.fori_loop(BISECT_COARSE, BISECT_COARSE + BISECT_FIXED,
                          lambda b, c: bit_step(b, *c, count_ge), (t, ct))

    zero_final = (c0 >= kf) & (count_ge_coarse(jnp.full((1, TQ), MIN_NORMAL_BITS, jnp.int32)) < kf)

    def pending(ct):
        return jnp.max(jnp.where((ct > kf) & jnp.logical_not(zero_final), 1.0, 0.0))

    def bit_body(c):
        t, ct = bit_step(c[0], c[2], c[3], count_ge)
        t, ct = bit_step(c[0] + 1, t, ct, count_ge)
        return c[0] + 2, pending(ct), t, ct

    assert (31 - BISECT_COARSE - BISECT_FIXED) % 2 == 0
    _, _, t, c_ge = lax.while_loop(lambda c: (c[0] < 31) & (c[1] > 0.5), bit_body,
                                   (jnp.int32(BISECT_COARSE + BISECT_FIXED), pending(ct), t, ct))

    def tie_limit():
        need = kf - count_ge(t + 1)

        def block_ties(j, c):
            hit = key_ref[j] == t
            acc = jnp.zeros((fold, TQ), F32)
            for r in range(TK // fold):
                acc = jnp.where(hit[r * fold:(r + 1) * fold, :], acc + 1.0, acc)
            cnt_ref[pl.ds(j, 1), :] = jnp.sum(acc, axis=0, keepdims=True)
            return c

        lax.fori_loop(0, i + 1, block_ties, 0)
        nkb = cnt_ref.shape[0]
        cnt = jnp.where(lax.broadcasted_iota(jnp.int32, (nkb, TQ), 0) <= i, cnt_ref[...], 0.0)
        run = jnp.zeros((1, TQ), F32)
        blk_of = jnp.zeros((1, TQ), F32)
        before = jnp.zeros((1, TQ), F32)
        for j in range(nkb):
            run = run + cnt[j:j + 1, :]
            ahead = run < need
            blk_of = blk_of + jnp.where(ahead, 1.0, 0.0)
            before = jnp.where(ahead, run, before)
        blk_of = blk_of.astype(jnp.int32)

        def pick_block(j, c):
            m_ref[...] = jnp.where((blk_of == j) & (key_ref[j] == t), 1.0, m_ref[...])
            return c

        m_ref[...] = jnp.zeros_like(m_ref)
        lax.fori_loop(0, i + 1, pick_block, 0)
        tri = (lax.broadcasted_iota(jnp.int32, (TK, TK), 0) >= lax.broadcasted_iota(jnp.int32, (TK, TK), 1))
        upto = _dot(jnp.where(tri, 1.0, 0.0).astype(BF16), m_ref[...].astype(BF16))
        row_of = jnp.sum(jnp.where(upto < need - before, 1.0, 0.0), axis=0, keepdims=True)
        return blk_of * TK + row_of.astype(jnp.int32)

    x = lax.cond(jnp.max(c_ge) > kf, tie_limit, lambda: jnp.full((1, TQ), 2 * seq, jnp.int32))

    heads = range(nh)
    grp = range(DSA_GROUP)
    pairs = [slice((h // 2) * LANES, (h // 2 + 1) * LANES) for h in heads]
    n_full = i // DSA_GROUP

    def scores(jg):
        off = pl.multiple_of(jg * (DSA_GROUP * TK), DSA_GROUP * TK)
        return tuple(_dot(k_ref[pl.ds(off, DSA_GROUP * TK), pairs[h]], qs_ref[h]) for h in heads)

    def attend(jg, carry, causal):
        ms, ss = carry
        ss_next = None if causal else scores(jg + 1)
        base = jg * DSA_GROUP
        biases = []
        for g in grp:
            key = key_ref[base + g]
            pos = (base + g) * TK + krow
            msk = (key > t) | ((key == t) & (pos <= x))
            if causal:
                msk = msk & (pos <= i * TQ + qcol)
            biases.append(jnp.where(msk, 0.0, NEG))
        sg = [[ss[h][g * TK:(g + 1) * TK, :] for g in grp] for h in heads]
        new = []
        for h in heads:
            m_new = ms[h]
            for g in grp:
                m_new = jnp.maximum(m_new, jnp.max(sg[h][g] + biases[g], axis=0, keepdims=True))
            new.append(m_new)
        ps = [[jnp.exp((sg[h][g] - new[h]) + biases[g]).astype(BF16) for g in grp] for h in heads]
        for h in heads:
            pv = None
            for g in grp:
                d = _dot(vt_ref[0, base + g, h * LANES:(h + 1) * LANES, :], ps[h][g])
                pv = d if pv is None else pv + d
            acc_ref[h] = jnp.exp(ms[h] - new[h]) * acc_ref[h] + pv
        return tuple(new), ss_next

    for h in heads:
        acc_ref[h] = jnp.zeros((LANES, TQ), F32)
    carry = lax.fori_loop(0, n_full, lambda jg, c: attend(jg, c, False),
                          (tuple(jnp.full((1, TQ), NEG, F32) for _ in heads), scores(0)))
    attend(n_full, carry, True)

    out_t = jnp.concatenate([_finish_softmax_head(acc_ref[h]) for h in range(nh)], axis=0)
    o_ref[...] = out_t.T


def _dsa_call(f_arr, h_arr, vt, batch, seq):
    nq = seq // TQ
    wb = N_HEADS_B * HEAD_DIM
    return pl.pallas_call(
        _dsa_kernel,
        grid=(batch, nq),
        in_specs=[pl.BlockSpec((TQ, IDX_HEADS * IDX_DIM), lambda b, i: (b * nq + i, H_QI // (IDX_HEADS * IDX_DIM))),
                  pl.BlockSpec((TQ, LANES), lambda b, i: (b * nq + i, F_WI // LANES)),
                  pl.BlockSpec((seq, LANES), lambda b, i: (b, H_KI // LANES)),
                  pl.BlockSpec((TQ, wb), lambda b, i: (b * nq + i, F_QB // wb)),
                  pl.BlockSpec((seq, wb), lambda b, i: (b, H_KB // wb)),
                  pl.BlockSpec((1, seq // TK, N_HEADS_B * LANES, TK), lambda b, i: (b, 0, VT_B // (N_HEADS_B * LANES), 0))],
        out_specs=pl.BlockSpec((TQ, wb), lambda b, i: (b * nq + i, 0)),
        out_shape=jax.ShapeDtypeStruct((batch * seq, wb), F32),
        scratch_shapes=[pltpu.VMEM((seq // TK, TK, TQ), jnp.int32),
                        pltpu.VMEM((seq // TK, TK, TQ), BF16),
                        pltpu.VMEM((IDX_HEADS, LANES, TQ), BF16),
                        pltpu.VMEM((N_HEADS_B, LANES, TQ), BF16),
                        pltpu.VMEM((N_HEADS_B, LANES, TQ), F32),
                        pltpu.VMEM((seq // TK, TQ), F32),
                        pltpu.VMEM((TK, TQ), F32)],
        compiler_params=_cparams(("parallel", "arbitrary")),
        name="dsa_attn",
    )(h_arr, f_arr, h_arr, f_arr, h_arr, vt)


def _outproj_kernel(x_ref, oa_ref, ob_ref, oc_ref, og_ref, gt_ref, w_ref, o_ref):
    wa = N_HEADS_A * HEAD_DIM
    wb = wa + N_HEADS_B * HEAD_DIM
    og = og_ref[...]
    y = (_dot((oa_ref[...] * og[:, :wa]).astype(BF16), w_ref[:wa, :])
         + _dot((ob_ref[...] * og[:, wa:wb]).astype(BF16), w_ref[wa:wb, :])
         + _dot((oc_ref[...] * og[:, wb:]).astype(BF16), w_ref[wb:, :]))
    o_ref[...] = x_ref[...] + gt_ref[0] * y


def _outproj_call(x2, oa, ob, oc, og, gt, w_out, seq):
    m, d = x2.shape
    tm = 512
    row = lambda i: (i, 0)
    return pl.pallas_call(
        _outproj_kernel,
        grid=(m // tm,),
        in_specs=[pl.BlockSpec((tm, d), row),
                  pl.BlockSpec((tm, oa.shape[1]), row),
                  pl.BlockSpec((tm, ob.shape[1]), row),
                  pl.BlockSpec((tm, oc.shape[1]), row),
                  pl.BlockSpec((1, d), lambda i: (0, 0)),
                  pl.BlockSpec((1, 1, d), lambda i: ((i * tm) // seq, 0, 0)),
                  pl.BlockSpec((d, d), lambda i: (0, 0))],
        out_specs=pl.BlockSpec((tm, d), row),
        out_shape=jax.ShapeDtypeStruct((m, d), F32),
        compiler_params=_cparams(("parallel",)),
        name="out_proj",
    )(x2, oa, ob, oc, og, gt, w_out)


def _rope_tables(seq):
    pos = jnp.arange(seq, dtype=F32)
    inv = ROPE_THETA ** (-jnp.arange(0, ROPE_DIM, 2, dtype=F32) / ROPE_DIM)
    ang = pos[:, None] * inv[None, :]
    cos, sin = jnp.cos(ang), jnp.sin(ang)
    zeros = jnp.zeros((seq, HEAD_DIM - ROPE_DIM), F32)
    zh = jnp.zeros((seq, ROPE_HALF), F32)
    cos_h = jnp.concatenate([cos, cos, jnp.ones_like(zeros)], axis=1)
    s1_h = jnp.concatenate([-sin, zh, zeros], axis=1)
    s2_h = jnp.concatenate([zh, sin, zeros], axis=1)
    two = lambda t: jnp.concatenate([t, t], axis=1)
    return two(cos_h), two(s1_h), two(s2_h)


def _permute_w_in(w):
    d = w.shape[0]
    a, b, c = N_HEADS_A * HEAD_DIM, N_HEADS_B * HEAD_DIM, N_HEADS_C * HEAD_DIM
    sizes = (a, a, a, b, b, b, IDX_HEADS * IDX_DIM, IDX_DIM, IDX_HEADS, c, c, c)
    offs = np.concatenate([[0], np.cumsum(sizes)])
    qa, ka, va, qb, kb, vb, qi, ki, wi, qc, kc, vc = [w[:, int(offs[k]):int(offs[k + 1])] for k in range(12)]
    z = lambda n: jnp.zeros((d, n), w.dtype)
    out = jnp.concatenate([qa, qb, qc, ka, va, kb, vb, kc, vc, qi,
                           ki, z(LANES - IDX_DIM), wi, z(LANES - IDX_HEADS)], axis=1)
    assert out.shape[1] == W_PERM_WIDTH
    return out.astype(BF16)


def _suffix_sum_matrix():
    s = np.arange(TK + SB_TAIL)[:, None]
    j = np.arange(TK)[None, :]
    return jnp.asarray((j > s) | (s >= TK), BF16)


def kernel(x, c, w_ada, b_ada, norm_g, w_in, qk_g, out_g, w_out, ffn_w1, ffn_w3, ffn_w2):
    batch, seq, d = x.shape
    depth = w_ada.shape[0]
    assert seq % PROJ_TM == 0 and d % LANES == 0 and batch <= 8
    assert (seq // TK) % MOBA_GROUP == 0 and (seq // TK) % DSA_GROUP == 0

    c_pad = jnp.zeros((8, d), F32).at[:batch].set(c)
    mod = _mod_call(c_pad, w_ada, b_ada)
    cos_t, s1_t, s2_t = _rope_tables(seq)
    hd = np.arange(LANES) // HEAD_DIM
    bd = jnp.asarray(hd[:, None] == hd[None, :], BF16)
    u_mat = _suffix_sum_matrix()

    x2 = x.reshape(batch * seq, d)
    for layer in range(depth):
        mods = [mod[layer, :batch, k * d:(k + 1) * d].reshape(batch, 1, d) for k in range(N_MOD)]
        sh1, sc1, g1, sh2, sc2, g2, sh3, sc3, g3 = mods
        ng = norm_g[layer]
        w1 = ffn_w1[layer].astype(BF16)
        w3 = ffn_w3[layer].astype(BF16)
        w2 = ffn_w2[layer].astype(BF16)

        x2 = _ffn_call(x2, ng[0:1], sh1, sc1, g1, w1[0], w3[0], w2[0], seq)

        qkg = jnp.concatenate([jnp.tile(qk_g[layer], (1, 2)), jnp.ones((4, LANES), F32)], axis=0)
        f_arr, h_arr, kmean, vt = _proj_call(x2, ng[1:2], sh2, sc2, _permute_w_in(w_in[layer]), qkg,
                                             cos_t, s1_t, s2_t, bd, batch, seq)
        per_tile = PROJ_TM // MOBA_BLOCK
        kmean = kmean[:, :per_tile, :].reshape(batch, seq // MOBA_BLOCK, N_HEADS_A * HEAD_DIM)
        oa = _moba_call(f_arr, h_arr, kmean, vt, batch, seq)
        ob = _dsa_call(f_arr, h_arr, vt, batch, seq)
        oc = _sb_call(f_arr, h_arr, vt, u_mat, batch, seq)
        x2 = _outproj_call(x2, oa, ob, oc, out_g[layer].reshape(1, d), g2, w_out[layer].astype(BF16), seq)

        x2 = _ffn_call(x2, ng[2:3], sh3, sc3, g3, w1[1], w3[1], w2[1], seq)
    return x2.reshape(batch, seq, d)
```

```python
import jax
import jax.numpy as jnp
import numpy as np
from jax import lax
from jax.experimental import pallas as pl
from jax.experimental.pallas import tpu as pltpu

F32 = jnp.float32
BF16 = jnp.bfloat16

HEAD_DIM = 64
N_HEADS_A = 4
N_HEADS_B = 4
N_HEADS_C = 8
ROPE_DIM = HEAD_DIM // 4
ROPE_HALF = ROPE_DIM // 2
ROPE_THETA = 500000.0
MOBA_BLOCK = 256
MOBA_TOPK = 3
DSA_TOPK = 256
IDX_HEADS = 8
IDX_DIM = 64
N_MOD = 9
RMS_EPS = 1e-6
NEG = -1e30
ATT_SCALE = HEAD_DIM ** -0.5

LANES = 128
MXU_N = 256
TQ = 256
TK = 256
PROJ_TM = 512
VMEM_LIMIT = 56 * 1024 * 1024

SB_DEAD_LOG = -88.0
SB_TAIL = 16
SB_HEADS = 8
MOBA_GROUP = 4
DSA_GROUP = 1
BOUND_SLACK = 1.02
SAFE_SHIFT = 40.0
BISECT_COARSE = 15
BISECT_FIXED = 6

F_QA, F_QB, F_QC, F_WI, F_WIDTH = 0, 256, 512, 1024, 1152
H_KA, H_KB, H_KC, H_QI, H_KI, H_WIDTH = 0, 256, 512, 1024, 1536, 1664
W_PERM_WIDTH = 15 * MXU_N
VT_A, VT_B, VT_C, VT_ROWS = 0, 512, 1024, 1536


def _dot(a, b):
    return jnp.dot(a, b, preferred_element_type=F32)


def _dot_nt(a, b):
    return lax.dot_general(a, b, (((1,), (1,)), ((), ())), preferred_element_type=F32)


def _split3(x):
    a = x.astype(BF16)
    r = x - a.astype(F32)
    b = r.astype(BF16)
    c = (r - b.astype(F32)).astype(BF16)
    return a, b, c


def _cparams(sem):
    return pltpu.CompilerParams(dimension_semantics=sem, vmem_limit_bytes=VMEM_LIMIT)


def _mod_kernel(c_ref, w_ref, b_ref, o_ref):
    c = c_ref[...]
    sc = c * (1.0 / (1.0 + jnp.exp(-c)))
    a, b, c3 = _split3(sc)
    w = w_ref[0]
    wa, wb, wc = _split3(w)
    acc = _dot(a, wa) + (_dot(a, wb) + _dot(b, wa)) + (_dot(a, wc) + _dot(b, wb) + _dot(c3, wa))
    o_ref[0] = acc + b_ref[0]


def _mod_call(c_pad, w_ada, b_ada):
    depth, d, n = w_ada.shape
    tn = 1024
    rows = c_pad.shape[0]
    return pl.pallas_call(
        _mod_kernel,
        grid=(depth, n // tn),
        in_specs=[pl.BlockSpec((rows, d), lambda l, j: (0, 0)),
                  pl.BlockSpec((1, d, tn), lambda l, j: (l, 0, j)),
                  pl.BlockSpec((1, 1, tn), lambda l, j: (l, 0, j))],
        out_specs=pl.BlockSpec((1, rows, tn), lambda l, j: (l, 0, j)),
        out_shape=jax.ShapeDtypeStruct((depth, rows, n), F32),
        compiler_params=_cparams(("parallel", "parallel")),
        name="adaln_mod",
    )(c_pad, w_ada, b_ada.reshape(depth, 1, n))


def _norm_modulate(x, ng, sh, sc):
    ms = jnp.mean(x * x, axis=-1, keepdims=True)
    h = x * lax.rsqrt(ms + RMS_EPS) * ng
    return h * (1.0 + sc) + sh


def _ffn_kernel(x_ref, ng_ref, sh_ref, sc_ref, gt_ref, w1_ref, w3_ref, w2_ref, o_ref, h_ref, acc_ref):
    j = pl.program_id(1)

    @pl.when(j == 0)
    def _():
        h = _norm_modulate(x_ref[...], ng_ref[...], sh_ref[0], sc_ref[0])
        h_ref[...] = h.astype(BF16)
        acc_ref[...] = jnp.zeros_like(acc_ref)

    h = h_ref[...]
    a = _dot(h, w1_ref[...])
    b = _dot(h, w3_ref[...])
    u = (a * (1.0 / (1.0 + jnp.exp(-a))) * b).astype(BF16)
    acc_ref[...] += _dot(u, w2_ref[...])

    @pl.when(j == pl.num_programs(1) - 1)
    def _():
        o_ref[...] = x_ref[...] + 0.5 * gt_ref[0] * acc_ref[...]


def _ffn_call(x2, ng, sh, sc, gt, w1, w3, w2, seq):
    m, d = x2.shape
    dff = w1.shape[1]
    tm = 1024
    tf = dff // 2 if (dff // 2) % LANES == 0 else MXU_N
    bidx = lambda i, j: ((i * tm) // seq, 0, 0)
    return pl.pallas_call(
        _ffn_kernel,
        grid=(m // tm, dff // tf),
        in_specs=[pl.BlockSpec((tm, d), lambda i, j: (i, 0)),
                  pl.BlockSpec((1, d), lambda i, j: (0, 0)),
                  pl.BlockSpec((1, 1, d), bidx),
                  pl.BlockSpec((1, 1, d), bidx),
                  pl.BlockSpec((1, 1, d), bidx),
                  pl.BlockSpec((d, tf), lambda i, j: (0, j)),
                  pl.BlockSpec((d, tf), lambda i, j: (0, j)),
                  pl.BlockSpec((tf, d), lambda i, j: (j, 0))],
        out_specs=pl.BlockSpec((tm, d), lambda i, j: (i, 0)),
        out_shape=jax.ShapeDtypeStruct((m, d), F32),
        scratch_shapes=[pltpu.VMEM((tm, d), BF16), pltpu.VMEM((tm, d), F32)],
        compiler_params=_cparams(("parallel", "arbitrary")),
        name="swiglu_ffn",
    )(x2, ng, sh, sc, gt, w1, w3, w2)


def _proj_kernel(x_ref, ng_ref, sh_ref, sc_ref, w_ref, qkg_ref, cos_ref, s1_ref, s2_ref, bd_ref,
                 f_ref, h_ref, km_ref, vt_ref, hs_ref):
    hs_ref[...] = _norm_modulate(x_ref[...], ng_ref[...], sh_ref[0], sc_ref[0]).astype(BF16)
    cos, s1, s2 = cos_ref[...], s1_ref[...], s2_ref[...]
    bd = bd_ref[...]
    tm = x_ref.shape[0]

    def rope(v):
        return v * cos + pltpu.roll(v, LANES - ROPE_HALF, 1) * s1 + pltpu.roll(v, ROPE_HALF, 1) * s2

    def headnorm(v, g):
        a, b, _ = _split3(v * v)
        ss = _dot(a, bd) + _dot(b, bd)
        return v * lax.rsqrt(ss * (1.0 / HEAD_DIM) + RMS_EPS) * g

    def chunk(c):
        y = _dot(hs_ref[...], w_ref[:, c * MXU_N:(c + 1) * MXU_N])
        return y[:, :LANES], y[:, LANES:]

    def normrope_chunk(c, grow):
        g = qkg_ref[grow:grow + 1, :]
        return [rope(headnorm(v, g)) for v in chunk(c)]

    def put(ref, off, halves, dtype):
        for k, v in enumerate(halves):
            ref[:, off + k * LANES: off + (k + 1) * LANES] = v.astype(dtype)

    def put_vt(base, halves, with_ones):
        ones = jnp.ones((HEAD_DIM, TK), BF16)
        for r in range(tm // TK):
            for k, v in enumerate(halves):
                t = v[r * TK:(r + 1) * TK, :].T.astype(BF16)
                if with_ones:
                    for hh in range(2):
                        row = base + (2 * k + hh) * LANES
                        vt_ref[0, r, row:row + HEAD_DIM, :] = t[hh * HEAD_DIM:(hh + 1) * HEAD_DIM, :]
                        vt_ref[0, r, row + HEAD_DIM:row + LANES, :] = ones
                else:
                    vt_ref[0, r, base + k * LANES:base + (k + 1) * LANES, :] = t

    put(f_ref, F_QA, normrope_chunk(0, 0), F32)
    put(f_ref, F_QB, normrope_chunk(1, 2), F32)
    put(f_ref, F_QC, chunk(2), F32)
    put(f_ref, F_QC + MXU_N, chunk(3), F32)

    ka = normrope_chunk(4, 1)
    put(h_ref, H_KA, ka, BF16)
    rows = lax.broadcasted_iota(jnp.int32, (8, LANES), 0)
    for k, v in enumerate(ka):
        km = jnp.zeros((8, LANES), F32)
        for r in range(tm // MOBA_BLOCK):
            s = jnp.sum(v[r * MOBA_BLOCK:(r + 1) * MOBA_BLOCK, :], axis=0, keepdims=True) * (1.0 / MOBA_BLOCK)
            km = jnp.where(rows == r, s, km)
        km_ref[0, :, k * LANES:(k + 1) * LANES] = km

    put_vt(VT_A, chunk(5), True)
    put(h_ref, H_KB, normrope_chunk(6, 3), BF16)
    put_vt(VT_B, chunk(7), True)
    put(h_ref, H_KC, chunk(8), BF16)
    put(h_ref, H_KC + MXU_N, chunk(9), BF16)
    put_vt(VT_C, chunk(10), False)
    put_vt(VT_C + MXU_N, chunk(11), False)
    put(h_ref, H_QI, [rope(v) for v in chunk(12)], BF16)
    put(h_ref, H_QI + MXU_N, [rope(v) for v in chunk(13)], BF16)
    ki, wi = chunk(14)
    h_ref[:, H_KI:H_KI + LANES] = rope(ki).astype(BF16)
    f_ref[:, F_WI:F_WI + LANES] = wi


def _proj_call(x2, ng, sh, sc, w_perm, qkg, cos_t, s1_t, s2_t, bd, batch, seq):
    m, d = x2.shape
    tm = PROJ_TM
    nt = seq // tm
    bidx = lambda i: ((i * tm) // seq, 0, 0)
    tab = pl.BlockSpec((tm, LANES), lambda i: (i % nt, 0))
    return pl.pallas_call(
        _proj_kernel,
        grid=(m // tm,),
        in_specs=[pl.BlockSpec((tm, d), lambda i: (i, 0)),
                  pl.BlockSpec((1, d), lambda i: (0, 0)),
                  pl.BlockSpec((1, 1, d), bidx),
                  pl.BlockSpec((1, 1, d), bidx),
                  pl.BlockSpec((d, W_PERM_WIDTH), lambda i: (0, 0)),
                  pl.BlockSpec((8, LANES), lambda i: (0, 0)),
                  tab, tab, tab,
                  pl.BlockSpec((LANES, LANES), lambda i: (0, 0))],
        out_specs=[pl.BlockSpec((tm, F_WIDTH), lambda i: (i, 0)),
                   pl.BlockSpec((tm, H_WIDTH), lambda i: (i, 0)),
                   pl.BlockSpec((1, 8, 2 * LANES), lambda i: (i, 0, 0)),
                   pl.BlockSpec((1, tm // TK, VT_ROWS, TK), lambda i: (i // nt, i % nt, 0, 0))],
        out_shape=[jax.ShapeDtypeStruct((m, F_WIDTH), F32),
                   jax.ShapeDtypeStruct((m, H_WIDTH), BF16),
                   jax.ShapeDtypeStruct((m // tm, 8, 2 * LANES), F32),
                   jax.ShapeDtypeStruct((batch, seq // TK, VT_ROWS, TK), BF16)],
        scratch_shapes=[pltpu.VMEM((tm, d), BF16)],
        compiler_params=_cparams(("parallel",)),
        name="in_proj",
    )(x2, ng, sh, sc, w_perm, qkg, cos_t, s1_t, s2_t, bd)


def _head_masks():
    lane = lax.broadcasted_iota(jnp.int32, (1, LANES), 1)
    return [(lane >= HEAD_DIM * h) & (lane < HEAD_DIM * (h + 1)) for h in range(2)]


def _head_rms_t(o):
    ss = jnp.sum(o * o, axis=0, keepdims=True) * (1.0 / HEAD_DIM)
    return o * lax.rsqrt(ss + RMS_EPS)


def _finish_softmax_head(acc):
    return _head_rms_t(acc[:HEAD_DIM, :] / acc[HEAD_DIM:HEAD_DIM + 1, :])


def _moba_kernel(q_ref, km_ref, k_ref, vt_ref, o_ref, bias_ref, qs_ref, acc_ref, kmax2_ref):
    i = pl.program_id(2)
    nb = km_ref.shape[1]
    q = q_ref[...]
    kma, kmb, _ = _split3(km_ref[0])
    blk = lax.broadcasted_iota(jnp.int32, (nb, TQ), 0).astype(F32)
    i_f = i.astype(F32)
    krow = lax.broadcasted_iota(jnp.int32, (TK, TQ), 0)
    qcol = lax.broadcasted_iota(jnp.int32, (TK, TQ), 1)
    causal = krow <= qcol

    for h, hm in enumerate(_head_masks()):
        qh_t = jnp.where(hm, q, 0.0).T
        qa, qb, _ = _split3(qh_t)
        gate = _dot(kma, qa) + (_dot(kmb, qa) + _dot(kma, qb))
        gate = jnp.where(blk < i_f, gate, NEG)
        sel = jnp.zeros((nb, TQ), F32)
        for _ in range(MOBA_TOPK):
            mx = jnp.max(gate, axis=0, keepdims=True)
            idx = jnp.min(jnp.where(gate == mx, blk, float(nb)), axis=0, keepdims=True)
            pick = blk == idx
            sel = jnp.where(pick, 1.0, sel)
            gate = jnp.where(pick, -jnp.inf, gate)
        bias_ref[h] = jnp.where((sel > 0.5) & (blk < i_f), 0.0, NEG)
        qs_ref[h] = (qh_t * ATT_SCALE).astype(BF16)

    head_of = lambda a: lax.shift_right_logical(lax.broadcasted_iota(jnp.int32, (LANES, LANES), a), 6)
    ones_bd = jnp.where(head_of(0) == head_of(1), 1.0, 0.0).astype(BF16)

    @pl.when(i == 0)
    def _():
        def norm_body(j, mx):
            kk = k_ref[pl.ds(pl.multiple_of(j * TK, TK), TK), :].astype(F32)
            return jnp.maximum(mx, jnp.max(_dot((kk * kk).astype(BF16), ones_bd), axis=0, keepdims=True))
        kmax2_ref[...] = lax.fori_loop(0, nb, norm_body, jnp.zeros((1, LANES), F32))

    q_norm2 = _dot((q * q).astype(BF16), ones_bd)
    bound_t = (jnp.sqrt(q_norm2 * kmax2_ref[...]) * (ATT_SCALE * BOUND_SLACK)).T
    m_fix = [bound_t[h * HEAD_DIM:h * HEAD_DIM + 1, :] for h in range(2)]

    own = pl.multiple_of(i * TK, TK)
    heads = range(2)
    grp = range(MOBA_GROUP)
    n_groups = (i + MOBA_GROUP - 1) // MOBA_GROUP
    last_group = km_ref.shape[1] // MOBA_GROUP - 1

    def scores(jg):
        off = pl.multiple_of(jnp.minimum(jg, last_group) * (MOBA_GROUP * TK), MOBA_GROUP * TK)
        kb = k_ref[pl.ds(off, MOBA_GROUP * TK), :]
        return tuple(_dot(kb, qs_ref[h]) for h in heads)

    def own_block(refs):
        k_own = k_ref[pl.ds(own, TK), :]
        used = []
        for h in heads:
            s = jnp.where(causal, _dot(k_own, qs_ref[h]), NEG)
            m0 = jnp.max(s, axis=0, keepdims=True) if refs is None else refs[h]
            acc_ref[h] = _dot(vt_ref[0, i, h * LANES:(h + 1) * LANES, :], jnp.exp(s - m0).astype(BF16))
            used.append(m0)
        return tuple(used)

    def body_fixed(jg, ss):
        ss_next = scores(jg + 1)
        base = jg * MOBA_GROUP
        for h in heads:
            pv = None
            for g in grp:
                b = bias_ref[h, pl.ds(base + g, 1), :]
                p = jnp.exp(ss[h][g * TK:(g + 1) * TK, :] - (m_fix[h] - b)).astype(BF16)
                d = _dot(vt_ref[0, base + g, h * LANES:(h + 1) * LANES, :], p)
                pv = d if pv is None else pv + d
            acc_ref[h] += pv
        return ss_next

    def body(jg, carry):
        ms, ss = carry
        ss_next = scores(jg + 1)
        base = jg * MOBA_GROUP
        sg = [[ss[h][g * TK:(g + 1) * TK, :] for g in grp] for h in heads]
        bs = [[bias_ref[h, pl.ds(base + g, 1), :] for g in grp] for h in heads]
        new = []
        for h in heads:
            m_new = ms[h]
            for g in grp:
                m_new = jnp.maximum(m_new, jnp.max(sg[h][g], axis=0, keepdims=True) + bs[h][g])
            new.append(m_new)
        ps = [[jnp.exp(sg[h][g] - (new[h] - bs[h][g])).astype(BF16) for g in grp] for h in heads]
        for h in heads:
            pv = None
            for g in grp:
                d = _dot(vt_ref[0, base + g, h * LANES:(h + 1) * LANES, :], ps[h][g])
                pv = d if pv is None else pv + d
            acc_ref[h] = jnp.exp(ms[h] - new[h]) * acc_ref[h] + pv
        return tuple(new), ss_next

    def run_online():
        lax.fori_loop(0, n_groups, body, (own_block(None), scores(0)))
        return 0

    def run_fixed():
        own_block(m_fix)
        lax.fori_loop(0, n_groups, body_fixed, scores(0))
        return 0

    lax.cond(jnp.max(bound_t) <= SAFE_SHIFT, run_fixed, run_online)
    out_t = jnp.concatenate([_finish_softmax_head(acc_ref[h]) for h in range(2)], axis=0)
    o_ref[...] = out_t.T


def _moba_call(f_arr, h_arr, kmean, vt, batch, seq):
    nq = seq // TQ
    nb = seq // MOBA_BLOCK
    npair = N_HEADS_A // 2
    return pl.pallas_call(
        _moba_kernel,
        grid=(batch, npair, nq),
        in_specs=[pl.BlockSpec((TQ, LANES), lambda b, p, i: (b * nq + i, F_QA // LANES + p)),
                  pl.BlockSpec((1, nb, LANES), lambda b, p, i: (b, 0, p)),
                  pl.BlockSpec((seq, LANES), lambda b, p, i: (b, H_KA // LANES + p)),
                  pl.BlockSpec((1, seq // TK, 2 * LANES, TK), lambda b, p, i: (b, 0, VT_A // (2 * LANES) + p, 0))],
        out_specs=pl.BlockSpec((TQ, LANES), lambda b, p, i: (b * nq + i, p)),
        out_shape=jax.ShapeDtypeStruct((batch * seq, N_HEADS_A * HEAD_DIM), F32),
        scratch_shapes=[pltpu.VMEM((2, nb, TQ), F32),
                        pltpu.VMEM((2, LANES, TQ), BF16),
                        pltpu.VMEM((2, LANES, TQ), F32),
                        pltpu.VMEM((1, LANES), F32)],
        compiler_params=_cparams(("parallel", "parallel", "arbitrary")),
        name="moba_attn",
    )(f_arr, kmean, h_arr, vt)


def _sb_kernel(q_ref, k_ref, vt_ref, u_ref, o_ref, qs_ref, acc_ref):
    i = pl.program_id(2)
    krow = lax.broadcasted_iota(jnp.int32, (TK, TQ), 0)
    qcol = lax.broadcasted_iota(jnp.int32, (TK, TQ), 1)
    strict = krow < qcol
    hms = _head_masks()
    for h in range(SB_HEADS):
        pr = slice((h // 2) * LANES, (h // 2 + 1) * LANES)
        qs_ref[h] = (jnp.where(hms[h % 2], q_ref[:, pr], 0.0) * ATT_SCALE).T.astype(BF16)
        acc_ref[h] = jnp.zeros((LANES, TQ), F32)

    heads = range(SB_HEADS)
    pairs = [slice((h // 2) * LANES, (h // 2 + 1) * LANES) for h in heads]

    def logits(j):
        off = pl.multiple_of(jnp.maximum(j, 0) * TK, TK)
        return tuple(_dot(k_ref[pl.ds(off, TK), pairs[h]], qs_ref[h]) for h in heads)

    def block(j, rs, zs, diag):
        zs_next = logits(j - 1)
        u = u_ref[...]
        log_betas, his, los = [], [], []
        for h in heads:
            z = zs[h]
            log_beta = jnp.minimum(z, 0.0) - jnp.log(1.0 + jnp.exp(-jnp.abs(z)))
            log_1m = log_beta - z
            if diag:
                log_1m = jnp.where(strict, log_1m, 0.0)
            hi = log_1m.astype(BF16)
            log_betas.append(log_beta)
            his.append(hi)
            los.append((log_1m - hi.astype(F32)).astype(BF16))
        sums = [_dot(u, his[h]) + _dot(u, los[h]) for h in heads]
        weights = []
        for h in heads:
            after = sums[h][:TK, :] + rs[h]
            a = jnp.exp(log_betas[h] + after)
            if diag:
                a = jnp.where(strict, a, 0.0)
            weights.append(a.astype(BF16))
        for h in heads:
            acc_ref[h] += _dot(vt_ref[0, j, pairs[h], :], weights[h])
        return tuple(rs[h] + sums[h][TK:TK + 1, :] for h in heads), zs_next

    zero = jnp.zeros((1, TQ), F32)
    rs, zs = block(i, (zero,) * SB_HEADS, logits(i), True)

    def alive(rs):
        m = rs[0]
        for r in rs[1:]:
            m = jnp.maximum(m, r)
        return jnp.max(m)

    def cond(c):
        return (c[0] >= 0) & (c[1] > SB_DEAD_LOG)

    def body(c):
        rs, zs = block(c[0], c[2], c[3], False)
        return c[0] - 1, alive(rs), rs, zs

    lax.while_loop(cond, body, (i - 1, alive(rs), rs, zs))
    out_t = jnp.concatenate(
        [_head_rms_t(acc_ref[h][(h % 2) * HEAD_DIM:(h % 2 + 1) * HEAD_DIM, :]) for h in range(SB_HEADS)], axis=0)
    o_ref[...] = out_t.T


def _sb_call(f_arr, h_arr, vt, u_mat, batch, seq):
    nq = seq // TQ
    w = SB_HEADS * HEAD_DIM
    ngrp = N_HEADS_C // SB_HEADS
    return pl.pallas_call(
        _sb_kernel,
        grid=(batch, ngrp, nq),
        in_specs=[pl.BlockSpec((TQ, w), lambda b, p, i: (b * nq + i, F_QC // w + p)),
                  pl.BlockSpec((seq, w), lambda b, p, i: (b, H_KC // w + p)),
                  pl.BlockSpec((1, seq // TK, w, TK), lambda b, p, i: (b, 0, VT_C // w + p, 0)),
                  pl.BlockSpec((TK + SB_TAIL, TK), lambda b, p, i: (0, 0))],
        out_specs=pl.BlockSpec((TQ, w), lambda b, p, i: (b * nq + i, p)),
        out_shape=jax.ShapeDtypeStruct((batch * seq, N_HEADS_C * HEAD_DIM), F32),
        scratch_shapes=[pltpu.VMEM((SB_HEADS, LANES, TQ), BF16), pltpu.VMEM((SB_HEADS, LANES, TQ), F32)],
        compiler_params=_cparams(("parallel", "parallel", "arbitrary")),
        name="stickbreak_attn",
    )(f_arr, h_arr, vt, u_mat)


def _sortable_key(x):
    bits = lax.bitcast_convert_type(x, jnp.int32)
    return jnp.where(bits < 0, bits ^ jnp.int32(0x7FFFFFFF), bits)


_NEG_BITS = int(np.array(NEG, np.float32).view(np.int32))
NEG_KEY = _NEG_BITS ^ 0x7FFFFFFF
INT_MIN = -2 ** 31
MIN_NORMAL_BITS = 0x00800000


def _dsa_kernel(qi_ref, wi_ref, ki_ref, q_ref, k_ref, vt_ref, o_ref,
                key_ref, coarse_ref, qx_ref, qs_ref, acc_ref, cnt_ref, m_ref, kmax2_ref):
    i = pl.program_id(1)
    seq = k_ref.shape[0]
    topk = min(DSA_TOPK, seq // 4)
    nh = N_HEADS_B
    lane = lax.broadcasted_iota(jnp.int32, (1, LANES), 1)
    low = lane < HEAD_DIM
    krow = lax.broadcasted_iota(jnp.int32, (TK, TQ), 0)
    qcol = lax.broadcasted_iota(jnp.int32, (TK, TQ), 1)
    diag_causal = krow <= qcol
    hms = _head_masks()

    qi = qi_ref[...].astype(F32)
    for h in range(IDX_HEADS):
        qp = qi[:, (h // 2) * LANES:(h // 2 + 1) * LANES]
        if h % 2:
            qp = pltpu.roll(qp, HEAD_DIM, 1)
        qx_ref[h] = jnp.where(low, qp, 0.0).T.astype(BF16)
    w_t = wi_ref[...].T
    q = q_ref[...]
    for h in range(nh):
        qh = jnp.where(hms[h % 2], q[:, (h // 2) * LANES:(h // 2 + 1) * LANES], 0.0)
        qs_ref[h] = (qh * ATT_SCALE).T.astype(BF16)

    wq = q.shape[1]
    head_of = lambda a: lax.shift_right_logical(lax.broadcasted_iota(jnp.int32, (wq, wq), a), 6)
    ones_bd = jnp.where(head_of(0) == head_of(1), 1.0, 0.0).astype(BF16)

    @pl.when(i == 0)
    def _():
        def body(j, mx):
            kk = k_ref[pl.ds(pl.multiple_of(j * TK, TK), TK), :].astype(F32)
            return jnp.maximum(mx, jnp.max(_dot((kk * kk).astype(BF16), ones_bd), axis=0, keepdims=True))
        kmax2_ref[...] = lax.fori_loop(0, seq // TK, body, jnp.zeros((1, wq), F32))

    q_norm2 = _dot((q * q).astype(BF16), ones_bd)
    bound_t = (jnp.sqrt(q_norm2 * kmax2_ref[...]) * (ATT_SCALE * BOUND_SLACK)).T
    m_fix = [bound_t[h * HEAD_DIM:h * HEAD_DIM + 1, :] for h in range(nh)]

    def score_block(j, diag):
        off = pl.multiple_of(j * TK, TK)
        kz = ki_ref[pl.ds(off, TK), :]
        sc = jnp.zeros((TK, TQ), F32)
        for h in range(IDX_HEADS):
            sc = sc + w_t[h:h + 1, :] * jnp.maximum(_dot(kz, qx_ref[h]), 0.0)
        if diag:
            sc = jnp.where(diag_causal, sc, NEG)
        bits = lax.bitcast_convert_type(sc, jnp.int32)
        bits = jnp.where(bits == INT_MIN, 0, bits)
        key_ref[j] = jnp.where(bits < 0, bits ^ jnp.int32(0x7FFFFFFF), bits)
        coarse_ref[j] = lax.bitcast_convert_type(bits & jnp.int32(-65536), F32).astype(BF16)

    def score_pair(jj, c):
        score_block(2 * jj, False)
        score_block(2 * jj + 1, False)
        return c

    lax.fori_loop(0, i // 2, score_pair, 0)

    @pl.when(i % 2 == 1)
    def _():
        score_block(i - 1, False)

    score_block(i, True)
    for g in range(1, DSA_GROUP):
        @pl.when(i % DSA_GROUP + g < DSA_GROUP)
        def _():
            key_ref[i + g] = jnp.full((TK, TQ), NEG_KEY, jnp.int32)

    n_unscanned = (seq - (i + 1) * TK).astype(F32)
    fold = TK // 4

    def count(pred):
        def body(j, acc):
            hit = pred(key_ref[j], j)
            for r in range(TK // fold):
                acc = jnp.where(hit[r * fold:(r + 1) * fold, :], acc + 1.0, acc)
            return acc
        acc = lax.fori_loop(0, i + 1, body, jnp.zeros((fold, TQ), F32))
        return jnp.sum(acc, axis=0, keepdims=True)

    def count_ge(cand):
        return count(lambda key, j: key >= cand) + jnp.where(cand <= NEG_KEY, n_unscanned, 0.0)

    kf = float(topk)

    def count_ge_coarse(cand):
        cand_bits = jnp.where(cand < 0, (cand ^ jnp.int32(0x7FFFFFFF)) & jnp.int32(-65536), cand)
        cand_bits = jnp.where((cand > 0) & (cand < MIN_NORMAL_BITS), MIN_NORMAL_BITS, cand_bits)
        cand_b = lax.bitcast_convert_type(cand_bits, F32).astype(BF16)
        one, zero = jnp.ones((), BF16), jnp.zeros((), BF16)

        def body(j, acc):
            hit = jnp.where(coarse_ref[j] >= cand_b, one, zero)
            return acc + ((hit[:fold] + hit[fold:2 * fold]) + (hit[2 * fold:3 * fold] + hit[3 * fold:]))
        acc = lax.fori_loop(0, i + 1, body, jnp.zeros((fold, TQ), BF16))
        return (jnp.sum(acc.astype(F32), axis=0, keepdims=True)
                + jnp.where(cand <= NEG_KEY, n_unscanned, 0.0))

    c0 = count_ge_coarse(jnp.zeros((1, TQ), jnp.int32))
    t0 = jnp.where(c0 >= kf, 0, INT_MIN).astype(jnp.int32)
    ct0 = jnp.where(c0 >= kf, c0, float(seq))

    def bit_step(b, t, ct, counter):
        cand = t | jnp.left_shift(jnp.int32(1), 30 - b)
        cc = counter(cand)
        ok = cc >= kf
        return jnp.where(ok, cand, t), jnp.where(ok, cc, ct)

    t, ct = lax.fori_loop(0, BISECT_COARSE, lambda b, c: bit_step(b, *c, count_ge_coarse), (t0, ct0))
    t, ct = lax.fori_loop(BISECT_COARSE, BISECT_COARSE + BISECT_FIXED,
                          lambda b, c: bit_step(b, *c, count_ge), (t, ct))

    zero_final = (c0 >= kf) & (count_ge_coarse(jnp.full((1, TQ), MIN_NORMAL_BITS, jnp.int32)) < kf)

    def pending(ct):
        return jnp.max(jnp.where((ct > kf) & jnp.logical_not(zero_final), 1.0, 0.0))

    def bit_body(c):
        t, ct = bit_step(c[0], c[2], c[3], count_ge)
        t, ct = bit_step(c[0] + 1, t, ct, count_ge)
        return c[0] + 2, pending(ct), t, ct

    assert (31 - BISECT_COARSE - BISECT_FIXED) % 2 == 0
    _, _, t, c_ge = lax.while_loop(lambda c: (c[0] < 31) & (c[1] > 0.5), bit_body,
                                   (jnp.int32(BISECT_COARSE + BISECT_FIXED), pending(ct), t, ct))

    def tie_limit():
        need = kf - count_ge(t + 1)

        def block_ties(j, c):
            hit = key_ref[j] == t
            acc = jnp.zeros((fold, TQ), F32)
            for r in range(TK // fold):
                acc = jnp.where(hit[r * fold:(r + 1) * fold, :], acc + 1.0, acc)
            cnt_ref[pl.ds(j, 1), :] = jnp.sum(acc, axis=0, keepdims=True)
            return c

        lax.fori_loop(0, i + 1, block_ties, 0)
        nkb = cnt_ref.shape[0]
        cnt = jnp.where(lax.broadcasted_iota(jnp.int32, (nkb, TQ), 0) <= i, cnt_ref[...], 0.0)
        run = jnp.zeros((1, TQ), F32)
        blk_of = jnp.zeros((1, TQ), F32)
        before = jnp.zeros((1, TQ), F32)
        for j in range(nkb):
            run = run + cnt[j:j + 1, :]
            ahead = run < need
            blk_of = blk_of + jnp.where(ahead, 1.0, 0.0)
            before = jnp.where(ahead, run, before)
        blk_of = blk_of.astype(jnp.int32)

        def pick_block(j, c):
            m_ref[...] = jnp.where((blk_of == j) & (key_ref[j] == t), 1.0, m_ref[...])
            return c

        m_ref[...] = jnp.zeros_like(m_ref)
        lax.fori_loop(0, i + 1, pick_block, 0)
        tri = (lax.broadcasted_iota(jnp.int32, (TK, TK), 0) >= lax.broadcasted_iota(jnp.int32, (TK, TK), 1))
        upto = _dot(jnp.where(tri, 1.0, 0.0).astype(BF16), m_ref[...].astype(BF16))
        row_of = jnp.sum(jnp.where(upto < need - before, 1.0, 0.0), axis=0, keepdims=True)
        return blk_of * TK + row_of.astype(jnp.int32)

    x = lax.cond(jnp.max(c_ge) > kf, tie_limit, lambda: jnp.full((1, TQ), 2 * seq, jnp.int32))

    heads = range(nh)
    grp = range(DSA_GROUP)
    pairs = [slice((h // 2) * LANES, (h // 2 + 1) * LANES) for h in heads]
    n_full = i // DSA_GROUP

    def scores(jg):
        off = pl.multiple_of(jg * (DSA_GROUP * TK), DSA_GROUP * TK)
        return tuple(_dot(k_ref[pl.ds(off, DSA_GROUP * TK), pairs[h]], qs_ref[h]) for h in heads)

    def mask_biases(base, causal):
        biases = []
        for g in grp:
            key = key_ref[base + g]
            pos = (base + g) * TK + krow
            msk = (key > t) | ((key == t) & (pos <= x))
            if causal:
                msk = msk & (pos <= i * TQ + qcol)
            biases.append(jnp.where(msk, 0.0, NEG))
        return biases

    def attend_fixed(jg, ss, causal):
        ss_next = None if causal else scores(jg + 1)
        base = jg * DSA_GROUP
        biases = mask_biases(base, causal)
        for h in heads:
            pv = None
            for g in grp:
                p = jnp.exp((ss[h][g * TK:(g + 1) * TK, :] - m_fix[h]) + biases[g]).astype(BF16)
                d = _dot(vt_ref[0, base + g, h * LANES:(h + 1) * LANES, :], p)
                pv = d if pv is None else pv + d
            acc_ref[h] += pv
        return ss_next

    def attend(jg, carry, causal):
        ms, ss = carry
        ss_next = None if causal else scores(jg + 1)
        base = jg * DSA_GROUP
        biases = mask_biases(base, causal)
        sg = [[ss[h][g * TK:(g + 1) * TK, :] for g in grp] for h in heads]
        new = []
        for h in heads:
            m_new = ms[h]
            for g in grp:
                m_new = jnp.maximum(m_new, jnp.max(sg[h][g] + biases[g], axis=0, keepdims=True))
            new.append(m_new)
        ps = [[jnp.exp((sg[h][g] - new[h]) + biases[g]).astype(BF16) for g in grp] for h in heads]
        for h in heads:
            pv = None
            for g in grp:
                d = _dot(vt_ref[0, base + g, h * LANES:(h + 1) * LANES, :], ps[h][g])
                pv = d if pv is None else pv + d
            acc_ref[h] = jnp.exp(ms[h] - new[h]) * acc_ref[h] + pv
        return tuple(new), ss_next

    for h in heads:
        acc_ref[h] = jnp.zeros((LANES, TQ), F32)

    def run_online():
        carry = lax.fori_loop(0, n_full, lambda jg, c: attend(jg, c, False),
                              (tuple(jnp.full((1, TQ), NEG, F32) for _ in heads), scores(0)))
        attend(n_full, carry, True)
        return 0

    def run_fixed():
        ss = lax.fori_loop(0, n_full, lambda jg, c: attend_fixed(jg, c, False), scores(0))
        attend_fixed(n_full, ss, True)
        return 0

    lax.cond(jnp.max(bound_t) <= SAFE_SHIFT, run_fixed, run_online)

    out_t = jnp.concatenate([_finish_softmax_head(acc_ref[h]) for h in range(nh)], axis=0)
    o_ref[...] = out_t.T


def _dsa_call(f_arr, h_arr, vt, batch, seq):
    nq = seq // TQ
    wb = N_HEADS_B * HEAD_DIM
    return pl.pallas_call(
        _dsa_kernel,
        grid=(batch, nq),
        in_specs=[pl.BlockSpec((TQ, IDX_HEADS * IDX_DIM), lambda b, i: (b * nq + i, H_QI // (IDX_HEADS * IDX_DIM))),
                  pl.BlockSpec((TQ, LANES), lambda b, i: (b * nq + i, F_WI // LANES)),
                  pl.BlockSpec((seq, LANES), lambda b, i: (b, H_KI // LANES)),
                  pl.BlockSpec((TQ, wb), lambda b, i: (b * nq + i, F_QB // wb)),
                  pl.BlockSpec((seq, wb), lambda b, i: (b, H_KB // wb)),
                  pl.BlockSpec((1, seq // TK, N_HEADS_B * LANES, TK), lambda b, i: (b, 0, VT_B // (N_HEADS_B * LANES), 0))],
        out_specs=pl.BlockSpec((TQ, wb), lambda b, i: (b * nq + i, 0)),
        out_shape=jax.ShapeDtypeStruct((batch * seq, wb), F32),
        scratch_shapes=[pltpu.VMEM((seq // TK, TK, TQ), jnp.int32),
                        pltpu.VMEM((seq // TK, TK, TQ), BF16),
                        pltpu.VMEM((IDX_HEADS, LANES, TQ), BF16),
                        pltpu.VMEM((N_HEADS_B, LANES, TQ), BF16),
                        pltpu.VMEM((N_HEADS_B, LANES, TQ), F32),
                        pltpu.VMEM((seq // TK, TQ), F32),
                        pltpu.VMEM((TK, TQ), F32),
                        pltpu.VMEM((1, wb), F32)],
        compiler_params=_cparams(("parallel", "arbitrary")),
        name="dsa_attn",
    )(h_arr, f_arr, h_arr, f_arr, h_arr, vt)


def _outproj_kernel(x_ref, oa_ref, ob_ref, oc_ref, og_ref, gt_ref, w_ref, o_ref):
    wa = N_HEADS_A * HEAD_DIM
    wb = wa + N_HEADS_B * HEAD_DIM
    og = og_ref[...]
    y = (_dot((oa_ref[...] * og[:, :wa]).astype(BF16), w_ref[:wa, :])
         + _dot((ob_ref[...] * og[:, wa:wb]).astype(BF16), w_ref[wa:wb, :])
         + _dot((oc_ref[...] * og[:, wb:]).astype(BF16), w_ref[wb:, :]))
    o_ref[...] = x_ref[...] + gt_ref[0] * y


def _outproj_call(x2, oa, ob, oc, og, gt, w_out, seq):
    m, d = x2.shape
    tm = 512
    row = lambda i: (i, 0)
    return pl.pallas_call(
        _outproj_kernel,
        grid=(m // tm,),
        in_specs=[pl.BlockSpec((tm, d), row),
                  pl.BlockSpec((tm, oa.shape[1]), row),
                  pl.BlockSpec((tm, ob.shape[1]), row),
                  pl.BlockSpec((tm, oc.shape[1]), row),
                  pl.BlockSpec((1, d), lambda i: (0, 0)),
                  pl.BlockSpec((1, 1, d), lambda i: ((i * tm) // seq, 0, 0)),
                  pl.BlockSpec((d, d), lambda i: (0, 0))],
        out_specs=pl.BlockSpec((tm, d), row),
        out_shape=jax.ShapeDtypeStruct((m, d), F32),
        compiler_params=_cparams(("parallel",)),
        name="out_proj",
    )(x2, oa, ob, oc, og, gt, w_out)


def _rope_tables(seq):
    pos = jnp.arange(seq, dtype=F32)
    inv = ROPE_THETA ** (-jnp.arange(0, ROPE_DIM, 2, dtype=F32) / ROPE_DIM)
    ang = pos[:, None] * inv[None, :]
    cos, sin = jnp.cos(ang), jnp.sin(ang)
    zeros = jnp.zeros((seq, HEAD_DIM - ROPE_DIM), F32)
    zh = jnp.zeros((seq, ROPE_HALF), F32)
    cos_h = jnp.concatenate([cos, cos, jnp.ones_like(zeros)], axis=1)
    s1_h = jnp.concatenate([-sin, zh, zeros], axis=1)
    s2_h = jnp.concatenate([zh, sin, zeros], axis=1)
    two = lambda t: jnp.concatenate([t, t], axis=1)
    return two(cos_h), two(s1_h), two(s2_h)


def _permute_w_in(w):
    d = w.shape[0]
    a, b, c = N_HEADS_A * HEAD_DIM, N_HEADS_B * HEAD_DIM, N_HEADS_C * HEAD_DIM
    sizes = (a, a, a, b, b, b, IDX_HEADS * IDX_DIM, IDX_DIM, IDX_HEADS, c, c, c)
    offs = np.concatenate([[0], np.cumsum(sizes)])
    qa, ka, va, qb, kb, vb, qi, ki, wi, qc, kc, vc = [w[:, int(offs[k]):int(offs[k + 1])] for k in range(12)]
    z = lambda n: jnp.zeros((d, n), w.dtype)
    out = jnp.concatenate([qa, qb, qc, ka, va, kb, vb, kc, vc, qi,
                           ki, z(LANES - IDX_DIM), wi, z(LANES - IDX_HEADS)], axis=1)
    assert out.shape[1] == W_PERM_WIDTH
    return out.astype(BF16)


def _suffix_sum_matrix():
    s = np.arange(TK + SB_TAIL)[:, None]
    j = np.arange(TK)[None, :]
    return jnp.asarray((j > s) | (s >= TK), BF16)


def kernel(x, c, w_ada, b_ada, norm_g, w_in, qk_g, out_g, w_out, ffn_w1, ffn_w3, ffn_w2):
    batch, seq, d = x.shape
    depth = w_ada.shape[0]
    assert seq % PROJ_TM == 0 and d % LANES == 0 and batch <= 8
    assert (seq // TK) % MOBA_GROUP == 0 and (seq // TK) % DSA_GROUP == 0

    c_pad = jnp.zeros((8, d), F32).at[:batch].set(c)
    mod = _mod_call(c_pad, w_ada, b_ada)
    cos_t, s1_t, s2_t = _rope_tables(seq)
    hd = np.arange(LANES) // HEAD_DIM
    bd = jnp.asarray(hd[:, None] == hd[None, :], BF16)
    u_mat = _suffix_sum_matrix()

    x2 = x.reshape(batch * seq, d)
    for layer in range(depth):
        mods = [mod[layer, :batch, k * d:(k + 1) * d].reshape(batch, 1, d) for k in range(N_MOD)]
        sh1, sc1, g1, sh2, sc2, g2, sh3, sc3, g3 = mods
        ng = norm_g[layer]
        w1 = ffn_w1[layer].astype(BF16)
        w3 = ffn_w3[layer].astype(BF16)
        w2 = ffn_w2[layer].astype(BF16)

        x2 = _ffn_call(x2, ng[0:1], sh1, sc1, g1, w1[0], w3[0], w2[0], seq)

        qkg = jnp.concatenate([jnp.tile(qk_g[layer], (1, 2)), jnp.ones((4, LANES), F32)], axis=0)
        f_arr, h_arr, kmean, vt = _proj_call(x2, ng[1:2], sh2, sc2, _permute_w_in(w_in[layer]), qkg,
                                             cos_t, s1_t, s2_t, bd, batch, seq)
        per_tile = PROJ_TM // MOBA_BLOCK
        kmean = kmean[:, :per_tile, :].reshape(batch, seq // MOBA_BLOCK, N_HEADS_A * HEAD_DIM)
        oa = _moba_call(f_arr, h_arr, kmean, vt, batch, seq)
        ob = _dsa_call(f_arr, h_arr, vt, batch, seq)
        oc = _sb_call(f_arr, h_arr, vt, u_mat, batch, seq)
        x2 = _outproj_call(x2, oa, ob, oc, out_g[layer].reshape(1, d), g2, w_out[layer].astype(BF16), seq)

        x2 = _ffn_call(x2, ng[2:3], sh3, sc3, g3, w1[1], w3[1], w2[1], seq)
    return x2.reshape(batch, seq, d)
```

```python
import jax
import jax.numpy as jnp
import numpy as np
from jax import lax
from jax.experimental import pallas as pl
from jax.experimental.pallas import tpu as pltpu

F32 = jnp.float32
BF16 = jnp.bfloat16

HEAD_DIM = 64
N_HEADS_A = 4
N_HEADS_B = 4
N_HEADS_C = 8
ROPE_DIM = HEAD_DIM // 4
ROPE_HALF = ROPE_DIM // 2
ROPE_THETA = 500000.0
MOBA_BLOCK = 256
MOBA_TOPK = 3
DSA_TOPK = 256
IDX_HEADS = 8
IDX_DIM = 64
N_MOD = 9
RMS_EPS = 1e-6
NEG = -1e30
ATT_SCALE = HEAD_DIM ** -0.5

LANES = 128
MXU_N = 256
TQ = 256
TK = 256
PROJ_TM = 512
VMEM_LIMIT = 56 * 1024 * 1024

SB_DEAD_LOG = -88.0
SB_TAIL = 16
SB_HEADS = 8
MOBA_GROUP = 4
DSA_GROUP = 1
BOUND_SLACK = 1.02
SAFE_SHIFT = 40.0
BISECT_COARSE = 15
BISECT_FIXED = 6

F_QA, F_QB, F_QC, F_WI, F_WIDTH = 0, 256, 512, 1024, 1152
H_KA, H_KB, H_KC, H_QI, H_KI, H_WIDTH = 0, 256, 512, 1024, 1536, 1664
W_PERM_WIDTH = 15 * MXU_N
VT_A, VT_B, VT_C, VT_ROWS = 0, 512, 1024, 1536


def _dot(a, b):
    return jnp.dot(a, b, preferred_element_type=F32)


def _dot_nt(a, b):
    return lax.dot_general(a, b, (((1,), (1,)), ((), ())), preferred_element_type=F32)


def _split3(x):
    a = x.astype(BF16)
    r = x - a.astype(F32)
    b = r.astype(BF16)
    c = (r - b.astype(F32)).astype(BF16)
    return a, b, c


def _cparams(sem):
    return pltpu.CompilerParams(dimension_semantics=sem, vmem_limit_bytes=VMEM_LIMIT)


def _mod_kernel(c_ref, w_ref, b_ref, o_ref):
    c = c_ref[...]
    sc = c * (1.0 / (1.0 + jnp.exp(-c)))
    a, b, c3 = _split3(sc)
    w = w_ref[0]
    wa, wb, wc = _split3(w)
    acc = _dot(a, wa) + (_dot(a, wb) + _dot(b, wa)) + (_dot(a, wc) + _dot(b, wb) + _dot(c3, wa))
    o_ref[0] = acc + b_ref[0]


def _mod_call(c_pad, w_ada, b_ada):
    depth, d, n = w_ada.shape
    tn = 1024
    rows = c_pad.shape[0]
    return pl.pallas_call(
        _mod_kernel,
        grid=(depth, n // tn),
        in_specs=[pl.BlockSpec((rows, d), lambda l, j: (0, 0)),
                  pl.BlockSpec((1, d, tn), lambda l, j: (l, 0, j)),
                  pl.BlockSpec((1, 1, tn), lambda l, j: (l, 0, j))],
        out_specs=pl.BlockSpec((1, rows, tn), lambda l, j: (l, 0, j)),
        out_shape=jax.ShapeDtypeStruct((depth, rows, n), F32),
        compiler_params=_cparams(("parallel", "parallel")),
        name="adaln_mod",
    )(c_pad, w_ada, b_ada.reshape(depth, 1, n))


def _norm_modulate(x, ng, sh, sc):
    ms = jnp.mean(x * x, axis=-1, keepdims=True)
    h = x * lax.rsqrt(ms + RMS_EPS) * ng
    return h * (1.0 + sc) + sh


def _ffn_kernel(x_ref, ng_ref, sh_ref, sc_ref, gt_ref, w1_ref, w3_ref, w2_ref, o_ref, h_ref, acc_ref):
    j = pl.program_id(1)

    @pl.when(j == 0)
    def _():
        h = _norm_modulate(x_ref[...], ng_ref[...], sh_ref[0], sc_ref[0])
        h_ref[...] = h.astype(BF16)
        acc_ref[...] = jnp.zeros_like(acc_ref)

    h = h_ref[...]
    a = _dot(h, w1_ref[...])
    b = _dot(h, w3_ref[...])
    u = (a * (1.0 / (1.0 + jnp.exp(-a))) * b).astype(BF16)
    acc_ref[...] += _dot(u, w2_ref[...])

    @pl.when(j == pl.num_programs(1) - 1)
    def _():
        o_ref[...] = x_ref[...] + 0.5 * gt_ref[0] * acc_ref[...]


def _ffn_call(x2, ng, sh, sc, gt, w1, w3, w2, seq):
    m, d = x2.shape
    dff = w1.shape[1]
    tm = 1024
    tf = dff // 2 if (dff // 2) % LANES == 0 else MXU_N
    bidx = lambda i, j: ((i * tm) // seq, 0, 0)
    return pl.pallas_call(
        _ffn_kernel,
        grid=(m // tm, dff // tf),
        in_specs=[pl.BlockSpec((tm, d), lambda i, j: (i, 0)),
                  pl.BlockSpec((1, d), lambda i, j: (0, 0)),
                  pl.BlockSpec((1, 1, d), bidx),
                  pl.BlockSpec((1, 1, d), bidx),
                  pl.BlockSpec((1, 1, d), bidx),
                  pl.BlockSpec((d, tf), lambda i, j: (0, j)),
                  pl.BlockSpec((d, tf), lambda i, j: (0, j)),
                  pl.BlockSpec((tf, d), lambda i, j: (j, 0))],
        out_specs=pl.BlockSpec((tm, d), lambda i, j: (i, 0)),
        out_shape=jax.ShapeDtypeStruct((m, d), F32),
        scratch_shapes=[pltpu.VMEM((tm, d), BF16), pltpu.VMEM((tm, d), F32)],
        compiler_params=_cparams(("parallel", "arbitrary")),
        name="swiglu_ffn",
    )(x2, ng, sh, sc, gt, w1, w3, w2)


def _proj_kernel(x_ref, ng_ref, sh_ref, sc_ref, w_ref, qkg_ref, cos_ref, s1_ref, s2_ref, bd_ref,
                 f_ref, h_ref, km_ref, vt_ref, hs_ref):
    hs_ref[...] = _norm_modulate(x_ref[...], ng_ref[...], sh_ref[0], sc_ref[0]).astype(BF16)
    cos, s1, s2 = cos_ref[...], s1_ref[...], s2_ref[...]
    bd = bd_ref[...]
    tm = x_ref.shape[0]

    def rope(v):
        return v * cos + pltpu.roll(v, LANES - ROPE_HALF, 1) * s1 + pltpu.roll(v, ROPE_HALF, 1) * s2

    def headnorm(v, g):
        a, b, _ = _split3(v * v)
        ss = _dot(a, bd) + _dot(b, bd)
        return v * lax.rsqrt(ss * (1.0 / HEAD_DIM) + RMS_EPS) * g

    def chunk(c):
        y = _dot(hs_ref[...], w_ref[:, c * MXU_N:(c + 1) * MXU_N])
        return y[:, :LANES], y[:, LANES:]

    def normrope_chunk(c, grow):
        g = qkg_ref[grow:grow + 1, :]
        return [rope(headnorm(v, g)) for v in chunk(c)]

    def put(ref, off, halves, dtype):
        for k, v in enumerate(halves):
            ref[:, off + k * LANES: off + (k + 1) * LANES] = v.astype(dtype)

    def put_vt(base, halves, with_ones):
        ones = jnp.ones((HEAD_DIM, TK), BF16)
        for r in range(tm // TK):
            for k, v in enumerate(halves):
                t = v[r * TK:(r + 1) * TK, :].T.astype(BF16)
                if with_ones:
                    for hh in range(2):
                        row = base + (2 * k + hh) * LANES
                        vt_ref[0, r, row:row + HEAD_DIM, :] = t[hh * HEAD_DIM:(hh + 1) * HEAD_DIM, :]
                        vt_ref[0, r, row + HEAD_DIM:row + LANES, :] = ones
                else:
                    vt_ref[0, r, base + k * LANES:base + (k + 1) * LANES, :] = t

    put(f_ref, F_QA, normrope_chunk(0, 0), F32)
    put(f_ref, F_QB, normrope_chunk(1, 2), F32)
    put(f_ref, F_QC, chunk(2), F32)
    put(f_ref, F_QC + MXU_N, chunk(3), F32)

    ka = normrope_chunk(4, 1)
    put(h_ref, H_KA, ka, BF16)
    rows = lax.broadcasted_iota(jnp.int32, (8, LANES), 0)
    for k, v in enumerate(ka):
        km = jnp.zeros((8, LANES), F32)
        for r in range(tm // MOBA_BLOCK):
            s = jnp.sum(v[r * MOBA_BLOCK:(r + 1) * MOBA_BLOCK, :], axis=0, keepdims=True) * (1.0 / MOBA_BLOCK)
            km = jnp.where(rows == r, s, km)
        km_ref[0, :, k * LANES:(k + 1) * LANES] = km

    put_vt(VT_A, chunk(5), True)
    put(h_ref, H_KB, normrope_chunk(6, 3), BF16)
    put_vt(VT_B, chunk(7), True)
    put(h_ref, H_KC, chunk(8), BF16)
    put(h_ref, H_KC + MXU_N, chunk(9), BF16)
    put_vt(VT_C, chunk(10), False)
    put_vt(VT_C + MXU_N, chunk(11), False)
    put(h_ref, H_QI, [rope(v) for v in chunk(12)], BF16)
    put(h_ref, H_QI + MXU_N, [rope(v) for v in chunk(13)], BF16)
    ki, wi = chunk(14)
    h_ref[:, H_KI:H_KI + LANES] = rope(ki).astype(BF16)
    f_ref[:, F_WI:F_WI + LANES] = wi


def _proj_call(x2, ng, sh, sc, w_perm, qkg, cos_t, s1_t, s2_t, bd, batch, seq):
    m, d = x2.shape
    tm = PROJ_TM
    nt = seq // tm
    bidx = lambda i: ((i * tm) // seq, 0, 0)
    tab = pl.BlockSpec((tm, LANES), lambda i: (i % nt, 0))
    return pl.pallas_call(
        _proj_kernel,
        grid=(m // tm,),
        in_specs=[pl.BlockSpec((tm, d), lambda i: (i, 0)),
                  pl.BlockSpec((1, d), lambda i: (0, 0)),
                  pl.BlockSpec((1, 1, d), bidx),
                  pl.BlockSpec((1, 1, d), bidx),
                  pl.BlockSpec((d, W_PERM_WIDTH), lambda i: (0, 0)),
                  pl.BlockSpec((8, LANES), lambda i: (0, 0)),
                  tab, tab, tab,
                  pl.BlockSpec((LANES, LANES), lambda i: (0, 0))],
        out_specs=[pl.BlockSpec((tm, F_WIDTH), lambda i: (i, 0)),
                   pl.BlockSpec((tm, H_WIDTH), lambda i: (i, 0)),
                   pl.BlockSpec((1, 8, 2 * LANES), lambda i: (i, 0, 0)),
                   pl.BlockSpec((1, tm // TK, VT_ROWS, TK), lambda i: (i // nt, i % nt, 0, 0))],
        out_shape=[jax.ShapeDtypeStruct((m, F_WIDTH), F32),
                   jax.ShapeDtypeStruct((m, H_WIDTH), BF16),
                   jax.ShapeDtypeStruct((m // tm, 8, 2 * LANES), F32),
                   jax.ShapeDtypeStruct((batch, seq // TK, VT_ROWS, TK), BF16)],
        scratch_shapes=[pltpu.VMEM((tm, d), BF16)],
        compiler_params=_cparams(("parallel",)),
        name="in_proj",
    )(x2, ng, sh, sc, w_perm, qkg, cos_t, s1_t, s2_t, bd)


def _head_masks():
    lane = lax.broadcasted_iota(jnp.int32, (1, LANES), 1)
    return [(lane >= HEAD_DIM * h) & (lane < HEAD_DIM * (h + 1)) for h in range(2)]


def _head_rms_t(o):
    ss = jnp.sum(o * o, axis=0, keepdims=True) * (1.0 / HEAD_DIM)
    return o * lax.rsqrt(ss + RMS_EPS)


def _finish_softmax_head(acc):
    return _head_rms_t(acc[:HEAD_DIM, :] / acc[HEAD_DIM:HEAD_DIM + 1, :])


def _moba_kernel(q_ref, km_ref, k_ref, vt_ref, o_ref, bias_ref, qs_ref, acc_ref):
    i = pl.program_id(2)
    nb = km_ref.shape[1]
    q = q_ref[...]
    kma, kmb, _ = _split3(km_ref[0])
    blk = lax.broadcasted_iota(jnp.int32, (nb, TQ), 0).astype(F32)
    i_f = i.astype(F32)
    krow = lax.broadcasted_iota(jnp.int32, (TK, TQ), 0)
    qcol = lax.broadcasted_iota(jnp.int32, (TK, TQ), 1)
    causal = krow <= qcol

    for h, hm in enumerate(_head_masks()):
        qh_t = jnp.where(hm, q, 0.0).T
        qa, qb, _ = _split3(qh_t)
        gate = _dot(kma, qa) + (_dot(kmb, qa) + _dot(kma, qb))
        gate = jnp.where(blk < i_f, gate, NEG)
        sel = jnp.zeros((nb, TQ), F32)
        for _ in range(MOBA_TOPK):
            mx = jnp.max(gate, axis=0, keepdims=True)
            idx = jnp.min(jnp.where(gate == mx, blk, float(nb)), axis=0, keepdims=True)
            pick = blk == idx
            sel = jnp.where(pick, 1.0, sel)
            gate = jnp.where(pick, -jnp.inf, gate)
        bias_ref[h] = jnp.where((sel > 0.5) & (blk < i_f), 0.0, NEG)
        qs_ref[h] = (qh_t * ATT_SCALE).astype(BF16)

    own = pl.multiple_of(i * TK, TK)
    heads = range(2)
    grp = range(MOBA_GROUP)
    n_groups = (i + MOBA_GROUP - 1) // MOBA_GROUP
    last_group = km_ref.shape[1] // MOBA_GROUP - 1

    def scores(jg):
        off = pl.multiple_of(jnp.minimum(jg, last_group) * (MOBA_GROUP * TK), MOBA_GROUP * TK)
        kb = k_ref[pl.ds(off, MOBA_GROUP * TK), :]
        return tuple(_dot(kb, qs_ref[h]) for h in heads)

    def own_block():
        k_own = k_ref[pl.ds(own, TK), :]
        m_init = []
        for h in heads:
            s = jnp.where(causal, _dot(k_own, qs_ref[h]), NEG)
            m0 = jnp.max(s, axis=0, keepdims=True)
            acc_ref[h] = _dot(vt_ref[0, i, h * LANES:(h + 1) * LANES, :], jnp.exp(s - m0).astype(BF16))
            m_init.append(m0)
        return tuple(m_init)

    def body(jg, carry):
        ms, ss = carry
        ss_next = scores(jg + 1)
        base = jg * MOBA_GROUP
        sg = [[ss[h][g * TK:(g + 1) * TK, :] for g in grp] for h in heads]
        bs = [[bias_ref[h, pl.ds(base + g, 1), :] for g in grp] for h in heads]
        new = []
        for h in heads:
            m_new = ms[h]
            for g in grp:
                m_new = jnp.maximum(m_new, jnp.max(sg[h][g], axis=0, keepdims=True) + bs[h][g])
            new.append(m_new)
        ps = [[jnp.exp(sg[h][g] - (new[h] - bs[h][g])).astype(BF16) for g in grp] for h in heads]
        for h in heads:
            pv = None
            for g in grp:
                d = _dot(vt_ref[0, base + g, h * LANES:(h + 1) * LANES, :], ps[h][g])
                pv = d if pv is None else pv + d
            acc_ref[h] = jnp.exp(ms[h] - new[h]) * acc_ref[h] + pv
        return tuple(new), ss_next

    lax.fori_loop(0, n_groups, body, (own_block(), scores(0)))
    out_t = jnp.concatenate([_finish_softmax_head(acc_ref[h]) for h in range(2)], axis=0)
    o_ref[...] = out_t.T


def _moba_call(f_arr, h_arr, kmean, vt, batch, seq):
    nq = seq // TQ
    nb = seq // MOBA_BLOCK
    npair = N_HEADS_A // 2
    return pl.pallas_call(
        _moba_kernel,
        grid=(batch, npair, nq),
        in_specs=[pl.BlockSpec((TQ, LANES), lambda b, p, i: (b * nq + i, F_QA // LANES + p)),
                  pl.BlockSpec((1, nb, LANES), lambda b, p, i: (b, 0, p)),
                  pl.BlockSpec((seq, LANES), lambda b, p, i: (b, H_KA // LANES + p)),
                  pl.BlockSpec((1, seq // TK, 2 * LANES, TK), lambda b, p, i: (b, 0, VT_A // (2 * LANES) + p, 0))],
        out_specs=pl.BlockSpec((TQ, LANES), lambda b, p, i: (b * nq + i, p)),
        out_shape=jax.ShapeDtypeStruct((batch * seq, N_HEADS_A * HEAD_DIM), F32),
        scratch_shapes=[pltpu.VMEM((2, nb, TQ), F32),
                        pltpu.VMEM((2, LANES, TQ), BF16),
                        pltpu.VMEM((2, LANES, TQ), F32)],
        compiler_params=_cparams(("parallel", "parallel", "arbitrary")),
        name="moba_attn",
    )(f_arr, kmean, h_arr, vt)


def _sb_kernel(q_ref, k_ref, vt_ref, u_ref, o_ref, qs_ref, acc_ref):
    i = pl.program_id(2)
    krow = lax.broadcasted_iota(jnp.int32, (TK, TQ), 0)
    qcol = lax.broadcasted_iota(jnp.int32, (TK, TQ), 1)
    strict = krow < qcol
    hms = _head_masks()
    for h in range(SB_HEADS):
        pr = slice((h // 2) * LANES, (h // 2 + 1) * LANES)
        qs_ref[h] = (jnp.where(hms[h % 2], q_ref[:, pr], 0.0) * ATT_SCALE).T.astype(BF16)
        acc_ref[h] = jnp.zeros((LANES, TQ), F32)

    heads = range(SB_HEADS)
    pairs = [slice((h // 2) * LANES, (h // 2 + 1) * LANES) for h in heads]

    def logits(j):
        off = pl.multiple_of(jnp.maximum(j, 0) * TK, TK)
        return tuple(_dot(k_ref[pl.ds(off, TK), pairs[h]], qs_ref[h]) for h in heads)

    def block(j, rs, zs, diag):
        zs_next = logits(j - 1)
        u = u_ref[...]
        log_betas, his, los = [], [], []
        for h in heads:
            z = zs[h]
            log_beta = jnp.minimum(z, 0.0) - jnp.log(1.0 + jnp.exp(-jnp.abs(z)))
            log_1m = log_beta - z
            if diag:
                log_1m = jnp.where(strict, log_1m, 0.0)
            hi = log_1m.astype(BF16)
            log_betas.append(log_beta)
            his.append(hi)
            los.append((log_1m - hi.astype(F32)).astype(BF16))
        sums = [_dot(u, his[h]) + _dot(u, los[h]) for h in heads]
        weights = []
        for h in heads:
            after = sums[h][:TK, :] + rs[h]
            a = jnp.exp(log_betas[h] + after)
            if diag:
                a = jnp.where(strict, a, 0.0)
            weights.append(a.astype(BF16))
        for h in heads:
            acc_ref[h] += _dot(vt_ref[0, j, pairs[h], :], weights[h])
        return tuple(rs[h] + sums[h][TK:TK + 1, :] for h in heads), zs_next

    zero = jnp.zeros((1, TQ), F32)
    rs, zs = block(i, (zero,) * SB_HEADS, logits(i), True)

    def alive(rs):
        m = rs[0]
        for r in rs[1:]:
            m = jnp.maximum(m, r)
        return jnp.max(m)

    def cond(c):
        return (c[0] >= 0) & (c[1] > SB_DEAD_LOG)

    def body(c):
        rs, zs = block(c[0], c[2], c[3], False)
        return c[0] - 1, alive(rs), rs, zs

    lax.while_loop(cond, body, (i - 1, alive(rs), rs, zs))
    out_t = jnp.concatenate(
        [_head_rms_t(acc_ref[h][(h % 2) * HEAD_DIM:(h % 2 + 1) * HEAD_DIM, :]) for h in range(SB_HEADS)], axis=0)
    o_ref[...] = out_t.T


def _sb_call(f_arr, h_arr, vt, u_mat, batch, seq):
    nq = seq // TQ
    w = SB_HEADS * HEAD_DIM
    ngrp = N_HEADS_C // SB_HEADS
    return pl.pallas_call(
        _sb_kernel,
        grid=(batch, ngrp, nq),
        in_specs=[pl.BlockSpec((TQ, w), lambda b, p, i: (b * nq + i, F_QC // w + p)),
                  pl.BlockSpec((seq, w), lambda b, p, i: (b, H_KC // w + p)),
                  pl.BlockSpec((1, seq // TK, w, TK), lambda b, p, i: (b, 0, VT_C // w + p, 0)),
                  pl.BlockSpec((TK + SB_TAIL, TK), lambda b, p, i: (0, 0))],
        out_specs=pl.BlockSpec((TQ, w), lambda b, p, i: (b * nq + i, p)),
        out_shape=jax.ShapeDtypeStruct((batch * seq, N_HEADS_C * HEAD_DIM), F32),
        scratch_shapes=[pltpu.VMEM((SB_HEADS, LANES, TQ), BF16), pltpu.VMEM((SB_HEADS, LANES, TQ), F32)],
        compiler_params=_cparams(("parallel", "parallel", "arbitrary")),
        name="stickbreak_attn",
    )(f_arr, h_arr, vt, u_mat)


def _sortable_key(x):
    bits = lax.bitcast_convert_type(x, jnp.int32)
    return jnp.where(bits < 0, bits ^ jnp.int32(0x7FFFFFFF), bits)


_NEG_BITS = int(np.array(NEG, np.float32).view(np.int32))
NEG_KEY = _NEG_BITS ^ 0x7FFFFFFF
INT_MIN = -2 ** 31
MIN_NORMAL_BITS = 0x00800000


def _dsa_kernel(qi_ref, wi_ref, ki_ref, q_ref, k_ref, vt_ref, o_ref,
                key_ref, coarse_ref, qx_ref, qs_ref, acc_ref, cnt_ref, m_ref, kmax2_ref):
    i = pl.program_id(1)
    seq = k_ref.shape[0]
    topk = min(DSA_TOPK, seq // 4)
    nh = N_HEADS_B
    lane = lax.broadcasted_iota(jnp.int32, (1, LANES), 1)
    low = lane < HEAD_DIM
    krow = lax.broadcasted_iota(jnp.int32, (TK, TQ), 0)
    qcol = lax.broadcasted_iota(jnp.int32, (TK, TQ), 1)
    diag_causal = krow <= qcol
    hms = _head_masks()

    qi = qi_ref[...].astype(F32)
    for h in range(IDX_HEADS):
        qp = qi[:, (h // 2) * LANES:(h // 2 + 1) * LANES]
        if h % 2:
            qp = pltpu.roll(qp, HEAD_DIM, 1)
        qx_ref[h] = jnp.where(low, qp, 0.0).T.astype(BF16)
    w_t = wi_ref[...].T
    q = q_ref[...]
    q_norm2 = []
    for h in range(nh):
        qh = jnp.where(hms[h % 2], q[:, (h // 2) * LANES:(h // 2 + 1) * LANES], 0.0)
        qh_t = (qh * ATT_SCALE).T
        qs_ref[h] = qh_t.astype(BF16)
        q_norm2.append(jnp.sum(qh_t * qh_t, axis=0, keepdims=True))

    @pl.when(i == 0)
    def _():
        wk = k_ref.shape[1]
        head_of = lambda a: lax.shift_right_logical(lax.broadcasted_iota(jnp.int32, (wk, wk), a), 6)
        ones_bd = jnp.where(head_of(0) == head_of(1), 1.0, 0.0).astype(BF16)

        def body(j, mx):
            kk = k_ref[pl.ds(pl.multiple_of(j * TK, TK), TK), :].astype(F32)
            return jnp.maximum(mx, jnp.max(_dot((kk * kk).astype(BF16), ones_bd), axis=0, keepdims=True))
        kmax2_ref[...] = lax.fori_loop(0, seq // TK, body, jnp.zeros((1, wk), F32))

    m_fix = [jnp.sqrt(q_norm2[h] * kmax2_ref[0:1, h * HEAD_DIM:h * HEAD_DIM + 1] + 1e-30) * BOUND_SLACK
             for h in range(nh)]
    bound_max = jnp.max(jnp.maximum(jnp.maximum(m_fix[0], m_fix[1]), jnp.maximum(m_fix[2], m_fix[3])))

    def score_block(j, diag):
        off = pl.multiple_of(j * TK, TK)
        kz = ki_ref[pl.ds(off, TK), :]
        sc = jnp.zeros((TK, TQ), F32)
        for h in range(IDX_HEADS):
            sc = sc + w_t[h:h + 1, :] * jnp.maximum(_dot(kz, qx_ref[h]), 0.0)
        if diag:
            sc = jnp.where(diag_causal, sc, NEG)
        bits = lax.bitcast_convert_type(sc, jnp.int32)
        bits = jnp.where(bits == INT_MIN, 0, bits)
        key_ref[j] = jnp.where(bits < 0, bits ^ jnp.int32(0x7FFFFFFF), bits)
        coarse_ref[j] = lax.bitcast_convert_type(bits & jnp.int32(-65536), F32).astype(BF16)

    def score_pair(jj, c):
        score_block(2 * jj, False)
        score_block(2 * jj + 1, False)
        return c

    lax.fori_loop(0, i // 2, score_pair, 0)

    @pl.when(i % 2 == 1)
    def _():
        score_block(i - 1, False)

    score_block(i, True)
    for g in range(1, DSA_GROUP):
        @pl.when(i % DSA_GROUP + g < DSA_GROUP)
        def _():
            key_ref[i + g] = jnp.full((TK, TQ), NEG_KEY, jnp.int32)

    n_unscanned = (seq - (i + 1) * TK).astype(F32)
    fold = TK // 4

    def count(pred):
        def body(j, acc):
            hit = pred(key_ref[j], j)
            for r in range(TK // fold):
                acc = jnp.where(hit[r * fold:(r + 1) * fold, :], acc + 1.0, acc)
            return acc
        acc = lax.fori_loop(0, i + 1, body, jnp.zeros((fold, TQ), F32))
        return jnp.sum(acc, axis=0, keepdims=True)

    def count_ge(cand):
        return count(lambda key, j: key >= cand) + jnp.where(cand <= NEG_KEY, n_unscanned, 0.0)

    kf = float(topk)

    def count_ge_coarse(cand):
        cand_bits = jnp.where(cand < 0, (cand ^ jnp.int32(0x7FFFFFFF)) & jnp.int32(-65536), cand)
        cand_bits = jnp.where((cand > 0) & (cand < MIN_NORMAL_BITS), MIN_NORMAL_BITS, cand_bits)
        cand_b = lax.bitcast_convert_type(cand_bits, F32).astype(BF16)
        one, zero = jnp.ones((), BF16), jnp.zeros((), BF16)

        def body(j, acc):
            hit = jnp.where(coarse_ref[j] >= cand_b, one, zero)
            return acc + ((hit[:fold] + hit[fold:2 * fold]) + (hit[2 * fold:3 * fold] + hit[3 * fold:]))
        acc = lax.fori_loop(0, i + 1, body, jnp.zeros((fold, TQ), BF16))
        return (jnp.sum(acc.astype(F32), axis=0, keepdims=True)
                + jnp.where(cand <= NEG_KEY, n_unscanned, 0.0))

    c0 = count_ge_coarse(jnp.zeros((1, TQ), jnp.int32))
    t0 = jnp.where(c0 >= kf, 0, INT_MIN).astype(jnp.int32)
    ct0 = jnp.where(c0 >= kf, c0, float(seq))

    def bit_step(b, t, ct, counter):
        cand = t | jnp.left_shift(jnp.int32(1), 30 - b)
        cc = counter(cand)
        ok = cc >= kf
        return jnp.where(ok, cand, t), jnp.where(ok, cc, ct)

    t, ct = lax.fori_loop(0, BISECT_COARSE, lambda b, c: bit_step(b, *c, count_ge_coarse), (t0, ct0))
    t, ct = lax.fori_loop(BISECT_COARSE, BISECT_COARSE + BISECT_FIXED,
                          lambda b, c: bit_step(b, *c, count_ge), (t, ct))

    zero_final = (c0 >= kf) & (count_ge_coarse(jnp.full((1, TQ), MIN_NORMAL_BITS, jnp.int32)) < kf)

    def pending(ct):
        return jnp.max(jnp.where((ct > kf) & jnp.logical_not(zero_final), 1.0, 0.0))

    def bit_body(c):
        t, ct = bit_step(c[0], c[2], c[3], count_ge)
        t, ct = bit_step(c[0] + 1, t, ct, count_ge)
        return c[0] + 2, pending(ct), t, ct

    assert (31 - BISECT_COARSE - BISECT_FIXED) % 2 == 0
    _, _, t, c_ge = lax.while_loop(lambda c: (c[0] < 31) & (c[1] > 0.5), bit_body,
                                   (jnp.int32(BISECT_COARSE + BISECT_FIXED), pending(ct), t, ct))

    def tie_limit():
        need = kf - count_ge(t + 1)

        def block_ties(j, c):
            hit = key_ref[j] == t
            acc = jnp.zeros((fold, TQ), F32)
            for r in range(TK // fold):
                acc = jnp.where(hit[r * fold:(r + 1) * fold, :], acc + 1.0, acc)
            cnt_ref[pl.ds(j, 1), :] = jnp.sum(acc, axis=0, keepdims=True)
            return c

        lax.fori_loop(0, i + 1, block_ties, 0)
        nkb = cnt_ref.shape[0]
        cnt = jnp.where(lax.broadcasted_iota(jnp.int32, (nkb, TQ), 0) <= i, cnt_ref[...], 0.0)
        run = jnp.zeros((1, TQ), F32)
        blk_of = jnp.zeros((1, TQ), F32)
        before = jnp.zeros((1, TQ), F32)
        for j in range(nkb):
            run = run + cnt[j:j + 1, :]
            ahead = run < need
            blk_of = blk_of + jnp.where(ahead, 1.0, 0.0)
            before = jnp.where(ahead, run, before)
        blk_of = blk_of.astype(jnp.int32)

        def pick_block(j, c):
            m_ref[...] = jnp.where((blk_of == j) & (key_ref[j] == t), 1.0, m_ref[...])
            return c

        m_ref[...] = jnp.zeros_like(m_ref)
        lax.fori_loop(0, i + 1, pick_block, 0)
        tri = (lax.broadcasted_iota(jnp.int32, (TK, TK), 0) >= lax.broadcasted_iota(jnp.int32, (TK, TK), 1))
        upto = _dot(jnp.where(tri, 1.0, 0.0).astype(BF16), m_ref[...].astype(BF16))
        row_of = jnp.sum(jnp.where(upto < need - before, 1.0, 0.0), axis=0, keepdims=True)
        return blk_of * TK + row_of.astype(jnp.int32)

    x = lax.cond(jnp.max(c_ge) > kf, tie_limit, lambda: jnp.full((1, TQ), 2 * seq, jnp.int32))

    heads = range(nh)
    grp = range(DSA_GROUP)
    pairs = [slice((h // 2) * LANES, (h // 2 + 1) * LANES) for h in heads]
    n_full = i // DSA_GROUP

    def scores(jg):
        off = pl.multiple_of(jg * (DSA_GROUP * TK), DSA_GROUP * TK)
        return tuple(_dot(k_ref[pl.ds(off, DSA_GROUP * TK), pairs[h]], qs_ref[h]) for h in heads)

    def mask_biases(base, causal):
        biases = []
        for g in grp:
            key = key_ref[base + g]
            pos = (base + g) * TK + krow
            msk = (key > t) | ((key == t) & (pos <= x))
            if causal:
                msk = msk & (pos <= i * TQ + qcol)
            biases.append(jnp.where(msk, 0.0, NEG))
        return biases

    def attend_fixed(jg, ss, causal):
        ss_next = None if causal else scores(jg + 1)
        base = jg * DSA_GROUP
        biases = mask_biases(base, causal)
        for h in heads:
            pv = None
            for g in grp:
                p = jnp.exp((ss[h][g * TK:(g + 1) * TK, :] - m_fix[h]) + biases[g]).astype(BF16)
                d = _dot(vt_ref[0, base + g, h * LANES:(h + 1) * LANES, :], p)
                pv = d if pv is None else pv + d
            acc_ref[h] += pv
        return ss_next

    def attend(jg, carry, causal):
        ms, ss = carry
        ss_next = None if causal else scores(jg + 1)
        base = jg * DSA_GROUP
        biases = mask_biases(base, causal)
        sg = [[ss[h][g * TK:(g + 1) * TK, :] for g in grp] for h in heads]
        new = []
        for h in heads:
            m_new = ms[h]
            for g in grp:
                m_new = jnp.maximum(m_new, jnp.max(sg[h][g] + biases[g], axis=0, keepdims=True))
            new.append(m_new)
        ps = [[jnp.exp((sg[h][g] - new[h]) + biases[g]).astype(BF16) for g in grp] for h in heads]
        for h in heads:
            pv = None
            for g in grp:
                d = _dot(vt_ref[0, base + g, h * LANES:(h + 1) * LANES, :], ps[h][g])
                pv = d if pv is None else pv + d
            acc_ref[h] = jnp.exp(ms[h] - new[h]) * acc_ref[h] + pv
        return tuple(new), ss_next

    for h in heads:
        acc_ref[h] = jnp.zeros((LANES, TQ), F32)

    def run_online():
        carry = lax.fori_loop(0, n_full, lambda jg, c: attend(jg, c, False),
                              (tuple(jnp.full((1, TQ), NEG, F32) for _ in heads), scores(0)))
        attend(n_full, carry, True)
        return 0

    def run_fixed():
        ss = lax.fori_loop(0, n_full, lambda jg, c: attend_fixed(jg, c, False), scores(0))
        attend_fixed(n_full, ss, True)
        return 0

    lax.cond(bound_max <= SAFE_SHIFT, run_fixed, run_online)

    out_t = jnp.concatenate([_finish_softmax_head(acc_ref[h]) for h in range(nh)], axis=0)
    o_ref[...] = out_t.T


def _dsa_call(f_arr, h_arr, vt, batch, seq):
    nq = seq // TQ
    wb = N_HEADS_B * HEAD_DIM
    return pl.pallas_call(
        _dsa_kernel,
        grid=(batch, nq),
        in_specs=[pl.BlockSpec((TQ, IDX_HEADS * IDX_DIM), lambda b, i: (b * nq + i, H_QI // (IDX_HEADS * IDX_DIM))),
                  pl.BlockSpec((TQ, LANES), lambda b, i: (b * nq + i, F_WI // LANES)),
                  pl.BlockSpec((seq, LANES), lambda b, i: (b, H_KI // LANES)),
                  pl.BlockSpec((TQ, wb), lambda b, i: (b * nq + i, F_QB // wb)),
                  pl.BlockSpec((seq, wb), lambda b, i: (b, H_KB // wb)),
                  pl.BlockSpec((1, seq // TK, N_HEADS_B * LANES, TK), lambda b, i: (b, 0, VT_B // (N_HEADS_B * LANES), 0))],
        out_specs=pl.BlockSpec((TQ, wb), lambda b, i: (b * nq + i, 0)),
        out_shape=jax.ShapeDtypeStruct((batch * seq, wb), F32),
        scratch_shapes=[pltpu.VMEM((seq // TK, TK, TQ), jnp.int32),
                        pltpu.VMEM((seq // TK, TK, TQ), BF16),
                        pltpu.VMEM((IDX_HEADS, LANES, TQ), BF16),
                        pltpu.VMEM((N_HEADS_B, LANES, TQ), BF16),
                        pltpu.VMEM((N_HEADS_B, LANES, TQ), F32),
                        pltpu.VMEM((seq // TK, TQ), F32),
                        pltpu.VMEM((TK, TQ), F32),
                        pltpu.VMEM((1, wb), F32)],
        compiler_params=_cparams(("parallel", "arbitrary")),
        name="dsa_attn",
    )(h_arr, f_arr, h_arr, f_arr, h_arr, vt)


def _outproj_kernel(x_ref, oa_ref, ob_ref, oc_ref, og_ref, gt_ref, w_ref, o_ref):
    wa = N_HEADS_A * HEAD_DIM
    wb = wa + N_HEADS_B * HEAD_DIM
    og = og_ref[...]
    y = (_dot((oa_ref[...] * og[:, :wa]).astype(BF16), w_ref[:wa, :])
         + _dot((ob_ref[...] * og[:, wa:wb]).astype(BF16), w_ref[wa:wb, :])
         + _dot((oc_ref[...] * og[:, wb:]).astype(BF16), w_ref[wb:, :]))
    o_ref[...] = x_ref[...] + gt_ref[0] * y


def _outproj_call(x2, oa, ob, oc, og, gt, w_out, seq):
    m, d = x2.shape
    tm = 512
    row = lambda i: (i, 0)
    return pl.pallas_call(
        _outproj_kernel,
        grid=(m // tm,),
        in_specs=[pl.BlockSpec((tm, d), row),
                  pl.BlockSpec((tm, oa.shape[1]), row),
                  pl.BlockSpec((tm, ob.shape[1]), row),
                  pl.BlockSpec((tm, oc.shape[1]), row),
                  pl.BlockSpec((1, d), lambda i: (0, 0)),
                  pl.BlockSpec((1, 1, d), lambda i: ((i * tm) // seq, 0, 0)),
                  pl.BlockSpec((d, d), lambda i: (0, 0))],
        out_specs=pl.BlockSpec((tm, d), row),
        out_shape=jax.ShapeDtypeStruct((m, d), F32),
        compiler_params=_cparams(("parallel",)),
        name="out_proj",
    )(x2, oa, ob, oc, og, gt, w_out)


def _rope_tables(seq):
    pos = jnp.arange(seq, dtype=F32)
    inv = ROPE_THETA ** (-jnp.arange(0, ROPE_DIM, 2, dtype=F32) / ROPE_DIM)
    ang = pos[:, None] * inv[None, :]
    cos, sin = jnp.cos(ang), jnp.sin(ang)
    zeros = jnp.zeros((seq, HEAD_DIM - ROPE_DIM), F32)
    zh = jnp.zeros((seq, ROPE_HALF), F32)
    cos_h = jnp.concatenate([cos, cos, jnp.ones_like(zeros)], axis=1)
    s1_h = jnp.concatenate([-sin, zh, zeros], axis=1)
    s2_h = jnp.concatenate([zh, sin, zeros], axis=1)
    two = lambda t: jnp.concatenate([t, t], axis=1)
    return two(cos_h), two(s1_h), two(s2_h)


def _permute_w_in(w):
    d = w.shape[0]
    a, b, c = N_HEADS_A * HEAD_DIM, N_HEADS_B * HEAD_DIM, N_HEADS_C * HEAD_DIM
    sizes = (a, a, a, b, b, b, IDX_HEADS * IDX_DIM, IDX_DIM, IDX_HEADS, c, c, c)
    offs = np.concatenate([[0], np.cumsum(sizes)])
    qa, ka, va, qb, kb, vb, qi, ki, wi, qc, kc, vc = [w[:, int(offs[k]):int(offs[k + 1])] for k in range(12)]
    z = lambda n: jnp.zeros((d, n), w.dtype)
    out = jnp.concatenate([qa, qb, qc, ka, va, kb, vb, kc, vc, qi,
                           ki, z(LANES - IDX_DIM), wi, z(LANES - IDX_HEADS)], axis=1)
    assert out.shape[1] == W_PERM_WIDTH
    return out.astype(BF16)


def _suffix_sum_matrix():
    s = np.arange(TK + SB_TAIL)[:, None]
    j = np.arange(TK)[None, :]
    return jnp.asarray((j > s) | (s >= TK), BF16)


def kernel(x, c, w_ada, b_ada, norm_g, w_in, qk_g, out_g, w_out, ffn_w1, ffn_w3, ffn_w2):
    batch, seq, d = x.shape
    depth = w_ada.shape[0]
    assert seq % PROJ_TM == 0 and d % LANES == 0 and batch <= 8
    assert (seq // TK) % MOBA_GROUP == 0 and (seq // TK) % DSA_GROUP == 0

    c_pad = jnp.zeros((8, d), F32).at[:batch].set(c)
    mod = _mod_call(c_pad, w_ada, b_ada)
    cos_t, s1_t, s2_t = _rope_tables(seq)
    hd = np.arange(LANES) // HEAD_DIM
    bd = jnp.asarray(hd[:, None] == hd[None, :], BF16)
    u_mat = _suffix_sum_matrix()

    x2 = x.reshape(batch * seq, d)
    for layer in range(depth):
        mods = [mod[layer, :batch, k * d:(k + 1) * d].reshape(batch, 1, d) for k in range(N_MOD)]
        sh1, sc1, g1, sh2, sc2, g2, sh3, sc3, g3 = mods
        ng = norm_g[layer]
        w1 = ffn_w1[layer].astype(BF16)
        w3 = ffn_w3[layer].astype(BF16)
        w2 = ffn_w2[layer].astype(BF16)

        x2 = _ffn_call(x2, ng[0:1], sh1, sc1, g1, w1[0], w3[0], w2[0], seq)

        qkg = jnp.concatenate([jnp.tile(qk_g[layer], (1, 2)), jnp.ones((4, LANES), F32)], axis=0)
        f_arr, h_arr, kmean, vt = _proj_call(x2, ng[1:2], sh2, sc2, _permute_w_in(w_in[layer]), qkg,
                                             cos_t, s1_t, s2_t, bd, batch, seq)
        per_tile = PROJ_TM // MOBA_BLOCK
        kmean = kmean[:, :per_tile, :].reshape(batch, seq // MOBA_BLOCK, N_HEADS_A * HEAD_DIM)
        oa = _moba_call(f_arr, h_arr, kmean, vt, batch, seq)
        ob = _dsa_call(f_arr, h_arr, vt, batch, seq)
        oc = _sb_call(f_arr, h_arr, vt, u_mat, batch, seq)
        x2 = _outproj_call(x2, oa, ob, oc, out_g[layer].reshape(1, d), g2, w_out[layer].astype(BF16), seq)

        x2 = _ffn_call(x2, ng[2:3], sh3, sc3, g3, w1[1], w3[1], w2[1], seq)
    return x2.reshape(batch, seq, d)
```

```python
import jax
import jax.numpy as jnp
import numpy as np
from jax import lax
from jax.experimental import pallas as pl
from jax.experimental.pallas import tpu as pltpu

F32 = jnp.float32
BF16 = jnp.bfloat16

HEAD_DIM = 64
N_HEADS_A = 4
N_HEADS_B = 4
N_HEADS_C = 8
ROPE_DIM = HEAD_DIM // 4
ROPE_HALF = ROPE_DIM // 2
ROPE_THETA = 500000.0
MOBA_BLOCK = 256
MOBA_TOPK = 3
DSA_TOPK = 256
IDX_HEADS = 8
IDX_DIM = 64
N_MOD = 9
RMS_EPS = 1e-6
NEG = -1e30
ATT_SCALE = HEAD_DIM ** -0.5

LANES = 128
MXU_N = 256
TQ = 256
TK = 256
PROJ_TM = 512
VMEM_LIMIT = 56 * 1024 * 1024

SB_DEAD_LOG = -88.0
SB_TAIL = 16
SB_HEADS = 8
MOBA_GROUP = 4
DSA_GROUP = 2
BOUND_SLACK = 1.02
SAFE_SHIFT = 40.0
BISECT_COARSE = 15
BISECT_FIXED = 6

F_QA, F_QB, F_QC, F_WI, F_WIDTH = 0, 256, 512, 1024, 1152
H_KA, H_KB, H_KC, H_QI, H_KI, H_WIDTH = 0, 256, 512, 1024, 1536, 1664
W_PERM_WIDTH = 15 * MXU_N
VT_A, VT_B, VT_C, VT_ROWS = 0, 512, 1024, 1536


def _dot(a, b):
    return jnp.dot(a, b, preferred_element_type=F32)


def _dot_nt(a, b):
    return lax.dot_general(a, b, (((1,), (1,)), ((), ())), preferred_element_type=F32)


def _split3(x):
    a = x.astype(BF16)
    r = x - a.astype(F32)
    b = r.astype(BF16)
    c = (r - b.astype(F32)).astype(BF16)
    return a, b, c


def _cparams(sem):
    return pltpu.CompilerParams(dimension_semantics=sem, vmem_limit_bytes=VMEM_LIMIT)


def _mod_kernel(c_ref, w_ref, b_ref, o_ref):
    c = c_ref[...]
    sc = c * (1.0 / (1.0 + jnp.exp(-c)))
    a, b, c3 = _split3(sc)
    w = w_ref[0]
    wa, wb, wc = _split3(w)
    acc = _dot(a, wa) + (_dot(a, wb) + _dot(b, wa)) + (_dot(a, wc) + _dot(b, wb) + _dot(c3, wa))
    o_ref[0] = acc + b_ref[0]


def _mod_call(c_pad, w_ada, b_ada):
    depth, d, n = w_ada.shape
    tn = 1024
    rows = c_pad.shape[0]
    return pl.pallas_call(
        _mod_kernel,
        grid=(depth, n // tn),
        in_specs=[pl.BlockSpec((rows, d), lambda l, j: (0, 0)),
                  pl.BlockSpec((1, d, tn), lambda l, j: (l, 0, j)),
                  pl.BlockSpec((1, 1, tn), lambda l, j: (l, 0, j))],
        out_specs=pl.BlockSpec((1, rows, tn), lambda l, j: (l, 0, j)),
        out_shape=jax.ShapeDtypeStruct((depth, rows, n), F32),
        compiler_params=_cparams(("parallel", "parallel")),
        name="adaln_mod",
    )(c_pad, w_ada, b_ada.reshape(depth, 1, n))


def _norm_modulate(x, ng, sh, sc):
    ms = jnp.mean(x * x, axis=-1, keepdims=True)
    h = x * lax.rsqrt(ms + RMS_EPS) * ng
    return h * (1.0 + sc) + sh


def _ffn_kernel(x_ref, ng_ref, sh_ref, sc_ref, gt_ref, w1_ref, w3_ref, w2_ref, o_ref, h_ref, acc_ref):
    j = pl.program_id(1)

    @pl.when(j == 0)
    def _():
        h = _norm_modulate(x_ref[...], ng_ref[...], sh_ref[0], sc_ref[0])
        h_ref[...] = h.astype(BF16)
        acc_ref[...] = jnp.zeros_like(acc_ref)

    h = h_ref[...]
    a = _dot(h, w1_ref[...])
    b = _dot(h, w3_ref[...])
    u = (a * (1.0 / (1.0 + jnp.exp(-a))) * b).astype(BF16)
    acc_ref[...] += _dot(u, w2_ref[...])

    @pl.when(j == pl.num_programs(1) - 1)
    def _():
        o_ref[...] = x_ref[...] + 0.5 * gt_ref[0] * acc_ref[...]


def _ffn_call(x2, ng, sh, sc, gt, w1, w3, w2, seq):
    m, d = x2.shape
    dff = w1.shape[1]
    tm = 1024
    tf = dff // 2 if (dff // 2) % LANES == 0 else MXU_N
    bidx = lambda i, j: ((i * tm) // seq, 0, 0)
    return pl.pallas_call(
        _ffn_kernel,
        grid=(m // tm, dff // tf),
        in_specs=[pl.BlockSpec((tm, d), lambda i, j: (i, 0)),
                  pl.BlockSpec((1, d), lambda i, j: (0, 0)),
                  pl.BlockSpec((1, 1, d), bidx),
                  pl.BlockSpec((1, 1, d), bidx),
                  pl.BlockSpec((1, 1, d), bidx),
                  pl.BlockSpec((d, tf), lambda i, j: (0, j)),
                  pl.BlockSpec((d, tf), lambda i, j: (0, j)),
                  pl.BlockSpec((tf, d), lambda i, j: (j, 0))],
        out_specs=pl.BlockSpec((tm, d), lambda i, j: (i, 0)),
        out_shape=jax.ShapeDtypeStruct((m, d), F32),
        scratch_shapes=[pltpu.VMEM((tm, d), BF16), pltpu.VMEM((tm, d), F32)],
        compiler_params=_cparams(("parallel", "arbitrary")),
        name="swiglu_ffn",
    )(x2, ng, sh, sc, gt, w1, w3, w2)


def _proj_kernel(x_ref, ng_ref, sh_ref, sc_ref, w_ref, qkg_ref, cos_ref, s1_ref, s2_ref, bd_ref,
                 f_ref, h_ref, km_ref, vt_ref, hs_ref):
    hs_ref[...] = _norm_modulate(x_ref[...], ng_ref[...], sh_ref[0], sc_ref[0]).astype(BF16)
    cos, s1, s2 = cos_ref[...], s1_ref[...], s2_ref[...]
    bd = bd_ref[...]
    tm = x_ref.shape[0]

    def rope(v):
        return v * cos + pltpu.roll(v, LANES - ROPE_HALF, 1) * s1 + pltpu.roll(v, ROPE_HALF, 1) * s2

    def headnorm(v, g):
        a, b, _ = _split3(v * v)
        ss = _dot(a, bd) + _dot(b, bd)
        return v * lax.rsqrt(ss * (1.0 / HEAD_DIM) + RMS_EPS) * g

    def chunk(c):
        y = _dot(hs_ref[...], w_ref[:, c * MXU_N:(c + 1) * MXU_N])
        return y[:, :LANES], y[:, LANES:]

    def normrope_chunk(c, grow):
        g = qkg_ref[grow:grow + 1, :]
        return [rope(headnorm(v, g)) for v in chunk(c)]

    def put(ref, off, halves, dtype):
        for k, v in enumerate(halves):
            ref[:, off + k * LANES: off + (k + 1) * LANES] = v.astype(dtype)

    def put_vt(base, halves, with_ones):
        ones = jnp.ones((HEAD_DIM, TK), BF16)
        for r in range(tm // TK):
            for k, v in enumerate(halves):
                t = v[r * TK:(r + 1) * TK, :].T.astype(BF16)
                if with_ones:
                    for hh in range(2):
                        row = base + (2 * k + hh) * LANES
                        vt_ref[0, r, row:row + HEAD_DIM, :] = t[hh * HEAD_DIM:(hh + 1) * HEAD_DIM, :]
                        vt_ref[0, r, row + HEAD_DIM:row + LANES, :] = ones
                else:
                    vt_ref[0, r, base + k * LANES:base + (k + 1) * LANES, :] = t

    put(f_ref, F_QA, normrope_chunk(0, 0), F32)
    put(f_ref, F_QB, normrope_chunk(1, 2), F32)
    put(f_ref, F_QC, chunk(2), F32)
    put(f_ref, F_QC + MXU_N, chunk(3), F32)

    ka = normrope_chunk(4, 1)
    put(h_ref, H_KA, ka, BF16)
    rows = lax.broadcasted_iota(jnp.int32, (8, LANES), 0)
    for k, v in enumerate(ka):
        km = jnp.zeros((8, LANES), F32)
        for r in range(tm // MOBA_BLOCK):
            s = jnp.sum(v[r * MOBA_BLOCK:(r + 1) * MOBA_BLOCK, :], axis=0, keepdims=True) * (1.0 / MOBA_BLOCK)
            km = jnp.where(rows == r, s, km)
        km_ref[0, :, k * LANES:(k + 1) * LANES] = km

    put_vt(VT_A, chunk(5), True)
    put(h_ref, H_KB, normrope_chunk(6, 3), BF16)
    put_vt(VT_B, chunk(7), True)
    put(h_ref, H_KC, chunk(8), BF16)
    put(h_ref, H_KC + MXU_N, chunk(9), BF16)
    put_vt(VT_C, chunk(10), False)
    put_vt(VT_C + MXU_N, chunk(11), False)
    put(h_ref, H_QI, [rope(v) for v in chunk(12)], BF16)
    put(h_ref, H_QI + MXU_N, [rope(v) for v in chunk(13)], BF16)
    ki, wi = chunk(14)
    h_ref[:, H_KI:H_KI + LANES] = rope(ki).astype(BF16)
    f_ref[:, F_WI:F_WI + LANES] = wi


def _proj_call(x2, ng, sh, sc, w_perm, qkg, cos_t, s1_t, s2_t, bd, batch, seq):
    m, d = x2.shape
    tm = PROJ_TM
    nt = seq // tm
    bidx = lambda i: ((i * tm) // seq, 0, 0)
    tab = pl.BlockSpec((tm, LANES), lambda i: (i % nt, 0))
    return pl.pallas_call(
        _proj_kernel,
        grid=(m // tm,),
        in_specs=[pl.BlockSpec((tm, d), lambda i: (i, 0)),
                  pl.BlockSpec((1, d), lambda i: (0, 0)),
                  pl.BlockSpec((1, 1, d), bidx),
                  pl.BlockSpec((1, 1, d), bidx),
                  pl.BlockSpec((d, W_PERM_WIDTH), lambda i: (0, 0)),
                  pl.BlockSpec((8, LANES), lambda i: (0, 0)),
                  tab, tab, tab,
                  pl.BlockSpec((LANES, LANES), lambda i: (0, 0))],
        out_specs=[pl.BlockSpec((tm, F_WIDTH), lambda i: (i, 0)),
                   pl.BlockSpec((tm, H_WIDTH), lambda i: (i, 0)),
                   pl.BlockSpec((1, 8, 2 * LANES), lambda i: (i, 0, 0)),
                   pl.BlockSpec((1, tm // TK, VT_ROWS, TK), lambda i: (i // nt, i % nt, 0, 0))],
        out_shape=[jax.ShapeDtypeStruct((m, F_WIDTH), F32),
                   jax.ShapeDtypeStruct((m, H_WIDTH), BF16),
                   jax.ShapeDtypeStruct((m // tm, 8, 2 * LANES), F32),
                   jax.ShapeDtypeStruct((batch, seq // TK, VT_ROWS, TK), BF16)],
        scratch_shapes=[pltpu.VMEM((tm, d), BF16)],
        compiler_params=_cparams(("parallel",)),
        name="in_proj",
    )(x2, ng, sh, sc, w_perm, qkg, cos_t, s1_t, s2_t, bd)


def _head_masks():
    lane = lax.broadcasted_iota(jnp.int32, (1, LANES), 1)
    return [(lane >= HEAD_DIM * h) & (lane < HEAD_DIM * (h + 1)) for h in range(2)]


def _head_rms_t(o):
    ss = jnp.sum(o * o, axis=0, keepdims=True) * (1.0 / HEAD_DIM)
    return o * lax.rsqrt(ss + RMS_EPS)


def _finish_softmax_head(acc):
    return _head_rms_t(acc[:HEAD_DIM, :] / acc[HEAD_DIM:HEAD_DIM + 1, :])


def _moba_kernel(q_ref, km_ref, k_ref, vt_ref, o_ref, bias_ref, qs_ref, acc_ref):
    i = pl.program_id(2)
    nb = km_ref.shape[1]
    q = q_ref[...]
    kma, kmb, _ = _split3(km_ref[0])
    blk = lax.broadcasted_iota(jnp.int32, (nb, TQ), 0).astype(F32)
    i_f = i.astype(F32)
    krow = lax.broadcasted_iota(jnp.int32, (TK, TQ), 0)
    qcol = lax.broadcasted_iota(jnp.int32, (TK, TQ), 1)
    causal = krow <= qcol

    for h, hm in enumerate(_head_masks()):
        qh_t = jnp.where(hm, q, 0.0).T
        qa, qb, _ = _split3(qh_t)
        gate = _dot(kma, qa) + (_dot(kmb, qa) + _dot(kma, qb))
        gate = jnp.where(blk < i_f, gate, NEG)
        sel = jnp.zeros((nb, TQ), F32)
        for _ in range(MOBA_TOPK):
            mx = jnp.max(gate, axis=0, keepdims=True)
            idx = jnp.min(jnp.where(gate == mx, blk, float(nb)), axis=0, keepdims=True)
            pick = blk == idx
            sel = jnp.where(pick, 1.0, sel)
            gate = jnp.where(pick, -jnp.inf, gate)
        bias_ref[h] = jnp.where((sel > 0.5) & (blk < i_f), 0.0, NEG)
        qs_ref[h] = (qh_t * ATT_SCALE).astype(BF16)

    own = pl.multiple_of(i * TK, TK)
    heads = range(2)
    grp = range(MOBA_GROUP)
    n_groups = (i + MOBA_GROUP - 1) // MOBA_GROUP
    last_group = km_ref.shape[1] // MOBA_GROUP - 1

    def scores(jg):
        off = pl.multiple_of(jnp.minimum(jg, last_group) * (MOBA_GROUP * TK), MOBA_GROUP * TK)
        kb = k_ref[pl.ds(off, MOBA_GROUP * TK), :]
        return tuple(_dot(kb, qs_ref[h]) for h in heads)

    def own_block():
        k_own = k_ref[pl.ds(own, TK), :]
        m_init = []
        for h in heads:
            s = jnp.where(causal, _dot(k_own, qs_ref[h]), NEG)
            m0 = jnp.max(s, axis=0, keepdims=True)
            acc_ref[h] = _dot(vt_ref[0, i, h * LANES:(h + 1) * LANES, :], jnp.exp(s - m0).astype(BF16))
            m_init.append(m0)
        return tuple(m_init)

    def body(jg, carry):
        ms, ss = carry
        ss_next = scores(jg + 1)
        base = jg * MOBA_GROUP
        sg = [[ss[h][g * TK:(g + 1) * TK, :] for g in grp] for h in heads]
        bs = [[bias_ref[h, pl.ds(base + g, 1), :] for g in grp] for h in heads]
        new = []
        for h in heads:
            m_new = ms[h]
            for g in grp:
                m_new = jnp.maximum(m_new, jnp.max(sg[h][g], axis=0, keepdims=True) + bs[h][g])
            new.append(m_new)
        ps = [[jnp.exp(sg[h][g] - (new[h] - bs[h][g])).astype(BF16) for g in grp] for h in heads]
        for h in heads:
            pv = None
            for g in grp:
                d = _dot(vt_ref[0, base + g, h * LANES:(h + 1) * LANES, :], ps[h][g])
                pv = d if pv is None else pv + d
            acc_ref[h] = jnp.exp(ms[h] - new[h]) * acc_ref[h] + pv
        return tuple(new), ss_next

    lax.fori_loop(0, n_groups, body, (own_block(), scores(0)))
    out_t = jnp.concatenate([_finish_softmax_head(acc_ref[h]) for h in range(2)], axis=0)
    o_ref[...] = out_t.T


def _moba_call(f_arr, h_arr, kmean, vt, batch, seq):
    nq = seq // TQ
    nb = seq // MOBA_BLOCK
    npair = N_HEADS_A // 2
    return pl.pallas_call(
        _moba_kernel,
        grid=(batch, npair, nq),
        in_specs=[pl.BlockSpec((TQ, LANES), lambda b, p, i: (b * nq + i, F_QA // LANES + p)),
                  pl.BlockSpec((1, nb, LANES), lambda b, p, i: (b, 0, p)),
                  pl.BlockSpec((seq, LANES), lambda b, p, i: (b, H_KA // LANES + p)),
                  pl.BlockSpec((1, seq // TK, 2 * LANES, TK), lambda b, p, i: (b, 0, VT_A // (2 * LANES) + p, 0))],
        out_specs=pl.BlockSpec((TQ, LANES), lambda b, p, i: (b * nq + i, p)),
        out_shape=jax.ShapeDtypeStruct((batch * seq, N_HEADS_A * HEAD_DIM), F32),
        scratch_shapes=[pltpu.VMEM((2, nb, TQ), F32),
                        pltpu.VMEM((2, LANES, TQ), BF16),
                        pltpu.VMEM((2, LANES, TQ), F32)],
        compiler_params=_cparams(("parallel", "parallel", "arbitrary")),
        name="moba_attn",
    )(f_arr, kmean, h_arr, vt)


def _sb_kernel(q_ref, k_ref, vt_ref, u_ref, o_ref, qs_ref, acc_ref):
    i = pl.program_id(2)
    krow = lax.broadcasted_iota(jnp.int32, (TK, TQ), 0)
    qcol = lax.broadcasted_iota(jnp.int32, (TK, TQ), 1)
    strict = krow < qcol
    hms = _head_masks()
    for h in range(SB_HEADS):
        pr = slice((h // 2) * LANES, (h // 2 + 1) * LANES)
        qs_ref[h] = (jnp.where(hms[h % 2], q_ref[:, pr], 0.0) * ATT_SCALE).T.astype(BF16)
        acc_ref[h] = jnp.zeros((LANES, TQ), F32)

    heads = range(SB_HEADS)
    pairs = [slice((h // 2) * LANES, (h // 2 + 1) * LANES) for h in heads]

    def logits(j):
        off = pl.multiple_of(jnp.maximum(j, 0) * TK, TK)
        return tuple(_dot(k_ref[pl.ds(off, TK), pairs[h]], qs_ref[h]) for h in heads)

    def block(j, rs, zs, diag):
        zs_next = logits(j - 1)
        u = u_ref[...]
        log_betas, his, los = [], [], []
        for h in heads:
            z = zs[h]
            log_beta = jnp.minimum(z, 0.0) - jnp.log(1.0 + jnp.exp(-jnp.abs(z)))
            log_1m = log_beta - z
            if diag:
                log_1m = jnp.where(strict, log_1m, 0.0)
            hi = log_1m.astype(BF16)
            log_betas.append(log_beta)
            his.append(hi)
            los.append((log_1m - hi.astype(F32)).astype(BF16))
        sums = [_dot(u, his[h]) + _dot(u, los[h]) for h in heads]
        weights = []
        for h in heads:
            after = sums[h][:TK, :] + rs[h]
            a = jnp.exp(log_betas[h] + after)
            if diag:
                a = jnp.where(strict, a, 0.0)
            weights.append(a.astype(BF16))
        for h in heads:
            acc_ref[h] += _dot(vt_ref[0, j, pairs[h], :], weights[h])
        return tuple(rs[h] + sums[h][TK:TK + 1, :] for h in heads), zs_next

    zero = jnp.zeros((1, TQ), F32)
    rs, zs = block(i, (zero,) * SB_HEADS, logits(i), True)

    def alive(rs):
        m = rs[0]
        for r in rs[1:]:
            m = jnp.maximum(m, r)
        return jnp.max(m)

    def cond(c):
        return (c[0] >= 0) & (c[1] > SB_DEAD_LOG)

    def body(c):
        rs, zs = block(c[0], c[2], c[3], False)
        return c[0] - 1, alive(rs), rs, zs

    lax.while_loop(cond, body, (i - 1, alive(rs), rs, zs))
    out_t = jnp.concatenate(
        [_head_rms_t(acc_ref[h][(h % 2) * HEAD_DIM:(h % 2 + 1) * HEAD_DIM, :]) for h in range(SB_HEADS)], axis=0)
    o_ref[...] = out_t.T


def _sb_call(f_arr, h_arr, vt, u_mat, batch, seq):
    nq = seq // TQ
    w = SB_HEADS * HEAD_DIM
    ngrp = N_HEADS_C // SB_HEADS
    return pl.pallas_call(
        _sb_kernel,
        grid=(batch, ngrp, nq),
        in_specs=[pl.BlockSpec((TQ, w), lambda b, p, i: (b * nq + i, F_QC // w + p)),
                  pl.BlockSpec((seq, w), lambda b, p, i: (b, H_KC // w + p)),
                  pl.BlockSpec((1, seq // TK, w, TK), lambda b, p, i: (b, 0, VT_C // w + p, 0)),
                  pl.BlockSpec((TK + SB_TAIL, TK), lambda b, p, i: (0, 0))],
        out_specs=pl.BlockSpec((TQ, w), lambda b, p, i: (b * nq + i, p)),
        out_shape=jax.ShapeDtypeStruct((batch * seq, N_HEADS_C * HEAD_DIM), F32),
        scratch_shapes=[pltpu.VMEM((SB_HEADS, LANES, TQ), BF16), pltpu.VMEM((SB_HEADS, LANES, TQ), F32)],
        compiler_params=_cparams(("parallel", "parallel", "arbitrary")),
        name="stickbreak_attn",
    )(f_arr, h_arr, vt, u_mat)


def _sortable_key(x):
    bits = lax.bitcast_convert_type(x, jnp.int32)
    return jnp.where(bits < 0, bits ^ jnp.int32(0x7FFFFFFF), bits)


_NEG_BITS = int(np.array(NEG, np.float32).view(np.int32))
NEG_KEY = _NEG_BITS ^ 0x7FFFFFFF
INT_MIN = -2 ** 31
MIN_NORMAL_BITS = 0x00800000


def _dsa_kernel(qi_ref, wi_ref, ki_ref, q_ref, k_ref, vt_ref, o_ref,
                key_ref, coarse_ref, qx_ref, qs_ref, acc_ref, cnt_ref, m_ref, kmax2_ref):
    i = pl.program_id(1)
    seq = k_ref.shape[0]
    topk = min(DSA_TOPK, seq // 4)
    nh = N_HEADS_B
    lane = lax.broadcasted_iota(jnp.int32, (1, LANES), 1)
    low = lane < HEAD_DIM
    krow = lax.broadcasted_iota(jnp.int32, (TK, TQ), 0)
    qcol = lax.broadcasted_iota(jnp.int32, (TK, TQ), 1)
    diag_causal = krow <= qcol
    hms = _head_masks()

    qi = qi_ref[...].astype(F32)
    for h in range(IDX_HEADS):
        qp = qi[:, (h // 2) * LANES:(h // 2 + 1) * LANES]
        if h % 2:
            qp = pltpu.roll(qp, HEAD_DIM, 1)
        qx_ref[h] = jnp.where(low, qp, 0.0).T.astype(BF16)
    w_t = wi_ref[...].T
    q = q_ref[...]
    q_norm2 = []
    for h in range(nh):
        qh = jnp.where(hms[h % 2], q[:, (h // 2) * LANES:(h // 2 + 1) * LANES], 0.0)
        qh_t = (qh * ATT_SCALE).T
        qs_ref[h] = qh_t.astype(BF16)
        q_norm2.append(jnp.sum(qh_t * qh_t, axis=0, keepdims=True))

    @pl.when(i == 0)
    def _():
        wk = k_ref.shape[1]
        head_of = lambda a: lax.shift_right_logical(lax.broadcasted_iota(jnp.int32, (wk, wk), a), 6)
        ones_bd = jnp.where(head_of(0) == head_of(1), 1.0, 0.0).astype(BF16)

        def body(j, mx):
            kk = k_ref[pl.ds(pl.multiple_of(j * TK, TK), TK), :].astype(F32)
            return jnp.maximum(mx, jnp.max(_dot((kk * kk).astype(BF16), ones_bd), axis=0, keepdims=True))
        kmax2_ref[...] = lax.fori_loop(0, seq // TK, body, jnp.zeros((1, wk), F32))

    m_fix = [jnp.sqrt(q_norm2[h] * kmax2_ref[0:1, h * HEAD_DIM:h * HEAD_DIM + 1] + 1e-30) * BOUND_SLACK
             for h in range(nh)]
    bound_max = jnp.max(jnp.maximum(jnp.maximum(m_fix[0], m_fix[1]), jnp.maximum(m_fix[2], m_fix[3])))

    def score_block(j, diag):
        off = pl.multiple_of(j * TK, TK)
        kz = ki_ref[pl.ds(off, TK), :]
        sc = jnp.zeros((TK, TQ), F32)
        for h in range(IDX_HEADS):
            sc = sc + w_t[h:h + 1, :] * jnp.maximum(_dot(kz, qx_ref[h]), 0.0)
        if diag:
            sc = jnp.where(diag_causal, sc, NEG)
        bits = lax.bitcast_convert_type(sc, jnp.int32)
        bits = jnp.where(bits == INT_MIN, 0, bits)
        key_ref[j] = jnp.where(bits < 0, bits ^ jnp.int32(0x7FFFFFFF), bits)
        coarse_ref[j] = lax.bitcast_convert_type(bits & jnp.int32(-65536), F32).astype(BF16)

    def score_pair(jj, c):
        score_block(2 * jj, False)
        score_block(2 * jj + 1, False)
        return c

    lax.fori_loop(0, i // 2, score_pair, 0)

    @pl.when(i % 2 == 1)
    def _():
        score_block(i - 1, False)

    score_block(i, True)
    for g in range(1, DSA_GROUP):
        @pl.when(i % DSA_GROUP + g < DSA_GROUP)
        def _():
            key_ref[i + g] = jnp.full((TK, TQ), NEG_KEY, jnp.int32)

    n_unscanned = (seq - (i + 1) * TK).astype(F32)
    fold = TK // 4

    def count(pred):
        def body(j, acc):
            hit = pred(key_ref[j], j)
            for r in range(TK // fold):
                acc = jnp.where(hit[r * fold:(r + 1) * fold, :], acc + 1.0, acc)
            return acc
        acc = lax.fori_loop(0, i + 1, body, jnp.zeros((fold, TQ), F32))
        return jnp.sum(acc, axis=0, keepdims=True)

    def count_ge(cand):
        return count(lambda key, j: key >= cand) + jnp.where(cand <= NEG_KEY, n_unscanned, 0.0)

    kf = float(topk)

    def count_ge_coarse(cand):
        cand_bits = jnp.where(cand < 0, (cand ^ jnp.int32(0x7FFFFFFF)) & jnp.int32(-65536), cand)
        cand_bits = jnp.where((cand > 0) & (cand < MIN_NORMAL_BITS), MIN_NORMAL_BITS, cand_bits)
        cand_b = lax.bitcast_convert_type(cand_bits, F32).astype(BF16)
        one, zero = jnp.ones((), BF16), jnp.zeros((), BF16)

        def body(j, acc):
            hit = jnp.where(coarse_ref[j] >= cand_b, one, zero)
            return acc + ((hit[:fold] + hit[fold:2 * fold]) + (hit[2 * fold:3 * fold] + hit[3 * fold:]))
        acc = lax.fori_loop(0, i + 1, body, jnp.zeros((fold, TQ), BF16))
        return (jnp.sum(acc.astype(F32), axis=0, keepdims=True)
                + jnp.where(cand <= NEG_KEY, n_unscanned, 0.0))

    c0 = count_ge_coarse(jnp.zeros((1, TQ), jnp.int32))
    t0 = jnp.where(c0 >= kf, 0, INT_MIN).astype(jnp.int32)
    ct0 = jnp.where(c0 >= kf, c0, float(seq))

    def bit_step(b, t, ct, counter):
        cand = t | jnp.left_shift(jnp.int32(1), 30 - b)
        cc = counter(cand)
        ok = cc >= kf
        return jnp.where(ok, cand, t), jnp.where(ok, cc, ct)

    t, ct = lax.fori_loop(0, BISECT_COARSE, lambda b, c: bit_step(b, *c, count_ge_coarse), (t0, ct0))
    t, ct = lax.fori_loop(BISECT_COARSE, BISECT_COARSE + BISECT_FIXED,
                          lambda b, c: bit_step(b, *c, count_ge), (t, ct))

    zero_final = (c0 >= kf) & (count_ge_coarse(jnp.full((1, TQ), MIN_NORMAL_BITS, jnp.int32)) < kf)

    def pending(ct):
        return jnp.max(jnp.where((ct > kf) & jnp.logical_not(zero_final), 1.0, 0.0))

    def bit_body(c):
        t, ct = bit_step(c[0], c[2], c[3], count_ge)
        t, ct = bit_step(c[0] + 1, t, ct, count_ge)
        return c[0] + 2, pending(ct), t, ct

    assert (31 - BISECT_COARSE - BISECT_FIXED) % 2 == 0
    _, _, t, c_ge = lax.while_loop(lambda c: (c[0] < 31) & (c[1] > 0.5), bit_body,
                                   (jnp.int32(BISECT_COARSE + BISECT_FIXED), pending(ct), t, ct))

    def tie_limit():
        need = kf - count_ge(t + 1)

        def block_ties(j, c):
            hit = key_ref[j] == t
            acc = jnp.zeros((fold, TQ), F32)
            for r in range(TK // fold):
                acc = jnp.where(hit[r * fold:(r + 1) * fold, :], acc + 1.0, acc)
            cnt_ref[pl.ds(j, 1), :] = jnp.sum(acc, axis=0, keepdims=True)
            return c

        lax.fori_loop(0, i + 1, block_ties, 0)
        nkb = cnt_ref.shape[0]
        cnt = jnp.where(lax.broadcasted_iota(jnp.int32, (nkb, TQ), 0) <= i, cnt_ref[...], 0.0)
        run = jnp.zeros((1, TQ), F32)
        blk_of = jnp.zeros((1, TQ), F32)
        before = jnp.zeros((1, TQ), F32)
        for j in range(nkb):
            run = run + cnt[j:j + 1, :]
            ahead = run < need
            blk_of = blk_of + jnp.where(ahead, 1.0, 0.0)
            before = jnp.where(ahead, run, before)
        blk_of = blk_of.astype(jnp.int32)

        def pick_block(j, c):
            m_ref[...] = jnp.where((blk_of == j) & (key_ref[j] == t), 1.0, m_ref[...])
            return c

        m_ref[...] = jnp.zeros_like(m_ref)
        lax.fori_loop(0, i + 1, pick_block, 0)
        tri = (lax.broadcasted_iota(jnp.int32, (TK, TK), 0) >= lax.broadcasted_iota(jnp.int32, (TK, TK), 1))
        upto = _dot(jnp.where(tri, 1.0, 0.0).astype(BF16), m_ref[...].astype(BF16))
        row_of = jnp.sum(jnp.where(upto < need - before, 1.0, 0.0), axis=0, keepdims=True)
        return blk_of * TK + row_of.astype(jnp.int32)

    x = lax.cond(jnp.max(c_ge) > kf, tie_limit, lambda: jnp.full((1, TQ), 2 * seq, jnp.int32))

    heads = range(nh)
    grp = range(DSA_GROUP)
    pairs = [slice((h // 2) * LANES, (h // 2 + 1) * LANES) for h in heads]
    n_full = i // DSA_GROUP

    def scores(jg):
        off = pl.multiple_of(jg * (DSA_GROUP * TK), DSA_GROUP * TK)
        return tuple(_dot(k_ref[pl.ds(off, DSA_GROUP * TK), pairs[h]], qs_ref[h]) for h in heads)

    def mask_biases(base, causal):
        biases = []
        for g in grp:
            key = key_ref[base + g]
            pos = (base + g) * TK + krow
            msk = (key > t) | ((key == t) & (pos <= x))
            if causal:
                msk = msk & (pos <= i * TQ + qcol)
            biases.append(jnp.where(msk, 0.0, NEG))
        return biases

    def attend_fixed(jg, _, causal):
        ss = scores(jg)
        ss_next = 0
        base = jg * DSA_GROUP
        biases = mask_biases(base, causal)
        for h in heads:
            pv = None
            for g in grp:
                p = jnp.exp((ss[h][g * TK:(g + 1) * TK, :] - m_fix[h]) + biases[g]).astype(BF16)
                d = _dot(vt_ref[0, base + g, h * LANES:(h + 1) * LANES, :], p)
                pv = d if pv is None else pv + d
            acc_ref[h] += pv
        return ss_next

    def attend(jg, carry, causal):
        ms, ss = carry
        ss_next = None if causal else scores(jg + 1)
        base = jg * DSA_GROUP
        biases = mask_biases(base, causal)
        sg = [[ss[h][g * TK:(g + 1) * TK, :] for g in grp] for h in heads]
        new = []
        for h in heads:
            m_new = ms[h]
            for g in grp:
                m_new = jnp.maximum(m_new, jnp.max(sg[h][g] + biases[g], axis=0, keepdims=True))
            new.append(m_new)
        ps = [[jnp.exp((sg[h][g] - new[h]) + biases[g]).astype(BF16) for g in grp] for h in heads]
        for h in heads:
            pv = None
            for g in grp:
                d = _dot(vt_ref[0, base + g, h * LANES:(h + 1) * LANES, :], ps[h][g])
                pv = d if pv is None else pv + d
            acc_ref[h] = jnp.exp(ms[h] - new[h]) * acc_ref[h] + pv
        return tuple(new), ss_next

    for h in heads:
        acc_ref[h] = jnp.zeros((LANES, TQ), F32)

    def run_online():
        carry = lax.fori_loop(0, n_full, lambda jg, c: attend(jg, c, False),
                              (tuple(jnp.full((1, TQ), NEG, F32) for _ in heads), scores(0)))
        attend(n_full, carry, True)
        return 0

    def run_fixed():
        lax.fori_loop(0, n_full, lambda jg, c: attend_fixed(jg, c, False), 0)
        attend_fixed(n_full, 0, True)
        return 0

    lax.cond(bound_max <= SAFE_SHIFT, run_fixed, run_online)

    out_t = jnp.concatenate([_finish_softmax_head(acc_ref[h]) for h in range(nh)], axis=0)
    o_ref[...] = out_t.T


def _dsa_call(f_arr, h_arr, vt, batch, seq):
    nq = seq // TQ
    wb = N_HEADS_B * HEAD_DIM
    return pl.pallas_call(
        _dsa_kernel,
        grid=(batch, nq),
        in_specs=[pl.BlockSpec((TQ, IDX_HEADS * IDX_DIM), lambda b, i: (b * nq + i, H_QI // (IDX_HEADS * IDX_DIM))),
                  pl.BlockSpec((TQ, LANES), lambda b, i: (b * nq + i, F_WI // LANES)),
                  pl.BlockSpec((seq, LANES), lambda b, i: (b, H_KI // LANES)),
                  pl.BlockSpec((TQ, wb), lambda b, i: (b * nq + i, F_QB // wb)),
                  pl.BlockSpec((seq, wb), lambda b, i: (b, H_KB // wb)),
                  pl.BlockSpec((1, seq // TK, N_HEADS_B * LANES, TK), lambda b, i: (b, 0, VT_B // (N_HEADS_B * LANES), 0))],
        out_specs=pl.BlockSpec((TQ, wb), lambda b, i: (b * nq + i, 0)),
        out_shape=jax.ShapeDtypeStruct((batch * seq, wb), F32),
        scratch_shapes=[pltpu.VMEM((seq // TK, TK, TQ), jnp.int32),
                        pltpu.VMEM((seq // TK, TK, TQ), BF16),
                        pltpu.VMEM((IDX_HEADS, LANES, TQ), BF16),
                        pltpu.VMEM((N_HEADS_B, LANES, TQ), BF16),
                        pltpu.VMEM((N_HEADS_B, LANES, TQ), F32),
                        pltpu.VMEM((seq // TK, TQ), F32),
                        pltpu.VMEM((TK, TQ), F32),
                        pltpu.VMEM((1, wb), F32)],
        compiler_params=_cparams(("parallel", "arbitrary")),
        name="dsa_attn",
    )(h_arr, f_arr, h_arr, f_arr, h_arr, vt)


def _outproj_kernel(x_ref, oa_ref, ob_ref, oc_ref, og_ref, gt_ref, w_ref, o_ref):
    wa = N_HEADS_A * HEAD_DIM
    wb = wa + N_HEADS_B * HEAD_DIM
    og = og_ref[...]
    y = (_dot((oa_ref[...] * og[:, :wa]).astype(BF16), w_ref[:wa, :])
         + _dot((ob_ref[...] * og[:, wa:wb]).astype(BF16), w_ref[wa:wb, :])
         + _dot((oc_ref[...] * og[:, wb:]).astype(BF16), w_ref[wb:, :]))
    o_ref[...] = x_ref[...] + gt_ref[0] * y


def _outproj_call(x2, oa, ob, oc, og, gt, w_out, seq):
    m, d = x2.shape
    tm = 512
    row = lambda i: (i, 0)
    return pl.pallas_call(
        _outproj_kernel,
        grid=(m // tm,),
        in_specs=[pl.BlockSpec((tm, d), row),
                  pl.BlockSpec((tm, oa.shape[1]), row),
                  pl.BlockSpec((tm, ob.shape[1]), row),
                  pl.BlockSpec((tm, oc.shape[1]), row),
                  pl.BlockSpec((1, d), lambda i: (0, 0)),
                  pl.BlockSpec((1, 1, d), lambda i: ((i * tm) // seq, 0, 0)),
                  pl.BlockSpec((d, d), lambda i: (0, 0))],
        out_specs=pl.BlockSpec((tm, d), row),
        out_shape=jax.ShapeDtypeStruct((m, d), F32),
        compiler_params=_cparams(("parallel",)),
        name="out_proj",
    )(x2, oa, ob, oc, og, gt, w_out)


def _rope_tables(seq):
    pos = jnp.arange(seq, dtype=F32)
    inv = ROPE_THETA ** (-jnp.arange(0, ROPE_DIM, 2, dtype=F32) / ROPE_DIM)
    ang = pos[:, None] * inv[None, :]
    cos, sin = jnp.cos(ang), jnp.sin(ang)
    zeros = jnp.zeros((seq, HEAD_DIM - ROPE_DIM), F32)
    zh = jnp.zeros((seq, ROPE_HALF), F32)
    cos_h = jnp.concatenate([cos, cos, jnp.ones_like(zeros)], axis=1)
    s1_h = jnp.concatenate([-sin, zh, zeros], axis=1)
    s2_h = jnp.concatenate([zh, sin, zeros], axis=1)
    two = lambda t: jnp.concatenate([t, t], axis=1)
    return two(cos_h), two(s1_h), two(s2_h)


def _permute_w_in(w):
    d = w.shape[0]
    a, b, c = N_HEADS_A * HEAD_DIM, N_HEADS_B * HEAD_DIM, N_HEADS_C * HEAD_DIM
    sizes = (a, a, a, b, b, b, IDX_HEADS * IDX_DIM, IDX_DIM, IDX_HEADS, c, c, c)
    offs = np.concatenate([[0], np.cumsum(sizes)])
    qa, ka, va, qb, kb, vb, qi, ki, wi, qc, kc, vc = [w[:, int(offs[k]):int(offs[k + 1])] for k in range(12)]
    z = lambda n: jnp.zeros((d, n), w.dtype)
    out = jnp.concatenate([qa, qb, qc, ka, va, kb, vb, kc, vc, qi,
                           ki, z(LANES - IDX_DIM), wi, z(LANES - IDX_HEADS)], axis=1)
    assert out.shape[1] == W_PERM_WIDTH
    return out.astype(BF16)


def _suffix_sum_matrix():
    s = np.arange(TK + SB_TAIL)[:, None]
    j = np.arange(TK)[None, :]
    return jnp.asarray((j > s) | (s >= TK), BF16)


def kernel(x, c, w_ada, b_ada, norm_g, w_in, qk_g, out_g, w_out, ffn_w1, ffn_w3, ffn_w2):
    batch, seq, d = x.shape
    depth = w_ada.shape[0]
    assert seq % PROJ_TM == 0 and d % LANES == 0 and batch <= 8
    assert (seq // TK) % MOBA_GROUP == 0 and (seq // TK) % DSA_GROUP == 0

    c_pad = jnp.zeros((8, d), F32).at[:batch].set(c)
    mod = _mod_call(c_pad, w_ada, b_ada)
    cos_t, s1_t, s2_t = _rope_tables(seq)
    hd = np.arange(LANES) // HEAD_DIM
    bd = jnp.asarray(hd[:, None] == hd[None, :], BF16)
    u_mat = _suffix_sum_matrix()

    x2 = x.reshape(batch * seq, d)
    for layer in range(depth):
        mods = [mod[layer, :batch, k * d:(k + 1) * d].reshape(batch, 1, d) for k in range(N_MOD)]
        sh1, sc1, g1, sh2, sc2, g2, sh3, sc3, g3 = mods
        ng = norm_g[layer]
        w1 = ffn_w1[layer].astype(BF16)
        w3 = ffn_w3[layer].astype(BF16)
        w2 = ffn_w2[layer].astype(BF16)

        x2 = _ffn_call(x2, ng[0:1], sh1, sc1, g1, w1[0], w3[0], w2[0], seq)

        qkg = jnp.concatenate([jnp.tile(qk_g[layer], (1, 2)), jnp.ones((4, LANES), F32)], axis=0)
        f_arr, h_arr, kmean, vt = _proj_call(x2, ng[1:2], sh2, sc2, _permute_w_in(w_in[layer]), qkg,
                                             cos_t, s1_t, s2_t, bd, batch, seq)
        per_tile = PROJ_TM // MOBA_BLOCK
        kmean = kmean[:, :per_tile, :].reshape(batch, seq // MOBA_BLOCK, N_HEADS_A * HEAD_DIM)
        oa = _moba_call(f_arr, h_arr, kmean, vt, batch, seq)
        ob = _dsa_call(f_arr, h_arr, vt, batch, seq)
        oc = _sb_call(f_arr, h_arr, vt, u_mat, batch, seq)
        x2 = _outproj_call(x2, oa, ob, oc, out_g[layer].reshape(1, d), g2, w_out[layer].astype(BF16), seq)

        x2 = _ffn_call(x2, ng[2:3], sh3, sc3, g3, w1[1], w3[1], w2[1], seq)
    return x2.reshape(batch, seq, d)
```

```python
import jax
import jax.numpy as jnp
import numpy as np
from jax import lax
from jax.experimental import pallas as pl
from jax.experimental.pallas import tpu as pltpu

F32 = jnp.float32
BF16 = jnp.bfloat16

HEAD_DIM = 64
N_HEADS_A = 4
N_HEADS_B = 4
N_HEADS_C = 8
ROPE_DIM = HEAD_DIM // 4
ROPE_HALF = ROPE_DIM // 2
ROPE_THETA = 500000.0
MOBA_BLOCK = 256
MOBA_TOPK = 3
DSA_TOPK = 256
IDX_HEADS = 8
IDX_DIM = 64
N_MOD = 9
RMS_EPS = 1e-6
NEG = -1e30
ATT_SCALE = HEAD_DIM ** -0.5

LANES = 128
MXU_N = 256
TQ = 256
TK = 256
PROJ_TM = 512
VMEM_LIMIT = 56 * 1024 * 1024

SB_DEAD_LOG = -88.0
SB_TAIL = 16
SB_HEADS = 8
MOBA_GROUP = 4
DSA_GROUP = 2
BOUND_SLACK = 1.02
SAFE_SHIFT = 40.0
BISECT_COARSE = 15
BISECT_FIXED = 6

F_QA, F_QB, F_QC, F_WI, F_WIDTH = 0, 256, 512, 1024, 1152
H_KA, H_KB, H_KC, H_QI, H_KI, H_WIDTH = 0, 256, 512, 1024, 1536, 1664
W_PERM_WIDTH = 15 * MXU_N
VT_A, VT_B, VT_C, VT_ROWS = 0, 512, 1024, 1536


def _dot(a, b):
    return jnp.dot(a, b, preferred_element_type=F32)


def _dot_nt(a, b):
    return lax.dot_general(a, b, (((1,), (1,)), ((), ())), preferred_element_type=F32)


def _split3(x):
    a = x.astype(BF16)
    r = x - a.astype(F32)
    b = r.astype(BF16)
    c = (r - b.astype(F32)).astype(BF16)
    return a, b, c


def _cparams(sem):
    return pltpu.CompilerParams(dimension_semantics=sem, vmem_limit_bytes=VMEM_LIMIT)


def _mod_kernel(c_ref, w_ref, b_ref, o_ref):
    c = c_ref[...]
    sc = c * (1.0 / (1.0 + jnp.exp(-c)))
    a, b, c3 = _split3(sc)
    w = w_ref[0]
    wa, wb, wc = _split3(w)
    acc = _dot(a, wa) + (_dot(a, wb) + _dot(b, wa)) + (_dot(a, wc) + _dot(b, wb) + _dot(c3, wa))
    o_ref[0] = acc + b_ref[0]


def _mod_call(c_pad, w_ada, b_ada):
    depth, d, n = w_ada.shape
    tn = 1024
    rows = c_pad.shape[0]
    return pl.pallas_call(
        _mod_kernel,
        grid=(depth, n // tn),
        in_specs=[pl.BlockSpec((rows, d), lambda l, j: (0, 0)),
                  pl.BlockSpec((1, d, tn), lambda l, j: (l, 0, j)),
                  pl.BlockSpec((1, 1, tn), lambda l, j: (l, 0, j))],
        out_specs=pl.BlockSpec((1, rows, tn), lambda l, j: (l, 0, j)),
        out_shape=jax.ShapeDtypeStruct((depth, rows, n), F32),
        compiler_params=_cparams(("parallel", "parallel")),
        name="adaln_mod",
    )(c_pad, w_ada, b_ada.reshape(depth, 1, n))


def _norm_modulate(x, ng, sh, sc):
    ms = jnp.mean(x * x, axis=-1, keepdims=True)
    h = x * lax.rsqrt(ms + RMS_EPS) * ng
    return h * (1.0 + sc) + sh


def _ffn_kernel(x_ref, ng_ref, sh_ref, sc_ref, gt_ref, w1_ref, w3_ref, w2_ref, o_ref, h_ref, acc_ref):
    j = pl.program_id(1)

    @pl.when(j == 0)
    def _():
        h = _norm_modulate(x_ref[...], ng_ref[...], sh_ref[0], sc_ref[0])
        h_ref[...] = h.astype(BF16)
        acc_ref[...] = jnp.zeros_like(acc_ref)

    h = h_ref[...]
    a = _dot(h, w1_ref[...])
    b = _dot(h, w3_ref[...])
    u = (a * (1.0 / (1.0 + jnp.exp(-a))) * b).astype(BF16)
    acc_ref[...] += _dot(u, w2_ref[...])

    @pl.when(j == pl.num_programs(1) - 1)
    def _():
        o_ref[...] = x_ref[...] + 0.5 * gt_ref[0] * acc_ref[...]


def _ffn_call(x2, ng, sh, sc, gt, w1, w3, w2, seq):
    m, d = x2.shape
    dff = w1.shape[1]
    tm = 1024
    tf = dff // 2 if (dff // 2) % LANES == 0 else MXU_N
    bidx = lambda i, j: ((i * tm) // seq, 0, 0)
    return pl.pallas_call(
        _ffn_kernel,
        grid=(m // tm, dff // tf),
        in_specs=[pl.BlockSpec((tm, d), lambda i, j: (i, 0)),
                  pl.BlockSpec((1, d), lambda i, j: (0, 0)),
                  pl.BlockSpec((1, 1, d), bidx),
                  pl.BlockSpec((1, 1, d), bidx),
                  pl.BlockSpec((1, 1, d), bidx),
                  pl.BlockSpec((d, tf), lambda i, j: (0, j)),
                  pl.BlockSpec((d, tf), lambda i, j: (0, j)),
                  pl.BlockSpec((tf, d), lambda i, j: (j, 0))],
        out_specs=pl.BlockSpec((tm, d), lambda i, j: (i, 0)),
        out_shape=jax.ShapeDtypeStruct((m, d), F32),
        scratch_shapes=[pltpu.VMEM((tm, d), BF16), pltpu.VMEM((tm, d), F32)],
        compiler_params=_cparams(("parallel", "arbitrary")),
        name="swiglu_ffn",
    )(x2, ng, sh, sc, gt, w1, w3, w2)


def _proj_kernel(x_ref, ng_ref, sh_ref, sc_ref, w_ref, qkg_ref, cos_ref, s1_ref, s2_ref, bd_ref,
                 f_ref, h_ref, km_ref, vt_ref, hs_ref):
    hs_ref[...] = _norm_modulate(x_ref[...], ng_ref[...], sh_ref[0], sc_ref[0]).astype(BF16)
    cos, s1, s2 = cos_ref[...], s1_ref[...], s2_ref[...]
    bd = bd_ref[...]
    tm = x_ref.shape[0]

    def rope(v):
        return v * cos + pltpu.roll(v, LANES - ROPE_HALF, 1) * s1 + pltpu.roll(v, ROPE_HALF, 1) * s2

    def headnorm(v, g):
        a, b, _ = _split3(v * v)
        ss = _dot(a, bd) + _dot(b, bd)
        return v * lax.rsqrt(ss * (1.0 / HEAD_DIM) + RMS_EPS) * g

    def chunk(c):
        y = _dot(hs_ref[...], w_ref[:, c * MXU_N:(c + 1) * MXU_N])
        return y[:, :LANES], y[:, LANES:]

    def normrope_chunk(c, grow):
        g = qkg_ref[grow:grow + 1, :]
        return [rope(headnorm(v, g)) for v in chunk(c)]

    def put(ref, off, halves, dtype):
        for k, v in enumerate(halves):
            ref[:, off + k * LANES: off + (k + 1) * LANES] = v.astype(dtype)

    def put_vt(base, halves, with_ones):
        ones = jnp.ones((HEAD_DIM, TK), BF16)
        for r in range(tm // TK):
            for k, v in enumerate(halves):
                t = v[r * TK:(r + 1) * TK, :].T.astype(BF16)
                if with_ones:
                    for hh in range(2):
                        row = base + (2 * k + hh) * LANES
                        vt_ref[0, r, row:row + HEAD_DIM, :] = t[hh * HEAD_DIM:(hh + 1) * HEAD_DIM, :]
                        vt_ref[0, r, row + HEAD_DIM:row + LANES, :] = ones
                else:
                    vt_ref[0, r, base + k * LANES:base + (k + 1) * LANES, :] = t

    put(f_ref, F_QA, normrope_chunk(0, 0), F32)
    put(f_ref, F_QB, normrope_chunk(1, 2), F32)
    put(f_ref, F_QC, chunk(2), F32)
    put(f_ref, F_QC + MXU_N, chunk(3), F32)

    ka = normrope_chunk(4, 1)
    put(h_ref, H_KA, ka, BF16)
    rows = lax.broadcasted_iota(jnp.int32, (8, LANES), 0)
    for k, v in enumerate(ka):
        km = jnp.zeros((8, LANES), F32)
        for r in range(tm // MOBA_BLOCK):
            s = jnp.sum(v[r * MOBA_BLOCK:(r + 1) * MOBA_BLOCK, :], axis=0, keepdims=True) * (1.0 / MOBA_BLOCK)
            km = jnp.where(rows == r, s, km)
        km_ref[0, :, k * LANES:(k + 1) * LANES] = km

    put_vt(VT_A, chunk(5), True)
    put(h_ref, H_KB, normrope_chunk(6, 3), BF16)
    put_vt(VT_B, chunk(7), True)
    put(h_ref, H_KC, chunk(8), BF16)
    put(h_ref, H_KC + MXU_N, chunk(9), BF16)
    put_vt(VT_C, chunk(10), False)
    put_vt(VT_C + MXU_N, chunk(11), False)
    put(h_ref, H_QI, [rope(v) for v in chunk(12)], BF16)
    put(h_ref, H_QI + MXU_N, [rope(v) for v in chunk(13)], BF16)
    ki, wi = chunk(14)
    h_ref[:, H_KI:H_KI + LANES] = rope(ki).astype(BF16)
    f_ref[:, F_WI:F_WI + LANES] = wi


def _proj_call(x2, ng, sh, sc, w_perm, qkg, cos_t, s1_t, s2_t, bd, batch, seq):
    m, d = x2.shape
    tm = PROJ_TM
    nt = seq // tm
    bidx = lambda i: ((i * tm) // seq, 0, 0)
    tab = pl.BlockSpec((tm, LANES), lambda i: (i % nt, 0))
    return pl.pallas_call(
        _proj_kernel,
        grid=(m // tm,),
        in_specs=[pl.BlockSpec((tm, d), lambda i: (i, 0)),
                  pl.BlockSpec((1, d), lambda i: (0, 0)),
                  pl.BlockSpec((1, 1, d), bidx),
                  pl.BlockSpec((1, 1, d), bidx),
                  pl.BlockSpec((d, W_PERM_WIDTH), lambda i: (0, 0)),
                  pl.BlockSpec((8, LANES), lambda i: (0, 0)),
                  tab, tab, tab,
                  pl.BlockSpec((LANES, LANES), lambda i: (0, 0))],
        out_specs=[pl.BlockSpec((tm, F_WIDTH), lambda i: (i, 0)),
                   pl.BlockSpec((tm, H_WIDTH), lambda i: (i, 0)),
                   pl.BlockSpec((1, 8, 2 * LANES), lambda i: (i, 0, 0)),
                   pl.BlockSpec((1, tm // TK, VT_ROWS, TK), lambda i: (i // nt, i % nt, 0, 0))],
        out_shape=[jax.ShapeDtypeStruct((m, F_WIDTH), F32),
                   jax.ShapeDtypeStruct((m, H_WIDTH), BF16),
                   jax.ShapeDtypeStruct((m // tm, 8, 2 * LANES), F32),
                   jax.ShapeDtypeStruct((batch, seq // TK, VT_ROWS, TK), BF16)],
        scratch_shapes=[pltpu.VMEM((tm, d), BF16)],
        compiler_params=_cparams(("parallel",)),
        name="in_proj",
    )(x2, ng, sh, sc, w_perm, qkg, cos_t, s1_t, s2_t, bd)


def _head_masks():
    lane = lax.broadcasted_iota(jnp.int32, (1, LANES), 1)
    return [(lane >= HEAD_DIM * h) & (lane < HEAD_DIM * (h + 1)) for h in range(2)]


def _head_rms_t(o):
    ss = jnp.sum(o * o, axis=0, keepdims=True) * (1.0 / HEAD_DIM)
    return o * lax.rsqrt(ss + RMS_EPS)


def _finish_softmax_head(acc):
    return _head_rms_t(acc[:HEAD_DIM, :] / acc[HEAD_DIM:HEAD_DIM + 1, :])


def _moba_kernel(q_ref, km_ref, k_ref, vt_ref, o_ref, bias_ref, qs_ref, acc_ref, kmax2_ref):
    i = pl.program_id(2)
    nb = km_ref.shape[1]
    q = q_ref[...]
    kma, kmb, _ = _split3(km_ref[0])
    blk = lax.broadcasted_iota(jnp.int32, (nb, TQ), 0).astype(F32)
    i_f = i.astype(F32)
    krow = lax.broadcasted_iota(jnp.int32, (TK, TQ), 0)
    qcol = lax.broadcasted_iota(jnp.int32, (TK, TQ), 1)
    causal = krow <= qcol

    q_norm2 = []
    for h, hm in enumerate(_head_masks()):
        qh_t = jnp.where(hm, q, 0.0).T
        qa, qb, _ = _split3(qh_t)
        gate = _dot(kma, qa) + (_dot(kmb, qa) + _dot(kma, qb))
        gate = jnp.where(blk < i_f, gate, NEG)
        sel = jnp.zeros((nb, TQ), F32)
        for _ in range(MOBA_TOPK):
            mx = jnp.max(gate, axis=0, keepdims=True)
            idx = jnp.min(jnp.where(gate == mx, blk, float(nb)), axis=0, keepdims=True)
            pick = blk == idx
            sel = jnp.where(pick, 1.0, sel)
            gate = jnp.where(pick, -jnp.inf, gate)
        bias_ref[h] = jnp.where((sel > 0.5) & (blk < i_f), 0.0, NEG)
        qs_t = qh_t * ATT_SCALE
        qs_ref[h] = qs_t.astype(BF16)
        q_norm2.append(jnp.sum(qs_t * qs_t, axis=0, keepdims=True))

    @pl.when(i == 0)
    def _():
        head_of = lambda a: lax.shift_right_logical(lax.broadcasted_iota(jnp.int32, (LANES, LANES), a), 6)
        ones_bd = jnp.where(head_of(0) == head_of(1), 1.0, 0.0).astype(BF16)

        def norm_body(j, mx):
            kk = k_ref[pl.ds(pl.multiple_of(j * TK, TK), TK), :].astype(F32)
            return jnp.maximum(mx, jnp.max(_dot((kk * kk).astype(BF16), ones_bd), axis=0, keepdims=True))
        kmax2_ref[...] = lax.fori_loop(0, nb, norm_body, jnp.zeros((1, LANES), F32))

    m_fix = [jnp.sqrt(q_norm2[h] * kmax2_ref[0:1, h * HEAD_DIM:h * HEAD_DIM + 1] + 1e-30) * BOUND_SLACK
             for h in range(2)]

    own = pl.multiple_of(i * TK, TK)
    heads = range(2)
    grp = range(MOBA_GROUP)
    n_groups = (i + MOBA_GROUP - 1) // MOBA_GROUP
    last_group = km_ref.shape[1] // MOBA_GROUP - 1

    def scores(jg):
        off = pl.multiple_of(jnp.minimum(jg, last_group) * (MOBA_GROUP * TK), MOBA_GROUP * TK)
        kb = k_ref[pl.ds(off, MOBA_GROUP * TK), :]
        return tuple(_dot(kb, qs_ref[h]) for h in heads)

    def own_block(refs):
        k_own = k_ref[pl.ds(own, TK), :]
        used = []
        for h in heads:
            s = jnp.where(causal, _dot(k_own, qs_ref[h]), NEG)
            m0 = jnp.max(s, axis=0, keepdims=True) if refs is None else refs[h]
            acc_ref[h] = _dot(vt_ref[0, i, h * LANES:(h + 1) * LANES, :], jnp.exp(s - m0).astype(BF16))
            used.append(m0)
        return tuple(used)

    def body_fixed(jg, c):
        ss = scores(jg)
        base = jg * MOBA_GROUP
        for h in heads:
            pv = None
            for g in grp:
                b = bias_ref[h, pl.ds(base + g, 1), :]
                p = jnp.exp(ss[h][g * TK:(g + 1) * TK, :] - (m_fix[h] - b)).astype(BF16)
                d = _dot(vt_ref[0, base + g, h * LANES:(h + 1) * LANES, :], p)
                pv = d if pv is None else pv + d
            acc_ref[h] += pv
        return c

    def body(jg, carry):
        ms, ss = carry
        ss_next = scores(jg + 1)
        base = jg * MOBA_GROUP
        sg = [[ss[h][g * TK:(g + 1) * TK, :] for g in grp] for h in heads]
        bs = [[bias_ref[h, pl.ds(base + g, 1), :] for g in grp] for h in heads]
        new = []
        for h in heads:
            m_new = ms[h]
            for g in grp:
                m_new = jnp.maximum(m_new, jnp.max(sg[h][g], axis=0, keepdims=True) + bs[h][g])
            new.append(m_new)
        ps = [[jnp.exp(sg[h][g] - (new[h] - bs[h][g])).astype(BF16) for g in grp] for h in heads]
        for h in heads:
            pv = None
            for g in grp:
                d = _dot(vt_ref[0, base + g, h * LANES:(h + 1) * LANES, :], ps[h][g])
                pv = d if pv is None else pv + d
            acc_ref[h] = jnp.exp(ms[h] - new[h]) * acc_ref[h] + pv
        return tuple(new), ss_next

    def run_online():
        lax.fori_loop(0, n_groups, body, (own_block(None), scores(0)))
        return 0

    def run_fixed():
        own_block(m_fix)
        lax.fori_loop(0, n_groups, body_fixed, 0)
        return 0

    lax.cond(jnp.max(jnp.maximum(m_fix[0], m_fix[1])) <= SAFE_SHIFT, run_fixed, run_online)
    out_t = jnp.concatenate([_finish_softmax_head(acc_ref[h]) for h in range(2)], axis=0)
    o_ref[...] = out_t.T


def _moba_call(f_arr, h_arr, kmean, vt, batch, seq):
    nq = seq // TQ
    nb = seq // MOBA_BLOCK
    npair = N_HEADS_A // 2
    return pl.pallas_call(
        _moba_kernel,
        grid=(batch, npair, nq),
        in_specs=[pl.BlockSpec((TQ, LANES), lambda b, p, i: (b * nq + i, F_QA // LANES + p)),
                  pl.BlockSpec((1, nb, LANES), lambda b, p, i: (b, 0, p)),
                  pl.BlockSpec((seq, LANES), lambda b, p, i: (b, H_KA // LANES + p)),
                  pl.BlockSpec((1, seq // TK, 2 * LANES, TK), lambda b, p, i: (b, 0, VT_A // (2 * LANES) + p, 0))],
        out_specs=pl.BlockSpec((TQ, LANES), lambda b, p, i: (b * nq + i, p)),
        out_shape=jax.ShapeDtypeStruct((batch * seq, N_HEADS_A * HEAD_DIM), F32),
        scratch_shapes=[pltpu.VMEM((2, nb, TQ), F32),
                        pltpu.VMEM((2, LANES, TQ), BF16),
                        pltpu.VMEM((2, LANES, TQ), F32),
                        pltpu.VMEM((1, LANES), F32)],
        compiler_params=_cparams(("parallel", "parallel", "arbitrary")),
        name="moba_attn",
    )(f_arr, kmean, h_arr, vt)


def _sb_kernel(q_ref, k_ref, vt_ref, u_ref, o_ref, qs_ref, acc_ref):
    i = pl.program_id(2)
    krow = lax.broadcasted_iota(jnp.int32, (TK, TQ), 0)
    qcol = lax.broadcasted_iota(jnp.int32, (TK, TQ), 1)
    strict = krow < qcol
    hms = _head_masks()
    for h in range(SB_HEADS):
        pr = slice((h // 2) * LANES, (h // 2 + 1) * LANES)
        qs_ref[h] = (jnp.where(hms[h % 2], q_ref[:, pr], 0.0) * ATT_SCALE).T.astype(BF16)
        acc_ref[h] = jnp.zeros((LANES, TQ), F32)

    heads = range(SB_HEADS)
    pairs = [slice((h // 2) * LANES, (h // 2 + 1) * LANES) for h in heads]

    def logits(j):
        off = pl.multiple_of(jnp.maximum(j, 0) * TK, TK)
        return tuple(_dot(k_ref[pl.ds(off, TK), pairs[h]], qs_ref[h]) for h in heads)

    def block(j, rs, zs, diag):
        zs_next = logits(j - 1)
        u = u_ref[...]
        log_betas, his, los = [], [], []
        for h in heads:
            z = zs[h]
            log_beta = jnp.minimum(z, 0.0) - jnp.log(1.0 + jnp.exp(-jnp.abs(z)))
            log_1m = log_beta - z
            if diag:
                log_1m = jnp.where(strict, log_1m, 0.0)
            hi = log_1m.astype(BF16)
            log_betas.append(log_beta)
            his.append(hi)
            los.append((log_1m - hi.astype(F32)).astype(BF16))
        sums = [_dot(u, his[h]) + _dot(u, los[h]) for h in heads]
        weights = []
        for h in heads:
            after = sums[h][:TK, :] + rs[h]
            a = jnp.exp(log_betas[h] + after)
            if diag:
                a = jnp.where(strict, a, 0.0)
            weights.append(a.astype(BF16))
        for h in heads:
            acc_ref[h] += _dot(vt_ref[0, j, pairs[h], :], weights[h])
        return tuple(rs[h] + sums[h][TK:TK + 1, :] for h in heads), zs_next

    zero = jnp.zeros((1, TQ), F32)
    rs, zs = block(i, (zero,) * SB_HEADS, logits(i), True)

    def alive(rs):
        m = rs[0]
        for r in rs[1:]:
            m = jnp.maximum(m, r)
        return jnp.max(m)

    def cond(c):
        return (c[0] >= 0) & (c[1] > SB_DEAD_LOG)

    def body(c):
        rs, zs = block(c[0], c[2], c[3], False)
        return c[0] - 1, alive(rs), rs, zs

    lax.while_loop(cond, body, (i - 1, alive(rs), rs, zs))
    out_t = jnp.concatenate(
        [_head_rms_t(acc_ref[h][(h % 2) * HEAD_DIM:(h % 2 + 1) * HEAD_DIM, :]) for h in range(SB_HEADS)], axis=0)
    o_ref[...] = out_t.T


def _sb_call(f_arr, h_arr, vt, u_mat, batch, seq):
    nq = seq // TQ
    w = SB_HEADS * HEAD_DIM
    ngrp = N_HEADS_C // SB_HEADS
    return pl.pallas_call(
        _sb_kernel,
        grid=(batch, ngrp, nq),
        in_specs=[pl.BlockSpec((TQ, w), lambda b, p, i: (b * nq + i, F_QC // w + p)),
                  pl.BlockSpec((seq, w), lambda b, p, i: (b, H_KC // w + p)),
                  pl.BlockSpec((1, seq // TK, w, TK), lambda b, p, i: (b, 0, VT_C // w + p, 0)),
                  pl.BlockSpec((TK + SB_TAIL, TK), lambda b, p, i: (0, 0))],
        out_specs=pl.BlockSpec((TQ, w), lambda b, p, i: (b * nq + i, p)),
        out_shape=jax.ShapeDtypeStruct((batch * seq, N_HEADS_C * HEAD_DIM), F32),
        scratch_shapes=[pltpu.VMEM((SB_HEADS, LANES, TQ), BF16), pltpu.VMEM((SB_HEADS, LANES, TQ), F32)],
        compiler_params=_cparams(("parallel", "parallel", "arbitrary")),
        name="stickbreak_attn",
    )(f_arr, h_arr, vt, u_mat)


def _sortable_key(x):
    bits = lax.bitcast_convert_type(x, jnp.int32)
    return jnp.where(bits < 0, bits ^ jnp.int32(0x7FFFFFFF), bits)


_NEG_BITS = int(np.array(NEG, np.float32).view(np.int32))
NEG_KEY = _NEG_BITS ^ 0x7FFFFFFF
INT_MIN = -2 ** 31
MIN_NORMAL_BITS = 0x00800000


def _dsa_kernel(qi_ref, wi_ref, ki_ref, q_ref, k_ref, vt_ref, o_ref,
                key_ref, coarse_ref, qx_ref, qs_ref, acc_ref, cnt_ref, m_ref, kmax2_ref):
    i = pl.program_id(1)
    seq = k_ref.shape[0]
    topk = min(DSA_TOPK, seq // 4)
    nh = N_HEADS_B
    lane = lax.broadcasted_iota(jnp.int32, (1, LANES), 1)
    low = lane < HEAD_DIM
    krow = lax.broadcasted_iota(jnp.int32, (TK, TQ), 0)
    qcol = lax.broadcasted_iota(jnp.int32, (TK, TQ), 1)
    diag_causal = krow <= qcol
    hms = _head_masks()

    qi = qi_ref[...].astype(F32)
    for h in range(IDX_HEADS):
        qp = qi[:, (h // 2) * LANES:(h // 2 + 1) * LANES]
        if h % 2:
            qp = pltpu.roll(qp, HEAD_DIM, 1)
        qx_ref[h] = jnp.where(low, qp, 0.0).T.astype(BF16)
    w_t = wi_ref[...].T
    q = q_ref[...]
    q_norm2 = []
    for h in range(nh):
        qh = jnp.where(hms[h % 2], q[:, (h // 2) * LANES:(h // 2 + 1) * LANES], 0.0)
        qh_t = (qh * ATT_SCALE).T
        qs_ref[h] = qh_t.astype(BF16)
        q_norm2.append(jnp.sum(qh_t * qh_t, axis=0, keepdims=True))

    @pl.when(i == 0)
    def _():
        wk = k_ref.shape[1]
        head_of = lambda a: lax.shift_right_logical(lax.broadcasted_iota(jnp.int32, (wk, wk), a), 6)
        ones_bd = jnp.where(head_of(0) == head_of(1), 1.0, 0.0).astype(BF16)

        def body(j, mx):
            kk = k_ref[pl.ds(pl.multiple_of(j * TK, TK), TK), :].astype(F32)
            return jnp.maximum(mx, jnp.max(_dot((kk * kk).astype(BF16), ones_bd), axis=0, keepdims=True))
        kmax2_ref[...] = lax.fori_loop(0, seq // TK, body, jnp.zeros((1, wk), F32))

    m_fix = [jnp.sqrt(q_norm2[h] * kmax2_ref[0:1, h * HEAD_DIM:h * HEAD_DIM + 1] + 1e-30) * BOUND_SLACK
             for h in range(nh)]
    bound_max = jnp.max(jnp.maximum(jnp.maximum(m_fix[0], m_fix[1]), jnp.maximum(m_fix[2], m_fix[3])))

    def score_block(j, diag):
        off = pl.multiple_of(j * TK, TK)
        kz = ki_ref[pl.ds(off, TK), :]
        sc = jnp.zeros((TK, TQ), F32)
        for h in range(IDX_HEADS):
            sc = sc + w_t[h:h + 1, :] * jnp.maximum(_dot(kz, qx_ref[h]), 0.0)
        if diag:
            sc = jnp.where(diag_causal, sc, NEG)
        bits = lax.bitcast_convert_type(sc, jnp.int32)
        bits = jnp.where(bits == INT_MIN, 0, bits)
        key_ref[j] = jnp.where(bits < 0, bits ^ jnp.int32(0x7FFFFFFF), bits)
        coarse_ref[j] = lax.bitcast_convert_type(bits & jnp.int32(-65536), F32).astype(BF16)

    def score_pair(jj, c):
        score_block(2 * jj, False)
        score_block(2 * jj + 1, False)
        return c

    lax.fori_loop(0, i // 2, score_pair, 0)

    @pl.when(i % 2 == 1)
    def _():
        score_block(i - 1, False)

    score_block(i, True)
    for g in range(1, DSA_GROUP):
        @pl.when(i % DSA_GROUP + g < DSA_GROUP)
        def _():
            key_ref[i + g] = jnp.full((TK, TQ), NEG_KEY, jnp.int32)

    n_unscanned = (seq - (i + 1) * TK).astype(F32)
    fold = TK // 4

    def count(pred):
        def body(j, acc):
            hit = pred(key_ref[j], j)
            for r in range(TK // fold):
                acc = jnp.where(hit[r * fold:(r + 1) * fold, :], acc + 1.0, acc)
            return acc
        acc = lax.fori_loop(0, i + 1, body, jnp.zeros((fold, TQ), F32))
        return jnp.sum(acc, axis=0, keepdims=True)

    def count_ge(cand):
        return count(lambda key, j: key >= cand) + jnp.where(cand <= NEG_KEY, n_unscanned, 0.0)

    kf = float(topk)

    def count_ge_coarse(cand):
        cand_bits = jnp.where(cand < 0, (cand ^ jnp.int32(0x7FFFFFFF)) & jnp.int32(-65536), cand)
        cand_bits = jnp.where((cand > 0) & (cand < MIN_NORMAL_BITS), MIN_NORMAL_BITS, cand_bits)
        cand_b = lax.bitcast_convert_type(cand_bits, F32).astype(BF16)
        one, zero = jnp.ones((), BF16), jnp.zeros((), BF16)

        def body(j, acc):
            hit = jnp.where(coarse_ref[j] >= cand_b, one, zero)
            return acc + ((hit[:fold] + hit[fold:2 * fold]) + (hit[2 * fold:3 * fold] + hit[3 * fold:]))
        acc = lax.fori_loop(0, i + 1, body, jnp.zeros((fold, TQ), BF16))
        return (jnp.sum(acc.astype(F32), axis=0, keepdims=True)
                + jnp.where(cand <= NEG_KEY, n_unscanned, 0.0))

    c0 = count_ge_coarse(jnp.zeros((1, TQ), jnp.int32))
    t0 = jnp.where(c0 >= kf, 0, INT_MIN).astype(jnp.int32)
    ct0 = jnp.where(c0 >= kf, c0, float(seq))

    def bit_step(b, t, ct, counter):
        cand = t | jnp.left_shift(jnp.int32(1), 30 - b)
        cc = counter(cand)
        ok = cc >= kf
        return jnp.where(ok, cand, t), jnp.where(ok, cc, ct)

    t, ct = lax.fori_loop(0, BISECT_COARSE, lambda b, c: bit_step(b, *c, count_ge_coarse), (t0, ct0))
    t, ct = lax.fori_loop(BISECT_COARSE, BISECT_COARSE + BISECT_FIXED,
                          lambda b, c: bit_step(b, *c, count_ge), (t, ct))

    zero_final = (c0 >= kf) & (count_ge_coarse(jnp.full((1, TQ), MIN_NORMAL_BITS, jnp.int32)) < kf)

    def pending(ct):
        return jnp.max(jnp.where((ct > kf) & jnp.logical_not(zero_final), 1.0, 0.0))

    def bit_body(c):
        t, ct = bit_step(c[0], c[2], c[3], count_ge)
        t, ct = bit_step(c[0] + 1, t, ct, count_ge)
        return c[0] + 2, pending(ct), t, ct

    assert (31 - BISECT_COARSE - BISECT_FIXED) % 2 == 0
    _, _, t, c_ge = lax.while_loop(lambda c: (c[0] < 31) & (c[1] > 0.5), bit_body,
                                   (jnp.int32(BISECT_COARSE + BISECT_FIXED), pending(ct), t, ct))

    def tie_limit():
        need = kf - count_ge(t + 1)

        def block_ties(j, c):
            hit = key_ref[j] == t
            acc = jnp.zeros((fold, TQ), F32)
            for r in range(TK // fold):
                acc = jnp.where(hit[r * fold:(r + 1) * fold, :], acc + 1.0, acc)
            cnt_ref[pl.ds(j, 1), :] = jnp.sum(acc, axis=0, keepdims=True)
            return c

        lax.fori_loop(0, i + 1, block_ties, 0)
        nkb = cnt_ref.shape[0]
        cnt = jnp.where(lax.broadcasted_iota(jnp.int32, (nkb, TQ), 0) <= i, cnt_ref[...], 0.0)
        run = jnp.zeros((1, TQ), F32)
        blk_of = jnp.zeros((1, TQ), F32)
        before = jnp.zeros((1, TQ), F32)
        for j in range(nkb):
            run = run + cnt[j:j + 1, :]
            ahead = run < need
            blk_of = blk_of + jnp.where(ahead, 1.0, 0.0)
            before = jnp.where(ahead, run, before)
        blk_of = blk_of.astype(jnp.int32)

        def pick_block(j, c):
            m_ref[...] = jnp.where((blk_of == j) & (key_ref[j] == t), 1.0, m_ref[...])
            return c

        m_ref[...] = jnp.zeros_like(m_ref)
        lax.fori_loop(0, i + 1, pick_block, 0)
        tri = (lax.broadcasted_iota(jnp.int32, (TK, TK), 0) >= lax.broadcasted_iota(jnp.int32, (TK, TK), 1))
        upto = _dot(jnp.where(tri, 1.0, 0.0).astype(BF16), m_ref[...].astype(BF16))
        row_of = jnp.sum(jnp.where(upto < need - before, 1.0, 0.0), axis=0, keepdims=True)
        return blk_of * TK + row_of.astype(jnp.int32)

    x = lax.cond(jnp.max(c_ge) > kf, tie_limit, lambda: jnp.full((1, TQ), 2 * seq, jnp.int32))

    heads = range(nh)
    grp = range(DSA_GROUP)
    pairs = [slice((h // 2) * LANES, (h // 2 + 1) * LANES) for h in heads]
    n_full = i // DSA_GROUP

    def scores(jg):
        off = pl.multiple_of(jg * (DSA_GROUP * TK), DSA_GROUP * TK)
        return tuple(_dot(k_ref[pl.ds(off, DSA_GROUP * TK), pairs[h]], qs_ref[h]) for h in heads)

    def mask_biases(base, causal):
        biases = []
        for g in grp:
            key = key_ref[base + g]
            pos = (base + g) * TK + krow
            msk = (key > t) | ((key == t) & (pos <= x))
            if causal:
                msk = msk & (pos <= i * TQ + qcol)
            biases.append(jnp.where(msk, 0.0, NEG))
        return biases

    def attend_fixed(jg, _, causal):
        ss = scores(jg)
        ss_next = 0
        base = jg * DSA_GROUP
        biases = mask_biases(base, causal)
        for h in heads:
            pv = None
            for g in grp:
                p = jnp.exp((ss[h][g * TK:(g + 1) * TK, :] - m_fix[h]) + biases[g]).astype(BF16)
                d = _dot(vt_ref[0, base + g, h * LANES:(h + 1) * LANES, :], p)
                pv = d if pv is None else pv + d
            acc_ref[h] += pv
        return ss_next

    def attend(jg, carry, causal):
        ms, ss = carry
        ss_next = None if causal else scores(jg + 1)
        base = jg * DSA_GROUP
        biases = mask_biases(base, causal)
        sg = [[ss[h][g * TK:(g + 1) * TK, :] for g in grp] for h in heads]
        new = []
        for h in heads:
            m_new = ms[h]
            for g in grp:
                m_new = jnp.maximum(m_new, jnp.max(sg[h][g] + biases[g], axis=0, keepdims=True))
            new.append(m_new)
        ps = [[jnp.exp((sg[h][g] - new[h]) + biases[g]).astype(BF16) for g in grp] for h in heads]
        for h in heads:
            pv = None
            for g in grp:
                d = _dot(vt_ref[0, base + g, h * LANES:(h + 1) * LANES, :], ps[h][g])
                pv = d if pv is None else pv + d
            acc_ref[h] = jnp.exp(ms[h] - new[h]) * acc_ref[h] + pv
        return tuple(new), ss_next

    for h in heads:
        acc_ref[h] = jnp.zeros((LANES, TQ), F32)

    def run_online():
        carry = lax.fori_loop(0, n_full, lambda jg, c: attend(jg, c, False),
                              (tuple(jnp.full((1, TQ), NEG, F32) for _ in heads), scores(0)))
        attend(n_full, carry, True)
        return 0

    def run_fixed():
        lax.fori_loop(0, n_full, lambda jg, c: attend_fixed(jg, c, False), 0)
        attend_fixed(n_full, 0, True)
        return 0

    lax.cond(bound_max <= SAFE_SHIFT, run_fixed, run_online)

    out_t = jnp.concatenate([_finish_softmax_head(acc_ref[h]) for h in range(nh)], axis=0)
    o_ref[...] = out_t.T


def _dsa_call(f_arr, h_arr, vt, batch, seq):
    nq = seq // TQ
    wb = N_HEADS_B * HEAD_DIM
    return pl.pallas_call(
        _dsa_kernel,
        grid=(batch, nq),
        in_specs=[pl.BlockSpec((TQ, IDX_HEADS * IDX_DIM), lambda b, i: (b * nq + i, H_QI // (IDX_HEADS * IDX_DIM))),
                  pl.BlockSpec((TQ, LANES), lambda b, i: (b * nq + i, F_WI // LANES)),
                  pl.BlockSpec((seq, LANES), lambda b, i: (b, H_KI // LANES)),
                  pl.BlockSpec((TQ, wb), lambda b, i: (b * nq + i, F_QB // wb)),
                  pl.BlockSpec((seq, wb), lambda b, i: (b, H_KB // wb)),
                  pl.BlockSpec((1, seq // TK, N_HEADS_B * LANES, TK), lambda b, i: (b, 0, VT_B // (N_HEADS_B * LANES), 0))],
        out_specs=pl.BlockSpec((TQ, wb), lambda b, i: (b * nq + i, 0)),
        out_shape=jax.ShapeDtypeStruct((batch * seq, wb), F32),
        scratch_shapes=[pltpu.VMEM((seq // TK, TK, TQ), jnp.int32),
                        pltpu.VMEM((seq // TK, TK, TQ), BF16),
                        pltpu.VMEM((IDX_HEADS, LANES, TQ), BF16),
                        pltpu.VMEM((N_HEADS_B, LANES, TQ), BF16),
                        pltpu.VMEM((N_HEADS_B, LANES, TQ), F32),
                        pltpu.VMEM((seq // TK, TQ), F32),
                        pltpu.VMEM((TK, TQ), F32),
                        pltpu.VMEM((1, wb), F32)],
        compiler_params=_cparams(("parallel", "arbitrary")),
        name="dsa_attn",
    )(h_arr, f_arr, h_arr, f_arr, h_arr, vt)


def _outproj_kernel(x_ref, oa_ref, ob_ref, oc_ref, og_ref, gt_ref, w_ref, o_ref):
    wa = N_HEADS_A * HEAD_DIM
    wb = wa + N_HEADS_B * HEAD_DIM
    og = og_ref[...]
    y = (_dot((oa_ref[...] * og[:, :wa]).astype(BF16), w_ref[:wa, :])
         + _dot((ob_ref[...] * og[:, wa:wb]).astype(BF16), w_ref[wa:wb, :])
         + _dot((oc_ref[...] * og[:, wb:]).astype(BF16), w_ref[wb:, :]))
    o_ref[...] = x_ref[...] + gt_ref[0] * y


def _outproj_call(x2, oa, ob, oc, og, gt, w_out, seq):
    m, d = x2.shape
    tm = 512
    row = lambda i: (i, 0)
    return pl.pallas_call(
        _outproj_kernel,
        grid=(m // tm,),
        in_specs=[pl.BlockSpec((tm, d), row),
                  pl.BlockSpec((tm, oa.shape[1]), row),
                  pl.BlockSpec((tm, ob.shape[1]), row),
                  pl.BlockSpec((tm, oc.shape[1]), row),
                  pl.BlockSpec((1, d), lambda i: (0, 0)),
                  pl.BlockSpec((1, 1, d), lambda i: ((i * tm) // seq, 0, 0)),
                  pl.BlockSpec((d, d), lambda i: (0, 0))],
        out_specs=pl.BlockSpec((tm, d), row),
        out_shape=jax.ShapeDtypeStruct((m, d), F32),
        compiler_params=_cparams(("parallel",)),
        name="out_proj",
    )(x2, oa, ob, oc, og, gt, w_out)


def _rope_tables(seq):
    pos = jnp.arange(seq, dtype=F32)
    inv = ROPE_THETA ** (-jnp.arange(0, ROPE_DIM, 2, dtype=F32) / ROPE_DIM)
    ang = pos[:, None] * inv[None, :]
    cos, sin = jnp.cos(ang), jnp.sin(ang)
    zeros = jnp.zeros((seq, HEAD_DIM - ROPE_DIM), F32)
    zh = jnp.zeros((seq, ROPE_HALF), F32)
    cos_h = jnp.concatenate([cos, cos, jnp.ones_like(zeros)], axis=1)
    s1_h = jnp.concatenate([-sin, zh, zeros], axis=1)
    s2_h = jnp.concatenate([zh, sin, zeros], axis=1)
    two = lambda t: jnp.concatenate([t, t], axis=1)
    return two(cos_h), two(s1_h), two(s2_h)


def _permute_w_in(w):
    d = w.shape[0]
    a, b, c = N_HEADS_A * HEAD_DIM, N_HEADS_B * HEAD_DIM, N_HEADS_C * HEAD_DIM
    sizes = (a, a, a, b, b, b, IDX_HEADS * IDX_DIM, IDX_DIM, IDX_HEADS, c, c, c)
    offs = np.concatenate([[0], np.cumsum(sizes)])
    qa, ka, va, qb, kb, vb, qi, ki, wi, qc, kc, vc = [w[:, int(offs[k]):int(offs[k + 1])] for k in range(12)]
    z = lambda n: jnp.zeros((d, n), w.dtype)
    out = jnp.concatenate([qa, qb, qc, ka, va, kb, vb, kc, vc, qi,
                           ki, z(LANES - IDX_DIM), wi, z(LANES - IDX_HEADS)], axis=1)
    assert out.shape[1] == W_PERM_WIDTH
    return out.astype(BF16)


def _suffix_sum_matrix():
    s = np.arange(TK + SB_TAIL)[:, None]
    j = np.arange(TK)[None, :]
    return jnp.asarray((j > s) | (s >= TK), BF16)


def kernel(x, c, w_ada, b_ada, norm_g, w_in, qk_g, out_g, w_out, ffn_w1, ffn_w3, ffn_w2):
    batch, seq, d = x.shape
    depth = w_ada.shape[0]
    assert seq % PROJ_TM == 0 and d % LANES == 0 and batch <= 8
    assert (seq // TK) % MOBA_GROUP == 0 and (seq // TK) % DSA_GROUP == 0

    c_pad = jnp.zeros((8, d), F32).at[:batch].set(c)
    mod = _mod_call(c_pad, w_ada, b_ada)
    cos_t, s1_t, s2_t = _rope_tables(seq)
    hd = np.arange(LANES) // HEAD_DIM
    bd = jnp.asarray(hd[:, None] == hd[None, :], BF16)
    u_mat = _suffix_sum_matrix()

    x2 = x.reshape(batch * seq, d)
    for layer in range(depth):
        mods = [mod[layer, :batch, k * d:(k + 1) * d].reshape(batch, 1, d) for k in range(N_MOD)]
        sh1, sc1, g1, sh2, sc2, g2, sh3, sc3, g3 = mods
        ng = norm_g[layer]
        w1 = ffn_w1[layer].astype(BF16)
        w3 = ffn_w3[layer].astype(BF16)
        w2 = ffn_w2[layer].astype(BF16)

        x2 = _ffn_call(x2, ng[0:1], sh1, sc1, g1, w1[0], w3[0], w2[0], seq)

        qkg = jnp.concatenate([jnp.tile(qk_g[layer], (1, 2)), jnp.ones((4, LANES), F32)], axis=0)
        f_arr, h_arr, kmean, vt = _proj_call(x2, ng[1:2], sh2, sc2, _permute_w_in(w_in[layer]), qkg,
                                             cos_t, s1_t, s2_t, bd, batch, seq)
        per_tile = PROJ_TM // MOBA_BLOCK
        kmean = kmean[:, :per_tile, :].reshape(batch, seq // MOBA_BLOCK, N_HEADS_A * HEAD_DIM)
        oa = _moba_call(f_arr, h_arr, kmean, vt, batch, seq)
        ob = _dsa_call(f_arr, h_arr, vt, batch, seq)
        oc = _sb_call(f_arr, h_arr, vt, u_mat, batch, seq)
        x2 = _outproj_call(x2, oa, ob, oc, out_g[layer].reshape(1, d), g2, w_out[layer].astype(BF16), seq)

        x2 = _ffn_call(x2, ng[2:3], sh3, sc3, g3, w1[1], w3[1], w2[1], seq)
    return x2.reshape(batch, seq, d)
```

```python
import jax
import jax.numpy as jnp
import numpy as np
from jax import lax
from jax.experimental import pallas as pl
from jax.experimental.pallas import tpu as pltpu

F32 = jnp.float32
BF16 = jnp.bfloat16

HEAD_DIM = 64
N_HEADS_A = 4
N_HEADS_B = 4
N_HEADS_C = 8
ROPE_DIM = HEAD_DIM // 4
ROPE_HALF = ROPE_DIM // 2
ROPE_THETA = 500000.0
MOBA_BLOCK = 256
MOBA_TOPK = 3
DSA_TOPK = 256
IDX_HEADS = 8
IDX_DIM = 64
N_MOD = 9
RMS_EPS = 1e-6
NEG = -1e30
ATT_SCALE = HEAD_DIM ** -0.5

LANES = 128
MXU_N = 256
TQ = 256
TK = 256
PROJ_TM = 512
VMEM_LIMIT = 56 * 1024 * 1024

SB_DEAD_LOG = -88.0
SB_TAIL = 16
SB_HEADS = 8
MOBA_GROUP = 4
DSA_GROUP = 2
BOUND_SLACK = 1.02
SAFE_SHIFT = 40.0
BISECT_COARSE = 15
BISECT_FIXED = 6

F_QA, F_QB, F_QC, F_WI, F_WIDTH = 0, 256, 512, 1024, 1152
H_KA, H_KB, H_KC, H_QI, H_KI, H_WIDTH = 0, 256, 512, 1024, 1536, 1664
W_PERM_WIDTH = 15 * MXU_N
VT_A, VT_B, VT_C, VT_ROWS = 0, 512, 1024, 1536


def _dot(a, b):
    return jnp.dot(a, b, preferred_element_type=F32)


def _dot_nt(a, b):
    return lax.dot_general(a, b, (((1,), (1,)), ((), ())), preferred_element_type=F32)


def _split3(x):
    a = x.astype(BF16)
    r = x - a.astype(F32)
    b = r.astype(BF16)
    c = (r - b.astype(F32)).astype(BF16)
    return a, b, c


def _cparams(sem):
    return pltpu.CompilerParams(dimension_semantics=sem, vmem_limit_bytes=VMEM_LIMIT)


def _mod_kernel(c_ref, w_ref, b_ref, o_ref):
    c = c_ref[...]
    sc = c * (1.0 / (1.0 + jnp.exp(-c)))
    a, b, c3 = _split3(sc)
    w = w_ref[0]
    wa, wb, wc = _split3(w)
    acc = _dot(a, wa) + (_dot(a, wb) + _dot(b, wa)) + (_dot(a, wc) + _dot(b, wb) + _dot(c3, wa))
    o_ref[0] = acc + b_ref[0]


def _mod_call(c_pad, w_ada, b_ada):
    depth, d, n = w_ada.shape
    tn = 1024
    rows = c_pad.shape[0]
    return pl.pallas_call(
        _mod_kernel,
        grid=(depth, n // tn),
        in_specs=[pl.BlockSpec((rows, d), lambda l, j: (0, 0)),
                  pl.BlockSpec((1, d, tn), lambda l, j: (l, 0, j)),
                  pl.BlockSpec((1, 1, tn), lambda l, j: (l, 0, j))],
        out_specs=pl.BlockSpec((1, rows, tn), lambda l, j: (l, 0, j)),
        out_shape=jax.ShapeDtypeStruct((depth, rows, n), F32),
        compiler_params=_cparams(("parallel", "parallel")),
        name="adaln_mod",
    )(c_pad, w_ada, b_ada.reshape(depth, 1, n))


def _norm_modulate(x, ng, sh, sc):
    ms = jnp.mean(x * x, axis=-1, keepdims=True)
    h = x * lax.rsqrt(ms + RMS_EPS) * ng
    return h * (1.0 + sc) + sh


def _ffn_kernel(x_ref, ng_ref, sh_ref, sc_ref, gt_ref, w1_ref, w3_ref, w2_ref, o_ref, h_ref, acc_ref):
    j = pl.program_id(1)

    @pl.when(j == 0)
    def _():
        h = _norm_modulate(x_ref[...], ng_ref[...], sh_ref[0], sc_ref[0])
        h_ref[...] = h.astype(BF16)
        acc_ref[...] = jnp.zeros_like(acc_ref)

    h = h_ref[...]
    a = _dot(h, w1_ref[...])
    b = _dot(h, w3_ref[...])
    u = (a * (1.0 / (1.0 + jnp.exp(-a))) * b).astype(BF16)
    acc_ref[...] += _dot(u, w2_ref[...])

    @pl.when(j == pl.num_programs(1) - 1)
    def _():
        o_ref[...] = x_ref[...] + 0.5 * gt_ref[0] * acc_ref[...]


def _ffn_call(x2, ng, sh, sc, gt, w1, w3, w2, seq):
    m, d = x2.shape
    dff = w1.shape[1]
    tm = 1024
    tf = dff // 2 if (dff // 2) % LANES == 0 else MXU_N
    bidx = lambda i, j: ((i * tm) // seq, 0, 0)
    return pl.pallas_call(
        _ffn_kernel,
        grid=(m // tm, dff // tf),
        in_specs=[pl.BlockSpec((tm, d), lambda i, j: (i, 0)),
                  pl.BlockSpec((1, d), lambda i, j: (0, 0)),
                  pl.BlockSpec((1, 1, d), bidx),
                  pl.BlockSpec((1, 1, d), bidx),
                  pl.BlockSpec((1, 1, d), bidx),
                  pl.BlockSpec((d, tf), lambda i, j: (0, j)),
                  pl.BlockSpec((d, tf), lambda i, j: (0, j)),
                  pl.BlockSpec((tf, d), lambda i, j: (j, 0))],
        out_specs=pl.BlockSpec((tm, d), lambda i, j: (i, 0)),
        out_shape=jax.ShapeDtypeStruct((m, d), F32),
        scratch_shapes=[pltpu.VMEM((tm, d), BF16), pltpu.VMEM((tm, d), F32)],
        compiler_params=_cparams(("parallel", "arbitrary")),
        name="swiglu_ffn",
    )(x2, ng, sh, sc, gt, w1, w3, w2)


def _proj_kernel(x_ref, ng_ref, sh_ref, sc_ref, w_ref, qkg_ref, cos_ref, s1_ref, s2_ref, bd_ref,
                 f_ref, h_ref, km_ref, vt_ref, hs_ref):
    hs_ref[...] = _norm_modulate(x_ref[...], ng_ref[...], sh_ref[0], sc_ref[0]).astype(BF16)
    cos, s1, s2 = cos_ref[...], s1_ref[...], s2_ref[...]
    bd = bd_ref[...]
    tm = x_ref.shape[0]

    def rope(v):
        return v * cos + pltpu.roll(v, LANES - ROPE_HALF, 1) * s1 + pltpu.roll(v, ROPE_HALF, 1) * s2

    def headnorm(v, g):
        a, b, _ = _split3(v * v)
        ss = _dot(a, bd) + _dot(b, bd)
        return v * lax.rsqrt(ss * (1.0 / HEAD_DIM) + RMS_EPS) * g

    def chunk(c):
        y = _dot(hs_ref[...], w_ref[:, c * MXU_N:(c + 1) * MXU_N])
        return y[:, :LANES], y[:, LANES:]

    def normrope_chunk(c, grow):
        g = qkg_ref[grow:grow + 1, :]
        return [rope(headnorm(v, g)) for v in chunk(c)]

    def put(ref, off, halves, dtype):
        for k, v in enumerate(halves):
            ref[:, off + k * LANES: off + (k + 1) * LANES] = v.astype(dtype)

    def put_vt(base, halves, with_ones):
        ones = jnp.ones((HEAD_DIM, TK), BF16)
        for r in range(tm // TK):
            for k, v in enumerate(halves):
                t = v[r * TK:(r + 1) * TK, :].T.astype(BF16)
                if with_ones:
                    for hh in range(2):
                        row = base + (2 * k + hh) * LANES
                        vt_ref[0, r, row:row + HEAD_DIM, :] = t[hh * HEAD_DIM:(hh + 1) * HEAD_DIM, :]
                        vt_ref[0, r, row + HEAD_DIM:row + LANES, :] = ones
                else:
                    vt_ref[0, r, base + k * LANES:base + (k + 1) * LANES, :] = t

    put(f_ref, F_QA, normrope_chunk(0, 0), F32)
    put(f_ref, F_QB, normrope_chunk(1, 2), F32)
    put(f_ref, F_QC, chunk(2), F32)
    put(f_ref, F_QC + MXU_N, chunk(3), F32)

    ka = normrope_chunk(4, 1)
    put(h_ref, H_KA, ka, BF16)
    rows = lax.broadcasted_iota(jnp.int32, (8, LANES), 0)
    for k, v in enumerate(ka):
        km = jnp.zeros((8, LANES), F32)
        for r in range(tm // MOBA_BLOCK):
            s = jnp.sum(v[r * MOBA_BLOCK:(r + 1) * MOBA_BLOCK, :], axis=0, keepdims=True) * (1.0 / MOBA_BLOCK)
            km = jnp.where(rows == r, s, km)
        km_ref[0, :, k * LANES:(k + 1) * LANES] = km

    put_vt(VT_A, chunk(5), True)
    put(h_ref, H_KB, normrope_chunk(6, 3), BF16)
    put_vt(VT_B, chunk(7), True)
    put(h_ref, H_KC, chunk(8), BF16)
    put(h_ref, H_KC + MXU_N, chunk(9), BF16)
    put_vt(VT_C, chunk(10), False)
    put_vt(VT_C + MXU_N, chunk(11), False)
    put(h_ref, H_QI, [rope(v) for v in chunk(12)], BF16)
    put(h_ref, H_QI + MXU_N, [rope(v) for v in chunk(13)], BF16)
    ki, wi = chunk(14)
    h_ref[:, H_KI:H_KI + LANES] = rope(ki).astype(BF16)
    f_ref[:, F_WI:F_WI + LANES] = wi


def _proj_call(x2, ng, sh, sc, w_perm, qkg, cos_t, s1_t, s2_t, bd, batch, seq):
    m, d = x2.shape
    tm = PROJ_TM
    nt = seq // tm
    bidx = lambda i: ((i * tm) // seq, 0, 0)
    tab = pl.BlockSpec((tm, LANES), lambda i: (i % nt, 0))
    return pl.pallas_call(
        _proj_kernel,
        grid=(m // tm,),
        in_specs=[pl.BlockSpec((tm, d), lambda i: (i, 0)),
                  pl.BlockSpec((1, d), lambda i: (0, 0)),
                  pl.BlockSpec((1, 1, d), bidx),
                  pl.BlockSpec((1, 1, d), bidx),
                  pl.BlockSpec((d, W_PERM_WIDTH), lambda i: (0, 0)),
                  pl.BlockSpec((8, LANES), lambda i: (0, 0)),
                  tab, tab, tab,
                  pl.BlockSpec((LANES, LANES), lambda i: (0, 0))],
        out_specs=[pl.BlockSpec((tm, F_WIDTH), lambda i: (i, 0)),
                   pl.BlockSpec((tm, H_WIDTH), lambda i: (i, 0)),
                   pl.BlockSpec((1, 8, 2 * LANES), lambda i: (i, 0, 0)),
                   pl.BlockSpec((1, tm // TK, VT_ROWS, TK), lambda i: (i // nt, i % nt, 0, 0))],
        out_shape=[jax.ShapeDtypeStruct((m, F_WIDTH), F32),
                   jax.ShapeDtypeStruct((m, H_WIDTH), BF16),
                   jax.ShapeDtypeStruct((m // tm, 8, 2 * LANES), F32),
                   jax.ShapeDtypeStruct((batch, seq // TK, VT_ROWS, TK), BF16)],
        scratch_shapes=[pltpu.VMEM((tm, d), BF16)],
        compiler_params=_cparams(("parallel",)),
        name="in_proj",
    )(x2, ng, sh, sc, w_perm, qkg, cos_t, s1_t, s2_t, bd)


def _head_masks():
    lane = lax.broadcasted_iota(jnp.int32, (1, LANES), 1)
    return [(lane >= HEAD_DIM * h) & (lane < HEAD_DIM * (h + 1)) for h in range(2)]


def _head_rms_t(o):
    ss = jnp.sum(o * o, axis=0, keepdims=True) * (1.0 / HEAD_DIM)
    return o * lax.rsqrt(ss + RMS_EPS)


def _finish_softmax_head(acc):
    return _head_rms_t(acc[:HEAD_DIM, :] / acc[HEAD_DIM:HEAD_DIM + 1, :])


def _moba_kernel(q_ref, km_ref, k_ref, vt_ref, o_ref, bias_ref, qs_ref, acc_ref, kmax2_ref):
    i = pl.program_id(2)
    nb = km_ref.shape[1]
    q = q_ref[...]
    kma, kmb, _ = _split3(km_ref[0])
    blk = lax.broadcasted_iota(jnp.int32, (nb, TQ), 0).astype(F32)
    i_f = i.astype(F32)
    krow = lax.broadcasted_iota(jnp.int32, (TK, TQ), 0)
    qcol = lax.broadcasted_iota(jnp.int32, (TK, TQ), 1)
    causal = krow <= qcol

    q_norm2 = []
    for h, hm in enumerate(_head_masks()):
        qh_t = jnp.where(hm, q, 0.0).T
        qa, qb, _ = _split3(qh_t)
        gate = _dot(kma, qa) + (_dot(kmb, qa) + _dot(kma, qb))
        gate = jnp.where(blk < i_f, gate, NEG)
        sel = jnp.zeros((nb, TQ), F32)
        for _ in range(MOBA_TOPK):
            mx = jnp.max(gate, axis=0, keepdims=True)
            idx = jnp.min(jnp.where(gate == mx, blk, float(nb)), axis=0, keepdims=True)
            pick = blk == idx
            sel = jnp.where(pick, 1.0, sel)
            gate = jnp.where(pick, -jnp.inf, gate)
        bias_ref[h] = jnp.where((sel > 0.5) & (blk < i_f), 0.0, NEG)
        qs_t = qh_t * ATT_SCALE
        qs_ref[h] = qs_t.astype(BF16)
        q_norm2.append(jnp.sum(qs_t * qs_t, axis=0, keepdims=True))

    @pl.when(i == 0)
    def _():
        head_of = lambda a: lax.shift_right_logical(lax.broadcasted_iota(jnp.int32, (LANES, LANES), a), 6)
        ones_bd = jnp.where(head_of(0) == head_of(1), 1.0, 0.0).astype(BF16)

        def norm_body(j, mx):
            kk = k_ref[pl.ds(pl.multiple_of(j * TK, TK), TK), :].astype(F32)
            return jnp.maximum(mx, jnp.max(_dot((kk * kk).astype(BF16), ones_bd), axis=0, keepdims=True))
        kmax2_ref[...] = lax.fori_loop(0, nb, norm_body, jnp.zeros((1, LANES), F32))

    m_fix = [jnp.sqrt(q_norm2[h] * kmax2_ref[0:1, h * HEAD_DIM:h * HEAD_DIM + 1] + 1e-30) * BOUND_SLACK
             for h in range(2)]

    own = pl.multiple_of(i * TK, TK)
    heads = range(2)
    grp = range(MOBA_GROUP)
    n_groups = (i + MOBA_GROUP - 1) // MOBA_GROUP
    last_group = km_ref.shape[1] // MOBA_GROUP - 1

    def scores(jg):
        off = pl.multiple_of(jnp.minimum(jg, last_group) * (MOBA_GROUP * TK), MOBA_GROUP * TK)
        kb = k_ref[pl.ds(off, MOBA_GROUP * TK), :]
        return tuple(_dot(kb, qs_ref[h]) for h in heads)

    def own_block(refs):
        k_own = k_ref[pl.ds(own, TK), :]
        used = []
        for h in heads:
            s = jnp.where(causal, _dot(k_own, qs_ref[h]), NEG)
            m0 = jnp.max(s, axis=0, keepdims=True) if refs is None else refs[h]
            acc_ref[h] = _dot(vt_ref[0, i, h * LANES:(h + 1) * LANES, :], jnp.exp(s - m0).astype(BF16))
            used.append(m0)
        return tuple(used)

    def body_fixed(jg, c):
        ss = scores(jg)
        base = jg * MOBA_GROUP
        for h in heads:
            pv = None
            for g in grp:
                b = bias_ref[h, pl.ds(base + g, 1), :]
                p = jnp.exp(ss[h][g * TK:(g + 1) * TK, :] - (m_fix[h] - b)).astype(BF16)
                d = _dot(vt_ref[0, base + g, h * LANES:(h + 1) * LANES, :], p)
                pv = d if pv is None else pv + d
            acc_ref[h] += pv
        return c

    def body(jg, carry):
        ms, ss = carry
        ss_next = scores(jg + 1)
        base = jg * MOBA_GROUP
        sg = [[ss[h][g * TK:(g + 1) * TK, :] for g in grp] for h in heads]
        bs = [[bias_ref[h, pl.ds(base + g, 1), :] for g in grp] for h in heads]
        new = []
        for h in heads:
            m_new = ms[h]
            for g in grp:
                m_new = jnp.maximum(m_new, jnp.max(sg[h][g], axis=0, keepdims=True) + bs[h][g])
            new.append(m_new)
        ps = [[jnp.exp(sg[h][g] - (new[h] - bs[h][g])).astype(BF16) for g in grp] for h in heads]
        for h in heads:
            pv = None
            for g in grp:
                d = _dot(vt_ref[0, base + g, h * LANES:(h + 1) * LANES, :], ps[h][g])
                pv = d if pv is None else pv + d
            acc_ref[h] = jnp.exp(ms[h] - new[h]) * acc_ref[h] + pv
        return tuple(new), ss_next

    def run_online():
        lax.fori_loop(0, n_groups, body, (own_block(None), scores(0)))
        return 0

    def run_fixed():
        own_block(m_fix)
        lax.fori_loop(0, n_groups, body_fixed, 0)
        return 0

    lax.cond(jnp.max(jnp.maximum(m_fix[0], m_fix[1])) <= SAFE_SHIFT, run_fixed, run_online)
    out_t = jnp.concatenate([_finish_softmax_head(acc_ref[h]) for h in range(2)], axis=0)
    o_ref[...] = out_t.T


def _moba_call(f_arr, h_arr, kmean, vt, batch, seq):
    nq = seq // TQ
    nb = seq // MOBA_BLOCK
    npair = N_HEADS_A // 2
    return pl.pallas_call(
        _moba_kernel,
        grid=(batch, npair, nq),
        in_specs=[pl.BlockSpec((TQ, LANES), lambda b, p, i: (b * nq + i, F_QA // LANES + p)),
                  pl.BlockSpec((1, nb, LANES), lambda b, p, i: (b, 0, p)),
                  pl.BlockSpec((seq, LANES), lambda b, p, i: (b, H_KA // LANES + p)),
                  pl.BlockSpec((1, seq // TK, 2 * LANES, TK), lambda b, p, i: (b, 0, VT_A // (2 * LANES) + p, 0))],
        out_specs=pl.BlockSpec((TQ, LANES), lambda b, p, i: (b * nq + i, p)),
        out_shape=jax.ShapeDtypeStruct((batch * seq, N_HEADS_A * HEAD_DIM), F32),
        scratch_shapes=[pltpu.VMEM((2, nb, TQ), F32),
                        pltpu.VMEM((2, LANES, TQ), BF16),
                        pltpu.VMEM((2, LANES, TQ), F32),
                        pltpu.VMEM((1, LANES), F32)],
        compiler_params=_cparams(("parallel", "parallel", "arbitrary")),
        name="moba_attn",
    )(f_arr, kmean, h_arr, vt)


def _sb_kernel(q_ref, k_ref, vt_ref, u_ref, o_ref, qs_ref, acc_ref):
    i = pl.program_id(2)
    krow = lax.broadcasted_iota(jnp.int32, (TK, TQ), 0)
    qcol = lax.broadcasted_iota(jnp.int32, (TK, TQ), 1)
    strict = krow < qcol
    hms = _head_masks()
    for h in range(SB_HEADS):
        pr = slice((h // 2) * LANES, (h // 2 + 1) * LANES)
        qs_ref[h] = (jnp.where(hms[h % 2], q_ref[:, pr], 0.0) * ATT_SCALE).T.astype(BF16)
        acc_ref[h] = jnp.zeros((LANES, TQ), F32)

    heads = range(SB_HEADS)
    pairs = [slice((h // 2) * LANES, (h // 2 + 1) * LANES) for h in heads]

    def logits(j):
        off = pl.multiple_of(jnp.maximum(j, 0) * TK, TK)
        return tuple(_dot(k_ref[pl.ds(off, TK), pairs[h]], qs_ref[h]) for h in heads)

    def block(j, rs, diag):
        zs = logits(j)
        u = u_ref[...]
        log_betas, his, los = [], [], []
        for h in heads:
            z = zs[h]
            log_beta = jnp.minimum(z, 0.0) - jnp.log(1.0 + jnp.exp(-jnp.abs(z)))
            log_1m = log_beta - z
            if diag:
                log_1m = jnp.where(strict, log_1m, 0.0)
            hi = log_1m.astype(BF16)
            log_betas.append(log_beta)
            his.append(hi)
            los.append((log_1m - hi.astype(F32)).astype(BF16))
        sums = [_dot(u, his[h]) + _dot(u, los[h]) for h in heads]
        weights = []
        for h in heads:
            after = sums[h][:TK, :] + rs[h]
            a = jnp.exp(log_betas[h] + after)
            if diag:
                a = jnp.where(strict, a, 0.0)
            weights.append(a.astype(BF16))
        for h in heads:
            acc_ref[h] += _dot(vt_ref[0, j, pairs[h], :], weights[h])
        return tuple(rs[h] + sums[h][TK:TK + 1, :] for h in heads)

    zero = jnp.zeros((1, TQ), F32)
    rs = block(i, (zero,) * SB_HEADS, True)

    def alive(rs):
        m = rs[0]
        for r in rs[1:]:
            m = jnp.maximum(m, r)
        return jnp.max(m)

    def cond(c):
        return (c[0] >= 0) & (c[1] > SB_DEAD_LOG)

    def body(c):
        rs = block(c[0], c[2], False)
        return c[0] - 1, alive(rs), rs

    lax.while_loop(cond, body, (i - 1, alive(rs), rs))
    out_t = jnp.concatenate(
        [_head_rms_t(acc_ref[h][(h % 2) * HEAD_DIM:(h % 2 + 1) * HEAD_DIM, :]) for h in range(SB_HEADS)], axis=0)
    o_ref[...] = out_t.T


def _sb_call(f_arr, h_arr, vt, u_mat, batch, seq):
    nq = seq // TQ
    w = SB_HEADS * HEAD_DIM
    ngrp = N_HEADS_C // SB_HEADS
    return pl.pallas_call(
        _sb_kernel,
        grid=(batch, ngrp, nq),
        in_specs=[pl.BlockSpec((TQ, w), lambda b, p, i: (b * nq + i, F_QC // w + p)),
                  pl.BlockSpec((seq, w), lambda b, p, i: (b, H_KC // w + p)),
                  pl.BlockSpec((1, seq // TK, w, TK), lambda b, p, i: (b, 0, VT_C // w + p, 0)),
                  pl.BlockSpec((TK + SB_TAIL, TK), lambda b, p, i: (0, 0))],
        out_specs=pl.BlockSpec((TQ, w), lambda b, p, i: (b * nq + i, p)),
        out_shape=jax.ShapeDtypeStruct((batch * seq, N_HEADS_C * HEAD_DIM), F32),
        scratch_shapes=[pltpu.VMEM((SB_HEADS, LANES, TQ), BF16), pltpu.VMEM((SB_HEADS, LANES, TQ), F32)],
        compiler_params=_cparams(("parallel", "parallel", "arbitrary")),
        name="stickbreak_attn",
    )(f_arr, h_arr, vt, u_mat)


def _sortable_key(x):
    bits = lax.bitcast_convert_type(x, jnp.int32)
    return jnp.where(bits < 0, bits ^ jnp.int32(0x7FFFFFFF), bits)


_NEG_BITS = int(np.array(NEG, np.float32).view(np.int32))
NEG_KEY = _NEG_BITS ^ 0x7FFFFFFF
INT_MIN = -2 ** 31
MIN_NORMAL_BITS = 0x00800000


def _dsa_kernel(qi_ref, wi_ref, ki_ref, q_ref, k_ref, vt_ref, o_ref,
                key_ref, coarse_ref, qx_ref, qs_ref, acc_ref, cnt_ref, m_ref, kmax2_ref):
    i = pl.program_id(1)
    seq = k_ref.shape[0]
    topk = min(DSA_TOPK, seq // 4)
    nh = N_HEADS_B
    lane = lax.broadcasted_iota(jnp.int32, (1, LANES), 1)
    low = lane < HEAD_DIM
    krow = lax.broadcasted_iota(jnp.int32, (TK, TQ), 0)
    qcol = lax.broadcasted_iota(jnp.int32, (TK, TQ), 1)
    diag_causal = krow <= qcol
    hms = _head_masks()

    qi = qi_ref[...].astype(F32)
    for h in range(IDX_HEADS):
        qp = qi[:, (h // 2) * LANES:(h // 2 + 1) * LANES]
        if h % 2:
            qp = pltpu.roll(qp, HEAD_DIM, 1)
        qx_ref[h] = jnp.where(low, qp, 0.0).T.astype(BF16)
    w_t = wi_ref[...].T
    q = q_ref[...]
    q_norm2 = []
    for h in range(nh):
        qh = jnp.where(hms[h % 2], q[:, (h // 2) * LANES:(h // 2 + 1) * LANES], 0.0)
        qh_t = (qh * ATT_SCALE).T
        qs_ref[h] = qh_t.astype(BF16)
        q_norm2.append(jnp.sum(qh_t * qh_t, axis=0, keepdims=True))

    @pl.when(i == 0)
    def _():
        wk = k_ref.shape[1]
        head_of = lambda a: lax.shift_right_logical(lax.broadcasted_iota(jnp.int32, (wk, wk), a), 6)
        ones_bd = jnp.where(head_of(0) == head_of(1), 1.0, 0.0).astype(BF16)

        def body(j, mx):
            kk = k_ref[pl.ds(pl.multiple_of(j * TK, TK), TK), :].astype(F32)
            return jnp.maximum(mx, jnp.max(_dot((kk * kk).astype(BF16), ones_bd), axis=0, keepdims=True))
        kmax2_ref[...] = lax.fori_loop(0, seq // TK, body, jnp.zeros((1, wk), F32))

    m_fix = [jnp.sqrt(q_norm2[h] * kmax2_ref[0:1, h * HEAD_DIM:h * HEAD_DIM + 1] + 1e-30) * BOUND_SLACK
             for h in range(nh)]
    bound_max = jnp.max(jnp.maximum(jnp.maximum(m_fix[0], m_fix[1]), jnp.maximum(m_fix[2], m_fix[3])))

    def score_block(j, diag):
        off = pl.multiple_of(j * TK, TK)
        kz = ki_ref[pl.ds(off, TK), :]
        sc = jnp.zeros((TK, TQ), F32)
        for h in range(IDX_HEADS):
            sc = sc + w_t[h:h + 1, :] * jnp.maximum(_dot(kz, qx_ref[h]), 0.0)
        if diag:
            sc = jnp.where(diag_causal, sc, NEG)
        bits = lax.bitcast_convert_type(sc, jnp.int32)
        bits = jnp.where(bits == INT_MIN, 0, bits)
        key_ref[j] = jnp.where(bits < 0, bits ^ jnp.int32(0x7FFFFFFF), bits)
        coarse_ref[j] = lax.bitcast_convert_type(bits & jnp.int32(-65536), F32).astype(BF16)

    def score_pair(jj, c):
        score_block(2 * jj, False)
        score_block(2 * jj + 1, False)
        return c

    lax.fori_loop(0, i // 2, score_pair, 0)

    @pl.when(i % 2 == 1)
    def _():
        score_block(i - 1, False)

    score_block(i, True)
    for g in range(1, DSA_GROUP):
        @pl.when(i % DSA_GROUP + g < DSA_GROUP)
        def _():
            key_ref[i + g] = jnp.full((TK, TQ), NEG_KEY, jnp.int32)

    n_unscanned = (seq - (i + 1) * TK).astype(F32)
    fold = TK // 4

    def count(pred):
        def body(j, acc):
            hit = pred(key_ref[j], j)
            for r in range(TK // fold):
                acc = jnp.where(hit[r * fold:(r + 1) * fold, :], acc + 1.0, acc)
            return acc
        acc = lax.fori_loop(0, i + 1, body, jnp.zeros((fold, TQ), F32))
        return jnp.sum(acc, axis=0, keepdims=True)

    def count_ge(cand):
        return count(lambda key, j: key >= cand) + jnp.where(cand <= NEG_KEY, n_unscanned, 0.0)

    kf = float(topk)

    def count_ge_coarse(cand):
        cand_bits = jnp.where(cand < 0, (cand ^ jnp.int32(0x7FFFFFFF)) & jnp.int32(-65536), cand)
        cand_bits = jnp.where((cand > 0) & (cand < MIN_NORMAL_BITS), MIN_NORMAL_BITS, cand_bits)
        cand_b = lax.bitcast_convert_type(cand_bits, F32).astype(BF16)
        one, zero = jnp.ones((), BF16), jnp.zeros((), BF16)

        def body(j, acc):
            hit = jnp.where(coarse_ref[j] >= cand_b, one, zero)
            return acc + ((hit[:fold] + hit[fold:2 * fold]) + (hit[2 * fold:3 * fold] + hit[3 * fold:]))
        acc = lax.fori_loop(0, i + 1, body, jnp.zeros((fold, TQ), BF16))
        return (jnp.sum(acc.astype(F32), axis=0, keepdims=True)
                + jnp.where(cand <= NEG_KEY, n_unscanned, 0.0))

    c0 = count_ge_coarse(jnp.zeros((1, TQ), jnp.int32))
    t0 = jnp.where(c0 >= kf, 0, INT_MIN).astype(jnp.int32)
    ct0 = jnp.where(c0 >= kf, c0, float(seq))

    def bit_step(b, t, ct, counter):
        cand = t | jnp.left_shift(jnp.int32(1), 30 - b)
        cc = counter(cand)
        ok = cc >= kf
        return jnp.where(ok, cand, t), jnp.where(ok, cc, ct)

    t, ct = lax.fori_loop(0, BISECT_COARSE, lambda b, c: bit_step(b, *c, count_ge_coarse), (t0, ct0))
    t, ct = lax.fori_loop(BISECT_COARSE, BISECT_COARSE + BISECT_FIXED,
                          lambda b, c: bit_step(b, *c, count_ge), (t, ct))

    zero_final = (c0 >= kf) & (count_ge_coarse(jnp.full((1, TQ), MIN_NORMAL_BITS, jnp.int32)) < kf)

    def pending(ct):
        return jnp.max(jnp.where((ct > kf) & jnp.logical_not(zero_final), 1.0, 0.0))

    def bit_body(c):
        t, ct = bit_step(c[0], c[2], c[3], count_ge)
        t, ct = bit_step(c[0] + 1, t, ct, count_ge)
        return c[0] + 2, pending(ct), t, ct

    assert (31 - BISECT_COARSE - BISECT_FIXED) % 2 == 0
    _, _, t, c_ge = lax.while_loop(lambda c: (c[0] < 31) & (c[1] > 0.5), bit_body,
                                   (jnp.int32(BISECT_COARSE + BISECT_FIXED), pending(ct), t, ct))

    def tie_limit():
        need = kf - count_ge(t + 1)

        def block_ties(j, c):
            hit = key_ref[j] == t
            acc = jnp.zeros((fold, TQ), F32)
            for r in range(TK // fold):
                acc = jnp.where(hit[r * fold:(r + 1) * fold, :], acc + 1.0, acc)
            cnt_ref[pl.ds(j, 1), :] = jnp.sum(acc, axis=0, keepdims=True)
            return c

        lax.fori_loop(0, i + 1, block_ties, 0)
        nkb = cnt_ref.shape[0]
        cnt = jnp.where(lax.broadcasted_iota(jnp.int32, (nkb, TQ), 0) <= i, cnt_ref[...], 0.0)
        run = jnp.zeros((1, TQ), F32)
        blk_of = jnp.zeros((1, TQ), F32)
        before = jnp.zeros((1, TQ), F32)
        for j in range(nkb):
            run = run + cnt[j:j + 1, :]
            ahead = run < need
            blk_of = blk_of + jnp.where(ahead, 1.0, 0.0)
            before = jnp.where(ahead, run, before)
        blk_of = blk_of.astype(jnp.int32)

        def pick_block(j, c):
            m_ref[...] = jnp.where((blk_of == j) & (key_ref[j] == t), 1.0, m_ref[...])
            return c

        m_ref[...] = jnp.zeros_like(m_ref)
        lax.fori_loop(0, i + 1, pick_block, 0)
        tri = (lax.broadcasted_iota(jnp.int32, (TK, TK), 0) >= lax.broadcasted_iota(jnp.int32, (TK, TK), 1))
        upto = _dot(jnp.where(tri, 1.0, 0.0).astype(BF16), m_ref[...].astype(BF16))
        row_of = jnp.sum(jnp.where(upto < need - before, 1.0, 0.0), axis=0, keepdims=True)
        return blk_of * TK + row_of.astype(jnp.int32)

    x = lax.cond(jnp.max(c_ge) > kf, tie_limit, lambda: jnp.full((1, TQ), 2 * seq, jnp.int32))

    heads = range(nh)
    grp = range(DSA_GROUP)
    pairs = [slice((h // 2) * LANES, (h // 2 + 1) * LANES) for h in heads]
    n_full = i // DSA_GROUP

    def scores(jg):
        off = pl.multiple_of(jg * (DSA_GROUP * TK), DSA_GROUP * TK)
        return tuple(_dot(k_ref[pl.ds(off, DSA_GROUP * TK), pairs[h]], qs_ref[h]) for h in heads)

    def mask_biases(base, causal):
        biases = []
        for g in grp:
            key = key_ref[base + g]
            pos = (base + g) * TK + krow
            msk = (key > t) | ((key == t) & (pos <= x))
            if causal:
                msk = msk & (pos <= i * TQ + qcol)
            biases.append(jnp.where(msk, 0.0, NEG))
        return biases

    def attend_fixed(jg, _, causal):
        ss = scores(jg)
        ss_next = 0
        base = jg * DSA_GROUP
        biases = mask_biases(base, causal)
        for h in heads:
            pv = None
            for g in grp:
                p = jnp.exp((ss[h][g * TK:(g + 1) * TK, :] - m_fix[h]) + biases[g]).astype(BF16)
                d = _dot(vt_ref[0, base + g, h * LANES:(h + 1) * LANES, :], p)
                pv = d if pv is None else pv + d
            acc_ref[h] += pv
        return ss_next

    def attend(jg, carry, causal):
        ms, ss = carry
        ss_next = None if causal else scores(jg + 1)
        base = jg * DSA_GROUP
        biases = mask_biases(base, causal)
        sg = [[ss[h][g * TK:(g + 1) * TK, :] for g in grp] for h in heads]
        new = []
        for h in heads:
            m_new = ms[h]
            for g in grp:
                m_new = jnp.maximum(m_new, jnp.max(sg[h][g] + biases[g], axis=0, keepdims=True))
            new.append(m_new)
        ps = [[jnp.exp((sg[h][g] - new[h]) + biases[g]).astype(BF16) for g in grp] for h in heads]
        for h in heads:
            pv = None
            for g in grp:
                d = _dot(vt_ref[0, base + g, h * LANES:(h + 1) * LANES, :], ps[h][g])
                pv = d if pv is None else pv + d
            acc_ref[h] = jnp.exp(ms[h] - new[h]) * acc_ref[h] + pv
        return tuple(new), ss_next

    for h in heads:
        acc_ref[h] = jnp.zeros((LANES, TQ), F32)

    def run_online():
        carry = lax.fori_loop(0, n_full, lambda jg, c: attend(jg, c, False),
                              (tuple(jnp.full((1, TQ), NEG, F32) for _ in heads), scores(0)))
        attend(n_full, carry, True)
        return 0

    def run_fixed():
        lax.fori_loop(0, n_full, lambda jg, c: attend_fixed(jg, c, False), 0)
        attend_fixed(n_full, 0, True)
        return 0

    lax.cond(bound_max <= SAFE_SHIFT, run_fixed, run_online)

    out_t = jnp.concatenate([_finish_softmax_head(acc_ref[h]) for h in range(nh)], axis=0)
    o_ref[...] = out_t.T


def _dsa_call(f_arr, h_arr, vt, batch, seq):
    nq = seq // TQ
    wb = N_HEADS_B * HEAD_DIM
    return pl.pallas_call(
        _dsa_kernel,
        grid=(batch, nq),
        in_specs=[pl.BlockSpec((TQ, IDX_HEADS * IDX_DIM), lambda b, i: (b * nq + i, H_QI // (IDX_HEADS * IDX_DIM))),
                  pl.BlockSpec((TQ, LANES), lambda b, i: (b * nq + i, F_WI // LANES)),
                  pl.BlockSpec((seq, LANES), lambda b, i: (b, H_KI // LANES)),
                  pl.BlockSpec((TQ, wb), lambda b, i: (b * nq + i, F_QB // wb)),
                  pl.BlockSpec((seq, wb), lambda b, i: (b, H_KB // wb)),
                  pl.BlockSpec((1, seq // TK, N_HEADS_B * LANES, TK), lambda b, i: (b, 0, VT_B // (N_HEADS_B * LANES), 0))],
        out_specs=pl.BlockSpec((TQ, wb), lambda b, i: (b * nq + i, 0)),
        out_shape=jax.ShapeDtypeStruct((batch * seq, wb), F32),
        scratch_shapes=[pltpu.VMEM((seq // TK, TK, TQ), jnp.int32),
                        pltpu.VMEM((seq // TK, TK, TQ), BF16),
                        pltpu.VMEM((IDX_HEADS, LANES, TQ), BF16),
                        pltpu.VMEM((N_HEADS_B, LANES, TQ), BF16),
                        pltpu.VMEM((N_HEADS_B, LANES, TQ), F32),
                        pltpu.VMEM((seq // TK, TQ), F32),
                        pltpu.VMEM((TK, TQ), F32),
                        pltpu.VMEM((1, wb), F32)],
        compiler_params=_cparams(("parallel", "arbitrary")),
        name="dsa_attn",
    )(h_arr, f_arr, h_arr, f_arr, h_arr, vt)


def _outproj_kernel(x_ref, oa_ref, ob_ref, oc_ref, og_ref, gt_ref, w_ref, o_ref):
    wa = N_HEADS_A * HEAD_DIM
    wb = wa + N_HEADS_B * HEAD_DIM
    og = og_ref[...]
    y = (_dot((oa_ref[...] * og[:, :wa]).astype(BF16), w_ref[:wa, :])
         + _dot((ob_ref[...] * og[:, wa:wb]).astype(BF16), w_ref[wa:wb, :])
         + _dot((oc_ref[...] * og[:, wb:]).astype(BF16), w_ref[wb:, :]))
    o_ref[...] = x_ref[...] + gt_ref[0] * y


def _outproj_call(x2, oa, ob, oc, og, gt, w_out, seq):
    m, d = x2.shape
    tm = 512
    row = lambda i: (i, 0)
    return pl.pallas_call(
        _outproj_kernel,
        grid=(m // tm,),
        in_specs=[pl.BlockSpec((tm, d), row),
                  pl.BlockSpec((tm, oa.shape[1]), row),
                  pl.BlockSpec((tm, ob.shape[1]), row),
                  pl.BlockSpec((tm, oc.shape[1]), row),
                  pl.BlockSpec((1, d), lambda i: (0, 0)),
                  pl.BlockSpec((1, 1, d), lambda i: ((i * tm) // seq, 0, 0)),
                  pl.BlockSpec((d, d), lambda i: (0, 0))],
        out_specs=pl.BlockSpec((tm, d), row),
        out_shape=jax.ShapeDtypeStruct((m, d), F32),
        compiler_params=_cparams(("parallel",)),
        name="out_proj",
    )(x2, oa, ob, oc, og, gt, w_out)


def _rope_tables(seq):
    pos = jnp.arange(seq, dtype=F32)
    inv = ROPE_THETA ** (-jnp.arange(0, ROPE_DIM, 2, dtype=F32) / ROPE_DIM)
    ang = pos[:, None] * inv[None, :]
    cos, sin = jnp.cos(ang), jnp.sin(ang)
    zeros = jnp.zeros((seq, HEAD_DIM - ROPE_DIM), F32)
    zh = jnp.zeros((seq, ROPE_HALF), F32)
    cos_h = jnp.concatenate([cos, cos, jnp.ones_like(zeros)], axis=1)
    s1_h = jnp.concatenate([-sin, zh, zeros], axis=1)
    s2_h = jnp.concatenate([zh, sin, zeros], axis=1)
    two = lambda t: jnp.concatenate([t, t], axis=1)
    return two(cos_h), two(s1_h), two(s2_h)


def _permute_w_in(w):
    d = w.shape[0]
    a, b, c = N_HEADS_A * HEAD_DIM, N_HEADS_B * HEAD_DIM, N_HEADS_C * HEAD_DIM
    sizes = (a, a, a, b, b, b, IDX_HEADS * IDX_DIM, IDX_DIM, IDX_HEADS, c, c, c)
    offs = np.concatenate([[0], np.cumsum(sizes)])
    qa, ka, va, qb, kb, vb, qi, ki, wi, qc, kc, vc = [w[:, int(offs[k]):int(offs[k + 1])] for k in range(12)]
    z = lambda n: jnp.zeros((d, n), w.dtype)
    out = jnp.concatenate([qa, qb, qc, ka, va, kb, vb, kc, vc, qi,
                           ki, z(LANES - IDX_DIM), wi, z(LANES - IDX_HEADS)], axis=1)
    assert out.shape[1] == W_PERM_WIDTH
    return out.astype(BF16)


def _suffix_sum_matrix():
    s = np.arange(TK + SB_TAIL)[:, None]
    j = np.arange(TK)[None, :]
    return jnp.asarray((j > s) | (s >= TK), BF16)


def kernel(x, c, w_ada, b_ada, norm_g, w_in, qk_g, out_g, w_out, ffn_w1, ffn_w3, ffn_w2):
    batch, seq, d = x.shape
    depth = w_ada.shape[0]
    assert seq % PROJ_TM == 0 and d % LANES == 0 and batch <= 8
    assert (seq // TK) % MOBA_GROUP == 0 and (seq // TK) % DSA_GROUP == 0

    c_pad = jnp.zeros((8, d), F32).at[:batch].set(c)
    mod = _mod_call(c_pad, w_ada, b_ada)
    cos_t, s1_t, s2_t = _rope_tables(seq)
    hd = np.arange(LANES) // HEAD_DIM
    bd = jnp.asarray(hd[:, None] == hd[None, :], BF16)
    u_mat = _suffix_sum_matrix()

    x2 = x.reshape(batch * seq, d)
    for layer in range(depth):
        mods = [mod[layer, :batch, k * d:(k + 1) * d].reshape(batch, 1, d) for k in range(N_MOD)]
        sh1, sc1, g1, sh2, sc2, g2, sh3, sc3, g3 = mods
        ng = norm_g[layer]
        w1 = ffn_w1[layer].astype(BF16)
        w3 = ffn_w3[layer].astype(BF16)
        w2 = ffn_w2[layer].astype(BF16)

        x2 = _ffn_call(x2, ng[0:1], sh1, sc1, g1, w1[0], w3[0], w2[0], seq)

        qkg = jnp.concatenate([jnp.tile(qk_g[layer], (1, 2)), jnp.ones((4, LANES), F32)], axis=0)
        f_arr, h_arr, kmean, vt = _proj_call(x2, ng[1:2], sh2, sc2, _permute_w_in(w_in[layer]), qkg,
                                             cos_t, s1_t, s2_t, bd, batch, seq)
        per_tile = PROJ_TM // MOBA_BLOCK
        kmean = kmean[:, :per_tile, :].reshape(batch, seq // MOBA_BLOCK, N_HEADS_A * HEAD_DIM)
        oa = _moba_call(f_arr, h_arr, kmean, vt, batch, seq)
        ob = _dsa_call(f_arr, h_arr, vt, batch, seq)
        oc = _sb_call(f_arr, h_arr, vt, u_mat, batch, seq)
        x2 = _outproj_call(x2, oa, ob, oc, out_g[layer].reshape(1, d), g2, w_out[layer].astype(BF16), seq)

        x2 = _ffn_call(x2, ng[2:3], sh3, sc3, g3, w1[1], w3[1], w2[1], seq)
    return x2.reshape(batch, seq, d)
```

```python
import jax
import jax.numpy as jnp
import numpy as np
from jax import lax
from jax.experimental import pallas as pl
from jax.experimental.pallas import tpu as pltpu

F32 = jnp.float32
BF16 = jnp.bfloat16

HEAD_DIM = 64
N_HEADS_A = 4
N_HEADS_B = 4
N_HEADS_C = 8
ROPE_DIM = HEAD_DIM // 4
ROPE_HALF = ROPE_DIM // 2
ROPE_THETA = 500000.0
MOBA_BLOCK = 256
MOBA_TOPK = 3
DSA_TOPK = 256
IDX_HEADS = 8
IDX_DIM = 64
N_MOD = 9
RMS_EPS = 1e-6
NEG = -1e30
ATT_SCALE = HEAD_DIM ** -0.5
SOFTMAX_SCALE2 = ATT_SCALE * 1.4426950408889634

LANES = 128
MXU_N = 256
TQ = 256
TK = 256
PROJ_TM = 512
VMEM_LIMIT = 56 * 1024 * 1024

SB_DEAD_LOG = -88.0
SB_TAIL = 16
SB_HEADS = 8
MOBA_GROUP = 4
DSA_GROUP = 2
BOUND_SLACK = 1.02
SAFE_SHIFT = 40.0
BISECT_COARSE = 15
BISECT_FIXED = 8

F_QA, F_QB, F_QC, F_WI, F_WIDTH = 0, 256, 512, 1024, 1152
H_KA, H_KB, H_KC, H_QI, H_KI, H_WIDTH = 0, 256, 512, 1024, 1536, 1664
W_PERM_WIDTH = 15 * MXU_N
VT_A, VT_B, VT_C, VT_ROWS = 0, 512, 1024, 1536


def _dot(a, b):
    return jnp.dot(a, b, preferred_element_type=F32)


def _split3(x):
    a = x.astype(BF16)
    r = x - a.astype(F32)
    b = r.astype(BF16)
    c = (r - b.astype(F32)).astype(BF16)
    return a, b, c


def _cparams(sem):
    return pltpu.CompilerParams(dimension_semantics=sem, vmem_limit_bytes=VMEM_LIMIT)


def _mod_kernel(c_ref, w_ref, b_ref, o_ref):
    c = c_ref[...]
    sc = c * (1.0 / (1.0 + jnp.exp(-c)))
    a, b, c3 = _split3(sc)
    w = w_ref[0]
    wa, wb, wc = _split3(w)
    acc = _dot(a, wa) + (_dot(a, wb) + _dot(b, wa)) + (_dot(a, wc) + _dot(b, wb) + _dot(c3, wa))
    o_ref[0] = acc + b_ref[0]


def _mod_call(c_pad, w_ada, b_ada):
    depth, d, n = w_ada.shape
    tn = 1024
    rows = c_pad.shape[0]
    return pl.pallas_call(
        _mod_kernel,
        grid=(depth, n // tn),
        in_specs=[pl.BlockSpec((rows, d), lambda l, j: (0, 0)),
                  pl.BlockSpec((1, d, tn), lambda l, j: (l, 0, j)),
                  pl.BlockSpec((1, 1, tn), lambda l, j: (l, 0, j))],
        out_specs=pl.BlockSpec((1, rows, tn), lambda l, j: (l, 0, j)),
        out_shape=jax.ShapeDtypeStruct((depth, rows, n), F32),
        compiler_params=_cparams(("parallel", "parallel")),
        name="adaln_mod",
    )(c_pad, w_ada, b_ada.reshape(depth, 1, n))


def _norm_modulate(x, ng, sh, sc):
    ms = jnp.mean(x * x, axis=-1, keepdims=True)
    h = x * lax.rsqrt(ms + RMS_EPS) * ng
    return h * (1.0 + sc) + sh


def _ffn_kernel(x_ref, ng_ref, sh_ref, sc_ref, gt_ref, w1_ref, w3_ref, w2_ref, o_ref, h_ref, acc_ref):
    j = pl.program_id(1)

    @pl.when(j == 0)
    def _():
        h = _norm_modulate(x_ref[...], ng_ref[...], sh_ref[0], sc_ref[0])
        h_ref[...] = h.astype(BF16)
        acc_ref[...] = jnp.zeros_like(acc_ref)

    h = h_ref[...]
    a = _dot(h, w1_ref[...])
    b = _dot(h, w3_ref[...])
    u = (a * (1.0 / (1.0 + jnp.exp(-a))) * b).astype(BF16)
    acc_ref[...] += _dot(u, w2_ref[...])

    @pl.when(j == pl.num_programs(1) - 1)
    def _():
        o_ref[...] = x_ref[...] + 0.5 * gt_ref[0] * acc_ref[...]


def _ffn_call(x2, ng, sh, sc, gt, w1, w3, w2, seq):
    m, d = x2.shape
    dff = w1.shape[1]
    tm = 1024
    tf = dff // 2 if (dff // 2) % LANES == 0 else MXU_N
    bidx = lambda i, j: ((i * tm) // seq, 0, 0)
    return pl.pallas_call(
        _ffn_kernel,
        grid=(m // tm, dff // tf),
        in_specs=[pl.BlockSpec((tm, d), lambda i, j: (i, 0)),
                  pl.BlockSpec((1, d), lambda i, j: (0, 0)),
                  pl.BlockSpec((1, 1, d), bidx),
                  pl.BlockSpec((1, 1, d), bidx),
                  pl.BlockSpec((1, 1, d), bidx),
                  pl.BlockSpec((d, tf), lambda i, j: (0, j)),
                  pl.BlockSpec((d, tf), lambda i, j: (0, j)),
                  pl.BlockSpec((tf, d), lambda i, j: (j, 0))],
        out_specs=pl.BlockSpec((tm, d), lambda i, j: (i, 0)),
        out_shape=jax.ShapeDtypeStruct((m, d), F32),
        scratch_shapes=[pltpu.VMEM((tm, d), BF16), pltpu.VMEM((tm, d), F32)],
        compiler_params=_cparams(("parallel", "arbitrary")),
        name="swiglu_ffn",
    )(x2, ng, sh, sc, gt, w1, w3, w2)


def _proj_kernel(x_ref, ng_ref, sh_ref, sc_ref, w_ref, qkg_ref, cos_ref, s1_ref, s2_ref, bd_ref,
                 f_ref, h_ref, km_ref, vt_ref, hs_ref):
    hs_ref[...] = _norm_modulate(x_ref[...], ng_ref[...], sh_ref[0], sc_ref[0]).astype(BF16)
    cos, s1, s2 = cos_ref[...], s1_ref[...], s2_ref[...]
    bd = bd_ref[...]
    tm = x_ref.shape[0]

    def rope(v):
        return v * cos + pltpu.roll(v, LANES - ROPE_HALF, 1) * s1 + pltpu.roll(v, ROPE_HALF, 1) * s2

    def headnorm(v, g):
        a, b, _ = _split3(v * v)
        ss = _dot(a, bd) + _dot(b, bd)
        return v * lax.rsqrt(ss * (1.0 / HEAD_DIM) + RMS_EPS) * g

    def chunk(c):
        y = _dot(hs_ref[...], w_ref[:, c * MXU_N:(c + 1) * MXU_N])
        return y[:, :LANES], y[:, LANES:]

    def normrope_chunk(c, grow):
        g = qkg_ref[grow:grow + 1, :]
        return [rope(headnorm(v, g)) for v in chunk(c)]

    def put(ref, off, halves, dtype):
        for k, v in enumerate(halves):
            ref[:, off + k * LANES: off + (k + 1) * LANES] = v.astype(dtype)

    def put_vt(base, halves, with_ones):
        ones = jnp.ones((HEAD_DIM, TK), BF16)
        for r in range(tm // TK):
            for k, v in enumerate(halves):
                t = v[r * TK:(r + 1) * TK, :].T.astype(BF16)
                if with_ones:
                    for hh in range(2):
                        row = base + (2 * k + hh) * LANES
                        vt_ref[0, r, row:row + HEAD_DIM, :] = t[hh * HEAD_DIM:(hh + 1) * HEAD_DIM, :]
                        vt_ref[0, r, row + HEAD_DIM:row + LANES, :] = ones
                else:
                    vt_ref[0, r, base + k * LANES:base + (k + 1) * LANES, :] = t

    put(f_ref, F_QA, normrope_chunk(0, 0), F32)
    put(f_ref, F_QB, normrope_chunk(1, 2), F32)
    put(f_ref, F_QC, chunk(2), F32)
    put(f_ref, F_QC + MXU_N, chunk(3), F32)

    ka = normrope_chunk(4, 1)
    put(h_ref, H_KA, ka, BF16)
    rows = lax.broadcasted_iota(jnp.int32, (8, LANES), 0)
    for k, v in enumerate(ka):
        km = jnp.zeros((8, LANES), F32)
        for r in range(tm // MOBA_BLOCK):
            s = jnp.sum(v[r * MOBA_BLOCK:(r + 1) * MOBA_BLOCK, :], axis=0, keepdims=True) * (1.0 / MOBA_BLOCK)
            km = jnp.where(rows == r, s, km)
        km_ref[0, :, k * LANES:(k + 1) * LANES] = km

    put_vt(VT_A, chunk(5), True)
    put(h_ref, H_KB, normrope_chunk(6, 3), BF16)
    put_vt(VT_B, chunk(7), True)
    put(h_ref, H_KC, chunk(8), BF16)
    put(h_ref, H_KC + MXU_N, chunk(9), BF16)
    put_vt(VT_C, chunk(10), False)
    put_vt(VT_C + MXU_N, chunk(11), False)
    put(h_ref, H_QI, [rope(v) for v in chunk(12)], BF16)
    put(h_ref, H_QI + MXU_N, [rope(v) for v in chunk(13)], BF16)
    ki, wi = chunk(14)
    h_ref[:, H_KI:H_KI + LANES] = rope(ki).astype(BF16)
    f_ref[:, F_WI:F_WI + LANES] = wi


def _proj_call(x2, ng, sh, sc, w_perm, qkg, cos_t, s1_t, s2_t, bd, batch, seq):
    m, d = x2.shape
    tm = PROJ_TM
    nt = seq // tm
    bidx = lambda i: ((i * tm) // seq, 0, 0)
    tab = pl.BlockSpec((tm, LANES), lambda i: (i % nt, 0))
    return pl.pallas_call(
        _proj_kernel,
        grid=(m // tm,),
        in_specs=[pl.BlockSpec((tm, d), lambda i: (i, 0)),
                  pl.BlockSpec((1, d), lambda i: (0, 0)),
                  pl.BlockSpec((1, 1, d), bidx),
                  pl.BlockSpec((1, 1, d), bidx),
                  pl.BlockSpec((d, W_PERM_WIDTH), lambda i: (0, 0)),
                  pl.BlockSpec((8, LANES), lambda i: (0, 0)),
                  tab, tab, tab,
                  pl.BlockSpec((LANES, LANES), lambda i: (0, 0))],
        out_specs=[pl.BlockSpec((tm, F_WIDTH), lambda i: (i, 0)),
                   pl.BlockSpec((tm, H_WIDTH), lambda i: (i, 0)),
                   pl.BlockSpec((1, 8, 2 * LANES), lambda i: (i, 0, 0)),
                   pl.BlockSpec((1, tm // TK, VT_ROWS, TK), lambda i: (i // nt, i % nt, 0, 0))],
        out_shape=[jax.ShapeDtypeStruct((m, F_WIDTH), F32),
                   jax.ShapeDtypeStruct((m, H_WIDTH), BF16),
                   jax.ShapeDtypeStruct((m // tm, 8, 2 * LANES), F32),
                   jax.ShapeDtypeStruct((batch, seq // TK, VT_ROWS, TK), BF16)],
        scratch_shapes=[pltpu.VMEM((tm, d), BF16)],
        compiler_params=_cparams(("parallel",)),
        name="in_proj",
    )(x2, ng, sh, sc, w_perm, qkg, cos_t, s1_t, s2_t, bd)


def _head_masks():
    lane = lax.broadcasted_iota(jnp.int32, (1, LANES), 1)
    return [(lane >= HEAD_DIM * h) & (lane < HEAD_DIM * (h + 1)) for h in range(2)]


def _head_rms_t(o):
    ss = jnp.sum(o * o, axis=0, keepdims=True) * (1.0 / HEAD_DIM)
    return o * lax.rsqrt(ss + RMS_EPS)


def _finish_softmax_head(acc):
    return _head_rms_t(acc[:HEAD_DIM, :] / acc[HEAD_DIM:HEAD_DIM + 1, :])


def _moba_kernel(q_ref, km_ref, k_ref, vt_ref, o_ref, bias_ref, qs_ref, acc_ref, kmax2_ref):
    i = pl.program_id(2)
    nb = km_ref.shape[1]
    q = q_ref[...]
    kma, kmb, _ = _split3(km_ref[0])
    blk = lax.broadcasted_iota(jnp.int32, (nb, TQ), 0).astype(F32)
    i_f = i.astype(F32)
    krow = lax.broadcasted_iota(jnp.int32, (TK, TQ), 0)
    qcol = lax.broadcasted_iota(jnp.int32, (TK, TQ), 1)
    causal = krow <= qcol

    q_norm2 = []
    for h, hm in enumerate(_head_masks()):
        qh_t = jnp.where(hm, q, 0.0).T
        qa, qb, _ = _split3(qh_t)
        gate = _dot(kma, qa) + (_dot(kmb, qa) + _dot(kma, qb))
        gate = jnp.where(blk < i_f, gate, NEG)
        sel = jnp.zeros((nb, TQ), F32)
        for _ in range(MOBA_TOPK):
            mx = jnp.max(gate, axis=0, keepdims=True)
            idx = jnp.min(jnp.where(gate == mx, blk, float(nb)), axis=0, keepdims=True)
            pick = blk == idx
            sel = jnp.where(pick, 1.0, sel)
            gate = jnp.where(pick, -jnp.inf, gate)
        bias_ref[h] = jnp.where((sel > 0.5) & (blk < i_f), 0.0, NEG)
        qs_t = qh_t * SOFTMAX_SCALE2
        qs_ref[h] = qs_t.astype(BF16)
        q_norm2.append(jnp.sum(qs_t * qs_t, axis=0, keepdims=True))

    @pl.when(i == 0)
    def _():
        head_of = lambda a: lax.shift_right_logical(lax.broadcasted_iota(jnp.int32, (LANES, LANES), a), 6)
        ones_bd = jnp.where(head_of(0) == head_of(1), 1.0, 0.0).astype(BF16)

        def norm_body(j, mx):
            kk = k_ref[pl.ds(pl.multiple_of(j * TK, TK), TK), :].astype(F32)
            return jnp.maximum(mx, jnp.max(_dot((kk * kk).astype(BF16), ones_bd), axis=0, keepdims=True))
        kmax2_ref[...] = lax.fori_loop(0, nb, norm_body, jnp.zeros((1, LANES), F32))

    m_fix = [jnp.sqrt(q_norm2[h] * kmax2_ref[0:1, h * HEAD_DIM:h * HEAD_DIM + 1] + 1e-30) * BOUND_SLACK
             for h in range(2)]

    own = pl.multiple_of(i * TK, TK)
    heads = range(2)
    grp = range(MOBA_GROUP)
    n_groups = (i + MOBA_GROUP - 1) // MOBA_GROUP
    last_group = km_ref.shape[1] // MOBA_GROUP - 1

    def scores(jg):
        off = pl.multiple_of(jnp.minimum(jg, last_group) * (MOBA_GROUP * TK), MOBA_GROUP * TK)
        kb = k_ref[pl.ds(off, MOBA_GROUP * TK), :]
        return tuple(_dot(kb, qs_ref[h]) for h in heads)

    def own_block(refs):
        k_own = k_ref[pl.ds(own, TK), :]
        used = []
        for h in heads:
            s = jnp.where(causal, _dot(k_own, qs_ref[h]), NEG)
            m0 = jnp.max(s, axis=0, keepdims=True) if refs is None else refs[h]
            acc_ref[h] = _dot(vt_ref[0, i, h * LANES:(h + 1) * LANES, :], jnp.exp2(s - m0).astype(BF16))
            used.append(m0)
        return tuple(used)

    def body_fixed(jg, c):
        ss = scores(jg)
        base = jg * MOBA_GROUP
        for h in heads:
            pv = None
            for g in grp:
                b = bias_ref[h, pl.ds(base + g, 1), :]
                p = jnp.exp2(ss[h][g * TK:(g + 1) * TK, :] - (m_fix[h] - b)).astype(BF16)
                d = _dot(vt_ref[0, base + g, h * LANES:(h + 1) * LANES, :], p)
                pv = d if pv is None else pv + d
            acc_ref[h] += pv
        return c

    def body(jg, carry):
        ms, ss = carry
        ss_next = scores(jg + 1)
        base = jg * MOBA_GROUP
        sg = [[ss[h][g * TK:(g + 1) * TK, :] for g in grp] for h in heads]
        bs = [[bias_ref[h, pl.ds(base + g, 1), :] for g in grp] for h in heads]
        new = []
        for h in heads:
            m_new = ms[h]
            for g in grp:
                m_new = jnp.maximum(m_new, jnp.max(sg[h][g], axis=0, keepdims=True) + bs[h][g])
            new.append(m_new)
        ps = [[jnp.exp2(sg[h][g] - (new[h] - bs[h][g])).astype(BF16) for g in grp] for h in heads]
        for h in heads:
            pv = None
            for g in grp:
                d = _dot(vt_ref[0, base + g, h * LANES:(h + 1) * LANES, :], ps[h][g])
                pv = d if pv is None else pv + d
            acc_ref[h] = jnp.exp2(ms[h] - new[h]) * acc_ref[h] + pv
        return tuple(new), ss_next

    def run_online():
        lax.fori_loop(0, n_groups, body, (own_block(None), scores(0)))
        return 0

    def run_fixed():
        own_block(m_fix)
        lax.fori_loop(0, n_groups, body_fixed, 0)
        return 0

    lax.cond(jnp.max(jnp.maximum(m_fix[0], m_fix[1])) <= SAFE_SHIFT, run_fixed, run_online)
    out_t = jnp.concatenate([_finish_softmax_head(acc_ref[h]) for h in range(2)], axis=0)
    o_ref[...] = out_t.T


def _moba_call(f_arr, h_arr, kmean, vt, batch, seq):
    nq = seq // TQ
    nb = seq // MOBA_BLOCK
    npair = N_HEADS_A // 2
    return pl.pallas_call(
        _moba_kernel,
        grid=(batch, npair, nq),
        in_specs=[pl.BlockSpec((TQ, LANES), lambda b, p, i: (b * nq + i, F_QA // LANES + p)),
                  pl.BlockSpec((1, nb, LANES), lambda b, p, i: (b, 0, p)),
                  pl.BlockSpec((seq, LANES), lambda b, p, i: (b, H_KA // LANES + p)),
                  pl.BlockSpec((1, seq // TK, 2 * LANES, TK), lambda b, p, i: (b, 0, VT_A // (2 * LANES) + p, 0))],
        out_specs=pl.BlockSpec((TQ, LANES), lambda b, p, i: (b * nq + i, p)),
        out_shape=jax.ShapeDtypeStruct((batch * seq, N_HEADS_A * HEAD_DIM), F32),
        scratch_shapes=[pltpu.VMEM((2, nb, TQ), F32),
                        pltpu.VMEM((2, LANES, TQ), BF16),
                        pltpu.VMEM((2, LANES, TQ), F32),
                        pltpu.VMEM((1, LANES), F32)],
        compiler_params=_cparams(("parallel", "parallel", "arbitrary")),
        name="moba_attn",
    )(f_arr, kmean, h_arr, vt)


def _sb_kernel(q_ref, k_ref, vt_ref, u_ref, o_ref, qs_ref, acc_ref):
    i = pl.program_id(2)
    krow = lax.broadcasted_iota(jnp.int32, (TK, TQ), 0)
    qcol = lax.broadcasted_iota(jnp.int32, (TK, TQ), 1)
    strict = krow < qcol
    hms = _head_masks()
    for h in range(SB_HEADS):
        pr = slice((h // 2) * LANES, (h // 2 + 1) * LANES)
        qs_ref[h] = (jnp.where(hms[h % 2], q_ref[:, pr], 0.0) * ATT_SCALE).T.astype(BF16)
        acc_ref[h] = jnp.zeros((LANES, TQ), F32)

    heads = range(SB_HEADS)
    pairs = [slice((h // 2) * LANES, (h // 2 + 1) * LANES) for h in heads]

    def logits(j):
        off = pl.multiple_of(jnp.maximum(j, 0) * TK, TK)
        return tuple(_dot(k_ref[pl.ds(off, TK), pairs[h]], qs_ref[h]) for h in heads)

    def block(j, rs, diag):
        zs = logits(j)
        u = u_ref[...]
        log_betas, his, los = [], [], []
        for h in heads:
            z = zs[h]
            log_beta = jnp.minimum(z, 0.0) - jnp.log(1.0 + jnp.exp(-jnp.abs(z)))
            log_1m = log_beta - z
            if diag:
                log_1m = jnp.where(strict, log_1m, 0.0)
            hi = log_1m.astype(BF16)
            log_betas.append(log_beta)
            his.append(hi)
            los.append((log_1m - hi.astype(F32)).astype(BF16))
        sums = [_dot(u, his[h]) + _dot(u, los[h]) for h in heads]
        weights = []
        for h in heads:
            after = sums[h][:TK, :] + rs[h]
            a = jnp.exp(log_betas[h] + after)
            if diag:
                a = jnp.where(strict, a, 0.0)
            weights.append(a.astype(BF16))
        for h in heads:
            acc_ref[h] += _dot(vt_ref[0, j, pairs[h], :], weights[h])
        return tuple(rs[h] + sums[h][TK:TK + 1, :] for h in heads)

    zero = jnp.zeros((1, TQ), F32)
    rs = block(i, (zero,) * SB_HEADS, True)

    def alive(rs):
        m = rs[0]
        for r in rs[1:]:
            m = jnp.maximum(m, r)
        return jnp.max(m)

    def cond(c):
        return (c[0] >= 0) & (c[1] > SB_DEAD_LOG)

    def body(c):
        rs = block(c[0], c[2], False)
        return c[0] - 1, alive(rs), rs

    lax.while_loop(cond, body, (i - 1, alive(rs), rs))
    out_t = jnp.concatenate(
        [_head_rms_t(acc_ref[h][(h % 2) * HEAD_DIM:(h % 2 + 1) * HEAD_DIM, :]) for h in range(SB_HEADS)], axis=0)
    o_ref[...] = out_t.T


def _sb_call(f_arr, h_arr, vt, u_mat, batch, seq):
    nq = seq // TQ
    w = SB_HEADS * HEAD_DIM
    ngrp = N_HEADS_C // SB_HEADS
    return pl.pallas_call(
        _sb_kernel,
        grid=(batch, ngrp, nq),
        in_specs=[pl.BlockSpec((TQ, w), lambda b, p, i: (b * nq + i, F_QC // w + p)),
                  pl.BlockSpec((seq, w), lambda b, p, i: (b, H_KC // w + p)),
                  pl.BlockSpec((1, seq // TK, w, TK), lambda b, p, i: (b, 0, VT_C // w + p, 0)),
                  pl.BlockSpec((TK + SB_TAIL, TK), lambda b, p, i: (0, 0))],
        out_specs=pl.BlockSpec((TQ, w), lambda b, p, i: (b * nq + i, p)),
        out_shape=jax.ShapeDtypeStruct((batch * seq, N_HEADS_C * HEAD_DIM), F32),
        scratch_shapes=[pltpu.VMEM((SB_HEADS, LANES, TQ), BF16), pltpu.VMEM((SB_HEADS, LANES, TQ), F32)],
        compiler_params=_cparams(("parallel", "parallel", "arbitrary")),
        name="stickbreak_attn",
    )(f_arr, h_arr, vt, u_mat)


_NEG_BITS = int(np.array(NEG, np.float32).view(np.int32))
NEG_KEY = _NEG_BITS ^ 0x7FFFFFFF
INT_MIN = -2 ** 31
MIN_NORMAL_BITS = 0x00800000


def _dsa_kernel(qi_ref, wi_ref, ki_ref, q_ref, k_ref, vt_ref, o_ref,
                key_ref, coarse_ref, qx_ref, qs_ref, acc_ref, cnt_ref, m_ref, kmax2_ref):
    i = pl.program_id(1)
    seq = k_ref.shape[0]
    topk = min(DSA_TOPK, seq // 4)
    nh = N_HEADS_B
    lane = lax.broadcasted_iota(jnp.int32, (1, LANES), 1)
    low = lane < HEAD_DIM
    krow = lax.broadcasted_iota(jnp.int32, (TK, TQ), 0)
    qcol = lax.broadcasted_iota(jnp.int32, (TK, TQ), 1)
    diag_causal = krow <= qcol
    hms = _head_masks()

    qi = qi_ref[...].astype(F32)
    for h in range(IDX_HEADS):
        qp = qi[:, (h // 2) * LANES:(h // 2 + 1) * LANES]
        if h % 2:
            qp = pltpu.roll(qp, HEAD_DIM, 1)
        qx_ref[h] = jnp.where(low, qp, 0.0).T.astype(BF16)
    w_t = wi_ref[...].T
    q = q_ref[...]
    q_norm2 = []
    for h in range(nh):
        qh = jnp.where(hms[h % 2], q[:, (h // 2) * LANES:(h // 2 + 1) * LANES], 0.0)
        qh_t = (qh * SOFTMAX_SCALE2).T
        qs_ref[h] = qh_t.astype(BF16)
        q_norm2.append(jnp.sum(qh_t * qh_t, axis=0, keepdims=True))

    @pl.when(i == 0)
    def _():
        wk = k_ref.shape[1]
        head_of = lambda a: lax.shift_right_logical(lax.broadcasted_iota(jnp.int32, (wk, wk), a), 6)
        ones_bd = jnp.where(head_of(0) == head_of(1), 1.0, 0.0).astype(BF16)

        def body(j, mx):
            kk = k_ref[pl.ds(pl.multiple_of(j * TK, TK), TK), :].astype(F32)
            return jnp.maximum(mx, jnp.max(_dot((kk * kk).astype(BF16), ones_bd), axis=0, keepdims=True))
        kmax2_ref[...] = lax.fori_loop(0, seq // TK, body, jnp.zeros((1, wk), F32))

    m_fix = [jnp.sqrt(q_norm2[h] * kmax2_ref[0:1, h * HEAD_DIM:h * HEAD_DIM + 1] + 1e-30) * BOUND_SLACK
             for h in range(nh)]
    bound_max = jnp.max(jnp.maximum(jnp.maximum(m_fix[0], m_fix[1]), jnp.maximum(m_fix[2], m_fix[3])))

    def score_block(j, diag):
        off = pl.multiple_of(j * TK, TK)
        kz = ki_ref[pl.ds(off, TK), :]
        sc = jnp.zeros((TK, TQ), F32)
        for h in range(IDX_HEADS):
            sc = sc + w_t[h:h + 1, :] * jnp.maximum(_dot(kz, qx_ref[h]), 0.0)
        if diag:
            sc = jnp.where(diag_causal, sc, NEG)
        bits = lax.bitcast_convert_type(sc, jnp.int32)
        bits = jnp.where(bits == INT_MIN, 0, bits)
        key_ref[j] = jnp.where(bits < 0, bits ^ jnp.int32(0x7FFFFFFF), bits)
        coarse_ref[j] = lax.bitcast_convert_type(bits & jnp.int32(-65536), F32).astype(BF16)

    def score_pair(jj, c):
        score_block(2 * jj, False)
        score_block(2 * jj + 1, False)
        return c

    lax.fori_loop(0, i // 2, score_pair, 0)

    @pl.when(i % 2 == 1)
    def _():
        score_block(i - 1, False)

    score_block(i, True)
    for g in range(1, DSA_GROUP):
        @pl.when(i % DSA_GROUP + g < DSA_GROUP)
        def _():
            key_ref[i + g] = jnp.full((TK, TQ), NEG_KEY, jnp.int32)

    n_unscanned = (seq - (i + 1) * TK).astype(F32)
    fold = TK // 4

    def count(pred):
        def body(j, acc):
            hit = pred(key_ref[j], j)
            for r in range(TK // fold):
                acc = jnp.where(hit[r * fold:(r + 1) * fold, :], acc + 1.0, acc)
            return acc
        acc = lax.fori_loop(0, i + 1, body, jnp.zeros((fold, TQ), F32))
        return jnp.sum(acc, axis=0, keepdims=True)

    def count_ge(cand):
        return count(lambda key, j: key >= cand) + jnp.where(cand <= NEG_KEY, n_unscanned, 0.0)

    kf = float(topk)

    def count_ge_coarse(cand):
        cand_bits = jnp.where(cand < 0, (cand ^ jnp.int32(0x7FFFFFFF)) & jnp.int32(-65536), cand)
        cand_bits = jnp.where((cand > 0) & (cand < MIN_NORMAL_BITS), MIN_NORMAL_BITS, cand_bits)
        cand_b = lax.bitcast_convert_type(cand_bits, F32).astype(BF16)
        one, zero = jnp.ones((), BF16), jnp.zeros((), BF16)

        def body(j, acc):
            hit = jnp.where(coarse_ref[j] >= cand_b, one, zero)
            return acc + ((hit[:fold] + hit[fold:2 * fold]) + (hit[2 * fold:3 * fold] + hit[3 * fold:]))
        acc = lax.fori_loop(0, i + 1, body, jnp.zeros((fold, TQ), BF16))
        return (jnp.sum(acc.astype(F32), axis=0, keepdims=True)
                + jnp.where(cand <= NEG_KEY, n_unscanned, 0.0))

    c0 = count_ge_coarse(jnp.zeros((1, TQ), jnp.int32))
    t0 = jnp.where(c0 >= kf, 0, INT_MIN).astype(jnp.int32)
    ct0 = jnp.where(c0 >= kf, c0, float(seq))

    def bit_step(b, t, ct, counter):
        cand = t | jnp.left_shift(jnp.int32(1), 30 - b)
        cc = counter(cand)
        ok = cc >= kf
        return jnp.where(ok, cand, t), jnp.where(ok, cc, ct)

    t, ct = lax.fori_loop(0, BISECT_COARSE, lambda b, c: bit_step(b, *c, count_ge_coarse), (t0, ct0))
    t, ct = lax.fori_loop(BISECT_COARSE, BISECT_COARSE + BISECT_FIXED,
                          lambda b, c: bit_step(b, *c, count_ge), (t, ct))

    zero_final = (c0 >= kf) & (count_ge_coarse(jnp.full((1, TQ), MIN_NORMAL_BITS, jnp.int32)) < kf)

    def pending(ct):
        return jnp.max(jnp.where((ct > kf) & jnp.logical_not(zero_final), 1.0, 0.0))

    def bit_body(c):
        t, ct = bit_step(c[0], c[2], c[3], count_ge)
        return c[0] + 1, pending(ct), t, ct

    _, _, t, c_ge = lax.while_loop(lambda c: (c[0] < 31) & (c[1] > 0.5), bit_body,
                                   (jnp.int32(BISECT_COARSE + BISECT_FIXED), pending(ct), t, ct))

    def tie_limit():
        need = kf - count_ge(t + 1)

        def block_ties(j, c):
            hit = key_ref[j] == t
            acc = jnp.zeros((fold, TQ), F32)
            for r in range(TK // fold):
                acc = jnp.where(hit[r * fold:(r + 1) * fold, :], acc + 1.0, acc)
            cnt_ref[pl.ds(j, 1), :] = jnp.sum(acc, axis=0, keepdims=True)
            return c

        lax.fori_loop(0, i + 1, block_ties, 0)
        nkb = cnt_ref.shape[0]
        cnt = jnp.where(lax.broadcasted_iota(jnp.int32, (nkb, TQ), 0) <= i, cnt_ref[...], 0.0)
        run = jnp.zeros((1, TQ), F32)
        blk_of = jnp.zeros((1, TQ), F32)
        before = jnp.zeros((1, TQ), F32)
        for j in range(nkb):
            run = run + cnt[j:j + 1, :]
            ahead = run < need
            blk_of = blk_of + jnp.where(ahead, 1.0, 0.0)
            before = jnp.where(ahead, run, before)
        blk_of = blk_of.astype(jnp.int32)

        def pick_block(j, c):
            m_ref[...] = jnp.where((blk_of == j) & (key_ref[j] == t), 1.0, m_ref[...])
            return c

        m_ref[...] = jnp.zeros_like(m_ref)
        lax.fori_loop(0, i + 1, pick_block, 0)
        tri = (lax.broadcasted_iota(jnp.int32, (TK, TK), 0) >= lax.broadcasted_iota(jnp.int32, (TK, TK), 1))
        upto = _dot(jnp.where(tri, 1.0, 0.0).astype(BF16), m_ref[...].astype(BF16))
        row_of = jnp.sum(jnp.where(upto < need - before, 1.0, 0.0), axis=0, keepdims=True)
        return blk_of * TK + row_of.astype(jnp.int32)

    x = lax.cond(jnp.max(c_ge) > kf, tie_limit, lambda: jnp.full((1, TQ), 2 * seq, jnp.int32))

    heads = range(nh)
    grp = range(DSA_GROUP)
    pairs = [slice((h // 2) * LANES, (h // 2 + 1) * LANES) for h in heads]
    n_full = i // DSA_GROUP

    def scores(jg):
        off = pl.multiple_of(jg * (DSA_GROUP * TK), DSA_GROUP * TK)
        return tuple(_dot(k_ref[pl.ds(off, DSA_GROUP * TK), pairs[h]], qs_ref[h]) for h in heads)

    def mask_biases(base, causal):
        biases = []
        for g in grp:
            key = key_ref[base + g]
            pos = (base + g) * TK + krow
            msk = (key > t) | ((key == t) & (pos <= x))
            if causal:
                msk = msk & (pos <= i * TQ + qcol)
            biases.append(jnp.where(msk, 0.0, NEG))
        return biases

    def attend_fixed(jg, _, causal):
        ss = scores(jg)
        ss_next = 0
        base = jg * DSA_GROUP
        biases = mask_biases(base, causal)
        for h in heads:
            pv = None
            for g in grp:
                p = jnp.exp2((ss[h][g * TK:(g + 1) * TK, :] - m_fix[h]) + biases[g]).astype(BF16)
                d = _dot(vt_ref[0, base + g, h * LANES:(h + 1) * LANES, :], p)
                pv = d if pv is None else pv + d
            acc_ref[h] += pv
        return ss_next

    def attend(jg, carry, causal):
        ms, ss = carry
        ss_next = None if causal else scores(jg + 1)
        base = jg * DSA_GROUP
        biases = mask_biases(base, causal)
        sg = [[ss[h][g * TK:(g + 1) * TK, :] for g in grp] for h in heads]
        new = []
        for h in heads:
            m_new = ms[h]
            for g in grp:
                m_new = jnp.maximum(m_new, jnp.max(sg[h][g] + biases[g], axis=0, keepdims=True))
            new.append(m_new)
        ps = [[jnp.exp2((sg[h][g] - new[h]) + biases[g]).astype(BF16) for g in grp] for h in heads]
        for h in heads:
            pv = None
            for g in grp:
                d = _dot(vt_ref[0, base + g, h * LANES:(h + 1) * LANES, :], ps[h][g])
                pv = d if pv is None else pv + d
            acc_ref[h] = jnp.exp2(ms[h] - new[h]) * acc_ref[h] + pv
        return tuple(new), ss_next

    for h in heads:
        acc_ref[h] = jnp.zeros((LANES, TQ), F32)

    def run_online():
        carry = lax.fori_loop(0, n_full, lambda jg, c: attend(jg, c, False),
                              (tuple(jnp.full((1, TQ), NEG, F32) for _ in heads), scores(0)))
        attend(n_full, carry, True)
        return 0

    def run_fixed():
        lax.fori_loop(0, n_full, lambda jg, c: attend_fixed(jg, c, False), 0)
        attend_fixed(n_full, 0, True)
        return 0

    lax.cond(bound_max <= SAFE_SHIFT, run_fixed, run_online)

    out_t = jnp.concatenate([_finish_softmax_head(acc_ref[h]) for h in range(nh)], axis=0)
    o_ref[...] = out_t.T


def _dsa_call(f_arr, h_arr, vt, batch, seq):
    nq = seq // TQ
    wb = N_HEADS_B * HEAD_DIM
    return pl.pallas_call(
        _dsa_kernel,
        grid=(batch, nq),
        in_specs=[pl.BlockSpec((TQ, IDX_HEADS * IDX_DIM), lambda b, i: (b * nq + i, H_QI // (IDX_HEADS * IDX_DIM))),
                  pl.BlockSpec((TQ, LANES), lambda b, i: (b * nq + i, F_WI // LANES)),
                  pl.BlockSpec((seq, LANES), lambda b, i: (b, H_KI // LANES)),
                  pl.BlockSpec((TQ, wb), lambda b, i: (b * nq + i, F_QB // wb)),
                  pl.BlockSpec((seq, wb), lambda b, i: (b, H_KB // wb)),
                  pl.BlockSpec((1, seq // TK, N_HEADS_B * LANES, TK), lambda b, i: (b, 0, VT_B // (N_HEADS_B * LANES), 0))],
        out_specs=pl.BlockSpec((TQ, wb), lambda b, i: (b * nq + i, 0)),
        out_shape=jax.ShapeDtypeStruct((batch * seq, wb), F32),
        scratch_shapes=[pltpu.VMEM((seq // TK, TK, TQ), jnp.int32),
                        pltpu.VMEM((seq // TK, TK, TQ), BF16),
                        pltpu.VMEM((IDX_HEADS, LANES, TQ), BF16),
                        pltpu.VMEM((N_HEADS_B, LANES, TQ), BF16),
                        pltpu.VMEM((N_HEADS_B, LANES, TQ), F32),
                        pltpu.VMEM((seq // TK, TQ), F32),
                        pltpu.VMEM((TK, TQ), F32),
                        pltpu.VMEM((1, wb), F32)],
        compiler_params=_cparams(("parallel", "arbitrary")),
        name="dsa_attn",
    )(h_arr, f_arr, h_arr, f_arr, h_arr, vt)


def _outproj_kernel(x_ref, oa_ref, ob_ref, oc_ref, og_ref, gt_ref, w_ref, o_ref):
    wa = N_HEADS_A * HEAD_DIM
    wb = wa + N_HEADS_B * HEAD_DIM
    og = og_ref[...]
    y = (_dot((oa_ref[...] * og[:, :wa]).astype(BF16), w_ref[:wa, :])
         + _dot((ob_ref[...] * og[:, wa:wb]).astype(BF16), w_ref[wa:wb, :])
         + _dot((oc_ref[...] * og[:, wb:]).astype(BF16), w_ref[wb:, :]))
    o_ref[...] = x_ref[...] + gt_ref[0] * y


def _outproj_call(x2, oa, ob, oc, og, gt, w_out, seq):
    m, d = x2.shape
    tm = 512
    row = lambda i: (i, 0)
    return pl.pallas_call(
        _outproj_kernel,
        grid=(m // tm,),
        in_specs=[pl.BlockSpec((tm, d), row),
                  pl.BlockSpec((tm, oa.shape[1]), row),
                  pl.BlockSpec((tm, ob.shape[1]), row),
                  pl.BlockSpec((tm, oc.shape[1]), row),
                  pl.BlockSpec((1, d), lambda i: (0, 0)),
                  pl.BlockSpec((1, 1, d), lambda i: ((i * tm) // seq, 0, 0)),
                  pl.BlockSpec((d, d), lambda i: (0, 0))],
        out_specs=pl.BlockSpec((tm, d), row),
        out_shape=jax.ShapeDtypeStruct((m, d), F32),
        compiler_params=_cparams(("parallel",)),
        name="out_proj",
    )(x2, oa, ob, oc, og, gt, w_out)


def _rope_tables(seq):
    pos = jnp.arange(seq, dtype=F32)
    inv = ROPE_THETA ** (-jnp.arange(0, ROPE_DIM, 2, dtype=F32) / ROPE_DIM)
    ang = pos[:, None] * inv[None, :]
    cos, sin = jnp.cos(ang), jnp.sin(ang)
    zeros = jnp.zeros((seq, HEAD_DIM - ROPE_DIM), F32)
    zh = jnp.zeros((seq, ROPE_HALF), F32)
    cos_h = jnp.concatenate([cos, cos, jnp.ones_like(zeros)], axis=1)
    s1_h = jnp.concatenate([-sin, zh, zeros], axis=1)
    s2_h = jnp.concatenate([zh, sin, zeros], axis=1)
    two = lambda t: jnp.concatenate([t, t], axis=1)
    return two(cos_h), two(s1_h), two(s2_h)


def _permute_w_in(w):
    d = w.shape[0]
    a, b, c = N_HEADS_A * HEAD_DIM, N_HEADS_B * HEAD_DIM, N_HEADS_C * HEAD_DIM
    sizes = (a, a, a, b, b, b, IDX_HEADS * IDX_DIM, IDX_DIM, IDX_HEADS, c, c, c)
    offs = np.concatenate([[0], np.cumsum(sizes)])
    qa, ka, va, qb, kb, vb, qi, ki, wi, qc, kc, vc = [w[:, int(offs[k]):int(offs[k + 1])] for k in range(12)]
    z = lambda n: jnp.zeros((d, n), w.dtype)
    out = jnp.concatenate([qa, qb, qc, ka, va, kb, vb, kc, vc, qi,
                           ki, z(LANES - IDX_DIM), wi, z(LANES - IDX_HEADS)], axis=1)
    assert out.shape[1] == W_PERM_WIDTH
    return out.astype(BF16)


def _suffix_sum_matrix():
    s = np.arange(TK + SB_TAIL)[:, None]
    j = np.arange(TK)[None, :]
    return jnp.asarray((j > s) | (s >= TK), BF16)


def kernel(x, c, w_ada, b_ada, norm_g, w_in, qk_g, out_g, w_out, ffn_w1, ffn_w3, ffn_w2):
    batch, seq, d = x.shape
    depth = w_ada.shape[0]
    assert seq % PROJ_TM == 0 and d % LANES == 0 and batch <= 8
    assert (seq // TK) % MOBA_GROUP == 0 and (seq // TK) % DSA_GROUP == 0

    c_pad = jnp.zeros((8, d), F32).at[:batch].set(c)
    mod = _mod_call(c_pad, w_ada, b_ada)
    cos_t, s1_t, s2_t = _rope_tables(seq)
    hd = np.arange(LANES) // HEAD_DIM
    bd = jnp.asarray(hd[:, None] == hd[None, :], BF16)
    u_mat = _suffix_sum_matrix()

    x2 = x.reshape(batch * seq, d)
    for layer in range(depth):
        mods = [mod[layer, :batch, k * d:(k + 1) * d].reshape(batch, 1, d) for k in range(N_MOD)]
        sh1, sc1, g1, sh2, sc2, g2, sh3, sc3, g3 = mods
        ng = norm_g[layer]
        w1 = ffn_w1[layer].astype(BF16)
        w3 = ffn_w3[layer].astype(BF16)
        w2 = ffn_w2[layer].astype(BF16)

        x2 = _ffn_call(x2, ng[0:1], sh1, sc1, g1, w1[0], w3[0], w2[0], seq)

        qkg = jnp.concatenate([jnp.tile(qk_g[layer], (1, 2)), jnp.ones((4, LANES), F32)], axis=0)
        f_arr, h_arr, kmean, vt = _proj_call(x2, ng[1:2], sh2, sc2, _permute_w_in(w_in[layer]), qkg,
                                             cos_t, s1_t, s2_t, bd, batch, seq)
        per_tile = PROJ_TM // MOBA_BLOCK
        kmean = kmean[:, :per_tile, :].reshape(batch, seq // MOBA_BLOCK, N_HEADS_A * HEAD_DIM)
        oa = _moba_call(f_arr, h_arr, kmean, vt, batch, seq)
        ob = _dsa_call(f_arr, h_arr, vt, batch, seq)
        oc = _sb_call(f_arr, h_arr, vt, u_mat, batch, seq)
        x2 = _outproj_call(x2, oa, ob, oc, out_g[layer].reshape(1, d), g2, w_out[layer].astype(BF16), seq)

        x2 = _ffn_call(x2, ng[2:3], sh3, sc3, g3, w1[1], w3[1], w2[1], seq)
    return x2.reshape(batch, seq, d)
```

```python
import jax
import jax.numpy as jnp
import numpy as np
from jax import lax
from jax.experimental import pallas as pl
from jax.experimental.pallas import tpu as pltpu

F32 = jnp.float32
BF16 = jnp.bfloat16

HEAD_DIM = 64
N_HEADS_A = 4
N_HEADS_B = 4
N_HEADS_C = 8
ROPE_DIM = HEAD_DIM // 4
ROPE_HALF = ROPE_DIM // 2
ROPE_THETA = 500000.0
MOBA_BLOCK = 256
MOBA_TOPK = 3
DSA_TOPK = 256
IDX_HEADS = 8
IDX_DIM = 64
N_MOD = 9
RMS_EPS = 1e-6
NEG = -1e30
ATT_SCALE = HEAD_DIM ** -0.5
SOFTMAX_SCALE2 = ATT_SCALE * 1.4426950408889634

LANES = 128
MXU_N = 256
TQ = 256
TK = 256
PROJ_TM = 512
VMEM_LIMIT = 56 * 1024 * 1024

SB_DEAD_LOG = -88.0
SB_TAIL = 16
SB_HEADS = 8
MOBA_GROUP = 4
DSA_GROUP = 2
BOUND_SLACK = 1.02
SAFE_SHIFT = 40.0
BISECT_COARSE = 15
BISECT_FIXED = 8

F_QA, F_QB, F_QC, F_WI, F_WIDTH = 0, 256, 512, 1024, 1152
H_KA, H_KB, H_KC, H_QI, H_KI, H_WIDTH = 0, 256, 512, 1024, 1536, 1664
W_PERM_WIDTH = 15 * MXU_N
VT_A, VT_B, VT_C, VT_ROWS = 0, 512, 1024, 1536


def _dot(a, b):
    return jnp.dot(a, b, preferred_element_type=F32)


def _split3(x):
    a = x.astype(BF16)
    r = x - a.astype(F32)
    b = r.astype(BF16)
    c = (r - b.astype(F32)).astype(BF16)
    return a, b, c


def _cparams(sem):
    return pltpu.CompilerParams(dimension_semantics=sem, vmem_limit_bytes=VMEM_LIMIT)


def _mod_kernel(c_ref, w_ref, b_ref, o_ref):
    c = c_ref[...]
    sc = c * (1.0 / (1.0 + jnp.exp(-c)))
    a, b, c3 = _split3(sc)
    w = w_ref[0]
    wa, wb, wc = _split3(w)
    acc = _dot(a, wa) + (_dot(a, wb) + _dot(b, wa)) + (_dot(a, wc) + _dot(b, wb) + _dot(c3, wa))
    o_ref[0] = acc + b_ref[0]


def _mod_call(c_pad, w_ada, b_ada):
    depth, d, n = w_ada.shape
    tn = 1024
    rows = c_pad.shape[0]
    return pl.pallas_call(
        _mod_kernel,
        grid=(depth, n // tn),
        in_specs=[pl.BlockSpec((rows, d), lambda l, j: (0, 0)),
                  pl.BlockSpec((1, d, tn), lambda l, j: (l, 0, j)),
                  pl.BlockSpec((1, 1, tn), lambda l, j: (l, 0, j))],
        out_specs=pl.BlockSpec((1, rows, tn), lambda l, j: (l, 0, j)),
        out_shape=jax.ShapeDtypeStruct((depth, rows, n), F32),
        compiler_params=_cparams(("parallel", "parallel")),
        name="adaln_mod",
    )(c_pad, w_ada, b_ada.reshape(depth, 1, n))


def _norm_modulate(x, ng, sh, sc):
    ms = jnp.mean(x * x, axis=-1, keepdims=True)
    h = x * lax.rsqrt(ms + RMS_EPS) * ng
    return h * (1.0 + sc) + sh


def _ffn_kernel(x_ref, ng_ref, sh_ref, sc_ref, gt_ref, w1_ref, w3_ref, w2_ref, o_ref, h_ref, acc_ref):
    j = pl.program_id(1)

    @pl.when(j == 0)
    def _():
        h = _norm_modulate(x_ref[...], ng_ref[...], sh_ref[0], sc_ref[0])
        h_ref[...] = h.astype(BF16)
        acc_ref[...] = jnp.zeros_like(acc_ref)

    h = h_ref[...]
    tf = w1_ref.shape[1]
    part = None
    for c0 in range(0, tf, MXU_N):
        c1 = min(c0 + MXU_N, tf)
        a = _dot(h, w1_ref[:, c0:c1])
        b = _dot(h, w3_ref[:, c0:c1])
        u = (a * (1.0 / (1.0 + jnp.exp(-a))) * b).astype(BF16)
        d = _dot(u, w2_ref[c0:c1, :])
        part = d if part is None else part + d
    acc_ref[...] += part

    @pl.when(j == pl.num_programs(1) - 1)
    def _():
        o_ref[...] = x_ref[...] + 0.5 * gt_ref[0] * acc_ref[...]


def _ffn_call(x2, ng, sh, sc, gt, w1, w3, w2, seq):
    m, d = x2.shape
    dff = w1.shape[1]
    tm = 1024
    tf = dff // 2 if (dff // 2) % LANES == 0 else MXU_N
    bidx = lambda i, j: ((i * tm) // seq, 0, 0)
    return pl.pallas_call(
        _ffn_kernel,
        grid=(m // tm, dff // tf),
        in_specs=[pl.BlockSpec((tm, d), lambda i, j: (i, 0)),
                  pl.BlockSpec((1, d), lambda i, j: (0, 0)),
                  pl.BlockSpec((1, 1, d), bidx),
                  pl.BlockSpec((1, 1, d), bidx),
                  pl.BlockSpec((1, 1, d), bidx),
                  pl.BlockSpec((d, tf), lambda i, j: (0, j)),
                  pl.BlockSpec((d, tf), lambda i, j: (0, j)),
                  pl.BlockSpec((tf, d), lambda i, j: (j, 0))],
        out_specs=pl.BlockSpec((tm, d), lambda i, j: (i, 0)),
        out_shape=jax.ShapeDtypeStruct((m, d), F32),
        scratch_shapes=[pltpu.VMEM((tm, d), BF16), pltpu.VMEM((tm, d), F32)],
        compiler_params=_cparams(("parallel", "arbitrary")),
        name="swiglu_ffn",
    )(x2, ng, sh, sc, gt, w1, w3, w2)


def _proj_kernel(x_ref, ng_ref, sh_ref, sc_ref, w_ref, qkg_ref, cos_ref, s1_ref, s2_ref, bd_ref,
                 f_ref, h_ref, km_ref, vt_ref, hs_ref):
    hs_ref[...] = _norm_modulate(x_ref[...], ng_ref[...], sh_ref[0], sc_ref[0]).astype(BF16)
    cos, s1, s2 = cos_ref[...], s1_ref[...], s2_ref[...]
    bd = bd_ref[...]
    tm = x_ref.shape[0]

    def rope(v):
        return v * cos + pltpu.roll(v, LANES - ROPE_HALF, 1) * s1 + pltpu.roll(v, ROPE_HALF, 1) * s2

    def headnorm(v, g):
        a, b, _ = _split3(v * v)
        ss = _dot(a, bd) + _dot(b, bd)
        return v * lax.rsqrt(ss * (1.0 / HEAD_DIM) + RMS_EPS) * g

    def chunk(c):
        y = _dot(hs_ref[...], w_ref[:, c * MXU_N:(c + 1) * MXU_N])
        return y[:, :LANES], y[:, LANES:]

    def normrope_chunk(c, grow):
        g = qkg_ref[grow:grow + 1, :]
        return [rope(headnorm(v, g)) for v in chunk(c)]

    def put(ref, off, halves, dtype):
        for k, v in enumerate(halves):
            ref[:, off + k * LANES: off + (k + 1) * LANES] = v.astype(dtype)

    def put_vt(base, halves, with_ones):
        ones = jnp.ones((HEAD_DIM, TK), BF16)
        for r in range(tm // TK):
            for k, v in enumerate(halves):
                t = v[r * TK:(r + 1) * TK, :].T.astype(BF16)
                if with_ones:
                    for hh in range(2):
                        row = base + (2 * k + hh) * LANES
                        vt_ref[0, r, row:row + HEAD_DIM, :] = t[hh * HEAD_DIM:(hh + 1) * HEAD_DIM, :]
                        vt_ref[0, r, row + HEAD_DIM:row + LANES, :] = ones
                else:
                    vt_ref[0, r, base + k * LANES:base + (k + 1) * LANES, :] = t

    put(f_ref, F_QA, normrope_chunk(0, 0), F32)
    put(f_ref, F_QB, normrope_chunk(1, 2), F32)
    put(f_ref, F_QC, chunk(2), F32)
    put(f_ref, F_QC + MXU_N, chunk(3), F32)

    ka = normrope_chunk(4, 1)
    put(h_ref, H_KA, ka, BF16)
    rows = lax.broadcasted_iota(jnp.int32, (8, LANES), 0)
    for k, v in enumerate(ka):
        km = jnp.zeros((8, LANES), F32)
        for r in range(tm // MOBA_BLOCK):
            s = jnp.sum(v[r * MOBA_BLOCK:(r + 1) * MOBA_BLOCK, :], axis=0, keepdims=True) * (1.0 / MOBA_BLOCK)
            km = jnp.where(rows == r, s, km)
        km_ref[0, :, k * LANES:(k + 1) * LANES] = km

    put_vt(VT_A, chunk(5), True)
    put(h_ref, H_KB, normrope_chunk(6, 3), BF16)
    put_vt(VT_B, chunk(7), True)
    put(h_ref, H_KC, chunk(8), BF16)
    put(h_ref, H_KC + MXU_N, chunk(9), BF16)
    put_vt(VT_C, chunk(10), False)
    put_vt(VT_C + MXU_N, chunk(11), False)
    put(h_ref, H_QI, [rope(v) for v in chunk(12)], BF16)
    put(h_ref, H_QI + MXU_N, [rope(v) for v in chunk(13)], BF16)
    ki, wi = chunk(14)
    h_ref[:, H_KI:H_KI + LANES] = rope(ki).astype(BF16)
    f_ref[:, F_WI:F_WI + LANES] = wi


def _proj_call(x2, ng, sh, sc, w_perm, qkg, cos_t, s1_t, s2_t, bd, batch, seq):
    m, d = x2.shape
    tm = PROJ_TM
    nt = seq // tm
    bidx = lambda i: ((i * tm) // seq, 0, 0)
    tab = pl.BlockSpec((tm, LANES), lambda i: (i % nt, 0))
    return pl.pallas_call(
        _proj_kernel,
        grid=(m // tm,),
        in_specs=[pl.BlockSpec((tm, d), lambda i: (i, 0)),
                  pl.BlockSpec((1, d), lambda i: (0, 0)),
                  pl.BlockSpec((1, 1, d), bidx),
                  pl.BlockSpec((1, 1, d), bidx),
                  pl.BlockSpec((d, W_PERM_WIDTH), lambda i: (0, 0)),
                  pl.BlockSpec((8, LANES), lambda i: (0, 0)),
                  tab, tab, tab,
                  pl.BlockSpec((LANES, LANES), lambda i: (0, 0))],
        out_specs=[pl.BlockSpec((tm, F_WIDTH), lambda i: (i, 0)),
                   pl.BlockSpec((tm, H_WIDTH), lambda i: (i, 0)),
                   pl.BlockSpec((1, 8, 2 * LANES), lambda i: (i, 0, 0)),
                   pl.BlockSpec((1, tm // TK, VT_ROWS, TK), lambda i: (i // nt, i % nt, 0, 0))],
        out_shape=[jax.ShapeDtypeStruct((m, F_WIDTH), F32),
                   jax.ShapeDtypeStruct((m, H_WIDTH), BF16),
                   jax.ShapeDtypeStruct((m // tm, 8, 2 * LANES), F32),
                   jax.ShapeDtypeStruct((batch, seq // TK, VT_ROWS, TK), BF16)],
        scratch_shapes=[pltpu.VMEM((tm, d), BF16)],
        compiler_params=_cparams(("parallel",)),
        name="in_proj",
    )(x2, ng, sh, sc, w_perm, qkg, cos_t, s1_t, s2_t, bd)


def _head_masks():
    lane = lax.broadcasted_iota(jnp.int32, (1, LANES), 1)
    return [(lane >= HEAD_DIM * h) & (lane < HEAD_DIM * (h + 1)) for h in range(2)]


def _head_rms_t(o):
    ss = jnp.sum(o * o, axis=0, keepdims=True) * (1.0 / HEAD_DIM)
    return o * lax.rsqrt(ss + RMS_EPS)


def _finish_softmax_head(acc):
    return _head_rms_t(acc[:HEAD_DIM, :] / acc[HEAD_DIM:HEAD_DIM + 1, :])


def _moba_kernel(q_ref, km_ref, k_ref, vt_ref, o_ref, bias_ref, qs_ref, acc_ref, kmax2_ref):
    i = pl.program_id(2)
    nb = km_ref.shape[1]
    q = q_ref[...]
    kma, kmb, _ = _split3(km_ref[0])
    blk = lax.broadcasted_iota(jnp.int32, (nb, TQ), 0).astype(F32)
    i_f = i.astype(F32)
    krow = lax.broadcasted_iota(jnp.int32, (TK, TQ), 0)
    qcol = lax.broadcasted_iota(jnp.int32, (TK, TQ), 1)
    causal = krow <= qcol

    q_norm2 = []
    for h, hm in enumerate(_head_masks()):
        qh_t = jnp.where(hm, q, 0.0).T
        qa, qb, _ = _split3(qh_t)
        gate = _dot(kma, qa) + (_dot(kmb, qa) + _dot(kma, qb))
        gate = jnp.where(blk < i_f, gate, NEG)
        sel = jnp.zeros((nb, TQ), F32)
        for _ in range(MOBA_TOPK):
            mx = jnp.max(gate, axis=0, keepdims=True)
            idx = jnp.min(jnp.where(gate == mx, blk, float(nb)), axis=0, keepdims=True)
            pick = blk == idx
            sel = jnp.where(pick, 1.0, sel)
            gate = jnp.where(pick, -jnp.inf, gate)
        bias_ref[h] = jnp.where((sel > 0.5) & (blk < i_f), 0.0, NEG)
        qs_t = qh_t * SOFTMAX_SCALE2
        qs_ref[h] = qs_t.astype(BF16)
        q_norm2.append(jnp.sum(qs_t * qs_t, axis=0, keepdims=True))

    @pl.when(i == 0)
    def _():
        head_of = lambda a: lax.shift_right_logical(lax.broadcasted_iota(jnp.int32, (LANES, LANES), a), 6)
        ones_bd = jnp.where(head_of(0) == head_of(1), 1.0, 0.0).astype(BF16)

        def norm_body(j, mx):
            kk = k_ref[pl.ds(pl.multiple_of(j * TK, TK), TK), :].astype(F32)
            return jnp.maximum(mx, jnp.max(_dot((kk * kk).astype(BF16), ones_bd), axis=0, keepdims=True))
        kmax2_ref[...] = lax.fori_loop(0, nb, norm_body, jnp.zeros((1, LANES), F32))

    m_fix = [jnp.sqrt(q_norm2[h] * kmax2_ref[0:1, h * HEAD_DIM:h * HEAD_DIM + 1] + 1e-30) * BOUND_SLACK
             for h in range(2)]

    own = pl.multiple_of(i * TK, TK)
    heads = range(2)
    grp = range(MOBA_GROUP)
    n_groups = (i + MOBA_GROUP - 1) // MOBA_GROUP
    last_group = km_ref.shape[1] // MOBA_GROUP - 1

    def scores(jg):
        off = pl.multiple_of(jnp.minimum(jg, last_group) * (MOBA_GROUP * TK), MOBA_GROUP * TK)
        kb = k_ref[pl.ds(off, MOBA_GROUP * TK), :]
        return tuple(_dot(kb, qs_ref[h]) for h in heads)

    def own_block(refs):
        k_own = k_ref[pl.ds(own, TK), :]
        used = []
        for h in heads:
            s = jnp.where(causal, _dot(k_own, qs_ref[h]), NEG)
            m0 = jnp.max(s, axis=0, keepdims=True) if refs is None else refs[h]
            acc_ref[h] = _dot(vt_ref[0, i, h * LANES:(h + 1) * LANES, :], jnp.exp2(s - m0).astype(BF16))
            used.append(m0)
        return tuple(used)

    def body_fixed(jg, c):
        ss = scores(jg)
        base = jg * MOBA_GROUP
        for h in heads:
            pv = None
            for g in grp:
                b = bias_ref[h, pl.ds(base + g, 1), :]
                p = jnp.exp2(ss[h][g * TK:(g + 1) * TK, :] - (m_fix[h] - b)).astype(BF16)
                d = _dot(vt_ref[0, base + g, h * LANES:(h + 1) * LANES, :], p)
                pv = d if pv is None else pv + d
            acc_ref[h] += pv
        return c

    def body(jg, carry):
        ms, ss = carry
        ss_next = scores(jg + 1)
        base = jg * MOBA_GROUP
        sg = [[ss[h][g * TK:(g + 1) * TK, :] for g in grp] for h in heads]
        bs = [[bias_ref[h, pl.ds(base + g, 1), :] for g in grp] for h in heads]
        new = []
        for h in heads:
            m_new = ms[h]
            for g in grp:
                m_new = jnp.maximum(m_new, jnp.max(sg[h][g], axis=0, keepdims=True) + bs[h][g])
            new.append(m_new)
        ps = [[jnp.exp2(sg[h][g] - (new[h] - bs[h][g])).astype(BF16) for g in grp] for h in heads]
        for h in heads:
            pv = None
            for g in grp:
                d = _dot(vt_ref[0, base + g, h * LANES:(h + 1) * LANES, :], ps[h][g])
                pv = d if pv is None else pv + d
            acc_ref[h] = jnp.exp2(ms[h] - new[h]) * acc_ref[h] + pv
        return tuple(new), ss_next

    def run_online():
        lax.fori_loop(0, n_groups, body, (own_block(None), scores(0)))
        return 0

    def run_fixed():
        own_block(m_fix)
        lax.fori_loop(0, n_groups, body_fixed, 0)
        return 0

    lax.cond(jnp.max(jnp.maximum(m_fix[0], m_fix[1])) <= SAFE_SHIFT, run_fixed, run_online)
    out_t = jnp.concatenate([_finish_softmax_head(acc_ref[h]) for h in range(2)], axis=0)
    o_ref[...] = out_t.T


def _moba_call(f_arr, h_arr, kmean, vt, batch, seq):
    nq = seq // TQ
    nb = seq // MOBA_BLOCK
    npair = N_HEADS_A // 2
    return pl.pallas_call(
        _moba_kernel,
        grid=(batch, npair, nq),
        in_specs=[pl.BlockSpec((TQ, LANES), lambda b, p, i: (b * nq + i, F_QA // LANES + p)),
                  pl.BlockSpec((1, nb, LANES), lambda b, p, i: (b, 0, p)),
                  pl.BlockSpec((seq, LANES), lambda b, p, i: (b, H_KA // LANES + p)),
                  pl.BlockSpec((1, seq // TK, 2 * LANES, TK), lambda b, p, i: (b, 0, VT_A // (2 * LANES) + p, 0))],
        out_specs=pl.BlockSpec((TQ, LANES), lambda b, p, i: (b * nq + i, p)),
        out_shape=jax.ShapeDtypeStruct((batch * seq, N_HEADS_A * HEAD_DIM), F32),
        scratch_shapes=[pltpu.VMEM((2, nb, TQ), F32),
                        pltpu.VMEM((2, LANES, TQ), BF16),
                        pltpu.VMEM((2, LANES, TQ), F32),
                        pltpu.VMEM((1, LANES), F32)],
        compiler_params=_cparams(("parallel", "parallel", "arbitrary")),
        name="moba_attn",
    )(f_arr, kmean, h_arr, vt)


def _sb_kernel(q_ref, k_ref, vt_ref, u_ref, o_ref, qs_ref, acc_ref):
    i = pl.program_id(2)
    krow = lax.broadcasted_iota(jnp.int32, (TK, TQ), 0)
    qcol = lax.broadcasted_iota(jnp.int32, (TK, TQ), 1)
    strict = krow < qcol
    hms = _head_masks()
    for h in range(SB_HEADS):
        pr = slice((h // 2) * LANES, (h // 2 + 1) * LANES)
        qs_ref[h] = (jnp.where(hms[h % 2], q_ref[:, pr], 0.0) * ATT_SCALE).T.astype(BF16)
        acc_ref[h] = jnp.zeros((LANES, TQ), F32)

    heads = range(SB_HEADS)
    pairs = [slice((h // 2) * LANES, (h // 2 + 1) * LANES) for h in heads]

    def logits(j):
        off = pl.multiple_of(jnp.maximum(j, 0) * TK, TK)
        return tuple(_dot(k_ref[pl.ds(off, TK), pairs[h]], qs_ref[h]) for h in heads)

    def block(j, rs, diag):
        zs = logits(j)
        u = u_ref[...]
        log_betas, his, los = [], [], []
        for h in heads:
            z = zs[h]
            log_beta = jnp.minimum(z, 0.0) - jnp.log(1.0 + jnp.exp(-jnp.abs(z)))
            log_1m = log_beta - z
            if diag:
                log_1m = jnp.where(strict, log_1m, 0.0)
            hi = log_1m.astype(BF16)
            log_betas.append(log_beta)
            his.append(hi)
            los.append((log_1m - hi.astype(F32)).astype(BF16))
        sums = [_dot(u, his[h]) + _dot(u, los[h]) for h in heads]
        weights = []
        for h in heads:
            after = sums[h][:TK, :] + rs[h]
            a = jnp.exp(log_betas[h] + after)
            if diag:
                a = jnp.where(strict, a, 0.0)
            weights.append(a.astype(BF16))
        for h in heads:
            acc_ref[h] += _dot(vt_ref[0, j, pairs[h], :], weights[h])
        return tuple(rs[h] + sums[h][TK:TK + 1, :] for h in heads)

    zero = jnp.zeros((1, TQ), F32)
    rs = block(i, (zero,) * SB_HEADS, True)

    def alive(rs):
        m = rs[0]
        for r in rs[1:]:
            m = jnp.maximum(m, r)
        return jnp.max(m)

    def cond(c):
        return (c[0] >= 0) & (c[1] > SB_DEAD_LOG)

    def body(c):
        rs = block(c[0], c[2], False)
        return c[0] - 1, alive(rs), rs

    lax.while_loop(cond, body, (i - 1, alive(rs), rs))
    out_t = jnp.concatenate(
        [_head_rms_t(acc_ref[h][(h % 2) * HEAD_DIM:(h % 2 + 1) * HEAD_DIM, :]) for h in range(SB_HEADS)], axis=0)
    o_ref[...] = out_t.T


def _sb_call(f_arr, h_arr, vt, u_mat, batch, seq):
    nq = seq // TQ
    w = SB_HEADS * HEAD_DIM
    ngrp = N_HEADS_C // SB_HEADS
    return pl.pallas_call(
        _sb_kernel,
        grid=(batch, ngrp, nq),
        in_specs=[pl.BlockSpec((TQ, w), lambda b, p, i: (b * nq + i, F_QC // w + p)),
                  pl.BlockSpec((seq, w), lambda b, p, i: (b, H_KC // w + p)),
                  pl.BlockSpec((1, seq // TK, w, TK), lambda b, p, i: (b, 0, VT_C // w + p, 0)),
                  pl.BlockSpec((TK + SB_TAIL, TK), lambda b, p, i: (0, 0))],
        out_specs=pl.BlockSpec((TQ, w), lambda b, p, i: (b * nq + i, p)),
        out_shape=jax.ShapeDtypeStruct((batch * seq, N_HEADS_C * HEAD_DIM), F32),
        scratch_shapes=[pltpu.VMEM((SB_HEADS, LANES, TQ), BF16), pltpu.VMEM((SB_HEADS, LANES, TQ), F32)],
        compiler_params=_cparams(("parallel", "parallel", "arbitrary")),
        name="stickbreak_attn",
    )(f_arr, h_arr, vt, u_mat)


_NEG_BITS = int(np.array(NEG, np.float32).view(np.int32))
NEG_KEY = _NEG_BITS ^ 0x7FFFFFFF
INT_MIN = -2 ** 31
MIN_NORMAL_BITS = 0x00800000


def _dsa_kernel(qi_ref, wi_ref, ki_ref, q_ref, k_ref, vt_ref, o_ref,
                key_ref, coarse_ref, qx_ref, qs_ref, acc_ref, cnt_ref, m_ref, kmax2_ref):
    i = pl.program_id(1)
    seq = k_ref.shape[0]
    topk = min(DSA_TOPK, seq // 4)
    nh = N_HEADS_B
    lane = lax.broadcasted_iota(jnp.int32, (1, LANES), 1)
    low = lane < HEAD_DIM
    krow = lax.broadcasted_iota(jnp.int32, (TK, TQ), 0)
    qcol = lax.broadcasted_iota(jnp.int32, (TK, TQ), 1)
    diag_causal = krow <= qcol
    hms = _head_masks()

    qi = qi_ref[...].astype(F32)
    for h in range(IDX_HEADS):
        qp = qi[:, (h // 2) * LANES:(h // 2 + 1) * LANES]
        if h % 2:
            qp = pltpu.roll(qp, HEAD_DIM, 1)
        qx_ref[h] = jnp.where(low, qp, 0.0).T.astype(BF16)
    w_t = wi_ref[...].T
    q = q_ref[...]
    q_norm2 = []
    for h in range(nh):
        qh = jnp.where(hms[h % 2], q[:, (h // 2) * LANES:(h // 2 + 1) * LANES], 0.0)
        qh_t = (qh * SOFTMAX_SCALE2).T
        qs_ref[h] = qh_t.astype(BF16)
        q_norm2.append(jnp.sum(qh_t * qh_t, axis=0, keepdims=True))

    @pl.when(i == 0)
    def _():
        wk = k_ref.shape[1]
        head_of = lambda a: lax.shift_right_logical(lax.broadcasted_iota(jnp.int32, (wk, wk), a), 6)
        ones_bd = jnp.where(head_of(0) == head_of(1), 1.0, 0.0).astype(BF16)

        def body(j, mx):
            kk = k_ref[pl.ds(pl.multiple_of(j * TK, TK), TK), :].astype(F32)
            return jnp.maximum(mx, jnp.max(_dot((kk * kk).astype(BF16), ones_bd), axis=0, keepdims=True))
        kmax2_ref[...] = lax.fori_loop(0, seq // TK, body, jnp.zeros((1, wk), F32))

    m_fix = [jnp.sqrt(q_norm2[h] * kmax2_ref[0:1, h * HEAD_DIM:h * HEAD_DIM + 1] + 1e-30) * BOUND_SLACK
             for h in range(nh)]
    bound_max = jnp.max(jnp.maximum(jnp.maximum(m_fix[0], m_fix[1]), jnp.maximum(m_fix[2], m_fix[3])))

    def score_block(j, diag):
        off = pl.multiple_of(j * TK, TK)
        kz = ki_ref[pl.ds(off, TK), :]
        sc = jnp.zeros((TK, TQ), F32)
        for h in range(IDX_HEADS):
            sc = sc + w_t[h:h + 1, :] * jnp.maximum(_dot(kz, qx_ref[h]), 0.0)
        if diag:
            sc = jnp.where(diag_causal, sc, NEG)
        bits = lax.bitcast_convert_type(sc, jnp.int32)
        bits = jnp.where(bits == INT_MIN, 0, bits)
        key_ref[j] = jnp.where(bits < 0, bits ^ jnp.int32(0x7FFFFFFF), bits)
        coarse_ref[j] = lax.bitcast_convert_type(bits & jnp.int32(-65536), F32).astype(BF16)

    def score_pair(jj, c):
        score_block(2 * jj, False)
        score_block(2 * jj + 1, False)
        return c

    lax.fori_loop(0, i // 2, score_pair, 0)

    @pl.when(i % 2 == 1)
    def _():
        score_block(i - 1, False)

    score_block(i, True)
    for g in range(1, DSA_GROUP):
        @pl.when(i % DSA_GROUP + g < DSA_GROUP)
        def _():
            key_ref[i + g] = jnp.full((TK, TQ), NEG_KEY, jnp.int32)

    n_unscanned = (seq - (i + 1) * TK).astype(F32)
    fold = TK // 4

    def count(pred):
        def body(j, acc):
            hit = pred(key_ref[j], j)
            for r in range(TK // fold):
                acc = jnp.where(hit[r * fold:(r + 1) * fold, :], acc + 1.0, acc)
            return acc
        acc = lax.fori_loop(0, i + 1, body, jnp.zeros((fold, TQ), F32))
        return jnp.sum(acc, axis=0, keepdims=True)

    def count_ge(cand):
        return count(lambda key, j: key >= cand) + jnp.where(cand <= NEG_KEY, n_unscanned, 0.0)

    kf = float(topk)

    def count_ge_coarse(cand):
        cand_bits = jnp.where(cand < 0, (cand ^ jnp.int32(0x7FFFFFFF)) & jnp.int32(-65536), cand)
        cand_bits = jnp.where((cand > 0) & (cand < MIN_NORMAL_BITS), MIN_NORMAL_BITS, cand_bits)
        cand_b = lax.bitcast_convert_type(cand_bits, F32).astype(BF16)
        one, zero = jnp.ones((), BF16), jnp.zeros((), BF16)

        def body(j, acc):
            hit = jnp.where(coarse_ref[j] >= cand_b, one, zero)
            return acc + ((hit[:fold] + hit[fold:2 * fold]) + (hit[2 * fold:3 * fold] + hit[3 * fold:]))
        acc = lax.fori_loop(0, i + 1, body, jnp.zeros((fold, TQ), BF16))
        return (jnp.sum(acc.astype(F32), axis=0, keepdims=True)
                + jnp.where(cand <= NEG_KEY, n_unscanned, 0.0))

    c0 = count_ge_coarse(jnp.zeros((1, TQ), jnp.int32))
    t0 = jnp.where(c0 >= kf, 0, INT_MIN).astype(jnp.int32)
    ct0 = jnp.where(c0 >= kf, c0, float(seq))

    def bit_step(b, t, ct, counter):
        cand = t | jnp.left_shift(jnp.int32(1), 30 - b)
        cc = counter(cand)
        ok = cc >= kf
        return jnp.where(ok, cand, t), jnp.where(ok, cc, ct)

    t, ct = lax.fori_loop(0, BISECT_COARSE, lambda b, c: bit_step(b, *c, count_ge_coarse), (t0, ct0))
    t, ct = lax.fori_loop(BISECT_COARSE, BISECT_COARSE + BISECT_FIXED,
                          lambda b, c: bit_step(b, *c, count_ge), (t, ct))

    zero_final = (c0 >= kf) & (count_ge_coarse(jnp.full((1, TQ), MIN_NORMAL_BITS, jnp.int32)) < kf)

    def pending(ct):
        return jnp.max(jnp.where((ct > kf) & jnp.logical_not(zero_final), 1.0, 0.0))

    def bit_body(c):
        t, ct = bit_step(c[0], c[2], c[3], count_ge)
        return c[0] + 1, pending(ct), t, ct

    _, _, t, c_ge = lax.while_loop(lambda c: (c[0] < 31) & (c[1] > 0.5), bit_body,
                                   (jnp.int32(BISECT_COARSE + BISECT_FIXED), pending(ct), t, ct))

    def tie_limit():
        need = kf - count_ge(t + 1)

        def block_ties(j, c):
            hit = key_ref[j] == t
            acc = jnp.zeros((fold, TQ), F32)
            for r in range(TK // fold):
                acc = jnp.where(hit[r * fold:(r + 1) * fold, :], acc + 1.0, acc)
            cnt_ref[pl.ds(j, 1), :] = jnp.sum(acc, axis=0, keepdims=True)
            return c

        lax.fori_loop(0, i + 1, block_ties, 0)
        nkb = cnt_ref.shape[0]
        cnt = jnp.where(lax.broadcasted_iota(jnp.int32, (nkb, TQ), 0) <= i, cnt_ref[...], 0.0)
        run = jnp.zeros((1, TQ), F32)
        blk_of = jnp.zeros((1, TQ), F32)
        before = jnp.zeros((1, TQ), F32)
        for j in range(nkb):
            run = run + cnt[j:j + 1, :]
            ahead = run < need
            blk_of = blk_of + jnp.where(ahead, 1.0, 0.0)
            before = jnp.where(ahead, run, before)
        blk_of = blk_of.astype(jnp.int32)

        def pick_block(j, c):
            m_ref[...] = jnp.where((blk_of == j) & (key_ref[j] == t), 1.0, m_ref[...])
            return c

        m_ref[...] = jnp.zeros_like(m_ref)
        lax.fori_loop(0, i + 1, pick_block, 0)
        tri = (lax.broadcasted_iota(jnp.int32, (TK, TK), 0) >= lax.broadcasted_iota(jnp.int32, (TK, TK), 1))
        upto = _dot(jnp.where(tri, 1.0, 0.0).astype(BF16), m_ref[...].astype(BF16))
        row_of = jnp.sum(jnp.where(upto < need - before, 1.0, 0.0), axis=0, keepdims=True)
        return blk_of * TK + row_of.astype(jnp.int32)

    x = lax.cond(jnp.max(c_ge) > kf, tie_limit, lambda: jnp.full((1, TQ), 2 * seq, jnp.int32))

    heads = range(nh)
    grp = range(DSA_GROUP)
    pairs = [slice((h // 2) * LANES, (h // 2 + 1) * LANES) for h in heads]
    n_full = i // DSA_GROUP

    def scores(jg):
        off = pl.multiple_of(jg * (DSA_GROUP * TK), DSA_GROUP * TK)
        return tuple(_dot(k_ref[pl.ds(off, DSA_GROUP * TK), pairs[h]], qs_ref[h]) for h in heads)

    def mask_biases(base, causal):
        biases = []
        for g in grp:
            key = key_ref[base + g]
            pos = (base + g) * TK + krow
            msk = (key > t) | ((key == t) & (pos <= x))
            if causal:
                msk = msk & (pos <= i * TQ + qcol)
            biases.append(jnp.where(msk, 0.0, NEG))
        return biases

    def attend_fixed(jg, _, causal):
        ss = scores(jg)
        ss_next = 0
        base = jg * DSA_GROUP
        biases = mask_biases(base, causal)
        for h in heads:
            pv = None
            for g in grp:
                p = jnp.exp2((ss[h][g * TK:(g + 1) * TK, :] - m_fix[h]) + biases[g]).astype(BF16)
                d = _dot(vt_ref[0, base + g, h * LANES:(h + 1) * LANES, :], p)
                pv = d if pv is None else pv + d
            acc_ref[h] += pv
        return ss_next

    def attend(jg, carry, causal):
        ms, ss = carry
        ss_next = None if causal else scores(jg + 1)
        base = jg * DSA_GROUP
        biases = mask_biases(base, causal)
        sg = [[ss[h][g * TK:(g + 1) * TK, :] for g in grp] for h in heads]
        new = []
        for h in heads:
            m_new = ms[h]
            for g in grp:
                m_new = jnp.maximum(m_new, jnp.max(sg[h][g] + biases[g], axis=0, keepdims=True))
            new.append(m_new)
        ps = [[jnp.exp2((sg[h][g] - new[h]) + biases[g]).astype(BF16) for g in grp] for h in heads]
        for h in heads:
            pv = None
            for g in grp:
                d = _dot(vt_ref[0, base + g, h * LANES:(h + 1) * LANES, :], ps[h][g])
                pv = d if pv is None else pv + d
            acc_ref[h] = jnp.exp2(ms[h] - new[h]) * acc_ref[h] + pv
        return tuple(new), ss_next

    for h in heads:
        acc_ref[h] = jnp.zeros((LANES, TQ), F32)

    def run_online():
        carry = lax.fori_loop(0, n_full, lambda jg, c: attend(jg, c, False),
                              (tuple(jnp.full((1, TQ), NEG, F32) for _ in heads), scores(0)))
        attend(n_full, carry, True)
        return 0

    def run_fixed():
        lax.fori_loop(0, n_full, lambda jg, c: attend_fixed(jg, c, False), 0)
        attend_fixed(n_full, 0, True)
        return 0

    lax.cond(bound_max <= SAFE_SHIFT, run_fixed, run_online)

    out_t = jnp.concatenate([_finish_softmax_head(acc_ref[h]) for h in range(nh)], axis=0)
    o_ref[...] = out_t.T


def _dsa_call(f_arr, h_arr, vt, batch, seq):
    nq = seq // TQ
    wb = N_HEADS_B * HEAD_DIM
    return pl.pallas_call(
        _dsa_kernel,
        grid=(batch, nq),
        in_specs=[pl.BlockSpec((TQ, IDX_HEADS * IDX_DIM), lambda b, i: (b * nq + i, H_QI // (IDX_HEADS * IDX_DIM))),
                  pl.BlockSpec((TQ, LANES), lambda b, i: (b * nq + i, F_WI // LANES)),
                  pl.BlockSpec((seq, LANES), lambda b, i: (b, H_KI // LANES)),
                  pl.BlockSpec((TQ, wb), lambda b, i: (b * nq + i, F_QB // wb)),
                  pl.BlockSpec((seq, wb), lambda b, i: (b, H_KB // wb)),
                  pl.BlockSpec((1, seq // TK, N_HEADS_B * LANES, TK), lambda b, i: (b, 0, VT_B // (N_HEADS_B * LANES), 0))],
        out_specs=pl.BlockSpec((TQ, wb), lambda b, i: (b * nq + i, 0)),
        out_shape=jax.ShapeDtypeStruct((batch * seq, wb), F32),
        scratch_shapes=[pltpu.VMEM((seq // TK, TK, TQ), jnp.int32),
                        pltpu.VMEM((seq // TK, TK, TQ), BF16),
                        pltpu.VMEM((IDX_HEADS, LANES, TQ), BF16),
                        pltpu.VMEM((N_HEADS_B, LANES, TQ), BF16),
                        pltpu.VMEM((N_HEADS_B, LANES, TQ), F32),
                        pltpu.VMEM((seq // TK, TQ), F32),
                        pltpu.VMEM((TK, TQ), F32),
                        pltpu.VMEM((1, wb), F32)],
        compiler_params=_cparams(("parallel", "arbitrary")),
        name="dsa_attn",
    )(h_arr, f_arr, h_arr, f_arr, h_arr, vt)


def _outproj_kernel(x_ref, oa_ref, ob_ref, oc_ref, og_ref, gt_ref, w_ref, o_ref):
    wa = N_HEADS_A * HEAD_DIM
    wb = wa + N_HEADS_B * HEAD_DIM
    og = og_ref[...]
    y = (_dot((oa_ref[...] * og[:, :wa]).astype(BF16), w_ref[:wa, :])
         + _dot((ob_ref[...] * og[:, wa:wb]).astype(BF16), w_ref[wa:wb, :])
         + _dot((oc_ref[...] * og[:, wb:]).astype(BF16), w_ref[wb:, :]))
    o_ref[...] = x_ref[...] + gt_ref[0] * y


def _outproj_call(x2, oa, ob, oc, og, gt, w_out, seq):
    m, d = x2.shape
    tm = 512
    row = lambda i: (i, 0)
    return pl.pallas_call(
        _outproj_kernel,
        grid=(m // tm,),
        in_specs=[pl.BlockSpec((tm, d), row),
                  pl.BlockSpec((tm, oa.shape[1]), row),
                  pl.BlockSpec((tm, ob.shape[1]), row),
                  pl.BlockSpec((tm, oc.shape[1]), row),
                  pl.BlockSpec((1, d), lambda i: (0, 0)),
                  pl.BlockSpec((1, 1, d), lambda i: ((i * tm) // seq, 0, 0)),
                  pl.BlockSpec((d, d), lambda i: (0, 0))],
        out_specs=pl.BlockSpec((tm, d), row),
        out_shape=jax.ShapeDtypeStruct((m, d), F32),
        compiler_params=_cparams(("parallel",)),
        name="out_proj",
    )(x2, oa, ob, oc, og, gt, w_out)


def _rope_tables(seq):
    pos = jnp.arange(seq, dtype=F32)
    inv = ROPE_THETA ** (-jnp.arange(0, ROPE_DIM, 2, dtype=F32) / ROPE_DIM)
    ang = pos[:, None] * inv[None, :]
    cos, sin = jnp.cos(ang), jnp.sin(ang)
    zeros = jnp.zeros((seq, HEAD_DIM - ROPE_DIM), F32)
    zh = jnp.zeros((seq, ROPE_HALF), F32)
    cos_h = jnp.concatenate([cos, cos, jnp.ones_like(zeros)], axis=1)
    s1_h = jnp.concatenate([-sin, zh, zeros], axis=1)
    s2_h = jnp.concatenate([zh, sin, zeros], axis=1)
    two = lambda t: jnp.concatenate([t, t], axis=1)
    return two(cos_h), two(s1_h), two(s2_h)


def _permute_w_in(w):
    d = w.shape[0]
    a, b, c = N_HEADS_A * HEAD_DIM, N_HEADS_B * HEAD_DIM, N_HEADS_C * HEAD_DIM
    sizes = (a, a, a, b, b, b, IDX_HEADS * IDX_DIM, IDX_DIM, IDX_HEADS, c, c, c)
    offs = np.concatenate([[0], np.cumsum(sizes)])
    qa, ka, va, qb, kb, vb, qi, ki, wi, qc, kc, vc = [w[:, int(offs[k]):int(offs[k + 1])] for k in range(12)]
    z = lambda n: jnp.zeros((d, n), w.dtype)
    out = jnp.concatenate([qa, qb, qc, ka, va, kb, vb, kc, vc, qi,
                           ki, z(LANES - IDX_DIM), wi, z(LANES - IDX_HEADS)], axis=1)
    assert out.shape[1] == W_PERM_WIDTH
    return out.astype(BF16)


def _suffix_sum_matrix():
    s = np.arange(TK + SB_TAIL)[:, None]
    j = np.arange(TK)[None, :]
    return jnp.asarray((j > s) | (s >= TK), BF16)


def kernel(x, c, w_ada, b_ada, norm_g, w_in, qk_g, out_g, w_out, ffn_w1, ffn_w3, ffn_w2):
    batch, seq, d = x.shape
    depth = w_ada.shape[0]
    assert seq % PROJ_TM == 0 and d % LANES == 0 and batch <= 8
    assert (seq // TK) % MOBA_GROUP == 0 and (seq // TK) % DSA_GROUP == 0

    c_pad = jnp.zeros((8, d), F32).at[:batch].set(c)
    mod = _mod_call(c_pad, w_ada, b_ada)
    cos_t, s1_t, s2_t = _rope_tables(seq)
    hd = np.arange(LANES) // HEAD_DIM
    bd = jnp.asarray(hd[:, None] == hd[None, :], BF16)
    u_mat = _suffix_sum_matrix()

    x2 = x.reshape(batch * seq, d)
    for layer in range(depth):
        mods = [mod[layer, :batch, k * d:(k + 1) * d].reshape(batch, 1, d) for k in range(N_MOD)]
        sh1, sc1, g1, sh2, sc2, g2, sh3, sc3, g3 = mods
        ng = norm_g[layer]
        w1 = ffn_w1[layer].astype(BF16)
        w3 = ffn_w3[layer].astype(BF16)
        w2 = ffn_w2[layer].astype(BF16)

        x2 = _ffn_call(x2, ng[0:1], sh1, sc1, g1, w1[0], w3[0], w2[0], seq)

        qkg = jnp.concatenate([jnp.tile(qk_g[layer], (1, 2)), jnp.ones((4, LANES), F32)], axis=0)
        f_arr, h_arr, kmean, vt = _proj_call(x2, ng[1:2], sh2, sc2, _permute_w_in(w_in[layer]), qkg,
                                             cos_t, s1_t, s2_t, bd, batch, seq)
        per_tile = PROJ_TM // MOBA_BLOCK
        kmean = kmean[:, :per_tile, :].reshape(batch, seq // MOBA_BLOCK, N_HEADS_A * HEAD_DIM)
        oa = _moba_call(f_arr, h_arr, kmean, vt, batch, seq)
        ob = _dsa_call(f_arr, h_arr, vt, batch, seq)
        oc = _sb_call(f_arr, h_arr, vt, u_mat, batch, seq)
        x2 = _outproj_call(x2, oa, ob, oc, out_g[layer].reshape(1, d), g2, w_out[layer].astype(BF16), seq)

        x2 = _ffn_call(x2, ng[2:3], sh3, sc3, g3, w1[1], w3[1], w2[1], seq)
    return x2.reshape(batch, seq, d)
```
